```python
import math
import jax, jax.numpy as jnp
from jax import lax
import numpy as np

D_MODEL = 1024
BATCH = 8
SEQ = 2048
DEPTH = 1
DEC_BATCH = 128
DEC_SEQ = 8
PAST_LEN = 8192
PAGE_SIZE = 128

D_MIX = D_MODEL
ATT_HEADS = 8
ATT_KV_HEADS = 2
ATT_HEAD_DIM = 64
ATT_GROUP = ATT_HEADS // ATT_KV_HEADS
ATT_WIDTH = ATT_HEADS * ATT_HEAD_DIM
KV_WIDTH = ATT_KV_HEADS * ATT_HEAD_DIM
WINDOW = 128
M_HEADS = 4
M_HEAD_DIM = (D_MIX - ATT_WIDTH) // M_HEADS
M_WIDTH = M_HEADS * M_HEAD_DIM
M_CHUNK = 128
NORM_EPS = 1e-6
NEG_BIG = -1e30

SPLIT_SIZES = (ATT_WIDTH, KV_WIDTH, KV_WIDTH, ATT_WIDTH,
               M_WIDTH, M_WIDTH, M_WIDTH, M_WIDTH, M_WIDTH,
               M_HEADS, M_HEADS)
D_IN = sum(SPLIT_SIZES)
SPLIT_POINTS = tuple(int(s) for s in np.cumsum(SPLIT_SIZES)[:-1])

kernel_name = "hymba_mlstm_swa_sink_alibi_step"


def _rms(x, g):
    xf = x.astype(jnp.float32)
    y = xf * lax.rsqrt(jnp.mean(xf * xf, axis=-1, keepdims=True) + NORM_EPS)
    return (y * g.astype(jnp.float32)).astype(x.dtype)


def _alibi_slopes():
    s = 2.0 ** (-8.0 * (np.arange(ATT_HEADS) + 1) / ATT_HEADS)
    return jnp.asarray(s.astype(np.float32)).reshape(ATT_KV_HEADS, ATT_GROUP)


def _alibi_window_bias(diff, key_ok):
    valid = (diff >= 0) & (diff < WINDOW) & key_ok
    slopes = _alibi_slopes()[:, :, None, None]
    pen = -slopes * diff[..., None, None, :, :].astype(jnp.float32)
    return jnp.where(valid[..., None, None, :, :], pen, NEG_BIG)


def _sink_attention(q, k, v, bias, sinks):
    s = jnp.einsum('...qhgd,...khd->...hgqk', q, k).astype(jnp.float32) * (ATT_HEAD_DIM ** -0.5) + bias
    sink = jnp.broadcast_to(sinks.astype(jnp.float32)[:, :, None, None], s.shape[:-1] + (1,))
    p = jax.nn.softmax(jnp.concatenate([s, sink], axis=-1), axis=-1)[..., :-1]
    return jnp.einsum('...hgqk,...khd->...qhgd', p.astype(v.dtype), v)


def _attend_prompt(q, k, v, sinks):
    B, T = q.shape[:2]
    nb = T // WINDOW
    qb = q.reshape(B, nb, WINDOW, ATT_KV_HEADS, ATT_GROUP, ATT_HEAD_DIM)
    pad = jnp.zeros((B, WINDOW, ATT_KV_HEADS, ATT_HEAD_DIM), k.dtype)
    kp = jnp.concatenate([pad, k], axis=1).reshape(B, nb + 1, WINDOW, ATT_KV_HEADS, ATT_HEAD_DIM)
    vp = jnp.concatenate([pad, v], axis=1).reshape(B, nb + 1, WINDOW, ATT_KV_HEADS, ATT_HEAD_DIM)
    kb = jnp.concatenate([kp[:, :-1], kp[:, 1:]], axis=2)
    vb = jnp.concatenate([vp[:, :-1], vp[:, 1:]], axis=2)
    blk = jnp.arange(nb)
    qpos = blk[:, None] * WINDOW + jnp.arange(WINDOW)[None, :]
    kpos = (blk[:, None] - 1) * WINDOW + jnp.arange(2 * WINDOW)[None, :]
    diff = qpos[:, :, None] - kpos[:, None, :]
    bias = _alibi_window_bias(diff, (kpos >= 0)[:, None, :])
    out = _sink_attention(qb, kb, vb, bias, sinks)
    return out.reshape(B, T, ATT_WIDTH), k[:, -WINDOW:], v[:, -WINDOW:]


def _attend_sample(q, k, v, sinks, win_k, win_v):
    B, T = q.shape[:2]
    k_all = jnp.concatenate([win_k.astype(k.dtype), k], axis=1)
    v_all = jnp.concatenate([win_v.astype(v.dtype), v], axis=1)
    qpos = jnp.arange(T)
    kpos = jnp.concatenate([jnp.arange(WINDOW) - WINDOW, jnp.arange(T)])
    diff = qpos[:, None] - kpos[None, :]
    bias = _alibi_window_bias(diff, True)
    out = _sink_attention(q, k_all, v_all, bias, sinks)
    return out.reshape(B, T, ATT_WIDTH), k_all[:, -WINDOW:], v_all[:, -WINDOW:]


def _chunk_len(T):
    return T if T <= M_CHUNK else math.gcd(T, M_CHUNK)


def _mlstm(q, k, v, i_pre, logf, C0, n0, m0):
    B, T = q.shape[:2]
    L = _chunk_len(T)
    nc = T // L

    def to_chunks(a):
        a = a.astype(jnp.float32).reshape((B, nc, L) + a.shape[2:])
        a = jnp.moveaxis(a, 3, 2)
        return jnp.moveaxis(a, 1, 0)

    causal = jnp.tril(jnp.ones((L, L), dtype=bool))

    def step(carry, xs):
        C, n, m = carry
        qc, kc, vc, ic, fc = xs
        b = jnp.cumsum(fc, axis=-1)
        m_intra = b + lax.cummax(ic - b, axis=ic.ndim - 1)
        m_t = jnp.maximum(b + m[..., None], m_intra)
        logD = b[..., :, None] - b[..., None, :] + ic[..., None, :] - m_t[..., :, None]
        Dm = jnp.exp(jnp.where(causal, logD, -jnp.inf))
        S = jnp.einsum('bhtd,bhsd->bhts', qc, kc) * Dm
        g = jnp.exp(b + m[..., None] - m_t)
        num = jnp.einsum('bhts,bhsv->bhtv', S, vc) + g[..., None] * jnp.einsum('bhtd,bhdv->bhtv', qc, C)
        den = jnp.sum(S, axis=-1) + g * jnp.einsum('bhtd,bhd->bht', qc, n)
        h = num / jnp.maximum(jnp.abs(den), jnp.exp(-m_t))[..., None]
        m_new = m_t[..., -1]
        w = jnp.exp(b[..., -1:] - b + ic - m_new[..., None])
        decay = jnp.exp(b[..., -1] + m - m_new)
        C_new = decay[..., None, None] * C + jnp.einsum('bhs,bhsd,bhsv->bhdv', w, kc, vc)
        n_new = decay[..., None] * n + jnp.einsum('bhs,bhsd->bhd', w, kc)
        return (C_new, n_new, m_new), h

    init = (C0.astype(jnp.float32), n0.astype(jnp.float32), m0.astype(jnp.float32))
    xs = (to_chunks(q), to_chunks(k), to_chunks(v), to_chunks(i_pre), to_chunks(logf))
    (C, n, m), h = lax.scan(step, init, xs)
    h = jnp.moveaxis(jnp.moveaxis(h, 0, 1), 2, 3).reshape(B, T, M_HEADS, M_HEAD_DIM)
    return h, C, n, m


def _layer(x, g_pre, w_in, b_gate, sinks, g_mh, w_out, g_post, attend, C0, n0, m0):
    B, T, _ = x.shape
    xn = _rms(x, g_pre)
    proj = xn @ w_in
    qa, ka, va, za, qm, km, vm, om, zm, ig, fg = jnp.split(proj, SPLIT_POINTS, axis=-1)
    qa = qa.reshape(B, T, ATT_KV_HEADS, ATT_GROUP, ATT_HEAD_DIM)
    ka = ka.reshape(B, T, ATT_KV_HEADS, ATT_HEAD_DIM)
    va = va.reshape(B, T, ATT_KV_HEADS, ATT_HEAD_DIM)
    att, wk, wv = attend(qa, ka, va, sinks.reshape(ATT_KV_HEADS, ATT_GROUP))
    a_out = att * jax.nn.silu(za)
    bg = b_gate.astype(jnp.float32)
    i_pre = ig.astype(jnp.float32) + bg[:M_HEADS]
    logf = jax.nn.log_sigmoid(fg.astype(jnp.float32) + bg[M_HEADS:])
    h, C, n, m = _mlstm(qm.reshape(B, T, M_HEADS, M_HEAD_DIM),
                        km.reshape(B, T, M_HEADS, M_HEAD_DIM) * (M_HEAD_DIM ** -0.5),
                        vm.reshape(B, T, M_HEADS, M_HEAD_DIM), i_pre, logf, C0, n0, m0)
    h = _rms(h, g_mh).astype(x.dtype).reshape(B, T, M_WIDTH)
    m_out = jax.nn.sigmoid(om) * h * jax.nn.silu(zm)
    y = jnp.concatenate([a_out, m_out], axis=-1) @ w_out
    return x + _rms(y, g_post), (wk, wv, C, n, m)


def setup_inputs(seed: int = 0) -> dict:
    key = jax.random.key(seed)
    ks = jax.random.split(key, 20)
    f32 = jnp.float32
    x_prompt = jax.random.normal(ks[0], (BATCH, SEQ, D_MODEL), f32)
    x_sample = jax.random.normal(ks[1], (DEC_BATCH, DEC_SEQ, D_MODEL), f32)
    cache_win_k = jax.random.normal(ks[2], (DEPTH, DEC_BATCH, WINDOW, ATT_KV_HEADS, ATT_HEAD_DIM), f32)
    cache_win_v = jax.random.normal(ks[3], (DEPTH, DEC_BATCH, WINDOW, ATT_KV_HEADS, ATT_HEAD_DIM), f32)
    state_C = 0.5 * jax.random.normal(ks[4], (DEPTH, DEC_BATCH, M_HEADS, M_HEAD_DIM, M_HEAD_DIM), f32)
    state_n = jax.random.normal(ks[5], (DEPTH, DEC_BATCH, M_HEADS, M_HEAD_DIM), f32)
    state_m = jax.random.normal(ks[6], (DEPTH, DEC_BATCH, M_HEADS), f32)
    g_pre = 1.0 + 0.05 * jax.random.normal(ks[7], (DEPTH, D_MODEL), f32)
    w_in = jax.random.normal(ks[8], (DEPTH, D_MODEL, D_IN), f32) * (D_MODEL ** -0.5)
    b_i = 0.1 * jax.random.normal(ks[9], (DEPTH, M_HEADS), f32)
    b_f = jnp.linspace(3.0, 6.0, M_HEADS, dtype=f32)[None, :] + 0.1 * jax.random.normal(ks[10], (DEPTH, M_HEADS), f32)
    b_gate = jnp.concatenate([b_i, b_f], axis=-1)
    attn_sinks = 0.5 * jax.random.normal(ks[11], (DEPTH, ATT_HEADS), f32)
    g_mh = 1.0 + 0.05 * jax.random.normal(ks[12], (DEPTH, M_HEADS, M_HEAD_DIM), f32)
    w_out = jax.random.normal(ks[13], (DEPTH, D_MIX, D_MODEL), f32) * (D_MIX ** -0.5)
    g_post = 1.0 + 0.05 * jax.random.normal(ks[14], (DEPTH, D_MODEL), f32)
    return {"x_prompt": x_prompt, "x_sample": x_sample,
            "cache_win_k": cache_win_k, "cache_win_v": cache_win_v,
            "state_C": state_C, "state_n": state_n, "state_m": state_m,
            "g_pre": g_pre, "w_in": w_in, "b_gate": b_gate, "attn_sinks": attn_sinks,
            "g_mh": g_mh, "w_out": w_out, "g_post": g_post}


def reference(x_prompt, x_sample, cache_win_k, cache_win_v, state_C, state_n, state_m,
              g_pre, w_in, b_gate, attn_sinks, g_mh, w_out, g_post):
    yp, ys = x_prompt, x_sample
    Bp = x_prompt.shape[0]
    p_states = ([], [], [], [], [])
    s_states = ([], [], [], [], [])
    for l in range(DEPTH):
        C0 = jnp.zeros((Bp, M_HEADS, M_HEAD_DIM, M_HEAD_DIM), jnp.float32)
        n0 = jnp.zeros((Bp, M_HEADS, M_HEAD_DIM), jnp.float32)
        m0 = jnp.zeros((Bp, M_HEADS), jnp.float32)
        yp, st_p = _layer(yp, g_pre[l], w_in[l], b_gate[l], attn_sinks[l], g_mh[l], w_out[l], g_post[l],
                          _attend_prompt, C0, n0, m0)
        wk_l, wv_l = cache_win_k[l], cache_win_v[l]
        attend_s = lambda q, k, v, s, wk=wk_l, wv=wv_l: _attend_sample(q, k, v, s, wk, wv)
        ys, st_s = _layer(ys, g_pre[l], w_in[l], b_gate[l], attn_sinks[l], g_mh[l], w_out[l], g_post[l],
                          attend_s, state_C[l], state_n[l], state_m[l])
        for lst, a in zip(p_states, st_p):
            lst.append(a)
        for lst, a in zip(s_states, st_s):
            lst.append(a)
    wk_p, wv_p, C_p, n_p, m_p = [jnp.stack(a, axis=0) for a in p_states]
    wk_s, wv_s, C_s, n_s, m_s = [jnp.stack(a, axis=0) for a in s_states]
    return (yp, ys, wk_p, wv_p, C_p, n_p, m_p, wk_s, wv_s, C_s, n_s, m_s)
```

```python
import functools

import jax
import jax.numpy as jnp
from jax import lax
from jax.experimental import pallas as pl
from jax.experimental.pallas import tpu as pltpu

F32 = jnp.float32
BF16 = jnp.bfloat16

D_MODEL = 1024
ROWS = 128
ATT_HEADS, ATT_KV, ATT_GROUP, ATT_DIM = 8, 2, 4, 64
M_HEADS, M_DIM = 4, 128
NORM_EPS = 1e-6
NEG_BIG = -1e30
ATT_SCALE = ATT_DIM ** -0.5
K_SCALE = M_DIM ** -0.5

QA, KA, VA, ZA = 0, 512, 640, 768
QM, KM, VM, OM, ZM = 1280, 1792, 2304, 2816, 3328
GATES = 3840
D_IN = 3848
D_IN_PAD = 3968

SAMPLE_SEQ = 8
SAMPLE_GROUP = ROWS // SAMPLE_SEQ

VMEM_LIMIT_BYTES = 56 * 1024 * 1024
NT_DIMS = (((1,), (1,)), ((), ()))


def _rms(x, g):
    return x * lax.rsqrt(jnp.mean(x * x, axis=-1, keepdims=True) + NORM_EPS) * g


def _silu(x):
    return x * jax.nn.sigmoid(x)


def _log_sigmoid(x):
    return -(jnp.maximum(-x, 0.0) + jnp.log1p(jnp.exp(-jnp.abs(x))))


def _slope(head):
    return 2.0 ** -(head + 1)


def _bdot(a, b):
    return jnp.dot(a.astype(BF16), b.astype(BF16), preferred_element_type=F32)


def _bdot_nt(a, b):
    return lax.dot_general(a.astype(BF16), b.astype(BF16), NT_DIMS, preferred_element_type=F32)


def _seg_scan(x, pos, shifts, op, ident):
    for sh in shifts:
        x = op(x, jnp.where(pos >= sh, pltpu.roll(x, sh, 1), ident))
    return x


def _seg_last(x, pos, shifts, seg):
    y = jnp.where(pos == seg - 1, x, -jnp.inf)
    for sh in shifts:
        y = jnp.maximum(y, jnp.where(pos + sh <= seg - 1, pltpu.roll(y, ROWS - sh, 1), -jnp.inf))
    return y


def _gate_math(gates_rows, bg_col, m0, seg):
    shifts = tuple(1 << i for i in range(seg.bit_length() - 1))
    x = gates_rows.T[0:8, :] + bg_col
    row = lax.broadcasted_iota(jnp.int32, (8, ROWS), 0)
    lane = lax.broadcasted_iota(jnp.int32, (8, ROWS), 1)
    pos = lane & (seg - 1)
    head_rows = row < M_HEADS
    ic = jnp.where(head_rows, x, 0.0)
    fc = jnp.where(head_rows, _log_sigmoid(pltpu.roll(x, M_HEADS, 0)), 0.0)
    b = _seg_scan(fc, pos, shifts, jnp.add, 0.0)
    a = ic - b
    m_t = jnp.maximum(b + m0, b + _seg_scan(a, pos, shifts, jnp.maximum, -jnp.inf))
    gexp = jnp.exp(b + m0 - m_t)
    enm = jnp.exp(-m_t)
    b_last = _seg_last(b, pos, shifts, seg)
    m_new = _seg_last(m_t, pos, shifts, seg)
    w = jnp.exp(b_last - b + ic - m_new)
    decay = jnp.exp(b_last + m0 - m_new)
    p1 = jnp.where(head_rows, b - m_t, pltpu.roll(gexp, M_HEADS, 0))
    p2 = jnp.where(head_rows, enm, pltpu.roll(w, M_HEADS, 0))
    p3 = jnp.where(head_rows, decay, 0.0)
    cols = jnp.concatenate([p1, p2, p3, jnp.zeros((ROWS - 24, ROWS), F32)], axis=0).T
    return a, m_new, decay, cols


def _mlstm_head(q, k, v, a_row, cols, h, mask, q_c, q_n):
    bm_col = cols[:, h:h + 1]
    g_col = cols[:, 4 + h:5 + h]
    enm_col = cols[:, 8 + h:9 + h]
    dm = jnp.exp(jnp.where(mask, bm_col + a_row, -jnp.inf))
    s = _bdot_nt(q, k) * dm
    num = _bdot(s, v) + g_col * q_c
    den = jnp.sum(s, axis=-1, keepdims=True) + g_col * q_n
    return num / jnp.maximum(jnp.abs(den), enm_col)


def _head_out(h_t, g_row, om, zm):
    return jax.nn.sigmoid(om) * _rms(h_t, g_row) * _silu(zm)


def _out_tail(cat, x, wout_ref, gpost_ref):
    y = jnp.dot(cat, wout_ref[...], preferred_element_type=F32)
    return x + _rms(y, gpost_ref[...])


def _prompt_bias(kv, first):
    i = lax.broadcasted_iota(jnp.int32, (ROWS, 2 * ROWS), 0)
    j = lax.broadcasted_iota(jnp.int32, (ROWS, 2 * ROWS), 1)
    diff = ROWS + i - j
    valid = (diff >= 0) & (diff < ROWS)
    if first:
        valid = valid & (j >= ROWS)
    dfl = diff.astype(F32)
    return jnp.concatenate(
        [jnp.where(valid, -_slope(kv * ATT_GROUP + g) * dfl, NEG_BIG) for g in range(ATT_GROUP)], axis=0)


def _prompt_kernel(x_ref, gpre_ref, win_ref, bg_ref, sink_ref, gmh_ref, wout_ref, gpost_ref,
                   y_ref, wk_ref, wv_ref, c_ref, n_ref, m_ref,
                   proj_ref, cat_ref, kprev_ref, vprev_ref, caug_ref, mst_ref, bias_ref, *, tb, nt):
    bi = pl.program_id(0)
    j = pl.program_id(1)

    @pl.when((bi == 0) & (j == 0))
    def _():
        for first in range(2):
            for kv in range(ATT_KV):
                bias_ref[first * ATT_KV + kv] = _prompt_bias(kv, first)

    @pl.when(j == 0)
    def _():
        kprev_ref[...] = jnp.zeros_like(kprev_ref)
        vprev_ref[...] = jnp.zeros_like(vprev_ref)
        caug_ref[...] = jnp.zeros_like(caug_ref)
        mst_ref[...] = jnp.zeros_like(mst_ref)

    x = x_ref[0]
    xn = _rms(x, gpre_ref[...])
    proj_ref[...] = jnp.dot(xn.astype(BF16), win_ref[...], preferred_element_type=F32)

    ri = lax.broadcasted_iota(jnp.int32, (ROWS, ROWS), 0)
    ci = lax.broadcasted_iota(jnp.int32, (ROWS, ROWS), 1)
    causal = ri >= ci

    def chunk(c, carry):
        r0 = pl.multiple_of(c * ROWS, ROWS)
        rows = pl.ds(r0, ROWS)
        first = ((j == 0) & (c == 0)).astype(jnp.int32)

        qa = proj_ref[rows, QA:QA + 512]
        ka = proj_ref[rows, KA:KA + 128].astype(BF16)
        va = proj_ref[rows, VA:VA + 128].astype(BF16)
        kcat = jnp.concatenate([kprev_ref[...], ka], axis=0)
        vcat = jnp.concatenate([vprev_ref[...], va], axis=0)
        att = []
        for kv in range(ATT_KV):
            kk = kcat[:, kv * ATT_DIM:(kv + 1) * ATT_DIM]
            vv = vcat[:, kv * ATT_DIM:(kv + 1) * ATT_DIM]
            heads = [kv * ATT_GROUP + g for g in range(ATT_GROUP)]
            q4 = jnp.concatenate([qa[:, hh * ATT_DIM:(hh + 1) * ATT_DIM] for hh in heads], axis=0)
            s = _bdot_nt(q4, kk) * ATT_SCALE + bias_ref[first * ATT_KV + kv]
            sink = jnp.concatenate(
                [jnp.broadcast_to(sink_ref[0:1, hh:hh + 1], (ROWS, 1)) for hh in heads], axis=0)
            mx = jnp.maximum(jnp.max(s, axis=-1, keepdims=True), sink)
            p = jnp.exp(s - mx)
            den = jnp.sum(p, axis=-1, keepdims=True) + jnp.exp(sink - mx)
            o = _bdot(p, vv) / den
            att += [o[g * ROWS:(g + 1) * ROWS] for g in range(ATT_GROUP)]
        a_out = jnp.concatenate(att, axis=1) * _silu(proj_ref[rows, ZA:ZA + 512])
        kprev_ref[...] = ka
        vprev_ref[...] = va

        a, m_new, decay, cols = _gate_math(proj_ref[rows, GATES:GATES + ROWS], bg_ref[...], mst_ref[...], ROWS)
        m_out = []
        for h in range(M_HEADS):
            q = proj_ref[rows, QM + h * M_DIM:QM + (h + 1) * M_DIM]
            k = proj_ref[rows, KM + h * M_DIM:KM + (h + 1) * M_DIM] * K_SCALE
            v = proj_ref[rows, VM + h * M_DIM:VM + (h + 1) * M_DIM]
            caug = caug_ref[h]
            qc = _bdot(q, caug)
            h_t = _mlstm_head(q, k, v, a[h:h + 1, :], cols, h, causal, qc[:, 0:M_DIM], qc[:, M_DIM:M_DIM + 1])
            m_out.append(_head_out(h_t, gmh_ref[h:h + 1, :],
                                   proj_ref[rows, OM + h * M_DIM:OM + (h + 1) * M_DIM],
                                   proj_ref[rows, ZM + h * M_DIM:ZM + (h + 1) * M_DIM]))
            w_col = cols[:, 12 + h:13 + h]
            wv = jnp.concatenate([w_col * v, jnp.where(ci == 0, w_col, 0.0)], axis=1)
            caug_ref[h] = decay[h:h + 1, 0:1] * caug + _bdot(k.T, wv)
        mst_ref[...] = m_new

        cat_ref[rows, :] = jnp.concatenate([a_out] + m_out, axis=1).astype(BF16)
        return carry

    lax.fori_loop(0, tb // ROWS, chunk, 0)

    y_ref[0] = _out_tail(cat_ref[...], x, wout_ref, gpost_ref)

    @pl.when(j == nt - 1)
    def _():
        wk_ref[0] = proj_ref[tb - ROWS:tb, KA:KA + 128]
        wv_ref[0] = proj_ref[tb - ROWS:tb, VA:VA + 128]
        for h in range(M_HEADS):
            caug = caug_ref[h]
            c_ref[0, h] = caug[:, 0:M_DIM]
            n_ref[0, h:h + 1, :] = caug[:, M_DIM:2 * M_DIM].T[0:1, :]
        m_ref[0] = mst_ref[...]


def _prompt_call(x, gpre, win, bg, sinks, gmh, wout, gpost, tb):
    bsz, seq, _ = x.shape
    nt = seq // tb
    full = lambda shape: pl.BlockSpec(shape, lambda b, j: (0,) * len(shape))
    out_shapes = (
        jax.ShapeDtypeStruct((bsz, seq, D_MODEL), F32),
        jax.ShapeDtypeStruct((bsz, ROWS, 128), F32),
        jax.ShapeDtypeStruct((bsz, ROWS, 128), F32),
        jax.ShapeDtypeStruct((bsz, M_HEADS, M_DIM, M_DIM), F32),
        jax.ShapeDtypeStruct((bsz, M_HEADS, M_DIM), F32),
        jax.ShapeDtypeStruct((bsz, 8, ROWS), F32),
    )
    return pl.pallas_call(
        functools.partial(_prompt_kernel, tb=tb, nt=nt),
        grid=(bsz, nt),
        in_specs=[
            pl.BlockSpec((1, tb, D_MODEL), lambda b, j: (b, j, 0)),
            full((1, D_MODEL)), full((D_MODEL, D_IN_PAD)), full((8, 1)), full((1, ATT_HEADS)),
            full((M_HEADS, M_DIM)), full((D_MODEL, D_MODEL)), full((1, D_MODEL)),
        ],
        out_specs=(
            pl.BlockSpec((1, tb, D_MODEL), lambda b, j: (b, j, 0)),
            pl.BlockSpec((1, ROWS, 128), lambda b, j: (b, 0, 0)),
            pl.BlockSpec((1, ROWS, 128), lambda b, j: (b, 0, 0)),
            pl.BlockSpec((1, M_HEADS, M_DIM, M_DIM), lambda b, j: (b, 0, 0, 0)),
            pl.BlockSpec((1, M_HEADS, M_DIM), lambda b, j: (b, 0, 0)),
            pl.BlockSpec((1, 8, ROWS), lambda b, j: (b, 0, 0)),
        ),
        out_shape=out_shapes,
        scratch_shapes=[
            pltpu.VMEM((tb, D_IN_PAD), F32),
            pltpu.VMEM((tb, D_MODEL), BF16),
            pltpu.VMEM((ROWS, 128), BF16),
            pltpu.VMEM((ROWS, 128), BF16),
            pltpu.VMEM((M_HEADS, M_DIM, 2 * M_DIM), F32),
            pltpu.VMEM((8, ROWS), F32),
            pltpu.VMEM((2 * ATT_KV, ATT_GROUP * ROWS, 2 * ROWS), F32),
        ],
        compiler_params=pltpu.CompilerParams(
            dimension_semantics=("arbitrary", "arbitrary"), vmem_limit_bytes=VMEM_LIMIT_BYTES),
        name="prompt_layer",
    )(x, gpre, win, bg, sinks, gmh, wout, gpost)


def _sample_bias_new(head):
    r = lax.broadcasted_iota(jnp.int32, (ROWS, ROWS), 0)
    c = lax.broadcasted_iota(jnp.int32, (ROWS, ROWS), 1)
    valid = ((r >> 3) == (c >> 3)) & (r >= c)
    return jnp.where(valid, -_slope(head) * (r - c).astype(F32), NEG_BIG)


def _sample_bias_cache(head):
    r = lax.broadcasted_iota(jnp.int32, (ROWS, ROWS), 0)
    c = lax.broadcasted_iota(jnp.int32, (ROWS, ROWS), 1)
    diff = (r & (SAMPLE_SEQ - 1)) + ROWS - c
    return jnp.where(diff < ROWS, -_slope(head) * diff.astype(F32), NEG_BIG)


def _sample_kernel(x_ref, kc_ref, vc_ref, cin_ref, nin_ref, m0_ref,
                   gpre_ref, win_ref, bg_ref, sink_ref, gmh_ref, wout_ref, gpost_ref,
                   y_ref, ko_ref, vo_ref, cout_ref, nout_ref, mout_ref,
                   proj_ref, sc_ref, oc_ref, qc_ref, cols_ref, wv_ref, kt_ref, bn_ref, bc_ref):
    @pl.when(pl.program_id(0) == 0)
    def _():
        for hh in range(ATT_HEADS):
            bn_ref[hh] = _sample_bias_new(hh)
            bc_ref[hh] = _sample_bias_cache(hh)

    x = x_ref[...]
    xn = _rms(x, gpre_ref[...])
    proj_ref[...] = jnp.dot(xn.astype(BF16), win_ref[...], preferred_element_type=F32)

    ri = lax.broadcasted_iota(jnp.int32, (ROWS, ROWS), 0)
    ci = lax.broadcasted_iota(jnp.int32, (ROWS, ROWS), 1)
    same_seq = (ri >> 3) == (ci >> 3)
    mask = same_seq & (ri >= ci)

    def seq_scores(b, carry):
        rows = pl.ds(pl.multiple_of(b * SAMPLE_SEQ, SAMPLE_SEQ), SAMPLE_SEQ)
        kc = kc_ref[b]
        for kv in range(ATT_KV):
            kk = kc[:, kv * ATT_DIM:(kv + 1) * ATT_DIM]
            for g in range(ATT_GROUP):
                hh = kv * ATT_GROUP + g
                sc_ref[hh, rows, :] = _bdot_nt(proj_ref[rows, QA + hh * ATT_DIM:QA + (hh + 1) * ATT_DIM], kk)
        for h in range(M_HEADS):
            qc_ref[rows, h * M_DIM:(h + 1) * M_DIM] = _bdot(
                proj_ref[rows, QM + h * M_DIM:QM + (h + 1) * M_DIM], cin_ref[b, h])
        return carry

    lax.fori_loop(0, SAMPLE_GROUP, seq_scores, 0)

    ka = proj_ref[:, KA:KA + 128]
    va = proj_ref[:, VA:VA + 128]
    o_new, dens = [], []
    for hh in range(ATT_HEADS):
        kv = hh // ATT_GROUP
        s_n = _bdot_nt(proj_ref[:, QA + hh * ATT_DIM:QA + (hh + 1) * ATT_DIM],
                       ka[:, kv * ATT_DIM:(kv + 1) * ATT_DIM]) * ATT_SCALE + bn_ref[hh]
        s_c = sc_ref[hh] * ATT_SCALE + bc_ref[hh]
        sink = sink_ref[0:1, hh:hh + 1]
        mx = jnp.maximum(jnp.maximum(jnp.max(s_n, axis=-1, keepdims=True),
                                     jnp.max(s_c, axis=-1, keepdims=True)), sink)
        p_n = jnp.exp(s_n - mx)
        p_c = jnp.exp(s_c - mx)
        dens.append(jnp.sum(p_n, axis=-1, keepdims=True) + jnp.sum(p_c, axis=-1, keepdims=True)
                    + jnp.exp(sink - mx))
        o_new.append(_bdot(p_n, va[:, kv * ATT_DIM:(kv + 1) * ATT_DIM]))
        sc_ref[hh] = p_c

    a, m_new, _, cols = _gate_math(proj_ref[:, GATES:GATES + ROWS], bg_ref[...], m0_ref[0], SAMPLE_SEQ)
    cols_ref[...] = cols
    mout_ref[0] = m_new
    first_row = (ri & (SAMPLE_SEQ - 1)) == 0
    m_out = []
    for h in range(M_HEADS):
        q = proj_ref[:, QM + h * M_DIM:QM + (h + 1) * M_DIM]
        k = proj_ref[:, KM + h * M_DIM:KM + (h + 1) * M_DIM] * K_SCALE
        v = proj_ref[:, VM + h * M_DIM:VM + (h + 1) * M_DIM]
        n_rep = jnp.broadcast_to(nin_ref[h][:, None, :], (SAMPLE_GROUP, SAMPLE_SEQ, M_DIM)).reshape(ROWS, M_DIM)
        q_n = jnp.sum(q * n_rep, axis=-1, keepdims=True)
        h_t = _mlstm_head(q, k, v, a[h:h + 1, :], cols, h, mask, qc_ref[:, h * M_DIM:(h + 1) * M_DIM], q_n)
        m_out.append(_head_out(h_t, gmh_ref[h:h + 1, :],
                               proj_ref[:, OM + h * M_DIM:OM + (h + 1) * M_DIM],
                               proj_ref[:, ZM + h * M_DIM:ZM + (h + 1) * M_DIM]))
        w_col = cols[:, 12 + h:13 + h]
        dec_col = cols[:, 16 + h:17 + h]
        n_terms = jnp.where(first_row, dec_col * n_rep, 0.0) + w_col * k
        nout_ref[h] = jnp.sum(n_terms.reshape(SAMPLE_GROUP, SAMPLE_SEQ, M_DIM), axis=1)
        wv_ref[h] = w_col * v
        kt_ref[h] = k.T.astype(BF16)

    def seq_update(b, carry):
        rows = pl.ds(pl.multiple_of(b * SAMPLE_SEQ, SAMPLE_SEQ), SAMPLE_SEQ)
        vc = vc_ref[b]
        for kv in range(ATT_KV):
            vv = vc[:, kv * ATT_DIM:(kv + 1) * ATT_DIM]
            for g in range(ATT_GROUP):
                hh = kv * ATT_GROUP + g
                oc_ref[rows, hh * ATT_DIM:(hh + 1) * ATT_DIM] = _bdot(sc_ref[hh, rows, :], vv)
        in_seq = (ri >> 3) == b
        for h in range(M_HEADS):
            dec = cols_ref[rows, 16 + h:17 + h][0:1, :]
            upd = jnp.dot(kt_ref[h], jnp.where(in_seq, wv_ref[h], 0.0).astype(BF16), preferred_element_type=F32)
            cout_ref[b, h] = dec * cin_ref[b, h] + upd
        return carry

    lax.fori_loop(0, SAMPLE_GROUP, seq_update, 0)

    att = [(o_new[hh] + oc_ref[:, hh * ATT_DIM:(hh + 1) * ATT_DIM]) / dens[hh] for hh in range(ATT_HEADS)]
    a_out = jnp.concatenate(att, axis=1) * _silu(proj_ref[:, ZA:ZA + 512])
    cat = jnp.concatenate([a_out] + m_out, axis=1).astype(BF16)
    y_ref[...] = _out_tail(cat, x, wout_ref, gpost_ref)

    keep = ROWS - SAMPLE_SEQ
    ko_ref[:, 0:keep, :] = kc_ref[:, SAMPLE_SEQ:ROWS, :]
    ko_ref[:, keep:ROWS, :] = ka.reshape(SAMPLE_GROUP, SAMPLE_SEQ, 128)
    vo_ref[:, 0:keep, :] = vc_ref[:, SAMPLE_SEQ:ROWS, :]
    vo_ref[:, keep:ROWS, :] = va.reshape(SAMPLE_GROUP, SAMPLE_SEQ, 128)


def _sample_call(x, kc, vc, cin, nin, m0, gpre, win, bg, sinks, gmh, wout, gpost):
    nrows = x.shape[0]
    ngroups = nrows // ROWS
    nseq = ngroups * SAMPLE_GROUP
    full = lambda shape: pl.BlockSpec(shape, lambda i: (0,) * len(shape))
    grp = SAMPLE_GROUP
    out_shapes = (
        jax.ShapeDtypeStruct((nrows, D_MODEL), F32),
        jax.ShapeDtypeStruct((nseq, ROWS, 128), F32),
        jax.ShapeDtypeStruct((nseq, ROWS, 128), F32),
        jax.ShapeDtypeStruct((nseq, M_HEADS, M_DIM, M_DIM), F32),
        jax.ShapeDtypeStruct((M_HEADS, nseq, M_DIM), F32),
        jax.ShapeDtypeStruct((ngroups, 8, ROWS), F32),
    )
    return pl.pallas_call(
        _sample_kernel,
        grid=(ngroups,),
        in_specs=[
            pl.BlockSpec((ROWS, D_MODEL), lambda i: (i, 0)),
            pl.BlockSpec((grp, ROWS, 128), lambda i: (i, 0, 0)),
            pl.BlockSpec((grp, ROWS, 128), lambda i: (i, 0, 0)),
            pl.BlockSpec((grp, M_HEADS, M_DIM, M_DIM), lambda i: (i, 0, 0, 0)),
            pl.BlockSpec((M_HEADS, grp, M_DIM), lambda i: (0, i, 0)),
            pl.BlockSpec((1, 8, ROWS), lambda i: (i, 0, 0)),
            full((1, D_MODEL)), full((D_MODEL, D_IN_PAD)), full((8, 1)), full((1, ATT_HEADS)),
            full((M_HEADS, M_DIM)), full((D_MODEL, D_MODEL)), full((1, D_MODEL)),
        ],
        out_specs=(
            pl.BlockSpec((ROWS, D_MODEL), lambda i: (i, 0)),
            pl.BlockSpec((grp, ROWS, 128), lambda i: (i, 0, 0)),
            pl.BlockSpec((grp, ROWS, 128), lambda i: (i, 0, 0)),
            pl.BlockSpec((grp, M_HEADS, M_DIM, M_DIM), lambda i: (i, 0, 0, 0)),
            pl.BlockSpec((M_HEADS, grp, M_DIM), lambda i: (0, i, 0)),
            pl.BlockSpec((1, 8, ROWS), lambda i: (i, 0, 0)),
        ),
        out_shape=out_shapes,
        scratch_shapes=[
            pltpu.VMEM((ROWS, D_IN_PAD), F32),
            pltpu.VMEM((ATT_HEADS, ROWS, ROWS), F32),
            pltpu.VMEM((ROWS, ATT_HEADS * ATT_DIM), F32),
            pltpu.VMEM((ROWS, M_HEADS * M_DIM), F32),
            pltpu.VMEM((ROWS, ROWS), F32),
            pltpu.VMEM((M_HEADS, ROWS, M_DIM), F32),
            pltpu.VMEM((M_HEADS, M_DIM, ROWS), BF16),
            pltpu.VMEM((ATT_HEADS, ROWS, ROWS), F32),
            pltpu.VMEM((ATT_HEADS, ROWS, ROWS), F32),
        ],
        compiler_params=pltpu.CompilerParams(
            dimension_semantics=("arbitrary",), vmem_limit_bytes=VMEM_LIMIT_BYTES),
        name="sample_layer",
    )(x, kc, vc, cin, nin, m0, gpre, win, bg, sinks, gmh, wout, gpost)


PROMPT_BLOCK = 256


def kernel(x_prompt, x_sample, cache_win_k, cache_win_v, state_C, state_n, state_m,
           g_pre, w_in, b_gate, attn_sinks, g_mh, w_out, g_post):
    depth = g_pre.shape[0]
    assert depth == 1, "single-layer trunk"
    bsz = x_prompt.shape[0]
    nseq, sseq, _ = x_sample.shape
    assert sseq == SAMPLE_SEQ and nseq % SAMPLE_GROUP == 0

    gpre = g_pre[0].reshape(1, D_MODEL)
    gpost = g_post[0].reshape(1, D_MODEL)
    win = jnp.pad(w_in[0], ((0, 0), (0, D_IN_PAD - D_IN))).astype(BF16)
    wout = w_out[0].astype(BF16)
    bg = b_gate[0].reshape(2 * M_HEADS, 1)
    sinks = attn_sinks[0].reshape(1, ATT_HEADS)
    gmh = g_mh[0]

    yp, wkp, wvp, cp, np_, mp = _prompt_call(x_prompt, gpre, win, bg, sinks, gmh, wout, gpost, PROMPT_BLOCK)

    ngroups = nseq // SAMPLE_GROUP
    m0 = jnp.broadcast_to(state_m[0].reshape(ngroups, SAMPLE_GROUP, 1, M_HEADS),
                          (ngroups, SAMPLE_GROUP, SAMPLE_SEQ, M_HEADS))
    m0 = jnp.pad(m0.reshape(ngroups, ROWS, M_HEADS).transpose(0, 2, 1), ((0, 0), (0, 8 - M_HEADS), (0, 0)))
    ys, wks, wvs, cs, ns, ms = _sample_call(
        x_sample.reshape(nseq * sseq, D_MODEL),
        cache_win_k[0].reshape(nseq, ROWS, 128), cache_win_v[0].reshape(nseq, ROWS, 128),
        state_C[0], state_n[0].transpose(1, 0, 2), m0,
        gpre, win, bg, sinks, gmh, wout, gpost)

    kv_shape = (ATT_KV, ATT_DIM)
    ms = ms[:, 0:M_HEADS, ::SAMPLE_SEQ].transpose(0, 2, 1).reshape(nseq, M_HEADS)
    return (yp, ys.reshape(nseq, sseq, D_MODEL),
            wkp.reshape((1, bsz, ROWS) + kv_shape), wvp.reshape((1, bsz, ROWS) + kv_shape),
            cp[None], np_[None], mp[:, 0:M_HEADS, 0][None],
            wks.reshape((1, nseq, ROWS) + kv_shape), wvs.reshape((1, nseq, ROWS) + kv_shape),
            cs[None], ns.transpose(1, 0, 2)[None], ms[None])
```

```python
import functools

import jax
import jax.numpy as jnp
from jax import lax
from jax.experimental import pallas as pl
from jax.experimental.pallas import tpu as pltpu

F32 = jnp.float32
BF16 = jnp.bfloat16

D_MODEL = 1024
ROWS = 128
ATT_HEADS, ATT_KV, ATT_GROUP, ATT_DIM = 8, 2, 4, 64
M_HEADS, M_DIM = 4, 128
NORM_EPS = 1e-6
NEG_BIG = -1e30
ATT_SCALE = ATT_DIM ** -0.5
K_SCALE = M_DIM ** -0.5

QA, KA, VA, ZA = 0, 512, 640, 768
QM, KM, VM, OM, ZM = 1280, 1792, 2304, 2816, 3328
GATES = 3840
D_IN = 3848
D_IN_PAD = 3968

P_QA, P_KA, P_ZA, P_QM, P_KM, P_OM, P_ZM = 0, 512, 640, 1152, 1664, 2176, 2688
P_MAIN = 3200
T_VA, T_VM, T_GATES = 0, 128, 640
T_ROWS = 656
STATE_ROWS = 144

SAMPLE_SEQ = 8
SAMPLE_GROUP = ROWS // SAMPLE_SEQ

VMEM_LIMIT_BYTES = 56 * 1024 * 1024
NT_DIMS = (((1,), (1,)), ((), ()))


def _rms(x, g):
    return x * lax.rsqrt(jnp.mean(x * x, axis=-1, keepdims=True) + NORM_EPS) * g


def _silu(x):
    return x * jax.nn.sigmoid(x)


def _log_sigmoid(x):
    return -(jnp.maximum(-x, 0.0) + jnp.log1p(jnp.exp(-jnp.abs(x))))


def _slope(head):
    return 2.0 ** -(head + 1)


def _bdot(a, b):
    return jnp.dot(a.astype(BF16), b.astype(BF16), preferred_element_type=F32)


def _bdot_nt(a, b):
    return lax.dot_general(a.astype(BF16), b.astype(BF16), NT_DIMS, preferred_element_type=F32)


def _seg_scan(x, pos, shifts, op, ident):
    for sh in shifts:
        x = op(x, jnp.where(pos >= sh, pltpu.roll(x, sh, 1), ident))
    return x


def _seg_last(x, pos, shifts, seg):
    y = jnp.where(pos == seg - 1, x, -jnp.inf)
    for sh in shifts:
        y = jnp.maximum(y, jnp.where(pos + sh <= seg - 1, pltpu.roll(y, ROWS - sh, 1), -jnp.inf))
    return y


def _gate_rows(x, m0, seg):
    shifts = tuple(1 << i for i in range(seg.bit_length() - 1))
    row = lax.broadcasted_iota(jnp.int32, (8, ROWS), 0)
    lane = lax.broadcasted_iota(jnp.int32, (8, ROWS), 1)
    pos = lane & (seg - 1)
    head_rows = row < M_HEADS
    ic = jnp.where(head_rows, x, 0.0)
    fc = jnp.where(head_rows, _log_sigmoid(pltpu.roll(x, M_HEADS, 0)), 0.0)
    b = _seg_scan(fc, pos, shifts, jnp.add, 0.0)
    a = ic - b
    m_t = jnp.maximum(b + m0, b + _seg_scan(a, pos, shifts, jnp.maximum, -jnp.inf))
    b_last = _seg_last(b, pos, shifts, seg)
    m_new = _seg_last(m_t, pos, shifts, seg)
    return dict(a=a, bm=b - m_t, gexp=jnp.exp(b + m0 - m_t), enm=jnp.exp(-m_t), m_new=m_new,
                w=jnp.exp(b_last - b + ic - m_new), decay=jnp.exp(b_last + m0 - m_new), head_rows=head_rows)


def _gate_math(gates_rows, bg_col, m0, seg):
    r = _gate_rows(gates_rows.T[0:8, :] + bg_col, m0, seg)
    head_rows = r["head_rows"]
    p1 = jnp.where(head_rows, r["bm"], pltpu.roll(r["gexp"], M_HEADS, 0))
    p2 = jnp.where(head_rows, r["enm"], pltpu.roll(r["w"], M_HEADS, 0))
    p3 = jnp.where(head_rows, r["decay"], 0.0)
    cols = jnp.concatenate([p1, p2, p3, jnp.zeros((ROWS - 24, ROWS), F32)], axis=0).T
    return r["a"], r["m_new"], cols


def _mlstm_head(q, k, v, a_row, cols, h, mask, q_c, q_n):
    bm_col = cols[:, h:h + 1]
    g_col = cols[:, 4 + h:5 + h]
    enm_col = cols[:, 8 + h:9 + h]
    dm = jnp.exp(jnp.where(mask, bm_col + a_row, -jnp.inf))
    s = _bdot_nt(q, k) * dm
    num = _bdot(s, v) + g_col * q_c
    den = jnp.sum(s, axis=-1, keepdims=True) + g_col * q_n
    return num / jnp.maximum(jnp.abs(den), enm_col)


def _head_out(h_t, g_row, om, zm):
    return jax.nn.sigmoid(om) * _rms(h_t, g_row) * _silu(zm)


def _out_tail(cat, x, wout_ref, gpost_ref):
    y = jnp.dot(cat, wout_ref[...], preferred_element_type=F32)
    return x + _rms(y, gpost_ref[...])


def _prompt_bias_t(kv, first):
    j = lax.broadcasted_iota(jnp.int32, (2 * ROWS, ROWS), 0)
    i = lax.broadcasted_iota(jnp.int32, (2 * ROWS, ROWS), 1)
    diff = ROWS + i - j
    valid = (diff >= 0) & (diff < ROWS)
    if first:
        valid = valid & (j >= ROWS)
    dfl = diff.astype(F32)
    return jnp.concatenate(
        [jnp.where(valid, -_slope(kv * ATT_GROUP + g) * dfl, NEG_BIG) for g in range(ATT_GROUP)], axis=1)


def _prompt_kernel(x_ref, gpre_ref, wmain_ref, wt_ref, bg_ref, sink_ref, gmhb_ref, wout_ref, gpost_ref,
                   y_ref, wk_ref, wv_ref, c_ref, n_ref, m_ref,
                   proj_ref, projt_ref, cat_ref, kprev_ref, vtprev_ref, ct_ref, mst_ref, bias_ref, *, tb, nt):
    bi = pl.program_id(0)
    j = pl.program_id(1)
    nchunks = tb // ROWS

    @pl.when((bi == 0) & (j == 0))
    def _():
        for first in range(2):
            for kv in range(ATT_KV):
                bias_ref[first * ATT_KV + kv] = _prompt_bias_t(kv, first)

    @pl.when(j == 0)
    def _():
        kprev_ref[...] = jnp.zeros_like(kprev_ref)
        vtprev_ref[...] = jnp.zeros_like(vtprev_ref)
        ct_ref[...] = jnp.zeros_like(ct_ref)
        mst_ref[...] = jnp.zeros_like(mst_ref)

    x = x_ref[0]
    xn = _rms(x, gpre_ref[...]).astype(BF16)
    proj_ref[...] = jnp.dot(xn, wmain_ref[...], preferred_element_type=F32)
    pt = lax.dot_general(wt_ref[...], xn, NT_DIMS, preferred_element_type=F32)
    for c in range(nchunks):
        projt_ref[c] = pt[:, c * ROWS:(c + 1) * ROWS]

    ri = lax.broadcasted_iota(jnp.int32, (ROWS, ROWS), 0)
    ci = lax.broadcasted_iota(jnp.int32, (ROWS, ROWS), 1)
    mask_t = ri <= ci
    st_row = lax.broadcasted_iota(jnp.int32, (STATE_ROWS - M_DIM, ROWS), 0)

    def chunk(c, carry):
        r0 = pl.multiple_of(c * ROWS, ROWS)
        rows = pl.ds(r0, ROWS)
        first = ((j == 0) & (c == 0)).astype(jnp.int32)

        qa = proj_ref[rows, P_QA:P_QA + 512] * ATT_SCALE
        kcur = proj_ref[rows, P_KA:P_KA + 128].astype(BF16)
        vtcur = projt_ref[c, T_VA:T_VA + 128, :].astype(BF16)
        kcat = jnp.concatenate([kprev_ref[...], kcur], axis=0)
        vtcat = jnp.concatenate([vtprev_ref[...], vtcur], axis=1)
        att = []
        for kv in range(ATT_KV):
            want_hi = kv == 1
            keep = (ci >= ATT_DIM) if want_hi else (ci < ATT_DIM)
            pieces = []
            for g in range(ATT_GROUP):
                hh = kv * ATT_GROUP + g
                blk = qa[:, (hh // 2) * 128:(hh // 2 + 1) * 128]
                if (hh % 2 == 1) != want_hi:
                    blk = pltpu.roll(blk, ATT_DIM, 1)
                pieces.append(jnp.where(keep, blk, 0.0))
            q4 = jnp.concatenate(pieces, axis=0)
            s = _bdot_nt(kcat, q4) + bias_ref[first * ATT_KV + kv]
            sink = jnp.concatenate(
                [jnp.broadcast_to(sink_ref[0:1, kv * ATT_GROUP + g:kv * ATT_GROUP + g + 1], (1, ROWS))
                 for g in range(ATT_GROUP)], axis=1)
            mx = jnp.maximum(jnp.max(s, axis=0, keepdims=True), sink)
            p = jnp.exp(s - mx).astype(BF16)
            lhs = jnp.concatenate([vtcat[kv * ATT_DIM:(kv + 1) * ATT_DIM, :],
                                   jnp.ones((16, 2 * ROWS), BF16)], axis=0)
            o = jnp.dot(lhs, p, preferred_element_type=F32)
            on = o[0:ATT_DIM, :] / (o[ATT_DIM:ATT_DIM + 1, :] + jnp.exp(sink - mx))
            for pair in range(2):
                two = jnp.concatenate([on[:, (2 * pair) * ROWS:(2 * pair + 1) * ROWS],
                                       on[:, (2 * pair + 1) * ROWS:(2 * pair + 2) * ROWS]], axis=0)
                att.append(two.T)
        a_out = jnp.concatenate(att, axis=1) * _silu(proj_ref[rows, P_ZA:P_ZA + 512])
        kprev_ref[...] = kcur
        vtprev_ref[...] = vtcur

        r = _gate_rows(projt_ref[c, T_GATES:T_GATES + 8, :] + bg_ref[...], mst_ref[...], ROWS)
        a_cols = jnp.concatenate([r["a"], jnp.zeros((ROWS - 8, ROWS), F32)], axis=0).T
        m_out = []
        for h in range(M_HEADS):
            q = proj_ref[rows, P_QM + h * M_DIM:P_QM + (h + 1) * M_DIM].astype(BF16)
            k = (proj_ref[rows, P_KM + h * M_DIM:P_KM + (h + 1) * M_DIM] * K_SCALE).astype(BF16)
            vt = projt_ref[c, T_VM + h * M_DIM:T_VM + (h + 1) * M_DIM, :]
            ct = ct_ref[h]
            r1 = lax.dot_general(jnp.concatenate([k, ct.astype(BF16)], axis=0), q, NT_DIMS,
                                 preferred_element_type=F32)
            dm = jnp.exp(jnp.where(mask_t, a_cols[:, h:h + 1] + r["bm"][h:h + 1, :], -jnp.inf))
            st = r1[0:ROWS] * dm
            g_row = r["gexp"][h:h + 1, :]
            num = _bdot(vt, st) + g_row * r1[ROWS:ROWS + M_DIM]
            den = jnp.sum(st, axis=0, keepdims=True) + g_row * r1[ROWS + M_DIM:ROWS + M_DIM + 1]
            ht = num / jnp.maximum(jnp.abs(den), r["enm"][h:h + 1, :])
            hn = ht * lax.rsqrt(jnp.mean(ht * ht, axis=0, keepdims=True) + NORM_EPS) * gmhb_ref[h]
            m_out.append(jax.nn.sigmoid(proj_ref[rows, P_OM + h * M_DIM:P_OM + (h + 1) * M_DIM]) * hn.T
                         * _silu(proj_ref[rows, P_ZM + h * M_DIM:P_ZM + (h + 1) * M_DIM]))
            w_row = r["w"][h:h + 1, :]
            lhs2 = jnp.concatenate([vt * w_row, jnp.where(st_row == 0, w_row, 0.0)], axis=0)
            ct_ref[h] = r["decay"][h:h + 1, 0:1] * ct + _bdot(lhs2, k)
        mst_ref[...] = r["m_new"]

        cat_ref[rows, :] = jnp.concatenate([a_out] + m_out, axis=1).astype(BF16)
        return carry

    lax.fori_loop(0, nchunks, chunk, 0)

    y_ref[0] = _out_tail(cat_ref[...], x, wout_ref, gpost_ref)

    @pl.when(j == nt - 1)
    def _():
        wk_ref[0] = proj_ref[tb - ROWS:tb, P_KA:P_KA + 128]
        wv_ref[0] = projt_ref[nchunks - 1, T_VA:T_VA + 128, :].T
        for h in range(M_HEADS):
            ct = ct_ref[h]
            c_ref[0, h] = ct[0:M_DIM].T
            n_ref[0, h:h + 1, :] = ct[M_DIM:M_DIM + 1]
        m_ref[0] = mst_ref[...]


def _prompt_call(x, gpre, wmain, wt, bg, sinks, gmhb, wout, gpost, tb):
    bsz, seq, _ = x.shape
    nt = seq // tb
    full = lambda shape: pl.BlockSpec(shape, lambda b, j: (0,) * len(shape))
    out_shapes = (
        jax.ShapeDtypeStruct((bsz, seq, D_MODEL), F32),
        jax.ShapeDtypeStruct((bsz, ROWS, 128), F32),
        jax.ShapeDtypeStruct((bsz, ROWS, 128), F32),
        jax.ShapeDtypeStruct((bsz, M_HEADS, M_DIM, M_DIM), F32),
        jax.ShapeDtypeStruct((bsz, M_HEADS, M_DIM), F32),
        jax.ShapeDtypeStruct((bsz, 8, ROWS), F32),
    )
    return pl.pallas_call(
        functools.partial(_prompt_kernel, tb=tb, nt=nt),
        grid=(bsz, nt),
        in_specs=[
            pl.BlockSpec((1, tb, D_MODEL), lambda b, j: (b, j, 0)),
            full((1, D_MODEL)), full((D_MODEL, P_MAIN)), full((T_ROWS, D_MODEL)), full((8, 1)),
            full((1, ATT_HEADS)), full((M_HEADS, M_DIM, ROWS)), full((D_MODEL, D_MODEL)), full((1, D_MODEL)),
        ],
        out_specs=(
            pl.BlockSpec((1, tb, D_MODEL), lambda b, j: (b, j, 0)),
            pl.BlockSpec((1, ROWS, 128), lambda b, j: (b, 0, 0)),
            pl.BlockSpec((1, ROWS, 128), lambda b, j: (b, 0, 0)),
            pl.BlockSpec((1, M_HEADS, M_DIM, M_DIM), lambda b, j: (b, 0, 0, 0)),
            pl.BlockSpec((1, M_HEADS, M_DIM), lambda b, j: (b, 0, 0)),
            pl.BlockSpec((1, 8, ROWS), lambda b, j: (b, 0, 0)),
        ),
        out_shape=out_shapes,
        scratch_shapes=[
            pltpu.VMEM((tb, P_MAIN), F32),
            pltpu.VMEM((tb // ROWS, T_ROWS, ROWS), F32),
            pltpu.VMEM((tb, D_MODEL), BF16),
            pltpu.VMEM((ROWS, 128), BF16),
            pltpu.VMEM((128, ROWS), BF16),
            pltpu.VMEM((M_HEADS, STATE_ROWS, M_DIM), F32),
            pltpu.VMEM((8, ROWS), F32),
            pltpu.VMEM((2 * ATT_KV, 2 * ROWS, ATT_GROUP * ROWS), F32),
        ],
        compiler_params=pltpu.CompilerParams(
            dimension_semantics=("arbitrary", "arbitrary"), vmem_limit_bytes=VMEM_LIMIT_BYTES),
        name="prompt_layer",
    )(x, gpre, wmain, wt, bg, sinks, gmhb, wout, gpost)


def _sample_bias_new(head):
    r = lax.broadcasted_iota(jnp.int32, (ROWS, ROWS), 0)
    c = lax.broadcasted_iota(jnp.int32, (ROWS, ROWS), 1)
    valid = ((r >> 3) == (c >> 3)) & (r >= c)
    return jnp.where(valid, -_slope(head) * (r - c).astype(F32), NEG_BIG)


def _sample_bias_cache(head):
    r = lax.broadcasted_iota(jnp.int32, (ROWS, ROWS), 0)
    c = lax.broadcasted_iota(jnp.int32, (ROWS, ROWS), 1)
    diff = (r & (SAMPLE_SEQ - 1)) + ROWS - c
    return jnp.where(diff < ROWS, -_slope(head) * diff.astype(F32), NEG_BIG)


def _sample_kernel(x_ref, kc_ref, vc_ref, cin_ref, nin_ref, m0_ref,
                   gpre_ref, win_ref, bg_ref, sink_ref, gmh_ref, wout_ref, gpost_ref,
                   y_ref, ko_ref, vo_ref, cout_ref, nout_ref, mout_ref,
                   proj_ref, sc_ref, oc_ref, qc_ref, cols_ref, wv_ref, kt_ref, bn_ref, bc_ref):
    @pl.when(pl.program_id(0) == 0)
    def _():
        for hh in range(ATT_HEADS):
            bn_ref[hh] = _sample_bias_new(hh)
            bc_ref[hh] = _sample_bias_cache(hh)

    x = x_ref[...]
    xn = _rms(x, gpre_ref[...])
    proj_ref[...] = jnp.dot(xn.astype(BF16), win_ref[...], preferred_element_type=F32)

    ri = lax.broadcasted_iota(jnp.int32, (ROWS, ROWS), 0)
    ci = lax.broadcasted_iota(jnp.int32, (ROWS, ROWS), 1)
    same_seq = (ri >> 3) == (ci >> 3)
    mask = same_seq & (ri >= ci)

    def seq_scores(b, carry):
        rows = pl.ds(pl.multiple_of(b * SAMPLE_SEQ, SAMPLE_SEQ), SAMPLE_SEQ)
        kc = kc_ref[b]
        for kv in range(ATT_KV):
            kk = kc[:, kv * ATT_DIM:(kv + 1) * ATT_DIM]
            for g in range(ATT_GROUP):
                hh = kv * ATT_GROUP + g
                sc_ref[hh, rows, :] = _bdot_nt(proj_ref[rows, QA + hh * ATT_DIM:QA + (hh + 1) * ATT_DIM], kk)
        for h in range(M_HEADS):
            qc_ref[rows, h * M_DIM:(h + 1) * M_DIM] = _bdot(
                proj_ref[rows, QM + h * M_DIM:QM + (h + 1) * M_DIM], cin_ref[b, h])
        return carry

    lax.fori_loop(0, SAMPLE_GROUP, seq_scores, 0)

    ka = proj_ref[:, KA:KA + 128]
    va = proj_ref[:, VA:VA + 128]
    o_new, dens = [], []
    for hh in range(ATT_HEADS):
        kv = hh // ATT_GROUP
        s_n = _bdot_nt(proj_ref[:, QA + hh * ATT_DIM:QA + (hh + 1) * ATT_DIM],
                       ka[:, kv * ATT_DIM:(kv + 1) * ATT_DIM]) * ATT_SCALE + bn_ref[hh]
        s_c = sc_ref[hh] * ATT_SCALE + bc_ref[hh]
        sink = sink_ref[0:1, hh:hh + 1]
        mx = jnp.maximum(jnp.maximum(jnp.max(s_n, axis=-1, keepdims=True),
                                     jnp.max(s_c, axis=-1, keepdims=True)), sink)
        p_n = jnp.exp(s_n - mx)
        p_c = jnp.exp(s_c - mx)
        dens.append(jnp.sum(p_n, axis=-1, keepdims=True) + jnp.sum(p_c, axis=-1, keepdims=True)
                    + jnp.exp(sink - mx))
        o_new.append(_bdot(p_n, va[:, kv * ATT_DIM:(kv + 1) * ATT_DIM]))
        sc_ref[hh] = p_c

    a, m_new, cols = _gate_math(proj_ref[:, GATES:GATES + ROWS], bg_ref[...], m0_ref[0], SAMPLE_SEQ)
    cols_ref[...] = cols
    mout_ref[0] = m_new
    first_row = (ri & (SAMPLE_SEQ - 1)) == 0
    m_out = []
    for h in range(M_HEADS):
        q = proj_ref[:, QM + h * M_DIM:QM + (h + 1) * M_DIM]
        k = proj_ref[:, KM + h * M_DIM:KM + (h + 1) * M_DIM] * K_SCALE
        v = proj_ref[:, VM + h * M_DIM:VM + (h + 1) * M_DIM]
        n_rep = jnp.broadcast_to(nin_ref[h][:, None, :], (SAMPLE_GROUP, SAMPLE_SEQ, M_DIM)).reshape(ROWS, M_DIM)
        q_n = jnp.sum(q * n_rep, axis=-1, keepdims=True)
        h_t = _mlstm_head(q, k, v, a[h:h + 1, :], cols, h, mask, qc_ref[:, h * M_DIM:(h + 1) * M_DIM], q_n)
        m_out.append(_head_out(h_t, gmh_ref[h:h + 1, :],
                               proj_ref[:, OM + h * M_DIM:OM + (h + 1) * M_DIM],
                               proj_ref[:, ZM + h * M_DIM:ZM + (h + 1) * M_DIM]))
        w_col = cols[:, 12 + h:13 + h]
        dec_col = cols[:, 16 + h:17 + h]
        n_terms = jnp.where(first_row, dec_col * n_rep, 0.0) + w_col * k
        nout_ref[h] = jnp.sum(n_terms.reshape(SAMPLE_GROUP, SAMPLE_SEQ, M_DIM), axis=1)
        wv_ref[h] = w_col * v
        kt_ref[h] = k.T.astype(BF16)

    def seq_update(b, carry):
        rows = pl.ds(pl.multiple_of(b * SAMPLE_SEQ, SAMPLE_SEQ), SAMPLE_SEQ)
        vc = vc_ref[b]
        for kv in range(ATT_KV):
            vv = vc[:, kv * ATT_DIM:(kv + 1) * ATT_DIM]
            for g in range(ATT_GROUP):
                hh = kv * ATT_GROUP + g
                oc_ref[rows, hh * ATT_DIM:(hh + 1) * ATT_DIM] = _bdot(sc_ref[hh, rows, :], vv)
        in_seq = (ri >> 3) == b
        for h in range(M_HEADS):
            dec = cols_ref[rows, 16 + h:17 + h][0:1, :]
            upd = jnp.dot(kt_ref[h], jnp.where(in_seq, wv_ref[h], 0.0).astype(BF16), preferred_element_type=F32)
            cout_ref[b, h] = dec * cin_ref[b, h] + upd
        return carry

    lax.fori_loop(0, SAMPLE_GROUP, seq_update, 0)

    att = [(o_new[hh] + oc_ref[:, hh * ATT_DIM:(hh + 1) * ATT_DIM]) / dens[hh] for hh in range(ATT_HEADS)]
    a_out = jnp.concatenate(att, axis=1) * _silu(proj_ref[:, ZA:ZA + 512])
    cat = jnp.concatenate([a_out] + m_out, axis=1).astype(BF16)
    y_ref[...] = _out_tail(cat, x, wout_ref, gpost_ref)

    keep = ROWS - SAMPLE_SEQ
    ko_ref[:, 0:keep, :] = kc_ref[:, SAMPLE_SEQ:ROWS, :]
    ko_ref[:, keep:ROWS, :] = ka.reshape(SAMPLE_GROUP, SAMPLE_SEQ, 128)
    vo_ref[:, 0:keep, :] = vc_ref[:, SAMPLE_SEQ:ROWS, :]
    vo_ref[:, keep:ROWS, :] = va.reshape(SAMPLE_GROUP, SAMPLE_SEQ, 128)


def _sample_call(x, kc, vc, cin, nin, m0, gpre, win, bg, sinks, gmh, wout, gpost):
    nrows = x.shape[0]
    ngroups = nrows // ROWS
    nseq = ngroups * SAMPLE_GROUP
    full = lambda shape: pl.BlockSpec(shape, lambda i: (0,) * len(shape))
    grp = SAMPLE_GROUP
    out_shapes = (
        jax.ShapeDtypeStruct((nrows, D_MODEL), F32),
        jax.ShapeDtypeStruct((nseq, ROWS, 128), F32),
        jax.ShapeDtypeStruct((nseq, ROWS, 128), F32),
        jax.ShapeDtypeStruct((nseq, M_HEADS, M_DIM, M_DIM), F32),
        jax.ShapeDtypeStruct((M_HEADS, nseq, M_DIM), F32),
        jax.ShapeDtypeStruct((ngroups, 8, ROWS), F32),
    )
    return pl.pallas_call(
        _sample_kernel,
        grid=(ngroups,),
        in_specs=[
            pl.BlockSpec((ROWS, D_MODEL), lambda i: (i, 0)),
            pl.BlockSpec((grp, ROWS, 128), lambda i: (i, 0, 0)),
            pl.BlockSpec((grp, ROWS, 128), lambda i: (i, 0, 0)),
            pl.BlockSpec((grp, M_HEADS, M_DIM, M_DIM), lambda i: (i, 0, 0, 0)),
            pl.BlockSpec((M_HEADS, grp, M_DIM), lambda i: (0, i, 0)),
            pl.BlockSpec((1, 8, ROWS), lambda i: (i, 0, 0)),
            full((1, D_MODEL)), full((D_MODEL, D_IN_PAD)), full((8, 1)), full((1, ATT_HEADS)),
            full((M_HEADS, M_DIM)), full((D_MODEL, D_MODEL)), full((1, D_MODEL)),
        ],
        out_specs=(
            pl.BlockSpec((ROWS, D_MODEL), lambda i: (i, 0)),
            pl.BlockSpec((grp, ROWS, 128), lambda i: (i, 0, 0)),
            pl.BlockSpec((grp, ROWS, 128), lambda i: (i, 0, 0)),
            pl.BlockSpec((grp, M_HEADS, M_DIM, M_DIM), lambda i: (i, 0, 0, 0)),
            pl.BlockSpec((M_HEADS, grp, M_DIM), lambda i: (0, i, 0)),
            pl.BlockSpec((1, 8, ROWS), lambda i: (i, 0, 0)),
        ),
        out_shape=out_shapes,
        scratch_shapes=[
            pltpu.VMEM((ROWS, D_IN_PAD), F32),
            pltpu.VMEM((ATT_HEADS, ROWS, ROWS), F32),
            pltpu.VMEM((ROWS, ATT_HEADS * ATT_DIM), F32),
            pltpu.VMEM((ROWS, M_HEADS * M_DIM), F32),
            pltpu.VMEM((ROWS, ROWS), F32),
            pltpu.VMEM((M_HEADS, ROWS, M_DIM), F32),
            pltpu.VMEM((M_HEADS, M_DIM, ROWS), BF16),
            pltpu.VMEM((ATT_HEADS, ROWS, ROWS), F32),
            pltpu.VMEM((ATT_HEADS, ROWS, ROWS), F32),
        ],
        compiler_params=pltpu.CompilerParams(
            dimension_semantics=("arbitrary",), vmem_limit_bytes=VMEM_LIMIT_BYTES),
        name="sample_layer",
    )(x, kc, vc, cin, nin, m0, gpre, win, bg, sinks, gmh, wout, gpost)


PROMPT_BLOCK = 512


def kernel(x_prompt, x_sample, cache_win_k, cache_win_v, state_C, state_n, state_m,
           g_pre, w_in, b_gate, attn_sinks, g_mh, w_out, g_post):
    depth = g_pre.shape[0]
    assert depth == 1, "single-layer trunk"
    bsz = x_prompt.shape[0]
    nseq, sseq, _ = x_sample.shape
    assert sseq == SAMPLE_SEQ and nseq % SAMPLE_GROUP == 0

    gpre = g_pre[0].reshape(1, D_MODEL)
    gpost = g_post[0].reshape(1, D_MODEL)
    w = w_in[0]
    win = jnp.pad(w, ((0, 0), (0, D_IN_PAD - D_IN))).astype(BF16)
    wmain = jnp.concatenate([w[:, QA:VA], w[:, ZA:VM], w[:, OM:GATES]], axis=1).astype(BF16)
    wt = jnp.concatenate([w[:, VA:ZA], w[:, VM:OM], w[:, GATES:D_IN]], axis=1).T
    wt = jnp.pad(wt, ((0, T_ROWS - wt.shape[0]), (0, 0))).astype(BF16)
    wout = w_out[0].astype(BF16)
    bg = b_gate[0].reshape(2 * M_HEADS, 1)
    sinks = attn_sinks[0].reshape(1, ATT_HEADS)
    gmh = g_mh[0]
    gmhb = jnp.broadcast_to(gmh[:, :, None], (M_HEADS, M_DIM, ROWS))

    yp, wkp, wvp, cp, np_, mp = _prompt_call(x_prompt, gpre, wmain, wt, bg, sinks, gmhb, wout, gpost, PROMPT_BLOCK)

    ngroups = nseq // SAMPLE_GROUP
    m0 = jnp.broadcast_to(state_m[0].reshape(ngroups, SAMPLE_GROUP, 1, M_HEADS),
                          (ngroups, SAMPLE_GROUP, SAMPLE_SEQ, M_HEADS))
    m0 = jnp.pad(m0.reshape(ngroups, ROWS, M_HEADS).transpose(0, 2, 1), ((0, 0), (0, 8 - M_HEADS), (0, 0)))
    ys, wks, wvs, cs, ns, ms = _sample_call(
        x_sample.reshape(nseq * sseq, D_MODEL),
        cache_win_k[0].reshape(nseq, ROWS, 128), cache_win_v[0].reshape(nseq, ROWS, 128),
        state_C[0], state_n[0].transpose(1, 0, 2), m0,
        gpre, win, bg, sinks, gmh, wout, gpost)

    kv_shape = (ATT_KV, ATT_DIM)
    ms = ms[:, 0:M_HEADS, ::SAMPLE_SEQ].transpose(0, 2, 1).reshape(nseq, M_HEADS)
    return (yp, ys.reshape(nseq, sseq, D_MODEL),
            wkp.reshape((1, bsz, ROWS) + kv_shape), wvp.reshape((1, bsz, ROWS) + kv_shape),
            cp[None], np_[None], mp[:, 0:M_HEADS, 0][None],
            wks.reshape((1, nseq, ROWS) + kv_shape), wvs.reshape((1, nseq, ROWS) + kv_shape),
            cs[None], ns.transpose(1, 0, 2)[None], ms[None])
```

```python
import functools

import jax
import jax.numpy as jnp
from jax import lax
from jax.experimental import pallas as pl
from jax.experimental.pallas import tpu as pltpu

F32 = jnp.float32
BF16 = jnp.bfloat16

D_MODEL = 1024
ROWS = 128
ATT_HEADS, ATT_KV, ATT_GROUP, ATT_DIM = 8, 2, 4, 64
M_HEADS, M_DIM = 4, 128
NORM_EPS = 1e-6
NEG_BIG = -1e30
ATT_SCALE = ATT_DIM ** -0.5
K_SCALE = M_DIM ** -0.5

QA, KA, VA, ZA = 0, 512, 640, 768
QM, KM, VM, OM, ZM = 1280, 1792, 2304, 2816, 3328
GATES = 3840
D_IN = 3848
D_IN_PAD = 3968

P_QA, P_KA, P_ZA, P_QM, P_KM, P_OM, P_ZM = 0, 512, 640, 1152, 1664, 2176, 2688
P_MAIN = 3200
T_VA, T_VM, T_GATES = 0, 128, 640
T_ROWS = 656
STATE_ROWS = 144

SAMPLE_SEQ = 8
SAMPLE_GROUP = ROWS // SAMPLE_SEQ

VMEM_LIMIT_BYTES = 56 * 1024 * 1024
NT_DIMS = (((1,), (1,)), ((), ()))


def _rms(x, g):
    return x * lax.rsqrt(jnp.mean(x * x, axis=-1, keepdims=True) + NORM_EPS) * g


def _silu(x):
    return x * jax.nn.sigmoid(x)


def _log_sigmoid(x):
    return -(jnp.maximum(-x, 0.0) + jnp.log1p(jnp.exp(-jnp.abs(x))))


def _slope(head):
    return 2.0 ** -(head + 1)


def _bdot(a, b):
    return jnp.dot(a.astype(BF16), b.astype(BF16), preferred_element_type=F32)


def _bdot_nt(a, b):
    return lax.dot_general(a.astype(BF16), b.astype(BF16), NT_DIMS, preferred_element_type=F32)


def _seg_scan(x, pos, shifts, op, ident):
    for sh in shifts:
        x = op(x, jnp.where(pos >= sh, pltpu.roll(x, sh, 1), ident))
    return x


def _seg_last(x, pos, shifts, seg):
    y = jnp.where(pos == seg - 1, x, -jnp.inf)
    for sh in shifts:
        y = jnp.maximum(y, jnp.where(pos + sh <= seg - 1, pltpu.roll(y, ROWS - sh, 1), -jnp.inf))
    return y


def _gate_rows(x, m0, seg):
    shifts = tuple(1 << i for i in range(seg.bit_length() - 1))
    row = lax.broadcasted_iota(jnp.int32, (8, ROWS), 0)
    lane = lax.broadcasted_iota(jnp.int32, (8, ROWS), 1)
    pos = lane & (seg - 1)
    head_rows = row < M_HEADS
    ic = jnp.where(head_rows, x, 0.0)
    fc = jnp.where(head_rows, _log_sigmoid(pltpu.roll(x, M_HEADS, 0)), 0.0)
    b = _seg_scan(fc, pos, shifts, jnp.add, 0.0)
    a = ic - b
    m_t = jnp.maximum(b + m0, b + _seg_scan(a, pos, shifts, jnp.maximum, -jnp.inf))
    b_last = _seg_last(b, pos, shifts, seg)
    m_new = _seg_last(m_t, pos, shifts, seg)
    return dict(a=a, bm=b - m_t, gexp=jnp.exp(b + m0 - m_t), enm=jnp.exp(-m_t), m_new=m_new,
                w=jnp.exp(b_last - b + ic - m_new), decay=jnp.exp(b_last + m0 - m_new), head_rows=head_rows)


def _gate_rows_chunk(x, m0, mask_t):
    row = lax.broadcasted_iota(jnp.int32, (8, ROWS), 0)
    head_rows = row < M_HEADS
    ic = jnp.where(head_rows, x, 0.0)
    fc = jnp.where(head_rows, _log_sigmoid(pltpu.roll(x, M_HEADS, 0)), 0.0)
    hi = fc.astype(BF16).astype(F32)
    mid = (fc - hi).astype(BF16).astype(F32)
    lo = (fc - hi - mid).astype(BF16).astype(F32)
    parts = jnp.dot(jnp.concatenate([hi, mid, lo, jnp.zeros_like(hi)], axis=0).astype(BF16),
                    mask_t.astype(BF16), preferred_element_type=F32)
    b = parts[0:8] + parts[8:16] + parts[16:24]
    a = ic - b
    a_cols = jnp.concatenate([a, jnp.zeros((ROWS - 8, ROWS), F32)], axis=0).T
    a_masked = [jnp.where(mask_t, a_cols[:, h:h + 1], -jnp.inf) for h in range(M_HEADS)]
    cm = jnp.concatenate([jnp.max(am, axis=0, keepdims=True) for am in a_masked]
                         + [jnp.zeros((8 - M_HEADS, ROWS), F32)], axis=0)
    m_t = jnp.maximum(b + m0, b + cm)
    b_last = jnp.broadcast_to(b[:, ROWS - 1:ROWS], b.shape)
    m_new = jnp.broadcast_to(m_t[:, ROWS - 1:ROWS], b.shape)
    return dict(a_masked=a_masked, bm=b - m_t, gexp=jnp.exp(b + m0 - m_t), enm=jnp.exp(-m_t), m_new=m_new,
                w=jnp.exp(b_last - b + ic - m_new), decay=jnp.exp(b_last + m0 - m_new))


def _gate_math(gates_rows, bg_col, m0, seg):
    r = _gate_rows(gates_rows.T[0:8, :] + bg_col, m0, seg)
    head_rows = r["head_rows"]
    p1 = jnp.where(head_rows, r["bm"], pltpu.roll(r["gexp"], M_HEADS, 0))
    p2 = jnp.where(head_rows, r["enm"], pltpu.roll(r["w"], M_HEADS, 0))
    p3 = jnp.where(head_rows, r["decay"], 0.0)
    cols = jnp.concatenate([p1, p2, p3, jnp.zeros((ROWS - 24, ROWS), F32)], axis=0).T
    return r["a"], r["m_new"], cols


def _mlstm_head(q, k, v, a_row, cols, h, mask, q_c, q_n):
    bm_col = cols[:, h:h + 1]
    g_col = cols[:, 4 + h:5 + h]
    enm_col = cols[:, 8 + h:9 + h]
    dm = jnp.exp(jnp.where(mask, bm_col + a_row, -jnp.inf))
    s = _bdot_nt(q, k) * dm
    num = _bdot(s, v) + g_col * q_c
    den = jnp.sum(s, axis=-1, keepdims=True) + g_col * q_n
    return num / jnp.maximum(jnp.abs(den), enm_col)


def _head_out(h_t, g_row, om, zm):
    return jax.nn.sigmoid(om) * _rms(h_t, g_row) * _silu(zm)


def _out_tail(cat, x, wout_ref, gpost_ref):
    y = jnp.dot(cat, wout_ref[...], preferred_element_type=F32)
    return x + _rms(y, gpost_ref[...])


def _prompt_bias_t(kv, first):
    j = lax.broadcasted_iota(jnp.int32, (2 * ROWS, ROWS), 0)
    i = lax.broadcasted_iota(jnp.int32, (2 * ROWS, ROWS), 1)
    diff = ROWS + i - j
    valid = (diff >= 0) & (diff < ROWS)
    if first:
        valid = valid & (j >= ROWS)
    dfl = diff.astype(F32)
    return jnp.concatenate(
        [jnp.where(valid, -_slope(kv * ATT_GROUP + g) * dfl, NEG_BIG) for g in range(ATT_GROUP)], axis=1)


def _prompt_kernel(x_ref, gpre_ref, wmain_ref, wt_ref, bg_ref, sink_ref, gmhb_ref, wout_ref, gpost_ref,
                   y_ref, wk_ref, wv_ref, c_ref, n_ref, m_ref,
                   proj_ref, projt_ref, cat_ref, kprev_ref, vtprev_ref, ct_ref, mst_ref, bias_ref, *, tb, nt):
    bi = pl.program_id(0)
    j = pl.program_id(1)
    nchunks = tb // ROWS

    @pl.when((bi == 0) & (j == 0))
    def _():
        for first in range(2):
            for kv in range(ATT_KV):
                bias_ref[first * ATT_KV + kv] = _prompt_bias_t(kv, first)

    @pl.when(j == 0)
    def _():
        kprev_ref[...] = jnp.zeros_like(kprev_ref)
        vtprev_ref[...] = jnp.zeros_like(vtprev_ref)
        ct_ref[...] = jnp.zeros_like(ct_ref)
        mst_ref[...] = jnp.zeros_like(mst_ref)

    x = x_ref[0]
    xn = _rms(x, gpre_ref[...]).astype(BF16)
    proj_ref[...] = jnp.dot(xn, wmain_ref[...], preferred_element_type=F32)
    pt = lax.dot_general(wt_ref[...], xn, NT_DIMS, preferred_element_type=F32)
    for c in range(nchunks):
        projt_ref[c] = pt[:, c * ROWS:(c + 1) * ROWS]

    ri = lax.broadcasted_iota(jnp.int32, (ROWS, ROWS), 0)
    ci = lax.broadcasted_iota(jnp.int32, (ROWS, ROWS), 1)
    mask_t = ri <= ci
    st_row = lax.broadcasted_iota(jnp.int32, (STATE_ROWS - M_DIM, ROWS), 0)

    def chunk(c, carry):
        r0 = pl.multiple_of(c * ROWS, ROWS)
        rows = pl.ds(r0, ROWS)
        first = ((j == 0) & (c == 0)).astype(jnp.int32)

        qa = proj_ref[rows, P_QA:P_QA + 512] * ATT_SCALE
        kcur = proj_ref[rows, P_KA:P_KA + 128].astype(BF16)
        vtcur = projt_ref[c, T_VA:T_VA + 128, :].astype(BF16)
        kcat = jnp.concatenate([kprev_ref[...], kcur], axis=0)
        vtcat = jnp.concatenate([vtprev_ref[...], vtcur], axis=1)
        r = _gate_rows_chunk(projt_ref[c, T_GATES:T_GATES + 8, :] + bg_ref[...], mst_ref[...], mask_t)

        scores, sinks = [], []
        for kv in range(ATT_KV):
            want_hi = kv == 1
            keep = (ci >= ATT_DIM) if want_hi else (ci < ATT_DIM)
            pieces = []
            for g in range(ATT_GROUP):
                hh = kv * ATT_GROUP + g
                blk = qa[:, (hh // 2) * 128:(hh // 2 + 1) * 128]
                if (hh % 2 == 1) != want_hi:
                    blk = pltpu.roll(blk, ATT_DIM, 1)
                pieces.append(jnp.where(keep, blk, 0.0))
            q4 = jnp.concatenate(pieces, axis=0)
            scores.append(_bdot_nt(kcat, q4) + bias_ref[first * ATT_KV + kv])
            sinks.append(jnp.concatenate(
                [jnp.broadcast_to(sink_ref[0:1, kv * ATT_GROUP + g:kv * ATT_GROUP + g + 1], (1, ROWS))
                 for g in range(ATT_GROUP)], axis=1))
        ks, vts, r1s = [], [], []
        for h in range(M_HEADS):
            q = proj_ref[rows, P_QM + h * M_DIM:P_QM + (h + 1) * M_DIM].astype(BF16)
            k = (proj_ref[rows, P_KM + h * M_DIM:P_KM + (h + 1) * M_DIM] * K_SCALE).astype(BF16)
            vt = projt_ref[c, T_VM + h * M_DIM:T_VM + (h + 1) * M_DIM, :]
            ct = ct_ref[h]
            r1s.append(lax.dot_general(jnp.concatenate([k, ct.astype(BF16)], axis=0), q, NT_DIMS,
                                       preferred_element_type=F32))
            w_row = r["w"][h:h + 1, :]
            lhs2 = jnp.concatenate([vt * w_row, jnp.where(st_row == 0, w_row, 0.0)], axis=0)
            ct_ref[h] = r["decay"][h:h + 1, 0:1] * ct + _bdot(lhs2, k)
            ks.append(k)
            vts.append(vt)
        mst_ref[...] = r["m_new"]

        outs, nums, dens = [], [], []
        for kv in range(ATT_KV):
            s, sink = scores[kv], sinks[kv]
            mx = jnp.maximum(jnp.max(s, axis=0, keepdims=True), sink)
            p = jnp.exp(s - mx).astype(BF16)
            lhs = jnp.concatenate([vtcat[kv * ATT_DIM:(kv + 1) * ATT_DIM, :],
                                   jnp.ones((16, 2 * ROWS), BF16)], axis=0)
            o = jnp.dot(lhs, p, preferred_element_type=F32)
            outs.append((o, jnp.exp(sink - mx)))
        for h in range(M_HEADS):
            r1 = r1s[h]
            dm = jnp.exp(r["a_masked"][h] + r["bm"][h:h + 1, :])
            st = r1[0:ROWS] * dm
            g_row = r["gexp"][h:h + 1, :]
            nums.append(_bdot(vts[h], st) + g_row * r1[ROWS:ROWS + M_DIM])
            dens.append(jnp.sum(st, axis=0, keepdims=True) + g_row * r1[ROWS + M_DIM:ROWS + M_DIM + 1])

        att = []
        for kv in range(ATT_KV):
            o, esink = outs[kv]
            on = o[0:ATT_DIM, :] / (o[ATT_DIM:ATT_DIM + 1, :] + esink)
            for pair in range(2):
                two = jnp.concatenate([on[:, (2 * pair) * ROWS:(2 * pair + 1) * ROWS],
                                       on[:, (2 * pair + 1) * ROWS:(2 * pair + 2) * ROWS]], axis=0)
                att.append(two.T)
        a_out = jnp.concatenate(att, axis=1) * _silu(proj_ref[rows, P_ZA:P_ZA + 512])
        kprev_ref[...] = kcur
        vtprev_ref[...] = vtcur
        m_out = []
        for h in range(M_HEADS):
            ht = nums[h] / jnp.maximum(jnp.abs(dens[h]), r["enm"][h:h + 1, :])
            hn = ht * lax.rsqrt(jnp.mean(ht * ht, axis=0, keepdims=True) + NORM_EPS) * gmhb_ref[h]
            m_out.append(jax.nn.sigmoid(proj_ref[rows, P_OM + h * M_DIM:P_OM + (h + 1) * M_DIM]) * hn.T
                         * _silu(proj_ref[rows, P_ZM + h * M_DIM:P_ZM + (h + 1) * M_DIM]))

        cat_ref[rows, :] = jnp.concatenate([a_out] + m_out, axis=1).astype(BF16)
        return carry

    lax.fori_loop(0, nchunks, chunk, 0)

    y_ref[0] = _out_tail(cat_ref[...], x, wout_ref, gpost_ref)

    @pl.when(j == nt - 1)
    def _():
        wk_ref[0] = proj_ref[tb - ROWS:tb, P_KA:P_KA + 128]
        wv_ref[0] = projt_ref[nchunks - 1, T_VA:T_VA + 128, :].T
        for h in range(M_HEADS):
            ct = ct_ref[h]
            c_ref[0, h] = ct[0:M_DIM].T
            n_ref[0, h:h + 1, :] = ct[M_DIM:M_DIM + 1]
        m_ref[0] = mst_ref[...]


def _prompt_call(x, gpre, wmain, wt, bg, sinks, gmhb, wout, gpost, tb):
    bsz, seq, _ = x.shape
    nt = seq // tb
    full = lambda shape: pl.BlockSpec(shape, lambda b, j: (0,) * len(shape))
    out_shapes = (
        jax.ShapeDtypeStruct((bsz, seq, D_MODEL), F32),
        jax.ShapeDtypeStruct((bsz, ROWS, 128), F32),
        jax.ShapeDtypeStruct((bsz, ROWS, 128), F32),
        jax.ShapeDtypeStruct((bsz, M_HEADS, M_DIM, M_DIM), F32),
        jax.ShapeDtypeStruct((bsz, M_HEADS, M_DIM), F32),
        jax.ShapeDtypeStruct((bsz, 8, ROWS), F32),
    )
    return pl.pallas_call(
        functools.partial(_prompt_kernel, tb=tb, nt=nt),
        grid=(bsz, nt),
        in_specs=[
            pl.BlockSpec((1, tb, D_MODEL), lambda b, j: (b, j, 0)),
            full((1, D_MODEL)), full((D_MODEL, P_MAIN)), full((T_ROWS, D_MODEL)), full((8, 1)),
            full((1, ATT_HEADS)), full((M_HEADS, M_DIM, ROWS)), full((D_MODEL, D_MODEL)), full((1, D_MODEL)),
        ],
        out_specs=(
            pl.BlockSpec((1, tb, D_MODEL), lambda b, j: (b, j, 0)),
            pl.BlockSpec((1, ROWS, 128), lambda b, j: (b, 0, 0)),
            pl.BlockSpec((1, ROWS, 128), lambda b, j: (b, 0, 0)),
            pl.BlockSpec((1, M_HEADS, M_DIM, M_DIM), lambda b, j: (b, 0, 0, 0)),
            pl.BlockSpec((1, M_HEADS, M_DIM), lambda b, j: (b, 0, 0)),
            pl.BlockSpec((1, 8, ROWS), lambda b, j: (b, 0, 0)),
        ),
        out_shape=out_shapes,
        scratch_shapes=[
            pltpu.VMEM((tb, P_MAIN), F32),
            pltpu.VMEM((tb // ROWS, T_ROWS, ROWS), F32),
            pltpu.VMEM((tb, D_MODEL), BF16),
            pltpu.VMEM((ROWS, 128), BF16),
            pltpu.VMEM((128, ROWS), BF16),
            pltpu.VMEM((M_HEADS, STATE_ROWS, M_DIM), F32),
            pltpu.VMEM((8, ROWS), F32),
            pltpu.VMEM((2 * ATT_KV, 2 * ROWS, ATT_GROUP * ROWS), F32),
        ],
        compiler_params=pltpu.CompilerParams(
            dimension_semantics=("arbitrary", "arbitrary"), vmem_limit_bytes=VMEM_LIMIT_BYTES),
        name="prompt_layer",
    )(x, gpre, wmain, wt, bg, sinks, gmhb, wout, gpost)


def _sample_bias_new(head):
    r = lax.broadcasted_iota(jnp.int32, (ROWS, ROWS), 0)
    c = lax.broadcasted_iota(jnp.int32, (ROWS, ROWS), 1)
    valid = ((r >> 3) == (c >> 3)) & (r >= c)
    return jnp.where(valid, -_slope(head) * (r - c).astype(F32), NEG_BIG)


def _sample_bias_cache(head):
    r = lax.broadcasted_iota(jnp.int32, (ROWS, ROWS), 0)
    c = lax.broadcasted_iota(jnp.int32, (ROWS, ROWS), 1)
    diff = (r & (SAMPLE_SEQ - 1)) + ROWS - c
    return jnp.where(diff < ROWS, -_slope(head) * diff.astype(F32), NEG_BIG)


def _sample_kernel(x_ref, kc_ref, vc_ref, cin_ref, nin_ref, m0_ref,
                   gpre_ref, win_ref, bg_ref, sink_ref, gmh_ref, wout_ref, gpost_ref,
                   y_ref, ko_ref, vo_ref, cout_ref, nout_ref, mout_ref,
                   proj_ref, sc_ref, oc_ref, qc_ref, cols_ref, wv_ref, kt_ref, bn_ref, bc_ref):
    @pl.when(pl.program_id(0) == 0)
    def _():
        for hh in range(ATT_HEADS):
            bn_ref[hh] = _sample_bias_new(hh)
            bc_ref[hh] = _sample_bias_cache(hh)

    x = x_ref[...]
    xn = _rms(x, gpre_ref[...])
    proj_ref[...] = jnp.dot(xn.astype(BF16), win_ref[...], preferred_element_type=F32)

    ri = lax.broadcasted_iota(jnp.int32, (ROWS, ROWS), 0)
    ci = lax.broadcasted_iota(jnp.int32, (ROWS, ROWS), 1)
    same_seq = (ri >> 3) == (ci >> 3)
    mask = same_seq & (ri >= ci)

    def seq_scores(b, carry):
        rows = pl.ds(pl.multiple_of(b * SAMPLE_SEQ, SAMPLE_SEQ), SAMPLE_SEQ)
        kc = kc_ref[b]
        for kv in range(ATT_KV):
            kk = kc[:, kv * ATT_DIM:(kv + 1) * ATT_DIM]
            for g in range(ATT_GROUP):
                hh = kv * ATT_GROUP + g
                sc_ref[hh, rows, :] = _bdot_nt(proj_ref[rows, QA + hh * ATT_DIM:QA + (hh + 1) * ATT_DIM], kk)
        for h in range(M_HEADS):
            qc_ref[rows, h * M_DIM:(h + 1) * M_DIM] = _bdot(
                proj_ref[rows, QM + h * M_DIM:QM + (h + 1) * M_DIM], cin_ref[b, h])
        return carry

    lax.fori_loop(0, SAMPLE_GROUP, seq_scores, 0)

    ka = proj_ref[:, KA:KA + 128]
    va = proj_ref[:, VA:VA + 128]
    o_new, dens = [], []
    for hh in range(ATT_HEADS):
        kv = hh // ATT_GROUP
        s_n = _bdot_nt(proj_ref[:, QA + hh * ATT_DIM:QA + (hh + 1) * ATT_DIM],
                       ka[:, kv * ATT_DIM:(kv + 1) * ATT_DIM]) * ATT_SCALE + bn_ref[hh]
        s_c = sc_ref[hh] * ATT_SCALE + bc_ref[hh]
        sink = sink_ref[0:1, hh:hh + 1]
        mx = jnp.maximum(jnp.maximum(jnp.max(s_n, axis=-1, keepdims=True),
                                     jnp.max(s_c, axis=-1, keepdims=True)), sink)
        p_n = jnp.exp(s_n - mx)
        p_c = jnp.exp(s_c - mx)
        dens.append(jnp.sum(p_n, axis=-1, keepdims=True) + jnp.sum(p_c, axis=-1, keepdims=True)
                    + jnp.exp(sink - mx))
        o_new.append(_bdot(p_n, va[:, kv * ATT_DIM:(kv + 1) * ATT_DIM]))
        sc_ref[hh] = p_c

    a, m_new, cols = _gate_math(proj_ref[:, GATES:GATES + ROWS], bg_ref[...], m0_ref[0], SAMPLE_SEQ)
    cols_ref[...] = cols
    mout_ref[0] = m_new
    first_row = (ri & (SAMPLE_SEQ - 1)) == 0
    m_out = []
    for h in range(M_HEADS):
        q = proj_ref[:, QM + h * M_DIM:QM + (h + 1) * M_DIM]
        k = proj_ref[:, KM + h * M_DIM:KM + (h + 1) * M_DIM] * K_SCALE
        v = proj_ref[:, VM + h * M_DIM:VM + (h + 1) * M_DIM]
        n_rep = jnp.broadcast_to(nin_ref[h][:, None, :], (SAMPLE_GROUP, SAMPLE_SEQ, M_DIM)).reshape(ROWS, M_DIM)
        q_n = jnp.sum(q * n_rep, axis=-1, keepdims=True)
        h_t = _mlstm_head(q, k, v, a[h:h + 1, :], cols, h, mask, qc_ref[:, h * M_DIM:(h + 1) * M_DIM], q_n)
        m_out.append(_head_out(h_t, gmh_ref[h:h + 1, :],
                               proj_ref[:, OM + h * M_DIM:OM + (h + 1) * M_DIM],
                               proj_ref[:, ZM + h * M_DIM:ZM + (h + 1) * M_DIM]))
        w_col = cols[:, 12 + h:13 + h]
        dec_col = cols[:, 16 + h:17 + h]
        n_terms = jnp.where(first_row, dec_col * n_rep, 0.0) + w_col * k
        nout_ref[h] = jnp.sum(n_terms.reshape(SAMPLE_GROUP, SAMPLE_SEQ, M_DIM), axis=1)
        wv_ref[h] = w_col * v
        kt_ref[h] = k.T.astype(BF16)

    def seq_update(b, carry):
        rows = pl.ds(pl.multiple_of(b * SAMPLE_SEQ, SAMPLE_SEQ), SAMPLE_SEQ)
        vc = vc_ref[b]
        for kv in range(ATT_KV):
            vv = vc[:, kv * ATT_DIM:(kv + 1) * ATT_DIM]
            for g in range(ATT_GROUP):
                hh = kv * ATT_GROUP + g
                oc_ref[rows, hh * ATT_DIM:(hh + 1) * ATT_DIM] = _bdot(sc_ref[hh, rows, :], vv)
        in_seq = (ri >> 3) == b
        for h in range(M_HEADS):
            dec = cols_ref[rows, 16 + h:17 + h][0:1, :]
            upd = jnp.dot(kt_ref[h], jnp.where(in_seq, wv_ref[h], 0.0).astype(BF16), preferred_element_type=F32)
            cout_ref[b, h] = dec * cin_ref[b, h] + upd
        return carry

    lax.fori_loop(0, SAMPLE_GROUP, seq_update, 0)

    att = [(o_new[hh] + oc_ref[:, hh * ATT_DIM:(hh + 1) * ATT_DIM]) / dens[hh] for hh in range(ATT_HEADS)]
    a_out = jnp.concatenate(att, axis=1) * _silu(proj_ref[:, ZA:ZA + 512])
    cat = jnp.concatenate([a_out] + m_out, axis=1).astype(BF16)
    y_ref[...] = _out_tail(cat, x, wout_ref, gpost_ref)

    keep = ROWS - SAMPLE_SEQ
    ko_ref[:, 0:keep, :] = kc_ref[:, SAMPLE_SEQ:ROWS, :]
    ko_ref[:, keep:ROWS, :] = ka.reshape(SAMPLE_GROUP, SAMPLE_SEQ, 128)
    vo_ref[:, 0:keep, :] = vc_ref[:, SAMPLE_SEQ:ROWS, :]
    vo_ref[:, keep:ROWS, :] = va.reshape(SAMPLE_GROUP, SAMPLE_SEQ, 128)


def _sample_call(x, kc, vc, cin, nin, m0, gpre, win, bg, sinks, gmh, wout, gpost):
    nrows = x.shape[0]
    ngroups = nrows // ROWS
    nseq = ngroups * SAMPLE_GROUP
    full = lambda shape: pl.BlockSpec(shape, lambda i: (0,) * len(shape))
    grp = SAMPLE_GROUP
    out_shapes = (
        jax.ShapeDtypeStruct((nrows, D_MODEL), F32),
        jax.ShapeDtypeStruct((nseq, ROWS, 128), F32),
        jax.ShapeDtypeStruct((nseq, ROWS, 128), F32),
        jax.ShapeDtypeStruct((nseq, M_HEADS, M_DIM, M_DIM), F32),
        jax.ShapeDtypeStruct((M_HEADS, nseq, M_DIM), F32),
        jax.ShapeDtypeStruct((ngroups, 8, ROWS), F32),
    )
    return pl.pallas_call(
        _sample_kernel,
        grid=(ngroups,),
        in_specs=[
            pl.BlockSpec((ROWS, D_MODEL), lambda i: (i, 0)),
            pl.BlockSpec((grp, ROWS, 128), lambda i: (i, 0, 0)),
            pl.BlockSpec((grp, ROWS, 128), lambda i: (i, 0, 0)),
            pl.BlockSpec((grp, M_HEADS, M_DIM, M_DIM), lambda i: (i, 0, 0, 0)),
            pl.BlockSpec((M_HEADS, grp, M_DIM), lambda i: (0, i, 0)),
            pl.BlockSpec((1, 8, ROWS), lambda i: (i, 0, 0)),
            full((1, D_MODEL)), full((D_MODEL, D_IN_PAD)), full((8, 1)), full((1, ATT_HEADS)),
            full((M_HEADS, M_DIM)), full((D_MODEL, D_MODEL)), full((1, D_MODEL)),
        ],
        out_specs=(
            pl.BlockSpec((ROWS, D_MODEL), lambda i: (i, 0)),
            pl.BlockSpec((grp, ROWS, 128), lambda i: (i, 0, 0)),
            pl.BlockSpec((grp, ROWS, 128), lambda i: (i, 0, 0)),
            pl.BlockSpec((grp, M_HEADS, M_DIM, M_DIM), lambda i: (i, 0, 0, 0)),
            pl.BlockSpec((M_HEADS, grp, M_DIM), lambda i: (0, i, 0)),
            pl.BlockSpec((1, 8, ROWS), lambda i: (i, 0, 0)),
        ),
        out_shape=out_shapes,
        scratch_shapes=[
            pltpu.VMEM((ROWS, D_IN_PAD), F32),
            pltpu.VMEM((ATT_HEADS, ROWS, ROWS), F32),
            pltpu.VMEM((ROWS, ATT_HEADS * ATT_DIM), F32),
            pltpu.VMEM((ROWS, M_HEADS * M_DIM), F32),
            pltpu.VMEM((ROWS, ROWS), F32),
            pltpu.VMEM((M_HEADS, ROWS, M_DIM), F32),
            pltpu.VMEM((M_HEADS, M_DIM, ROWS), BF16),
            pltpu.VMEM((ATT_HEADS, ROWS, ROWS), F32),
            pltpu.VMEM((ATT_HEADS, ROWS, ROWS), F32),
        ],
        compiler_params=pltpu.CompilerParams(
            dimension_semantics=("arbitrary",), vmem_limit_bytes=VMEM_LIMIT_BYTES),
        name="sample_layer",
    )(x, kc, vc, cin, nin, m0, gpre, win, bg, sinks, gmh, wout, gpost)


PROMPT_BLOCK = 512


def kernel(x_prompt, x_sample, cache_win_k, cache_win_v, state_C, state_n, state_m,
           g_pre, w_in, b_gate, attn_sinks, g_mh, w_out, g_post):
    depth = g_pre.shape[0]
    assert depth == 1, "single-layer trunk"
    bsz = x_prompt.shape[0]
    nseq, sseq, _ = x_sample.shape
    assert sseq == SAMPLE_SEQ and nseq % SAMPLE_GROUP == 0

    gpre = g_pre[0].reshape(1, D_MODEL)
    gpost = g_post[0].reshape(1, D_MODEL)
    w = w_in[0]
    win = jnp.pad(w, ((0, 0), (0, D_IN_PAD - D_IN))).astype(BF16)
    wmain = jnp.concatenate([w[:, QA:VA], w[:, ZA:VM], w[:, OM:GATES]], axis=1).astype(BF16)
    wt = jnp.concatenate([w[:, VA:ZA], w[:, VM:OM], w[:, GATES:D_IN]], axis=1).T
    wt = jnp.pad(wt, ((0, T_ROWS - wt.shape[0]), (0, 0))).astype(BF16)
    wout = w_out[0].astype(BF16)
    bg = b_gate[0].reshape(2 * M_HEADS, 1)
    sinks = attn_sinks[0].reshape(1, ATT_HEADS)
    gmh = g_mh[0]
    gmhb = jnp.broadcast_to(gmh[:, :, None], (M_HEADS, M_DIM, ROWS))

    yp, wkp, wvp, cp, np_, mp = _prompt_call(x_prompt, gpre, wmain, wt, bg, sinks, gmhb, wout, gpost, PROMPT_BLOCK)

    ngroups = nseq // SAMPLE_GROUP
    m0 = jnp.broadcast_to(state_m[0].reshape(ngroups, SAMPLE_GROUP, 1, M_HEADS),
                          (ngroups, SAMPLE_GROUP, SAMPLE_SEQ, M_HEADS))
    m0 = jnp.pad(m0.reshape(ngroups, ROWS, M_HEADS).transpose(0, 2, 1), ((0, 0), (0, 8 - M_HEADS), (0, 0)))
    ys, wks, wvs, cs, ns, ms = _sample_call(
        x_sample.reshape(nseq * sseq, D_MODEL),
        cache_win_k[0].reshape(nseq, ROWS, 128), cache_win_v[0].reshape(nseq, ROWS, 128),
        state_C[0], state_n[0].transpose(1, 0, 2), m0,
        gpre, win, bg, sinks, gmh, wout, gpost)

    kv_shape = (ATT_KV, ATT_DIM)
    ms = ms[:, 0:M_HEADS, ::SAMPLE_SEQ].transpose(0, 2, 1).reshape(nseq, M_HEADS)
    return (yp, ys.reshape(nseq, sseq, D_MODEL),
            wkp.reshape((1, bsz, ROWS) + kv_shape), wvp.reshape((1, bsz, ROWS) + kv_shape),
            cp[None], np_[None], mp[:, 0:M_HEADS, 0][None],
            wks.reshape((1, nseq, ROWS) + kv_shape), wvs.reshape((1, nseq, ROWS) + kv_shape),
            cs[None], ns.transpose(1, 0, 2)[None], ms[None])
```

```python
import functools

import jax
import jax.numpy as jnp
from jax import lax
from jax.experimental import pallas as pl
from jax.experimental.pallas import tpu as pltpu

F32 = jnp.float32
BF16 = jnp.bfloat16

D_MODEL = 1024
ROWS = 128
ATT_HEADS, ATT_KV, ATT_GROUP, ATT_DIM = 8, 2, 4, 64
M_HEADS, M_DIM = 4, 128
NORM_EPS = 1e-6
NEG_BIG = -1e30
ATT_SCALE = ATT_DIM ** -0.5
K_SCALE = M_DIM ** -0.5

QA, KA, VA, ZA = 0, 512, 640, 768
QM, KM, VM, OM, ZM = 1280, 1792, 2304, 2816, 3328
GATES = 3840
D_IN = 3848

P_QA, P_KA, P_ZA, P_QM, P_KM, P_OM, P_ZM = 0, 512, 640, 1152, 1664, 2176, 2688
P_MAIN = 3200
T_VA, T_VM, T_GATES = 0, 128, 640
T_ROWS = 656
STATE_ROWS = 144
ONES_ROWS = 16

SAMPLE_SEQ = 8
SAMPLE_GROUP = ROWS // SAMPLE_SEQ

VMEM_LIMIT_BYTES = 56 * 1024 * 1024
NT_DIMS = (((1,), (1,)), ((), ()))


def _rms(x, g):
    return x * lax.rsqrt(jnp.mean(x * x, axis=-1, keepdims=True) + NORM_EPS) * g


def _silu(x):
    return x * jax.nn.sigmoid(x)


def _log_sigmoid(x):
    return -(jnp.maximum(-x, 0.0) + jnp.log1p(jnp.exp(-jnp.abs(x))))


def _slope(head):
    return 2.0 ** -(head + 1)


def _bdot(a, b):
    return jnp.dot(a.astype(BF16), b.astype(BF16), preferred_element_type=F32)


def _bdot_nt(a, b):
    return lax.dot_general(a.astype(BF16), b.astype(BF16), NT_DIMS, preferred_element_type=F32)


def _exact_dot(x, m):
    hi = x.astype(BF16).astype(F32)
    mid = (x - hi).astype(BF16).astype(F32)
    lo = (x - hi - mid).astype(BF16).astype(F32)
    parts = jnp.dot(jnp.concatenate([hi, mid, lo, jnp.zeros_like(hi)], axis=0).astype(BF16), m,
                    preferred_element_type=F32)
    return parts[0:8] + parts[8:16] + parts[16:24]


def _gate_rows(x, m0, mask_t, last_sel):
    row = lax.broadcasted_iota(jnp.int32, (8, ROWS), 0)
    head_rows = row < M_HEADS
    ic = jnp.where(head_rows, x, 0.0)
    fc = jnp.where(head_rows, _log_sigmoid(pltpu.roll(x, M_HEADS, 0)), 0.0)
    b = _exact_dot(fc, mask_t.astype(BF16))
    a = ic - b
    a_cols = jnp.concatenate([a, jnp.zeros((ROWS - 8, ROWS), F32)], axis=0).T
    a_masked = [jnp.where(mask_t, a_cols[:, h:h + 1], -jnp.inf) for h in range(M_HEADS)]
    cm = jnp.concatenate([jnp.max(am, axis=0, keepdims=True) for am in a_masked]
                         + [jnp.zeros((8 - M_HEADS, ROWS), F32)], axis=0)
    m_t = jnp.maximum(b + m0, b + cm)
    if last_sel is None:
        b_last = jnp.broadcast_to(b[:, ROWS - 1:ROWS], b.shape)
        m_new = jnp.broadcast_to(m_t[:, ROWS - 1:ROWS], b.shape)
    else:
        both = _exact_dot(jnp.where(head_rows, b, pltpu.roll(m_t, M_HEADS, 0)), last_sel)
        b_last = jnp.where(head_rows, both, 0.0)
        m_new = jnp.where(head_rows, pltpu.roll(both, M_HEADS, 0), 0.0)
    return dict(a_masked=a_masked, bm=b - m_t, gexp=jnp.exp(b + m0 - m_t), enm=jnp.exp(-m_t), m_new=m_new,
                w=jnp.exp(b_last - b + ic - m_new), decay=jnp.exp(b_last + m0 - m_new))


def _out_tail(cat, x, wout_ref, gpost_ref):
    y = jnp.dot(cat, wout_ref[...], preferred_element_type=F32)
    return x + _rms(y, gpost_ref[...])


MAIN_BLOCKS = P_MAIN // 128


def _main_src_block(i):
    return jnp.where(i < 5, i, jnp.where(i < 17, i + 1, i + 5))


def _weights_kernel(main_ref, va_ref, vm0_ref, vm1_ref, vm2_ref, vm3_ref, gates_ref, wout_ref,
                    wmain_ref, wt_ref, woutb_ref):
    wmain_ref[...] = main_ref[...].T.astype(BF16)

    @pl.when(pl.program_id(0) == 0)
    def _():
        wt_ref[...] = jnp.concatenate(
            [va_ref[...], vm0_ref[...], vm1_ref[...], vm2_ref[...], vm3_ref[...], gates_ref[...],
             jnp.zeros((T_ROWS - T_GATES - 8, D_MODEL), F32)], axis=0).astype(BF16)
        woutb_ref[...] = wout_ref[...].astype(BF16)


def _weights_call(w_in_t, w_out):
    blk = lambda r: pl.BlockSpec((128, D_MODEL), lambda i, r=r: (r, 0))
    return pl.pallas_call(
        _weights_kernel,
        grid=(MAIN_BLOCKS,),
        in_specs=[
            pl.BlockSpec((128, D_MODEL), lambda i: (_main_src_block(i), 0)),
            blk(VA // 128), blk(VM // 128), blk(VM // 128 + 1), blk(VM // 128 + 2), blk(VM // 128 + 3),
            pl.BlockSpec((8, D_MODEL), lambda i: (GATES // 8, 0)),
            pl.BlockSpec((D_MODEL, D_MODEL), lambda i: (0, 0)),
        ],
        out_specs=(
            pl.BlockSpec((D_MODEL, 128), lambda i: (0, i)),
            pl.BlockSpec((T_ROWS, D_MODEL), lambda i: (0, 0)),
            pl.BlockSpec((D_MODEL, D_MODEL), lambda i: (0, 0)),
        ),
        out_shape=(
            jax.ShapeDtypeStruct((D_MODEL, P_MAIN), BF16),
            jax.ShapeDtypeStruct((T_ROWS, D_MODEL), BF16),
            jax.ShapeDtypeStruct((D_MODEL, D_MODEL), BF16),
        ),
        compiler_params=pltpu.CompilerParams(
            dimension_semantics=("arbitrary",), vmem_limit_bytes=VMEM_LIMIT_BYTES),
        name="layer_weights",
    )(w_in_t, w_in_t, w_in_t, w_in_t, w_in_t, w_in_t, w_in_t, w_out)


def _prompt_bias_t(kv, first):
    j = lax.broadcasted_iota(jnp.int32, (2 * ROWS, ROWS), 0)
    i = lax.broadcasted_iota(jnp.int32, (2 * ROWS, ROWS), 1)
    diff = ROWS + i - j
    valid = (diff >= 0) & (diff < ROWS)
    if first:
        valid = valid & (j >= ROWS)
    dfl = diff.astype(F32)
    return jnp.concatenate(
        [jnp.where(valid, -_slope(kv * ATT_GROUP + g) * dfl, NEG_BIG) for g in range(ATT_GROUP)], axis=1)


def _prompt_kernel(x_ref, gpre_ref, wmain_ref, wt_ref, bg_ref, sink_ref, gmhb_ref, wout_ref, gpost_ref,
                   y_ref, wk_ref, wv_ref, c_ref, n_ref, m_ref,
                   proj_ref, projt_ref, cat_ref, kprev_ref, vtprev_ref, ct_ref, mst_ref, bias_ref, *, tb, nt):
    bi = pl.program_id(0)
    j = pl.program_id(1)
    nchunks = tb // ROWS

    @pl.when((bi == 0) & (j == 0))
    def _():
        for first in range(2):
            for kv in range(ATT_KV):
                bias_ref[first * ATT_KV + kv] = _prompt_bias_t(kv, first)

    @pl.when(j == 0)
    def _():
        kprev_ref[...] = jnp.zeros_like(kprev_ref)
        vtprev_ref[...] = jnp.zeros_like(vtprev_ref)
        ct_ref[...] = jnp.zeros_like(ct_ref)
        mst_ref[...] = jnp.zeros_like(mst_ref)

    x = x_ref[0]
    xn = _rms(x, gpre_ref[...]).astype(BF16)
    proj_ref[...] = jnp.dot(xn, wmain_ref[...], preferred_element_type=F32)
    pt = lax.dot_general(wt_ref[...], xn, NT_DIMS, preferred_element_type=F32)
    for c in range(nchunks):
        projt_ref[c] = pt[:, c * ROWS:(c + 1) * ROWS]

    ri = lax.broadcasted_iota(jnp.int32, (ROWS, ROWS), 0)
    ci = lax.broadcasted_iota(jnp.int32, (ROWS, ROWS), 1)
    mask_t = ri <= ci
    st_row = lax.broadcasted_iota(jnp.int32, (STATE_ROWS - M_DIM, ROWS), 0)

    def chunk(c, carry):
        r0 = pl.multiple_of(c * ROWS, ROWS)
        rows = pl.ds(r0, ROWS)
        first = ((j == 0) & (c == 0)).astype(jnp.int32)

        qa = proj_ref[rows, P_QA:P_QA + 512] * ATT_SCALE
        kcur = proj_ref[rows, P_KA:P_KA + 128].astype(BF16)
        vtcur = projt_ref[c, T_VA:T_VA + 128, :].astype(BF16)
        kcat = jnp.concatenate([kprev_ref[...], kcur], axis=0)
        vtcat = jnp.concatenate([vtprev_ref[...], vtcur], axis=1)
        r = _gate_rows(projt_ref[c, T_GATES:T_GATES + 8, :] + bg_ref[...], mst_ref[...], mask_t, None)

        scores, sinks = [], []
        for kv in range(ATT_KV):
            want_hi = kv == 1
            keep = (ci >= ATT_DIM) if want_hi else (ci < ATT_DIM)
            pieces = []
            for g in range(ATT_GROUP):
                hh = kv * ATT_GROUP + g
                blk = qa[:, (hh // 2) * 128:(hh // 2 + 1) * 128]
                if (hh % 2 == 1) != want_hi:
                    blk = pltpu.roll(blk, ATT_DIM, 1)
                pieces.append(jnp.where(keep, blk, 0.0))
            q4 = jnp.concatenate(pieces, axis=0)
            scores.append(_bdot_nt(kcat, q4) + bias_ref[first * ATT_KV + kv])
            sinks.append(jnp.concatenate(
                [jnp.broadcast_to(sink_ref[0:1, kv * ATT_GROUP + g:kv * ATT_GROUP + g + 1], (1, ROWS))
                 for g in range(ATT_GROUP)], axis=1))
        vts, r1s = [], []
        for h in range(M_HEADS):
            q = proj_ref[rows, P_QM + h * M_DIM:P_QM + (h + 1) * M_DIM].astype(BF16)
            k = (proj_ref[rows, P_KM + h * M_DIM:P_KM + (h + 1) * M_DIM] * K_SCALE).astype(BF16)
            vt = projt_ref[c, T_VM + h * M_DIM:T_VM + (h + 1) * M_DIM, :]
            ct = ct_ref[h]
            r1s.append(lax.dot_general(jnp.concatenate([k, ct.astype(BF16)], axis=0), q, NT_DIMS,
                                       preferred_element_type=F32))
            w_row = r["w"][h:h + 1, :]
            lhs2 = jnp.concatenate([vt * w_row, jnp.where(st_row == 0, w_row, 0.0)], axis=0)
            ct_ref[h] = r["decay"][h:h + 1, 0:1] * ct + _bdot(lhs2, k)
            vts.append(vt)
        mst_ref[...] = r["m_new"]

        outs, nums, dens = [], [], []
        for kv in range(ATT_KV):
            s, sink = scores[kv], sinks[kv]
            mx = jnp.maximum(jnp.max(s, axis=0, keepdims=True), sink)
            p = jnp.exp(s - mx).astype(BF16)
            lhs = jnp.concatenate([vtcat[kv * ATT_DIM:(kv + 1) * ATT_DIM, :],
                                   jnp.ones((ONES_ROWS, 2 * ROWS), BF16)], axis=0)
            o = jnp.dot(lhs, p, preferred_element_type=F32)
            outs.append((o, jnp.exp(sink - mx)))
        for h in range(M_HEADS):
            r1 = r1s[h]
            dm = jnp.exp(r["a_masked"][h] + r["bm"][h:h + 1, :])
            st = r1[0:ROWS] * dm
            g_row = r["gexp"][h:h + 1, :]
            nums.append(_bdot(vts[h], st) + g_row * r1[ROWS:ROWS + M_DIM])
            dens.append(jnp.sum(st, axis=0, keepdims=True) + g_row * r1[ROWS + M_DIM:ROWS + M_DIM + 1])

        att = []
        for kv in range(ATT_KV):
            o, esink = outs[kv]
            on = o[0:ATT_DIM, :] / (o[ATT_DIM:ATT_DIM + 1, :] + esink)
            for pair in range(2):
                two = jnp.concatenate([on[:, (2 * pair) * ROWS:(2 * pair + 1) * ROWS],
                                       on[:, (2 * pair + 1) * ROWS:(2 * pair + 2) * ROWS]], axis=0)
                att.append(two.T)
        a_out = jnp.concatenate(att, axis=1) * _silu(proj_ref[rows, P_ZA:P_ZA + 512])
        kprev_ref[...] = kcur
        vtprev_ref[...] = vtcur
        m_out = []
        for h in range(M_HEADS):
            ht = nums[h] / jnp.maximum(jnp.abs(dens[h]), r["enm"][h:h + 1, :])
            hn = ht * lax.rsqrt(jnp.mean(ht * ht, axis=0, keepdims=True) + NORM_EPS) * gmhb_ref[h]
            m_out.append(jax.nn.sigmoid(proj_ref[rows, P_OM + h * M_DIM:P_OM + (h + 1) * M_DIM]) * hn.T
                         * _silu(proj_ref[rows, P_ZM + h * M_DIM:P_ZM + (h + 1) * M_DIM]))

        cat_ref[rows, :] = jnp.concatenate([a_out] + m_out, axis=1).astype(BF16)
        return carry

    lax.fori_loop(0, nchunks, chunk, 0)

    y_ref[0] = _out_tail(cat_ref[...], x, wout_ref, gpost_ref)

    @pl.when(j == nt - 1)
    def _():
        wk_ref[0] = proj_ref[tb - ROWS:tb, P_KA:P_KA + 128].T
        wv_ref[0] = projt_ref[nchunks - 1, T_VA:T_VA + 128, :]
        for h in range(M_HEADS):
            ct = ct_ref[h]
            c_ref[0, h] = ct[0:M_DIM].T
            n_ref[0, h:h + 1, :] = ct[M_DIM:M_DIM + 1]
        m_ref[0] = mst_ref[...]


def _prompt_call(x, gpre, wmain, wt, bg, sinks, gmhb, wout, gpost, tb):
    bsz, seq, _ = x.shape
    nt = seq // tb
    full = lambda shape: pl.BlockSpec(shape, lambda b, j: (0,) * len(shape))
    out_shapes = (
        jax.ShapeDtypeStruct((bsz, seq, D_MODEL), F32),
        jax.ShapeDtypeStruct((bsz, 128, ROWS), F32),
        jax.ShapeDtypeStruct((bsz, 128, ROWS), F32),
        jax.ShapeDtypeStruct((bsz, M_HEADS, M_DIM, M_DIM), F32),
        jax.ShapeDtypeStruct((bsz, M_HEADS, M_DIM), F32),
        jax.ShapeDtypeStruct((bsz, 8, ROWS), F32),
    )
    return pl.pallas_call(
        functools.partial(_prompt_kernel, tb=tb, nt=nt),
        grid=(bsz, nt),
        in_specs=[
            pl.BlockSpec((1, tb, D_MODEL), lambda b, j: (b, j, 0)),
            full((1, D_MODEL)), full((D_MODEL, P_MAIN)), full((T_ROWS, D_MODEL)), full((8, 1)),
            full((1, ATT_HEADS)), full((M_HEADS, M_DIM, ROWS)), full((D_MODEL, D_MODEL)), full((1, D_MODEL)),
        ],
        out_specs=(
            pl.BlockSpec((1, tb, D_MODEL), lambda b, j: (b, j, 0)),
            pl.BlockSpec((1, 128, ROWS), lambda b, j: (b, 0, 0)),
            pl.BlockSpec((1, 128, ROWS), lambda b, j: (b, 0, 0)),
            pl.BlockSpec((1, M_HEADS, M_DIM, M_DIM), lambda b, j: (b, 0, 0, 0)),
            pl.BlockSpec((1, M_HEADS, M_DIM), lambda b, j: (b, 0, 0)),
            pl.BlockSpec((1, 8, ROWS), lambda b, j: (b, 0, 0)),
        ),
        out_shape=out_shapes,
        scratch_shapes=[
            pltpu.VMEM((tb, P_MAIN), F32),
            pltpu.VMEM((tb // ROWS, T_ROWS, ROWS), F32),
            pltpu.VMEM((tb, D_MODEL), BF16),
            pltpu.VMEM((ROWS, 128), BF16),
            pltpu.VMEM((128, ROWS), BF16),
            pltpu.VMEM((M_HEADS, STATE_ROWS, M_DIM), F32),
            pltpu.VMEM((8, ROWS), F32),
            pltpu.VMEM((2 * ATT_KV, 2 * ROWS, ATT_GROUP * ROWS), F32),
        ],
        compiler_params=pltpu.CompilerParams(
            dimension_semantics=("arbitrary", "arbitrary"), vmem_limit_bytes=VMEM_LIMIT_BYTES),
        name="prompt_layer",
    )(x, gpre, wmain, wt, bg, sinks, gmhb, wout, gpost)


def _sample_bias_new(head):
    r = lax.broadcasted_iota(jnp.int32, (ROWS, ROWS), 0)
    c = lax.broadcasted_iota(jnp.int32, (ROWS, ROWS), 1)
    valid = ((r >> 3) == (c >> 3)) & (r >= c)
    return jnp.where(valid, -_slope(head) * (r - c).astype(F32), NEG_BIG)


def _sample_bias_cache(head):
    r = lax.broadcasted_iota(jnp.int32, (ROWS, ROWS), 0)
    c = lax.broadcasted_iota(jnp.int32, (ROWS, ROWS), 1)
    diff = (r & (SAMPLE_SEQ - 1)) + ROWS - c
    return jnp.where(diff < ROWS, -_slope(head) * diff.astype(F32), NEG_BIG)


def _sample_kernel(x_ref, kct_ref, vct_ref, cin_ref, nin_ref, m0_ref,
                   gpre_ref, wmain_ref, wt_ref, bg_ref, sink_ref, gmhb_ref, wout_ref, gpost_ref,
                   y_ref, kot_ref, vot_ref, cout_ref, nout_ref, mout_ref,
                   proj_ref, qh_ref, sc_ref, oc_ref, qc_ref, kat_ref, vat_ref, wv_ref, kt_ref, decb_ref,
                   bn_ref, bc_ref):
    @pl.when(pl.program_id(0) == 0)
    def _():
        for hh in range(ATT_HEADS):
            bn_ref[hh] = _sample_bias_new(hh)
            bc_ref[hh] = _sample_bias_cache(hh)

    x = x_ref[...]
    xn = _rms(x, gpre_ref[...]).astype(BF16)
    proj_ref[...] = jnp.dot(xn, wmain_ref[...], preferred_element_type=F32)
    pt = lax.dot_general(wt_ref[...], xn, NT_DIMS, preferred_element_type=F32)

    ri = lax.broadcasted_iota(jnp.int32, (ROWS, ROWS), 0)
    ci = lax.broadcasted_iota(jnp.int32, (ROWS, ROWS), 1)
    same_seq = (ri >> 3) == (ci >> 3)
    mask_t = same_seq & (ri <= ci)
    last_sel = (same_seq & ((ri & (SAMPLE_SEQ - 1)) == SAMPLE_SEQ - 1)).astype(BF16)
    row16 = lax.broadcasted_iota(jnp.int32, (SAMPLE_GROUP, ROWS), 0)
    lane16 = lax.broadcasted_iota(jnp.int32, (SAMPLE_GROUP, ROWS), 1)
    seq_of_lane = (lane16 >> 3) == row16
    ones_rows = jnp.ones((ONES_ROWS, ROWS), F32)

    qa = proj_ref[:, P_QA:P_QA + 512] * ATT_SCALE
    for hh in range(ATT_HEADS):
        blk = qa[:, (hh // 2) * 128:(hh // 2 + 1) * 128]
        if hh % 2 == 1:
            blk = pltpu.roll(blk, ATT_DIM, 1)
        qh_ref[hh] = blk[:, 0:ATT_DIM]
    ka = proj_ref[:, P_KA:P_KA + 128]
    kat = ka.T
    vat = pt[T_VA:T_VA + 128, :]
    kat_ref[...] = kat
    vat_ref[...] = vat

    def seq_scores(b, carry):
        rows = pl.ds(pl.multiple_of(b * SAMPLE_SEQ, SAMPLE_SEQ), SAMPLE_SEQ)
        kct = kct_ref[b]
        for kv in range(ATT_KV):
            lhs = jnp.concatenate([qh_ref[kv * ATT_GROUP + g, rows, :] for g in range(ATT_GROUP)], axis=0)
            res = _bdot(lhs, kct[kv * ATT_DIM:(kv + 1) * ATT_DIM, :])
            for g in range(ATT_GROUP):
                sc_ref[kv * ATT_GROUP + g, rows, :] = res[g * SAMPLE_SEQ:(g + 1) * SAMPLE_SEQ]
        for h in range(M_HEADS):
            qc_ref[h, rows, :] = _bdot(proj_ref[rows, P_QM + h * M_DIM:P_QM + (h + 1) * M_DIM], cin_ref[b, h])
        return carry

    lax.fori_loop(0, SAMPLE_GROUP, seq_scores, 0)

    o_new, esinks = [], []
    for hh in range(ATT_HEADS):
        kv = hh // ATT_GROUP
        s_n = _bdot(qh_ref[hh], kat[kv * ATT_DIM:(kv + 1) * ATT_DIM, :]) + bn_ref[hh]
        s_c = sc_ref[hh] + bc_ref[hh]
        sink = sink_ref[0:1, hh:hh + 1]
        mx = jnp.maximum(jnp.maximum(jnp.max(s_n, axis=-1, keepdims=True),
                                     jnp.max(s_c, axis=-1, keepdims=True)), sink)
        p_n = jnp.exp(s_n - mx)
        sc_ref[hh] = jnp.exp(s_c - mx)
        vaug = jnp.concatenate([vat[kv * ATT_DIM:(kv + 1) * ATT_DIM, :], ones_rows], axis=0)
        o_new.append(_bdot_nt(p_n, vaug))
        esinks.append(jnp.exp(sink - mx))

    r = _gate_rows(pt[T_GATES:T_GATES + 8, :] + bg_ref[...], m0_ref[0], mask_t, last_sel)
    mout_ref[0] = r["m_new"]
    m_out = []
    for h in range(M_HEADS):
        q = proj_ref[:, P_QM + h * M_DIM:P_QM + (h + 1) * M_DIM].astype(BF16)
        kf = proj_ref[:, P_KM + h * M_DIM:P_KM + (h + 1) * M_DIM] * K_SCALE
        k = kf.astype(BF16)
        vt = pt[T_VM + h * M_DIM:T_VM + (h + 1) * M_DIM, :]
        n_h = nin_ref[h]
        r1 = lax.dot_general(jnp.concatenate([k, n_h.astype(BF16)], axis=0), q, NT_DIMS,
                             preferred_element_type=F32)
        st = r1[0:ROWS] * jnp.exp(r["a_masked"][h] + r["bm"][h:h + 1, :])
        q_n = jnp.sum(jnp.where(seq_of_lane, r1[ROWS:ROWS + SAMPLE_GROUP], 0.0), axis=0, keepdims=True)
        g_row = r["gexp"][h:h + 1, :]
        num = _bdot(vt, st) + g_row * qc_ref[h].T
        den = jnp.sum(st, axis=0, keepdims=True) + g_row * q_n
        ht = num / jnp.maximum(jnp.abs(den), r["enm"][h:h + 1, :])
        hn = ht * lax.rsqrt(jnp.mean(ht * ht, axis=0, keepdims=True) + NORM_EPS) * gmhb_ref[h]
        m_out.append(jax.nn.sigmoid(proj_ref[:, P_OM + h * M_DIM:P_OM + (h + 1) * M_DIM]) * hn.T
                     * _silu(proj_ref[:, P_ZM + h * M_DIM:P_ZM + (h + 1) * M_DIM]))
        w_row = r["w"][h:h + 1, :]
        dec16 = jnp.sum(jnp.where(lane16 == row16 * SAMPLE_SEQ, r["decay"][h:h + 1, :], 0.0),
                        axis=1, keepdims=True)
        nout_ref[h] = dec16 * n_h + _bdot(jnp.where(seq_of_lane, w_row, 0.0), k)
        decb_ref[h] = jnp.broadcast_to(dec16, (SAMPLE_GROUP, ROWS))
        wv_ref[h] = (vt * w_row).T
        kt_ref[h] = kf.T.astype(BF16)

    keep_new = ci >= ROWS - SAMPLE_SEQ

    def seq_update(b, carry):
        rows = pl.ds(pl.multiple_of(b * SAMPLE_SEQ, SAMPLE_SEQ), SAMPLE_SEQ)
        vct = vct_ref[b]
        for kv in range(ATT_KV):
            vaug = jnp.concatenate([vct[kv * ATT_DIM:(kv + 1) * ATT_DIM, :], ones_rows], axis=0)
            pl_ = jnp.concatenate([sc_ref[kv * ATT_GROUP + g, rows, :] for g in range(ATT_GROUP)], axis=0)
            res = _bdot_nt(pl_, vaug)
            for g in range(ATT_GROUP):
                oc_ref[kv * ATT_GROUP + g, rows, 0:ATT_DIM + ONES_ROWS] = res[g * SAMPLE_SEQ:(g + 1) * SAMPLE_SEQ]
        in_seq = (ri >> 3) == b
        for h in range(M_HEADS):
            upd = jnp.dot(kt_ref[h], jnp.where(in_seq, wv_ref[h], 0.0).astype(BF16), preferred_element_type=F32)
            cout_ref[b, h] = decb_ref[h, pl.ds(b, 1), :] * cin_ref[b, h] + upd
        shift = (ROWS - SAMPLE_SEQ - b * SAMPLE_SEQ) & (ROWS - 1)
        kot_ref[b] = jnp.where(keep_new, pltpu.roll(kat_ref[...], shift, 1),
                               pltpu.roll(kct_ref[b], ROWS - SAMPLE_SEQ, 1))
        vot_ref[b] = jnp.where(keep_new, pltpu.roll(vat_ref[...], shift, 1),
                               pltpu.roll(vct, ROWS - SAMPLE_SEQ, 1))
        return carry

    lax.fori_loop(0, SAMPLE_GROUP, seq_update, 0)

    att = []
    for pair in range(ATT_HEADS // 2):
        halves = []
        for hh in (2 * pair, 2 * pair + 1):
            on, oc = o_new[hh], oc_ref[hh]
            den = on[:, ATT_DIM:ATT_DIM + 1] + oc[:, ATT_DIM:ATT_DIM + 1] + esinks[hh]
            halves.append((on[:, 0:ATT_DIM] + oc[:, 0:ATT_DIM]) / den)
        att.append(jnp.concatenate(halves, axis=1))
    a_out = jnp.concatenate(att, axis=1) * _silu(proj_ref[:, P_ZA:P_ZA + 512])
    cat = jnp.concatenate([a_out] + m_out, axis=1).astype(BF16)
    y_ref[...] = _out_tail(cat, x, wout_ref, gpost_ref)


def _sample_call(x, kct, vct, cin, nin, m0, gpre, wmain, wt, bg, sinks, gmhb, wout, gpost):
    nrows = x.shape[0]
    ngroups = nrows // ROWS
    nseq = ngroups * SAMPLE_GROUP
    full = lambda shape: pl.BlockSpec(shape, lambda i: (0,) * len(shape))
    grp = SAMPLE_GROUP
    out_shapes = (
        jax.ShapeDtypeStruct((nrows, D_MODEL), F32),
        jax.ShapeDtypeStruct((nseq, 128, ROWS), F32),
        jax.ShapeDtypeStruct((nseq, 128, ROWS), F32),
        jax.ShapeDtypeStruct((nseq, M_HEADS, M_DIM, M_DIM), F32),
        jax.ShapeDtypeStruct((M_HEADS, nseq, M_DIM), F32),
        jax.ShapeDtypeStruct((ngroups, 8, ROWS), F32),
    )
    return pl.pallas_call(
        _sample_kernel,
        grid=(ngroups,),
        in_specs=[
            pl.BlockSpec((ROWS, D_MODEL), lambda i: (i, 0)),
            pl.BlockSpec((grp, 128, ROWS), lambda i: (i, 0, 0)),
            pl.BlockSpec((grp, 128, ROWS), lambda i: (i, 0, 0)),
            pl.BlockSpec((grp, M_HEADS, M_DIM, M_DIM), lambda i: (i, 0, 0, 0)),
            pl.BlockSpec((M_HEADS, grp, M_DIM), lambda i: (0, i, 0)),
            pl.BlockSpec((1, 8, ROWS), lambda i: (i, 0, 0)),
            full((1, D_MODEL)), full((D_MODEL, P_MAIN)), full((T_ROWS, D_MODEL)), full((8, 1)),
            full((1, ATT_HEADS)), full((M_HEADS, M_DIM, ROWS)), full((D_MODEL, D_MODEL)), full((1, D_MODEL)),
        ],
        out_specs=(
            pl.BlockSpec((ROWS, D_MODEL), lambda i: (i, 0)),
            pl.BlockSpec((grp, 128, ROWS), lambda i: (i, 0, 0)),
            pl.BlockSpec((grp, 128, ROWS), lambda i: (i, 0, 0)),
            pl.BlockSpec((grp, M_HEADS, M_DIM, M_DIM), lambda i: (i, 0, 0, 0)),
            pl.BlockSpec((M_HEADS, grp, M_DIM), lambda i: (0, i, 0)),
            pl.BlockSpec((1, 8, ROWS), lambda i: (i, 0, 0)),
        ),
        out_shape=out_shapes,
        scratch_shapes=[
            pltpu.VMEM((ROWS, P_MAIN), F32),
            pltpu.VMEM((ATT_HEADS, ROWS, ATT_DIM), F32),
            pltpu.VMEM((ATT_HEADS, ROWS, ROWS), F32),
            pltpu.VMEM((ATT_HEADS, ROWS, ROWS), F32),
            pltpu.VMEM((M_HEADS, ROWS, M_DIM), F32),
            pltpu.VMEM((128, ROWS), F32),
            pltpu.VMEM((128, ROWS), F32),
            pltpu.VMEM((M_HEADS, ROWS, M_DIM), F32),
            pltpu.VMEM((M_HEADS, M_DIM, ROWS), BF16),
            pltpu.VMEM((M_HEADS, SAMPLE_GROUP, ROWS), F32),
            pltpu.VMEM((ATT_HEADS, ROWS, ROWS), F32),
            pltpu.VMEM((ATT_HEADS, ROWS, ROWS), F32),
        ],
        compiler_params=pltpu.CompilerParams(
            dimension_semantics=("arbitrary",), vmem_limit_bytes=VMEM_LIMIT_BYTES),
        name="sample_layer",
    )(x, kct, vct, cin, nin, m0, gpre, wmain, wt, bg, sinks, gmhb, wout, gpost)


PROMPT_BLOCK = 512


def _window_in(cache):
    nseq = cache.shape[0]
    return cache.transpose(0, 2, 3, 1).reshape(nseq, ATT_KV * ATT_DIM, ROWS)


def _window_out(win_t):
    nseq = win_t.shape[0]
    return win_t.reshape(nseq, ATT_KV, ATT_DIM, ROWS).transpose(0, 3, 1, 2)[None]


def kernel(x_prompt, x_sample, cache_win_k, cache_win_v, state_C, state_n, state_m,
           g_pre, w_in, b_gate, attn_sinks, g_mh, w_out, g_post):
    depth = g_pre.shape[0]
    assert depth == 1, "single-layer trunk"
    nseq, sseq, _ = x_sample.shape
    assert sseq == SAMPLE_SEQ and nseq % SAMPLE_GROUP == 0

    gpre = g_pre[0].reshape(1, D_MODEL)
    gpost = g_post[0].reshape(1, D_MODEL)
    wmain, wt, wout = _weights_call(w_in[0].T, w_out[0])
    bg = b_gate[0].reshape(2 * M_HEADS, 1)
    sinks = attn_sinks[0].reshape(1, ATT_HEADS)
    gmhb = jnp.broadcast_to(g_mh[0][:, :, None], (M_HEADS, M_DIM, ROWS))

    yp, wkp, wvp, cp, np_, mp = _prompt_call(x_prompt, gpre, wmain, wt, bg, sinks, gmhb, wout, gpost, PROMPT_BLOCK)

    ngroups = nseq // SAMPLE_GROUP
    m0 = jnp.broadcast_to(state_m[0].reshape(ngroups, SAMPLE_GROUP, 1, M_HEADS),
                          (ngroups, SAMPLE_GROUP, SAMPLE_SEQ, M_HEADS))
    m0 = jnp.pad(m0.reshape(ngroups, ROWS, M_HEADS).transpose(0, 2, 1), ((0, 0), (0, 8 - M_HEADS), (0, 0)))
    ys, wks, wvs, cs, ns, ms = _sample_call(
        x_sample.reshape(nseq * sseq, D_MODEL), _window_in(cache_win_k[0]), _window_in(cache_win_v[0]),
        state_C[0], state_n[0].transpose(1, 0, 2), m0,
        gpre, wmain, wt, bg, sinks, gmhb, wout, gpost)

    ms = ms[:, 0:M_HEADS, ::SAMPLE_SEQ].transpose(0, 2, 1).reshape(nseq, M_HEADS)
    return (yp, ys.reshape(nseq, sseq, D_MODEL), _window_out(wkp), _window_out(wvp),
            cp[None], np_[None], mp[:, 0:M_HEADS, 0][None],
            _window_out(wks), _window_out(wvs), cs[None], ns.transpose(1, 0, 2)[None], ms[None])
```

```python
import functools

import jax
import jax.numpy as jnp
from jax import lax
from jax.experimental import pallas as pl
from jax.experimental.pallas import tpu as pltpu

F32 = jnp.float32
BF16 = jnp.bfloat16

D_MODEL = 1024
ROWS = 128
ATT_HEADS, ATT_KV, ATT_GROUP, ATT_DIM = 8, 2, 4, 64
M_HEADS, M_DIM = 4, 128
NORM_EPS = 1e-6
NEG_BIG = -1e30
ATT_SCALE = ATT_DIM ** -0.5
K_SCALE = M_DIM ** -0.5

QA, KA, VA, ZA = 0, 512, 640, 768
QM, KM, VM, OM, ZM = 1280, 1792, 2304, 2816, 3328
GATES = 3840
D_IN = 3848

P_QA, P_KA, P_ZA, P_QM, P_KM, P_OM, P_ZM = 0, 512, 640, 1152, 1664, 2176, 2688
P_MAIN = 3200
T_VA, T_VM, T_GATES = 0, 128, 640
T_ROWS = 656
STATE_ROWS = 144
ONES_ROWS = 16

SAMPLE_SEQ = 8
SAMPLE_GROUP = ROWS // SAMPLE_SEQ

VMEM_LIMIT_BYTES = 56 * 1024 * 1024
NT_DIMS = (((1,), (1,)), ((), ()))


def _rms(x, g):
    return x * lax.rsqrt(jnp.mean(x * x, axis=-1, keepdims=True) + NORM_EPS) * g


def _silu(x):
    return x * jax.nn.sigmoid(x)


def _log_sigmoid(x):
    return -(jnp.maximum(-x, 0.0) + jnp.log1p(jnp.exp(-jnp.abs(x))))


def _slope(head):
    return 2.0 ** -(head + 1)


def _bdot(a, b):
    return jnp.dot(a.astype(BF16), b.astype(BF16), preferred_element_type=F32)


def _bdot_nt(a, b):
    return lax.dot_general(a.astype(BF16), b.astype(BF16), NT_DIMS, preferred_element_type=F32)


def _exact_dot(x, m):
    hi = x.astype(BF16).astype(F32)
    mid = (x - hi).astype(BF16).astype(F32)
    lo = (x - hi - mid).astype(BF16).astype(F32)
    parts = jnp.dot(jnp.concatenate([hi, mid, lo, jnp.zeros_like(hi)], axis=0).astype(BF16), m,
                    preferred_element_type=F32)
    return parts[0:8] + parts[8:16] + parts[16:24]


def _gate_rows(x, m0, mask_t, last_sel):
    row = lax.broadcasted_iota(jnp.int32, (8, ROWS), 0)
    head_rows = row < M_HEADS
    ic = jnp.where(head_rows, x, 0.0)
    fc = jnp.where(head_rows, _log_sigmoid(pltpu.roll(x, M_HEADS, 0)), 0.0)
    b = _exact_dot(fc, mask_t.astype(BF16))
    a = ic - b
    a_cols = jnp.concatenate([a, jnp.zeros((ROWS - 8, ROWS), F32)], axis=0).T
    a_masked = [jnp.where(mask_t, a_cols[:, h:h + 1], -jnp.inf) for h in range(M_HEADS)]
    cm = jnp.concatenate([jnp.max(am, axis=0, keepdims=True) for am in a_masked]
                         + [jnp.zeros((8 - M_HEADS, ROWS), F32)], axis=0)
    m_t = jnp.maximum(b + m0, b + cm)
    if last_sel is None:
        b_last = jnp.broadcast_to(b[:, ROWS - 1:ROWS], b.shape)
        m_new = jnp.broadcast_to(m_t[:, ROWS - 1:ROWS], b.shape)
    else:
        both = _exact_dot(jnp.where(head_rows, b, pltpu.roll(m_t, M_HEADS, 0)), last_sel)
        b_last = jnp.where(head_rows, both, 0.0)
        m_new = jnp.where(head_rows, pltpu.roll(both, M_HEADS, 0), 0.0)
    return dict(a_masked=a_masked, bm=b - m_t, gexp=jnp.exp(b + m0 - m_t), enm=jnp.exp(-m_t), m_new=m_new,
                w=jnp.exp(b_last - b + ic - m_new), decay=jnp.exp(b_last + m0 - m_new))


def _out_tail(cat, x, wout_ref, gpost_ref):
    y = jnp.dot(cat, wout_ref[...], preferred_element_type=F32)
    return x + _rms(y, gpost_ref[...])


MAIN_BLOCKS = P_MAIN // 128


def _main_src_block(i):
    return jnp.where(i < 5, i, jnp.where(i < 17, i + 1, i + 5))


def _weights_kernel(main_ref, va_ref, vm0_ref, vm1_ref, vm2_ref, vm3_ref, gates_ref, wout_ref,
                    wmain_ref, wt_ref, woutb_ref):
    wmain_ref[...] = main_ref[...].T.astype(BF16)

    @pl.when(pl.program_id(0) == 0)
    def _():
        wt_ref[...] = jnp.concatenate(
            [va_ref[...], vm0_ref[...], vm1_ref[...], vm2_ref[...], vm3_ref[...], gates_ref[...],
             jnp.zeros((T_ROWS - T_GATES - 8, D_MODEL), F32)], axis=0).astype(BF16)
        woutb_ref[...] = wout_ref[...].astype(BF16)


def _weights_call(w_in_t, w_out):
    blk = lambda r: pl.BlockSpec((128, D_MODEL), lambda i, r=r: (r, 0))
    return pl.pallas_call(
        _weights_kernel,
        grid=(MAIN_BLOCKS,),
        in_specs=[
            pl.BlockSpec((128, D_MODEL), lambda i: (_main_src_block(i), 0)),
            blk(VA // 128), blk(VM // 128), blk(VM // 128 + 1), blk(VM // 128 + 2), blk(VM // 128 + 3),
            pl.BlockSpec((8, D_MODEL), lambda i: (GATES // 8, 0)),
            pl.BlockSpec((D_MODEL, D_MODEL), lambda i: (0, 0)),
        ],
        out_specs=(
            pl.BlockSpec((D_MODEL, 128), lambda i: (0, i)),
            pl.BlockSpec((T_ROWS, D_MODEL), lambda i: (0, 0)),
            pl.BlockSpec((D_MODEL, D_MODEL), lambda i: (0, 0)),
        ),
        out_shape=(
            jax.ShapeDtypeStruct((D_MODEL, P_MAIN), BF16),
            jax.ShapeDtypeStruct((T_ROWS, D_MODEL), BF16),
            jax.ShapeDtypeStruct((D_MODEL, D_MODEL), BF16),
        ),
        compiler_params=pltpu.CompilerParams(
            dimension_semantics=("arbitrary",), vmem_limit_bytes=VMEM_LIMIT_BYTES),
        name="layer_weights",
    )(w_in_t, w_in_t, w_in_t, w_in_t, w_in_t, w_in_t, w_in_t, w_out)


def _prompt_bias_t(kv, first):
    j = lax.broadcasted_iota(jnp.int32, (2 * ROWS, ROWS), 0)
    i = lax.broadcasted_iota(jnp.int32, (2 * ROWS, ROWS), 1)
    diff = ROWS + i - j
    valid = (diff >= 0) & (diff < ROWS)
    if first:
        valid = valid & (j >= ROWS)
    dfl = diff.astype(F32)
    return jnp.concatenate(
        [jnp.where(valid, -_slope(kv * ATT_GROUP + g) * dfl, NEG_BIG) for g in range(ATT_GROUP)], axis=1)


def _prompt_kernel(x_ref, gpre_ref, wmain_ref, wt_ref, bg_ref, sink_ref, gmhb_ref, wout_ref, gpost_ref,
                   y_ref, wk_ref, wv_ref, c_ref, n_ref, m_ref,
                   proj_ref, projt_ref, cat_ref, kprev_ref, vtprev_ref, ct_ref, mst_ref, bias_ref, *, tb, nt):
    bi = pl.program_id(0)
    j = pl.program_id(1)
    nchunks = tb // ROWS

    @pl.when((bi == 0) & (j == 0))
    def _():
        for first in range(2):
            for kv in range(ATT_KV):
                bias_ref[first * ATT_KV + kv] = _prompt_bias_t(kv, first)

    @pl.when(j == 0)
    def _():
        kprev_ref[...] = jnp.zeros_like(kprev_ref)
        vtprev_ref[...] = jnp.zeros_like(vtprev_ref)
        ct_ref[...] = jnp.zeros_like(ct_ref)
        mst_ref[...] = jnp.zeros_like(mst_ref)

    x = x_ref[0]
    xn = _rms(x, gpre_ref[...]).astype(BF16)
    proj_ref[...] = jnp.dot(xn, wmain_ref[...], preferred_element_type=F32)
    pt = lax.dot_general(wt_ref[...], xn, NT_DIMS, preferred_element_type=F32)
    for c in range(nchunks):
        projt_ref[c] = pt[:, c * ROWS:(c + 1) * ROWS]

    ri = lax.broadcasted_iota(jnp.int32, (ROWS, ROWS), 0)
    ci = lax.broadcasted_iota(jnp.int32, (ROWS, ROWS), 1)
    mask_t = ri <= ci
    st_row = lax.broadcasted_iota(jnp.int32, (STATE_ROWS - M_DIM, ROWS), 0)

    def chunk(c, carry):
        rows = pl.ds(c * ROWS, ROWS)
        first = ((j == 0) & (c == 0)).astype(jnp.int32)

        qa = proj_ref[rows, P_QA:P_QA + 512] * ATT_SCALE
        kcur = proj_ref[rows, P_KA:P_KA + 128].astype(BF16)
        vtcur = projt_ref[c, T_VA:T_VA + 128, :].astype(BF16)
        kcat = jnp.concatenate([kprev_ref[...], kcur], axis=0)
        vtcat = jnp.concatenate([vtprev_ref[...], vtcur], axis=1)
        r = _gate_rows(projt_ref[c, T_GATES:T_GATES + 8, :] + bg_ref[...], mst_ref[...], mask_t, None)

        scores, sinks = [], []
        for kv in range(ATT_KV):
            want_hi = kv == 1
            keep = (ci >= ATT_DIM) if want_hi else (ci < ATT_DIM)
            pieces = []
            for g in range(ATT_GROUP):
                hh = kv * ATT_GROUP + g
                blk = qa[:, (hh // 2) * 128:(hh // 2 + 1) * 128]
                if (hh % 2 == 1) != want_hi:
                    blk = pltpu.roll(blk, ATT_DIM, 1)
                pieces.append(jnp.where(keep, blk, 0.0))
            q4 = jnp.concatenate(pieces, axis=0)
            scores.append(_bdot_nt(kcat, q4) + bias_ref[first * ATT_KV + kv])
            sinks.append(jnp.concatenate(
                [jnp.broadcast_to(sink_ref[0:1, kv * ATT_GROUP + g:kv * ATT_GROUP + g + 1], (1, ROWS))
                 for g in range(ATT_GROUP)], axis=1))
        vts, r1s = [], []
        for h in range(M_HEADS):
            q = proj_ref[rows, P_QM + h * M_DIM:P_QM + (h + 1) * M_DIM].astype(BF16)
            k = (proj_ref[rows, P_KM + h * M_DIM:P_KM + (h + 1) * M_DIM] * K_SCALE).astype(BF16)
            vt = projt_ref[c, T_VM + h * M_DIM:T_VM + (h + 1) * M_DIM, :]
            ct = ct_ref[h]
            r1s.append(lax.dot_general(jnp.concatenate([k, ct.astype(BF16)], axis=0), q, NT_DIMS,
                                       preferred_element_type=F32))
            w_row = r["w"][h:h + 1, :]
            lhs2 = jnp.concatenate([vt * w_row, jnp.where(st_row == 0, w_row, 0.0)], axis=0)
            ct_ref[h] = r["decay"][h:h + 1, 0:1] * ct + _bdot(lhs2, k)
            vts.append(vt)
        mst_ref[...] = r["m_new"]

        outs, nums, dens = [], [], []
        for kv in range(ATT_KV):
            s, sink = scores[kv], sinks[kv]
            mx = jnp.maximum(jnp.max(s, axis=0, keepdims=True), sink)
            p = jnp.exp(s - mx).astype(BF16)
            lhs = jnp.concatenate([vtcat[kv * ATT_DIM:(kv + 1) * ATT_DIM, :],
                                   jnp.ones((ONES_ROWS, 2 * ROWS), BF16)], axis=0)
            o = jnp.dot(lhs, p, preferred_element_type=F32)
            outs.append((o, jnp.exp(sink - mx)))
        for h in range(M_HEADS):
            r1 = r1s[h]
            dm = jnp.exp(r["a_masked"][h] + r["bm"][h:h + 1, :])
            st = r1[0:ROWS] * dm
            g_row = r["gexp"][h:h + 1, :]
            nums.append(_bdot(vts[h], st) + g_row * r1[ROWS:ROWS + M_DIM])
            dens.append(jnp.sum(st, axis=0, keepdims=True) + g_row * r1[ROWS + M_DIM:ROWS + M_DIM + 1])

        att = []
        for kv in range(ATT_KV):
            o, esink = outs[kv]
            on = o[0:ATT_DIM, :] / (o[ATT_DIM:ATT_DIM + 1, :] + esink)
            for pair in range(2):
                two = jnp.concatenate([on[:, (2 * pair) * ROWS:(2 * pair + 1) * ROWS],
                                       on[:, (2 * pair + 1) * ROWS:(2 * pair + 2) * ROWS]], axis=0)
                att.append(two.T)
        a_out = jnp.concatenate(att, axis=1) * _silu(proj_ref[rows, P_ZA:P_ZA + 512])
        kprev_ref[...] = kcur
        vtprev_ref[...] = vtcur
        m_out = []
        for h in range(M_HEADS):
            ht = nums[h] / jnp.maximum(jnp.abs(dens[h]), r["enm"][h:h + 1, :])
            hn = ht * lax.rsqrt(jnp.mean(ht * ht, axis=0, keepdims=True) + NORM_EPS) * gmhb_ref[h]
            m_out.append(jax.nn.sigmoid(proj_ref[rows, P_OM + h * M_DIM:P_OM + (h + 1) * M_DIM]) * hn.T
                         * _silu(proj_ref[rows, P_ZM + h * M_DIM:P_ZM + (h + 1) * M_DIM]))

        cat_ref[rows, :] = jnp.concatenate([a_out] + m_out, axis=1).astype(BF16)
        return carry

    for c in range(nchunks):
        chunk(c, 0)

    y_ref[0] = _out_tail(cat_ref[...], x, wout_ref, gpost_ref)

    @pl.when(j == nt - 1)
    def _():
        wk_ref[0] = proj_ref[tb - ROWS:tb, P_KA:P_KA + 128].T
        wv_ref[0] = projt_ref[nchunks - 1, T_VA:T_VA + 128, :]
        for h in range(M_HEADS):
            ct = ct_ref[h]
            c_ref[0, h] = ct[0:M_DIM].T
            n_ref[0, h:h + 1, :] = ct[M_DIM:M_DIM + 1]
        m_ref[0] = mst_ref[...]


def _prompt_call(x, gpre, wmain, wt, bg, sinks, gmhb, wout, gpost, tb):
    bsz, seq, _ = x.shape
    nt = seq // tb
    full = lambda shape: pl.BlockSpec(shape, lambda b, j: (0,) * len(shape))
    out_shapes = (
        jax.ShapeDtypeStruct((bsz, seq, D_MODEL), F32),
        jax.ShapeDtypeStruct((bsz, 128, ROWS), F32),
        jax.ShapeDtypeStruct((bsz, 128, ROWS), F32),
        jax.ShapeDtypeStruct((bsz, M_HEADS, M_DIM, M_DIM), F32),
        jax.ShapeDtypeStruct((bsz, M_HEADS, M_DIM), F32),
        jax.ShapeDtypeStruct((bsz, 8, ROWS), F32),
    )
    return pl.pallas_call(
        functools.partial(_prompt_kernel, tb=tb, nt=nt),
        grid=(bsz, nt),
        in_specs=[
            pl.BlockSpec((1, tb, D_MODEL), lambda b, j: (b, j, 0)),
            full((1, D_MODEL)), full((D_MODEL, P_MAIN)), full((T_ROWS, D_MODEL)), full((8, 1)),
            full((1, ATT_HEADS)), full((M_HEADS, M_DIM, ROWS)), full((D_MODEL, D_MODEL)), full((1, D_MODEL)),
        ],
        out_specs=(
            pl.BlockSpec((1, tb, D_MODEL), lambda b, j: (b, j, 0)),
            pl.BlockSpec((1, 128, ROWS), lambda b, j: (b, 0, 0)),
            pl.BlockSpec((1, 128, ROWS), lambda b, j: (b, 0, 0)),
            pl.BlockSpec((1, M_HEADS, M_DIM, M_DIM), lambda b, j: (b, 0, 0, 0)),
            pl.BlockSpec((1, M_HEADS, M_DIM), lambda b, j: (b, 0, 0)),
            pl.BlockSpec((1, 8, ROWS), lambda b, j: (b, 0, 0)),
        ),
        out_shape=out_shapes,
        scratch_shapes=[
            pltpu.VMEM((tb, P_MAIN), F32),
            pltpu.VMEM((tb // ROWS, T_ROWS, ROWS), F32),
            pltpu.VMEM((tb, D_MODEL), BF16),
            pltpu.VMEM((ROWS, 128), BF16),
            pltpu.VMEM((128, ROWS), BF16),
            pltpu.VMEM((M_HEADS, STATE_ROWS, M_DIM), F32),
            pltpu.VMEM((8, ROWS), F32),
            pltpu.VMEM((2 * ATT_KV, 2 * ROWS, ATT_GROUP * ROWS), F32),
        ],
        compiler_params=pltpu.CompilerParams(
            dimension_semantics=("arbitrary", "arbitrary"), vmem_limit_bytes=VMEM_LIMIT_BYTES),
        name="prompt_layer",
    )(x, gpre, wmain, wt, bg, sinks, gmhb, wout, gpost)


def _sample_bias_new(head):
    r = lax.broadcasted_iota(jnp.int32, (ROWS, ROWS), 0)
    c = lax.broadcasted_iota(jnp.int32, (ROWS, ROWS), 1)
    valid = ((r >> 3) == (c >> 3)) & (r >= c)
    return jnp.where(valid, -_slope(head) * (r - c).astype(F32), NEG_BIG)


def _sample_bias_cache(head):
    r = lax.broadcasted_iota(jnp.int32, (ROWS, ROWS), 0)
    c = lax.broadcasted_iota(jnp.int32, (ROWS, ROWS), 1)
    diff = (r & (SAMPLE_SEQ - 1)) + ROWS - c
    return jnp.where(diff < ROWS, -_slope(head) * diff.astype(F32), NEG_BIG)


def _sample_kernel(x_ref, kct_ref, vct_ref, cin_ref, nin_ref, m0_ref,
                   gpre_ref, wmain_ref, wt_ref, bg_ref, sink_ref, gmhb_ref, wout_ref, gpost_ref,
                   y_ref, kot_ref, vot_ref, cout_ref, nout_ref, mout_ref,
                   proj_ref, qh_ref, sc_ref, oc_ref, qc_ref, kat_ref, vat_ref, wv_ref, kt_ref, decb_ref,
                   bn_ref, bc_ref):
    @pl.when(pl.program_id(0) == 0)
    def _():
        for hh in range(ATT_HEADS):
            bn_ref[hh] = _sample_bias_new(hh)
            bc_ref[hh] = _sample_bias_cache(hh)

    x = x_ref[...]
    xn = _rms(x, gpre_ref[...]).astype(BF16)
    proj_ref[...] = jnp.dot(xn, wmain_ref[...], preferred_element_type=F32)
    pt = lax.dot_general(wt_ref[...], xn, NT_DIMS, preferred_element_type=F32)

    ri = lax.broadcasted_iota(jnp.int32, (ROWS, ROWS), 0)
    ci = lax.broadcasted_iota(jnp.int32, (ROWS, ROWS), 1)
    same_seq = (ri >> 3) == (ci >> 3)
    mask_t = same_seq & (ri <= ci)
    last_sel = (same_seq & ((ri & (SAMPLE_SEQ - 1)) == SAMPLE_SEQ - 1)).astype(BF16)
    row16 = lax.broadcasted_iota(jnp.int32, (SAMPLE_GROUP, ROWS), 0)
    lane16 = lax.broadcasted_iota(jnp.int32, (SAMPLE_GROUP, ROWS), 1)
    seq_of_lane = (lane16 >> 3) == row16
    ones_rows = jnp.ones((ONES_ROWS, ROWS), F32)

    qa = proj_ref[:, P_QA:P_QA + 512] * ATT_SCALE
    for hh in range(ATT_HEADS):
        blk = qa[:, (hh // 2) * 128:(hh // 2 + 1) * 128]
        if hh % 2 == 1:
            blk = pltpu.roll(blk, ATT_DIM, 1)
        qh_ref[hh] = blk[:, 0:ATT_DIM]
    ka = proj_ref[:, P_KA:P_KA + 128]
    kat = ka.T
    vat = pt[T_VA:T_VA + 128, :]
    kat_ref[...] = kat
    vat_ref[...] = vat

    def seq_scores(b, carry):
        rows = pl.ds(pl.multiple_of(b * SAMPLE_SEQ, SAMPLE_SEQ), SAMPLE_SEQ)
        kct = kct_ref[b]
        for kv in range(ATT_KV):
            lhs = jnp.concatenate([qh_ref[kv * ATT_GROUP + g, rows, :] for g in range(ATT_GROUP)], axis=0)
            res = _bdot(lhs, kct[kv * ATT_DIM:(kv + 1) * ATT_DIM, :])
            for g in range(ATT_GROUP):
                sc_ref[kv * ATT_GROUP + g, rows, :] = res[g * SAMPLE_SEQ:(g + 1) * SAMPLE_SEQ]
        for h in range(M_HEADS):
            qc_ref[h, rows, :] = _bdot(proj_ref[rows, P_QM + h * M_DIM:P_QM + (h + 1) * M_DIM], cin_ref[b, h])
        return carry

    lax.fori_loop(0, SAMPLE_GROUP, seq_scores, 0)

    o_new, esinks = [], []
    for hh in range(ATT_HEADS):
        kv = hh // ATT_GROUP
        s_n = _bdot(qh_ref[hh], kat[kv * ATT_DIM:(kv + 1) * ATT_DIM, :]) + bn_ref[hh]
        s_c = sc_ref[hh] + bc_ref[hh]
        sink = sink_ref[0:1, hh:hh + 1]
        mx = jnp.maximum(jnp.maximum(jnp.max(s_n, axis=-1, keepdims=True),
                                     jnp.max(s_c, axis=-1, keepdims=True)), sink)
        p_n = jnp.exp(s_n - mx)
        sc_ref[hh] = jnp.exp(s_c - mx)
        vaug = jnp.concatenate([vat[kv * ATT_DIM:(kv + 1) * ATT_DIM, :], ones_rows], axis=0)
        o_new.append(_bdot_nt(p_n, vaug))
        esinks.append(jnp.exp(sink - mx))

    r = _gate_rows(pt[T_GATES:T_GATES + 8, :] + bg_ref[...], m0_ref[0], mask_t, last_sel)
    mout_ref[0] = r["m_new"]
    m_out = []
    for h in range(M_HEADS):
        q = proj_ref[:, P_QM + h * M_DIM:P_QM + (h + 1) * M_DIM].astype(BF16)
        kf = proj_ref[:, P_KM + h * M_DIM:P_KM + (h + 1) * M_DIM] * K_SCALE
        k = kf.astype(BF16)
        vt = pt[T_VM + h * M_DIM:T_VM + (h + 1) * M_DIM, :]
        n_h = nin_ref[h]
        r1 = lax.dot_general(jnp.concatenate([k, n_h.astype(BF16)], axis=0), q, NT_DIMS,
                             preferred_element_type=F32)
        st = r1[0:ROWS] * jnp.exp(r["a_masked"][h] + r["bm"][h:h + 1, :])
        q_n = jnp.sum(jnp.where(seq_of_lane, r1[ROWS:ROWS + SAMPLE_GROUP], 0.0), axis=0, keepdims=True)
        g_row = r["gexp"][h:h + 1, :]
        num = _bdot(vt, st) + g_row * qc_ref[h].T
        den = jnp.sum(st, axis=0, keepdims=True) + g_row * q_n
        ht = num / jnp.maximum(jnp.abs(den), r["enm"][h:h + 1, :])
        hn = ht * lax.rsqrt(jnp.mean(ht * ht, axis=0, keepdims=True) + NORM_EPS) * gmhb_ref[h]
        m_out.append(jax.nn.sigmoid(proj_ref[:, P_OM + h * M_DIM:P_OM + (h + 1) * M_DIM]) * hn.T
                     * _silu(proj_ref[:, P_ZM + h * M_DIM:P_ZM + (h + 1) * M_DIM]))
        w_row = r["w"][h:h + 1, :]
        dec16 = jnp.sum(jnp.where(lane16 == row16 * SAMPLE_SEQ, r["decay"][h:h + 1, :], 0.0),
                        axis=1, keepdims=True)
        nout_ref[h] = dec16 * n_h + _bdot(jnp.where(seq_of_lane, w_row, 0.0), k)
        decb_ref[h] = jnp.broadcast_to(dec16, (SAMPLE_GROUP, ROWS))
        wv_ref[h] = (vt * w_row).T
        kt_ref[h] = kf.T.astype(BF16)

    keep_new = ci >= ROWS - SAMPLE_SEQ

    def seq_update(b, carry):
        rows = pl.ds(pl.multiple_of(b * SAMPLE_SEQ, SAMPLE_SEQ), SAMPLE_SEQ)
        vct = vct_ref[b]
        for kv in range(ATT_KV):
            vaug = jnp.concatenate([vct[kv * ATT_DIM:(kv + 1) * ATT_DIM, :], ones_rows], axis=0)
            pl_ = jnp.concatenate([sc_ref[kv * ATT_GROUP + g, rows, :] for g in range(ATT_GROUP)], axis=0)
            res = _bdot_nt(pl_, vaug)
            for g in range(ATT_GROUP):
                oc_ref[kv * ATT_GROUP + g, rows, 0:ATT_DIM + ONES_ROWS] = res[g * SAMPLE_SEQ:(g + 1) * SAMPLE_SEQ]
        in_seq = (ri >> 3) == b
        for h in range(M_HEADS):
            upd = jnp.dot(kt_ref[h], jnp.where(in_seq, wv_ref[h], 0.0).astype(BF16), preferred_element_type=F32)
            cout_ref[b, h] = decb_ref[h, pl.ds(b, 1), :] * cin_ref[b, h] + upd
        shift = (ROWS - SAMPLE_SEQ - b * SAMPLE_SEQ) & (ROWS - 1)
        kot_ref[b] = jnp.where(keep_new, pltpu.roll(kat_ref[...], shift, 1),
                               pltpu.roll(kct_ref[b], ROWS - SAMPLE_SEQ, 1))
        vot_ref[b] = jnp.where(keep_new, pltpu.roll(vat_ref[...], shift, 1),
                               pltpu.roll(vct, ROWS - SAMPLE_SEQ, 1))
        return carry

    lax.fori_loop(0, SAMPLE_GROUP, seq_update, 0)

    att = []
    for pair in range(ATT_HEADS // 2):
        halves = []
        for hh in (2 * pair, 2 * pair + 1):
            on, oc = o_new[hh], oc_ref[hh]
            den = on[:, ATT_DIM:ATT_DIM + 1] + oc[:, ATT_DIM:ATT_DIM + 1] + esinks[hh]
            halves.append((on[:, 0:ATT_DIM] + oc[:, 0:ATT_DIM]) / den)
        att.append(jnp.concatenate(halves, axis=1))
    a_out = jnp.concatenate(att, axis=1) * _silu(proj_ref[:, P_ZA:P_ZA + 512])
    cat = jnp.concatenate([a_out] + m_out, axis=1).astype(BF16)
    y_ref[...] = _out_tail(cat, x, wout_ref, gpost_ref)


def _sample_call(x, kct, vct, cin, nin, m0, gpre, wmain, wt, bg, sinks, gmhb, wout, gpost):
    nrows = x.shape[0]
    ngroups = nrows // ROWS
    nseq = ngroups * SAMPLE_GROUP
    full = lambda shape: pl.BlockSpec(shape, lambda i: (0,) * len(shape))
    grp = SAMPLE_GROUP
    out_shapes = (
        jax.ShapeDtypeStruct((nrows, D_MODEL), F32),
        jax.ShapeDtypeStruct((nseq, 128, ROWS), F32),
        jax.ShapeDtypeStruct((nseq, 128, ROWS), F32),
        jax.ShapeDtypeStruct((nseq, M_HEADS, M_DIM, M_DIM), F32),
        jax.ShapeDtypeStruct((M_HEADS, nseq, M_DIM), F32),
        jax.ShapeDtypeStruct((ngroups, 8, ROWS), F32),
    )
    return pl.pallas_call(
        _sample_kernel,
        grid=(ngroups,),
        in_specs=[
            pl.BlockSpec((ROWS, D_MODEL), lambda i: (i, 0)),
            pl.BlockSpec((grp, 128, ROWS), lambda i: (i, 0, 0)),
            pl.BlockSpec((grp, 128, ROWS), lambda i: (i, 0, 0)),
            pl.BlockSpec((grp, M_HEADS, M_DIM, M_DIM), lambda i: (i, 0, 0, 0)),
            pl.BlockSpec((M_HEADS, grp, M_DIM), lambda i: (0, i, 0)),
            pl.BlockSpec((1, 8, ROWS), lambda i: (i, 0, 0)),
            full((1, D_MODEL)), full((D_MODEL, P_MAIN)), full((T_ROWS, D_MODEL)), full((8, 1)),
            full((1, ATT_HEADS)), full((M_HEADS, M_DIM, ROWS)), full((D_MODEL, D_MODEL)), full((1, D_MODEL)),
        ],
        out_specs=(
            pl.BlockSpec((ROWS, D_MODEL), lambda i: (i, 0)),
            pl.BlockSpec((grp, 128, ROWS), lambda i: (i, 0, 0)),
            pl.BlockSpec((grp, 128, ROWS), lambda i: (i, 0, 0)),
            pl.BlockSpec((grp, M_HEADS, M_DIM, M_DIM), lambda i: (i, 0, 0, 0)),
            pl.BlockSpec((M_HEADS, grp, M_DIM), lambda i: (0, i, 0)),
            pl.BlockSpec((1, 8, ROWS), lambda i: (i, 0, 0)),
        ),
        out_shape=out_shapes,
        scratch_shapes=[
            pltpu.VMEM((ROWS, P_MAIN), F32),
            pltpu.VMEM((ATT_HEADS, ROWS, ATT_DIM), F32),
            pltpu.VMEM((ATT_HEADS, ROWS, ROWS), F32),
            pltpu.VMEM((ATT_HEADS, ROWS, ROWS), F32),
            pltpu.VMEM((M_HEADS, ROWS, M_DIM), F32),
            pltpu.VMEM((128, ROWS), F32),
            pltpu.VMEM((128, ROWS), F32),
            pltpu.VMEM((M_HEADS, ROWS, M_DIM), F32),
            pltpu.VMEM((M_HEADS, M_DIM, ROWS), BF16),
            pltpu.VMEM((M_HEADS, SAMPLE_GROUP, ROWS), F32),
            pltpu.VMEM((ATT_HEADS, ROWS, ROWS), F32),
            pltpu.VMEM((ATT_HEADS, ROWS, ROWS), F32),
        ],
        compiler_params=pltpu.CompilerParams(
            dimension_semantics=("arbitrary",), vmem_limit_bytes=VMEM_LIMIT_BYTES),
        name="sample_layer",
    )(x, kct, vct, cin, nin, m0, gpre, wmain, wt, bg, sinks, gmhb, wout, gpost)


PROMPT_BLOCK = 512


def _window_in(cache):
    nseq = cache.shape[0]
    return cache.transpose(0, 2, 3, 1).reshape(nseq, ATT_KV * ATT_DIM, ROWS)


def _window_out(win_t):
    nseq = win_t.shape[0]
    return win_t.reshape(nseq, ATT_KV, ATT_DIM, ROWS).transpose(0, 3, 1, 2)[None]


def kernel(x_prompt, x_sample, cache_win_k, cache_win_v, state_C, state_n, state_m,
           g_pre, w_in, b_gate, attn_sinks, g_mh, w_out, g_post):
    depth = g_pre.shape[0]
    assert depth == 1, "single-layer trunk"
    nseq, sseq, _ = x_sample.shape
    assert sseq == SAMPLE_SEQ and nseq % SAMPLE_GROUP == 0

    gpre = g_pre[0].reshape(1, D_MODEL)
    gpost = g_post[0].reshape(1, D_MODEL)
    wmain, wt, wout = _weights_call(w_in[0].T, w_out[0])
    bg = b_gate[0].reshape(2 * M_HEADS, 1)
    sinks = attn_sinks[0].reshape(1, ATT_HEADS)
    gmhb = jnp.broadcast_to(g_mh[0][:, :, None], (M_HEADS, M_DIM, ROWS))

    yp, wkp, wvp, cp, np_, mp = _prompt_call(x_prompt, gpre, wmain, wt, bg, sinks, gmhb, wout, gpost, PROMPT_BLOCK)

    ngroups = nseq // SAMPLE_GROUP
    m0 = jnp.broadcast_to(state_m[0].reshape(ngroups, SAMPLE_GROUP, 1, M_HEADS),
                          (ngroups, SAMPLE_GROUP, SAMPLE_SEQ, M_HEADS))
    m0 = jnp.pad(m0.reshape(ngroups, ROWS, M_HEADS).transpose(0, 2, 1), ((0, 0), (0, 8 - M_HEADS), (0, 0)))
    ys, wks, wvs, cs, ns, ms = _sample_call(
        x_sample.reshape(nseq * sseq, D_MODEL), _window_in(cache_win_k[0]), _window_in(cache_win_v[0]),
        state_C[0], state_n[0].transpose(1, 0, 2), m0,
        gpre, wmain, wt, bg, sinks, gmhb, wout, gpost)

    ms = ms[:, 0:M_HEADS, ::SAMPLE_SEQ].transpose(0, 2, 1).reshape(nseq, M_HEADS)
    return (yp, ys.reshape(nseq, sseq, D_MODEL), _window_out(wkp), _window_out(wvp),
            cp[None], np_[None], mp[:, 0:M_HEADS, 0][None],
            _window_out(wks), _window_out(wvs), cs[None], ns.transpose(1, 0, 2)[None], ms[None])
```

```python
import functools

import jax
import jax.numpy as jnp
from jax import lax
from jax.experimental import pallas as pl
from jax.experimental.pallas import tpu as pltpu

F32 = jnp.float32
BF16 = jnp.bfloat16

D_MODEL = 1024
ROWS = 128
ATT_HEADS, ATT_KV, ATT_GROUP, ATT_DIM = 8, 2, 4, 64
M_HEADS, M_DIM = 4, 128
NORM_EPS = 1e-6
NEG_BIG = -1e30
ATT_SCALE = ATT_DIM ** -0.5
K_SCALE = M_DIM ** -0.5

QA, KA, VA, ZA = 0, 512, 640, 768
QM, KM, VM, OM, ZM = 1280, 1792, 2304, 2816, 3328
GATES = 3840
D_IN = 3848

P_QA, P_KA, P_ZA, P_QM, P_KM, P_OM, P_ZM = 0, 512, 640, 1152, 1664, 2176, 2688
P_MAIN = 3200
T_VA, T_VM, T_GATES = 0, 128, 640
T_ROWS = 656
STATE_ROWS = 144
ONES_ROWS = 16

SAMPLE_SEQ = 8
SAMPLE_GROUP = ROWS // SAMPLE_SEQ
SEQ_UNROLL = 4

VMEM_LIMIT_BYTES = 56 * 1024 * 1024
NT_DIMS = (((1,), (1,)), ((), ()))


def _rms(x, g):
    return x * lax.rsqrt(jnp.mean(x * x, axis=-1, keepdims=True) + NORM_EPS) * g


def _silu(x):
    return x * jax.nn.sigmoid(x)


def _log_sigmoid(x):
    return -(jnp.maximum(-x, 0.0) + jnp.log1p(jnp.exp(-jnp.abs(x))))


def _slope(head):
    return 2.0 ** -(head + 1)


def _bdot(a, b):
    return jnp.dot(a.astype(BF16), b.astype(BF16), preferred_element_type=F32)


def _bdot_nt(a, b):
    return lax.dot_general(a.astype(BF16), b.astype(BF16), NT_DIMS, preferred_element_type=F32)


def _exact_dot(x, m):
    hi = x.astype(BF16).astype(F32)
    mid = (x - hi).astype(BF16).astype(F32)
    lo = (x - hi - mid).astype(BF16).astype(F32)
    parts = jnp.dot(jnp.concatenate([hi, mid, lo, jnp.zeros_like(hi)], axis=0).astype(BF16), m,
                    preferred_element_type=F32)
    return parts[0:8] + parts[8:16] + parts[16:24]


def _gate_rows(x, m0, mask_t, last_sel):
    row = lax.broadcasted_iota(jnp.int32, (8, ROWS), 0)
    head_rows = row < M_HEADS
    ic = jnp.where(head_rows, x, 0.0)
    fc = jnp.where(head_rows, _log_sigmoid(pltpu.roll(x, M_HEADS, 0)), 0.0)
    b = _exact_dot(fc, mask_t.astype(BF16))
    a = ic - b
    a_cols = jnp.concatenate([a, jnp.zeros((ROWS - 8, ROWS), F32)], axis=0).T
    a_masked = [jnp.where(mask_t, a_cols[:, h:h + 1], -jnp.inf) for h in range(M_HEADS)]
    cm = jnp.concatenate([jnp.max(am, axis=0, keepdims=True) for am in a_masked]
                         + [jnp.zeros((8 - M_HEADS, ROWS), F32)], axis=0)
    m_t = jnp.maximum(b + m0, b + cm)
    if last_sel is None:
        b_last = jnp.broadcast_to(b[:, ROWS - 1:ROWS], b.shape)
        m_new = jnp.broadcast_to(m_t[:, ROWS - 1:ROWS], b.shape)
    else:
        both = _exact_dot(jnp.where(head_rows, b, pltpu.roll(m_t, M_HEADS, 0)), last_sel)
        b_last = jnp.where(head_rows, both, 0.0)
        m_new = jnp.where(head_rows, pltpu.roll(both, M_HEADS, 0), 0.0)
    return dict(a_masked=a_masked, bm=b - m_t, gexp=jnp.exp(b + m0 - m_t), enm=jnp.exp(-m_t), m_new=m_new,
                w=jnp.exp(b_last - b + ic - m_new), decay=jnp.exp(b_last + m0 - m_new))


def _out_tail(cat, x, wout_ref, gpost_ref):
    y = jnp.dot(cat, wout_ref[...], preferred_element_type=F32)
    return x + _rms(y, gpost_ref[...])


MAIN_BLOCKS = P_MAIN // 128


def _main_src_block(i):
    return jnp.where(i < 5, i, jnp.where(i < 17, i + 1, i + 5))


def _weights_kernel(main_ref, va_ref, vm0_ref, vm1_ref, vm2_ref, vm3_ref, gates_ref, wout_ref,
                    wmain_ref, wt_ref, woutb_ref):
    wmain_ref[...] = main_ref[...].T.astype(BF16)

    @pl.when(pl.program_id(0) == 0)
    def _():
        wt_ref[...] = jnp.concatenate(
            [va_ref[...], vm0_ref[...], vm1_ref[...], vm2_ref[...], vm3_ref[...], gates_ref[...],
             jnp.zeros((T_ROWS - T_GATES - 8, D_MODEL), F32)], axis=0).astype(BF16)
        woutb_ref[...] = wout_ref[...].astype(BF16)


def _weights_call(w_in_t, w_out):
    blk = lambda r: pl.BlockSpec((128, D_MODEL), lambda i, r=r: (r, 0))
    return pl.pallas_call(
        _weights_kernel,
        grid=(MAIN_BLOCKS,),
        in_specs=[
            pl.BlockSpec((128, D_MODEL), lambda i: (_main_src_block(i), 0)),
            blk(VA // 128), blk(VM // 128), blk(VM // 128 + 1), blk(VM // 128 + 2), blk(VM // 128 + 3),
            pl.BlockSpec((8, D_MODEL), lambda i: (GATES // 8, 0)),
            pl.BlockSpec((D_MODEL, D_MODEL), lambda i: (0, 0)),
        ],
        out_specs=(
            pl.BlockSpec((D_MODEL, 128), lambda i: (0, i)),
            pl.BlockSpec((T_ROWS, D_MODEL), lambda i: (0, 0)),
            pl.BlockSpec((D_MODEL, D_MODEL), lambda i: (0, 0)),
        ),
        out_shape=(
            jax.ShapeDtypeStruct((D_MODEL, P_MAIN), BF16),
            jax.ShapeDtypeStruct((T_ROWS, D_MODEL), BF16),
            jax.ShapeDtypeStruct((D_MODEL, D_MODEL), BF16),
        ),
        compiler_params=pltpu.CompilerParams(
            dimension_semantics=("arbitrary",), vmem_limit_bytes=VMEM_LIMIT_BYTES),
        name="layer_weights",
    )(w_in_t, w_in_t, w_in_t, w_in_t, w_in_t, w_in_t, w_in_t, w_out)


def _prompt_bias_t(kv, first):
    j = lax.broadcasted_iota(jnp.int32, (2 * ROWS, ROWS), 0)
    i = lax.broadcasted_iota(jnp.int32, (2 * ROWS, ROWS), 1)
    diff = ROWS + i - j
    valid = (diff >= 0) & (diff < ROWS)
    if first:
        valid = valid & (j >= ROWS)
    dfl = diff.astype(F32)
    return jnp.concatenate(
        [jnp.where(valid, -_slope(kv * ATT_GROUP + g) * dfl, NEG_BIG) for g in range(ATT_GROUP)], axis=1)


def _prompt_kernel(x_ref, gpre_ref, wmain_ref, wt_ref, bg_ref, sink_ref, gmhb_ref, wout_ref, gpost_ref,
                   y_ref, wk_ref, wv_ref, c_ref, n_ref, m_ref,
                   proj_ref, projt_ref, cat_ref, kprev_ref, vtprev_ref, ct_ref, mst_ref, bias_ref, *, tb, nt):
    bi = pl.program_id(0)
    j = pl.program_id(1)
    nchunks = tb // ROWS

    @pl.when((bi == 0) & (j == 0))
    def _():
        for first in range(2):
            for kv in range(ATT_KV):
                bias_ref[first * ATT_KV + kv] = _prompt_bias_t(kv, first)

    @pl.when(j == 0)
    def _():
        kprev_ref[...] = jnp.zeros_like(kprev_ref)
        vtprev_ref[...] = jnp.zeros_like(vtprev_ref)
        ct_ref[...] = jnp.zeros_like(ct_ref)
        mst_ref[...] = jnp.zeros_like(mst_ref)

    x = x_ref[0]
    xn = _rms(x, gpre_ref[...]).astype(BF16)
    proj_ref[...] = jnp.dot(xn, wmain_ref[...], preferred_element_type=F32)
    pt = lax.dot_general(wt_ref[...], xn, NT_DIMS, preferred_element_type=F32)
    for c in range(nchunks):
        projt_ref[c] = pt[:, c * ROWS:(c + 1) * ROWS]

    ri = lax.broadcasted_iota(jnp.int32, (ROWS, ROWS), 0)
    ci = lax.broadcasted_iota(jnp.int32, (ROWS, ROWS), 1)
    mask_t = ri <= ci
    st_row = lax.broadcasted_iota(jnp.int32, (STATE_ROWS - M_DIM, ROWS), 0)

    def chunk(c, carry):
        rows = pl.ds(c * ROWS, ROWS)
        first = ((j == 0) & (c == 0)).astype(jnp.int32)

        qa = proj_ref[rows, P_QA:P_QA + 512] * ATT_SCALE
        kcur = proj_ref[rows, P_KA:P_KA + 128].astype(BF16)
        vtcur = projt_ref[c, T_VA:T_VA + 128, :].astype(BF16)
        kcat = jnp.concatenate([kprev_ref[...], kcur], axis=0)
        vtcat = jnp.concatenate([vtprev_ref[...], vtcur], axis=1)
        r = _gate_rows(projt_ref[c, T_GATES:T_GATES + 8, :] + bg_ref[...], mst_ref[...], mask_t, None)

        scores, sinks = [], []
        for kv in range(ATT_KV):
            want_hi = kv == 1
            keep = (ci >= ATT_DIM) if want_hi else (ci < ATT_DIM)
            pieces = []
            for g in range(ATT_GROUP):
                hh = kv * ATT_GROUP + g
                blk = qa[:, (hh // 2) * 128:(hh // 2 + 1) * 128]
                if (hh % 2 == 1) != want_hi:
                    blk = pltpu.roll(blk, ATT_DIM, 1)
                pieces.append(jnp.where(keep, blk, 0.0))
            q4 = jnp.concatenate(pieces, axis=0)
            scores.append(_bdot_nt(kcat, q4) + bias_ref[first * ATT_KV + kv])
            sinks.append(jnp.concatenate(
                [jnp.broadcast_to(sink_ref[0:1, kv * ATT_GROUP + g:kv * ATT_GROUP + g + 1], (1, ROWS))
                 for g in range(ATT_GROUP)], axis=1))
        vts, r1s = [], []
        for h in range(M_HEADS):
            q = proj_ref[rows, P_QM + h * M_DIM:P_QM + (h + 1) * M_DIM].astype(BF16)
            k = (proj_ref[rows, P_KM + h * M_DIM:P_KM + (h + 1) * M_DIM] * K_SCALE).astype(BF16)
            vt = projt_ref[c, T_VM + h * M_DIM:T_VM + (h + 1) * M_DIM, :]
            ct = ct_ref[h]
            r1s.append(lax.dot_general(jnp.concatenate([k, ct.astype(BF16)], axis=0), q, NT_DIMS,
                                       preferred_element_type=F32))
            w_row = r["w"][h:h + 1, :]
            lhs2 = jnp.concatenate([vt * w_row, jnp.where(st_row == 0, w_row, 0.0)], axis=0)
            ct_ref[h] = r["decay"][h:h + 1, 0:1] * ct + _bdot(lhs2, k)
            vts.append(vt)
        mst_ref[...] = r["m_new"]

        outs, nums, dens = [], [], []
        for kv in range(ATT_KV):
            s, sink = scores[kv], sinks[kv]
            mx = jnp.maximum(jnp.max(s, axis=0, keepdims=True), sink)
            p = jnp.exp(s - mx).astype(BF16)
            lhs = jnp.concatenate([vtcat[kv * ATT_DIM:(kv + 1) * ATT_DIM, :],
                                   jnp.ones((ONES_ROWS, 2 * ROWS), BF16)], axis=0)
            o = jnp.dot(lhs, p, preferred_element_type=F32)
            outs.append((o, jnp.exp(sink - mx)))
        for h in range(M_HEADS):
            r1 = r1s[h]
            dm = jnp.exp(r["a_masked"][h] + r["bm"][h:h + 1, :])
            st = r1[0:ROWS] * dm
            g_row = r["gexp"][h:h + 1, :]
            nums.append(_bdot(vts[h], st) + g_row * r1[ROWS:ROWS + M_DIM])
            dens.append(jnp.sum(st, axis=0, keepdims=True) + g_row * r1[ROWS + M_DIM:ROWS + M_DIM + 1])

        att = []
        for kv in range(ATT_KV):
            o, esink = outs[kv]
            on = o[0:ATT_DIM, :] / (o[ATT_DIM:ATT_DIM + 1, :] + esink)
            for pair in range(2):
                two = jnp.concatenate([on[:, (2 * pair) * ROWS:(2 * pair + 1) * ROWS],
                                       on[:, (2 * pair + 1) * ROWS:(2 * pair + 2) * ROWS]], axis=0)
                att.append(two.T)
        a_out = jnp.concatenate(att, axis=1) * _silu(proj_ref[rows, P_ZA:P_ZA + 512])
        kprev_ref[...] = kcur
        vtprev_ref[...] = vtcur
        m_out = []
        for h in range(M_HEADS):
            ht = nums[h] / jnp.maximum(jnp.abs(dens[h]), r["enm"][h:h + 1, :])
            hn = ht * lax.rsqrt(jnp.mean(ht * ht, axis=0, keepdims=True) + NORM_EPS) * gmhb_ref[h]
            m_out.append(jax.nn.sigmoid(proj_ref[rows, P_OM + h * M_DIM:P_OM + (h + 1) * M_DIM]) * hn.T
                         * _silu(proj_ref[rows, P_ZM + h * M_DIM:P_ZM + (h + 1) * M_DIM]))

        cat_ref[rows, :] = jnp.concatenate([a_out] + m_out, axis=1).astype(BF16)
        return carry

    for c in range(nchunks):
        chunk(c, 0)

    y_ref[0] = _out_tail(cat_ref[...], x, wout_ref, gpost_ref)

    @pl.when(j == nt - 1)
    def _():
        wk_ref[0] = proj_ref[tb - ROWS:tb, P_KA:P_KA + 128].T
        wv_ref[0] = projt_ref[nchunks - 1, T_VA:T_VA + 128, :]
        for h in range(M_HEADS):
            ct = ct_ref[h]
            c_ref[0, h] = ct[0:M_DIM].T
            n_ref[0, h:h + 1, :] = ct[M_DIM:M_DIM + 1]
        m_ref[0] = mst_ref[...]


def _prompt_call(x, gpre, wmain, wt, bg, sinks, gmhb, wout, gpost, tb):
    bsz, seq, _ = x.shape
    nt = seq // tb
    full = lambda shape: pl.BlockSpec(shape, lambda b, j: (0,) * len(shape))
    out_shapes = (
        jax.ShapeDtypeStruct((bsz, seq, D_MODEL), F32),
        jax.ShapeDtypeStruct((bsz, 128, ROWS), F32),
        jax.ShapeDtypeStruct((bsz, 128, ROWS), F32),
        jax.ShapeDtypeStruct((bsz, M_HEADS, M_DIM, M_DIM), F32),
        jax.ShapeDtypeStruct((bsz, M_HEADS, M_DIM), F32),
        jax.ShapeDtypeStruct((bsz, 8, ROWS), F32),
    )
    return pl.pallas_call(
        functools.partial(_prompt_kernel, tb=tb, nt=nt),
        grid=(bsz, nt),
        in_specs=[
            pl.BlockSpec((1, tb, D_MODEL), lambda b, j: (b, j, 0)),
            full((1, D_MODEL)), full((D_MODEL, P_MAIN)), full((T_ROWS, D_MODEL)), full((8, 1)),
            full((1, ATT_HEADS)), full((M_HEADS, M_DIM, ROWS)), full((D_MODEL, D_MODEL)), full((1, D_MODEL)),
        ],
        out_specs=(
            pl.BlockSpec((1, tb, D_MODEL), lambda b, j: (b, j, 0)),
            pl.BlockSpec((1, 128, ROWS), lambda b, j: (b, 0, 0)),
            pl.BlockSpec((1, 128, ROWS), lambda b, j: (b, 0, 0)),
            pl.BlockSpec((1, M_HEADS, M_DIM, M_DIM), lambda b, j: (b, 0, 0, 0)),
            pl.BlockSpec((1, M_HEADS, M_DIM), lambda b, j: (b, 0, 0)),
            pl.BlockSpec((1, 8, ROWS), lambda b, j: (b, 0, 0)),
        ),
        out_shape=out_shapes,
        scratch_shapes=[
            pltpu.VMEM((tb, P_MAIN), F32),
            pltpu.VMEM((tb // ROWS, T_ROWS, ROWS), F32),
            pltpu.VMEM((tb, D_MODEL), BF16),
            pltpu.VMEM((ROWS, 128), BF16),
            pltpu.VMEM((128, ROWS), BF16),
            pltpu.VMEM((M_HEADS, STATE_ROWS, M_DIM), F32),
            pltpu.VMEM((8, ROWS), F32),
            pltpu.VMEM((2 * ATT_KV, 2 * ROWS, ATT_GROUP * ROWS), F32),
        ],
        compiler_params=pltpu.CompilerParams(
            dimension_semantics=("arbitrary", "arbitrary"), vmem_limit_bytes=VMEM_LIMIT_BYTES),
        name="prompt_layer",
    )(x, gpre, wmain, wt, bg, sinks, gmhb, wout, gpost)


def _sample_bias_new(head):
    r = lax.broadcasted_iota(jnp.int32, (ROWS, ROWS), 0)
    c = lax.broadcasted_iota(jnp.int32, (ROWS, ROWS), 1)
    valid = ((r >> 3) == (c >> 3)) & (r >= c)
    return jnp.where(valid, -_slope(head) * (r - c).astype(F32), NEG_BIG)


def _sample_bias_cache(head):
    r = lax.broadcasted_iota(jnp.int32, (ROWS, ROWS), 0)
    c = lax.broadcasted_iota(jnp.int32, (ROWS, ROWS), 1)
    diff = (r & (SAMPLE_SEQ - 1)) + ROWS - c
    return jnp.where(diff < ROWS, -_slope(head) * diff.astype(F32), NEG_BIG)


def _sample_kernel(x_ref, kct_ref, vct_ref, cin_ref, nin_ref, m0_ref,
                   gpre_ref, wmain_ref, wt_ref, bg_ref, sink_ref, gmhb_ref, wout_ref, gpost_ref,
                   y_ref, kot_ref, vot_ref, cout_ref, nout_ref, mout_ref,
                   proj_ref, qh_ref, sc_ref, oc_ref, qc_ref, kat_ref, vat_ref, wv_ref, kt_ref, decb_ref,
                   bn_ref, bc_ref):
    @pl.when(pl.program_id(0) == 0)
    def _():
        for hh in range(ATT_HEADS):
            bn_ref[hh] = _sample_bias_new(hh)
            bc_ref[hh] = _sample_bias_cache(hh)

    x = x_ref[...]
    xn = _rms(x, gpre_ref[...]).astype(BF16)
    proj_ref[...] = jnp.dot(xn, wmain_ref[...], preferred_element_type=F32)
    pt = lax.dot_general(wt_ref[...], xn, NT_DIMS, preferred_element_type=F32)

    ri = lax.broadcasted_iota(jnp.int32, (ROWS, ROWS), 0)
    ci = lax.broadcasted_iota(jnp.int32, (ROWS, ROWS), 1)
    same_seq = (ri >> 3) == (ci >> 3)
    mask_t = same_seq & (ri <= ci)
    last_sel = (same_seq & ((ri & (SAMPLE_SEQ - 1)) == SAMPLE_SEQ - 1)).astype(BF16)
    row16 = lax.broadcasted_iota(jnp.int32, (SAMPLE_GROUP, ROWS), 0)
    lane16 = lax.broadcasted_iota(jnp.int32, (SAMPLE_GROUP, ROWS), 1)
    seq_of_lane = (lane16 >> 3) == row16
    ones_rows = jnp.ones((ONES_ROWS, ROWS), F32)

    qa = proj_ref[:, P_QA:P_QA + 512] * ATT_SCALE
    for hh in range(ATT_HEADS):
        blk = qa[:, (hh // 2) * 128:(hh // 2 + 1) * 128]
        if hh % 2 == 1:
            blk = pltpu.roll(blk, ATT_DIM, 1)
        qh_ref[hh] = blk[:, 0:ATT_DIM]
    ka = proj_ref[:, P_KA:P_KA + 128]
    kat = ka.T
    vat = pt[T_VA:T_VA + 128, :]
    kat_ref[...] = kat
    vat_ref[...] = vat

    def seq_scores(b, carry):
        rows = pl.ds(pl.multiple_of(b * SAMPLE_SEQ, SAMPLE_SEQ), SAMPLE_SEQ)
        kct = kct_ref[b]
        for kv in range(ATT_KV):
            lhs = jnp.concatenate([qh_ref[kv * ATT_GROUP + g, rows, :] for g in range(ATT_GROUP)], axis=0)
            res = _bdot(lhs, kct[kv * ATT_DIM:(kv + 1) * ATT_DIM, :])
            for g in range(ATT_GROUP):
                sc_ref[kv * ATT_GROUP + g, rows, :] = res[g * SAMPLE_SEQ:(g + 1) * SAMPLE_SEQ]
        for h in range(M_HEADS):
            qc_ref[h, rows, :] = _bdot(proj_ref[rows, P_QM + h * M_DIM:P_QM + (h + 1) * M_DIM], cin_ref[b, h])
        return carry

    lax.fori_loop(0, SAMPLE_GROUP, seq_scores, 0, unroll=SEQ_UNROLL)

    o_new, esinks = [], []
    for hh in range(ATT_HEADS):
        kv = hh // ATT_GROUP
        s_n = _bdot(qh_ref[hh], kat[kv * ATT_DIM:(kv + 1) * ATT_DIM, :]) + bn_ref[hh]
        s_c = sc_ref[hh] + bc_ref[hh]
        sink = sink_ref[0:1, hh:hh + 1]
        mx = jnp.maximum(jnp.maximum(jnp.max(s_n, axis=-1, keepdims=True),
                                     jnp.max(s_c, axis=-1, keepdims=True)), sink)
        p_n = jnp.exp(s_n - mx)
        sc_ref[hh] = jnp.exp(s_c - mx)
        vaug = jnp.concatenate([vat[kv * ATT_DIM:(kv + 1) * ATT_DIM, :], ones_rows], axis=0)
        o_new.append(_bdot_nt(p_n, vaug))
        esinks.append(jnp.exp(sink - mx))

    r = _gate_rows(pt[T_GATES:T_GATES + 8, :] + bg_ref[...], m0_ref[0], mask_t, last_sel)
    mout_ref[0] = r["m_new"]
    m_out = []
    for h in range(M_HEADS):
        q = proj_ref[:, P_QM + h * M_DIM:P_QM + (h + 1) * M_DIM].astype(BF16)
        kf = proj_ref[:, P_KM + h * M_DIM:P_KM + (h + 1) * M_DIM] * K_SCALE
        k = kf.astype(BF16)
        vt = pt[T_VM + h * M_DIM:T_VM + (h + 1) * M_DIM, :]
        n_h = nin_ref[h]
        r1 = lax.dot_general(jnp.concatenate([k, n_h.astype(BF16)], axis=0), q, NT_DIMS,
                             preferred_element_type=F32)
        st = r1[0:ROWS] * jnp.exp(r["a_masked"][h] + r["bm"][h:h + 1, :])
        q_n = jnp.sum(jnp.where(seq_of_lane, r1[ROWS:ROWS + SAMPLE_GROUP], 0.0), axis=0, keepdims=True)
        g_row = r["gexp"][h:h + 1, :]
        num = _bdot(vt, st) + g_row * qc_ref[h].T
        den = jnp.sum(st, axis=0, keepdims=True) + g_row * q_n
        ht = num / jnp.maximum(jnp.abs(den), r["enm"][h:h + 1, :])
        hn = ht * lax.rsqrt(jnp.mean(ht * ht, axis=0, keepdims=True) + NORM_EPS) * gmhb_ref[h]
        m_out.append(jax.nn.sigmoid(proj_ref[:, P_OM + h * M_DIM:P_OM + (h + 1) * M_DIM]) * hn.T
                     * _silu(proj_ref[:, P_ZM + h * M_DIM:P_ZM + (h + 1) * M_DIM]))
        w_row = r["w"][h:h + 1, :]
        dec16 = jnp.sum(jnp.where(lane16 == row16 * SAMPLE_SEQ, r["decay"][h:h + 1, :], 0.0),
                        axis=1, keepdims=True)
        nout_ref[h] = dec16 * n_h + _bdot(jnp.where(seq_of_lane, w_row, 0.0), k)
        decb_ref[h] = jnp.broadcast_to(dec16, (SAMPLE_GROUP, ROWS))
        wv_ref[h] = (vt * w_row).T
        kt_ref[h] = kf.T.astype(BF16)

    keep_new = ci >= ROWS - SAMPLE_SEQ

    def seq_update(b, carry):
        rows = pl.ds(pl.multiple_of(b * SAMPLE_SEQ, SAMPLE_SEQ), SAMPLE_SEQ)
        vct = vct_ref[b]
        for kv in range(ATT_KV):
            vaug = jnp.concatenate([vct[kv * ATT_DIM:(kv + 1) * ATT_DIM, :], ones_rows], axis=0)
            pl_ = jnp.concatenate([sc_ref[kv * ATT_GROUP + g, rows, :] for g in range(ATT_GROUP)], axis=0)
            res = _bdot_nt(pl_, vaug)
            for g in range(ATT_GROUP):
                oc_ref[kv * ATT_GROUP + g, rows, 0:ATT_DIM + ONES_ROWS] = res[g * SAMPLE_SEQ:(g + 1) * SAMPLE_SEQ]
        in_seq = (ri >> 3) == b
        for h in range(M_HEADS):
            upd = jnp.dot(kt_ref[h], jnp.where(in_seq, wv_ref[h], 0.0).astype(BF16), preferred_element_type=F32)
            cout_ref[b, h] = decb_ref[h, pl.ds(b, 1), :] * cin_ref[b, h] + upd
        shift = (ROWS - SAMPLE_SEQ - b * SAMPLE_SEQ) & (ROWS - 1)
        kot_ref[b] = jnp.where(keep_new, pltpu.roll(kat_ref[...], shift, 1),
                               pltpu.roll(kct_ref[b], ROWS - SAMPLE_SEQ, 1))
        vot_ref[b] = jnp.where(keep_new, pltpu.roll(vat_ref[...], shift, 1),
                               pltpu.roll(vct, ROWS - SAMPLE_SEQ, 1))
        return carry

    lax.fori_loop(0, SAMPLE_GROUP, seq_update, 0, unroll=SEQ_UNROLL)

    att = []
    for pair in range(ATT_HEADS // 2):
        halves = []
        for hh in (2 * pair, 2 * pair + 1):
            on, oc = o_new[hh], oc_ref[hh]
            den = on[:, ATT_DIM:ATT_DIM + 1] + oc[:, ATT_DIM:ATT_DIM + 1] + esinks[hh]
            halves.append((on[:, 0:ATT_DIM] + oc[:, 0:ATT_DIM]) / den)
        att.append(jnp.concatenate(halves, axis=1))
    a_out = jnp.concatenate(att, axis=1) * _silu(proj_ref[:, P_ZA:P_ZA + 512])
    cat = jnp.concatenate([a_out] + m_out, axis=1).astype(BF16)
    y_ref[...] = _out_tail(cat, x, wout_ref, gpost_ref)


def _sample_call(x, kct, vct, cin, nin, m0, gpre, wmain, wt, bg, sinks, gmhb, wout, gpost):
    nrows = x.shape[0]
    ngroups = nrows // ROWS
    nseq = ngroups * SAMPLE_GROUP
    full = lambda shape: pl.BlockSpec(shape, lambda i: (0,) * len(shape))
    grp = SAMPLE_GROUP
    out_shapes = (
        jax.ShapeDtypeStruct((nrows, D_MODEL), F32),
        jax.ShapeDtypeStruct((nseq, 128, ROWS), F32),
        jax.ShapeDtypeStruct((nseq, 128, ROWS), F32),
        jax.ShapeDtypeStruct((nseq, M_HEADS, M_DIM, M_DIM), F32),
        jax.ShapeDtypeStruct((M_HEADS, nseq, M_DIM), F32),
        jax.ShapeDtypeStruct((ngroups, 8, ROWS), F32),
    )
    return pl.pallas_call(
        _sample_kernel,
        grid=(ngroups,),
        in_specs=[
            pl.BlockSpec((ROWS, D_MODEL), lambda i: (i, 0)),
            pl.BlockSpec((grp, 128, ROWS), lambda i: (i, 0, 0)),
            pl.BlockSpec((grp, 128, ROWS), lambda i: (i, 0, 0)),
            pl.BlockSpec((grp, M_HEADS, M_DIM, M_DIM), lambda i: (i, 0, 0, 0)),
            pl.BlockSpec((M_HEADS, grp, M_DIM), lambda i: (0, i, 0)),
            pl.BlockSpec((1, 8, ROWS), lambda i: (i, 0, 0)),
            full((1, D_MODEL)), full((D_MODEL, P_MAIN)), full((T_ROWS, D_MODEL)), full((8, 1)),
            full((1, ATT_HEADS)), full((M_HEADS, M_DIM, ROWS)), full((D_MODEL, D_MODEL)), full((1, D_MODEL)),
        ],
        out_specs=(
            pl.BlockSpec((ROWS, D_MODEL), lambda i: (i, 0)),
            pl.BlockSpec((grp, 128, ROWS), lambda i: (i, 0, 0)),
            pl.BlockSpec((grp, 128, ROWS), lambda i: (i, 0, 0)),
            pl.BlockSpec((grp, M_HEADS, M_DIM, M_DIM), lambda i: (i, 0, 0, 0)),
            pl.BlockSpec((M_HEADS, grp, M_DIM), lambda i: (0, i, 0)),
            pl.BlockSpec((1, 8, ROWS), lambda i: (i, 0, 0)),
        ),
        out_shape=out_shapes,
        scratch_shapes=[
            pltpu.VMEM((ROWS, P_MAIN), F32),
            pltpu.VMEM((ATT_HEADS, ROWS, ATT_DIM), F32),
            pltpu.VMEM((ATT_HEADS, ROWS, ROWS), F32),
            pltpu.VMEM((ATT_HEADS, ROWS, ROWS), F32),
            pltpu.VMEM((M_HEADS, ROWS, M_DIM), F32),
            pltpu.VMEM((128, ROWS), F32),
            pltpu.VMEM((128, ROWS), F32),
            pltpu.VMEM((M_HEADS, ROWS, M_DIM), F32),
            pltpu.VMEM((M_HEADS, M_DIM, ROWS), BF16),
            pltpu.VMEM((M_HEADS, SAMPLE_GROUP, ROWS), F32),
            pltpu.VMEM((ATT_HEADS, ROWS, ROWS), F32),
            pltpu.VMEM((ATT_HEADS, ROWS, ROWS), F32),
        ],
        compiler_params=pltpu.CompilerParams(
            dimension_semantics=("arbitrary",), vmem_limit_bytes=VMEM_LIMIT_BYTES),
        name="sample_layer",
    )(x, kct, vct, cin, nin, m0, gpre, wmain, wt, bg, sinks, gmhb, wout, gpost)


PROMPT_BLOCK = 512


def _window_in(cache):
    nseq = cache.shape[0]
    return cache.transpose(0, 2, 3, 1).reshape(nseq, ATT_KV * ATT_DIM, ROWS)


def _window_out(win_t):
    nseq = win_t.shape[0]
    return win_t.reshape(nseq, ATT_KV, ATT_DIM, ROWS).transpose(0, 3, 1, 2)[None]


def kernel(x_prompt, x_sample, cache_win_k, cache_win_v, state_C, state_n, state_m,
           g_pre, w_in, b_gate, attn_sinks, g_mh, w_out, g_post):
    depth = g_pre.shape[0]
    assert depth == 1, "single-layer trunk"
    nseq, sseq, _ = x_sample.shape
    assert sseq == SAMPLE_SEQ and nseq % SAMPLE_GROUP == 0

    gpre = g_pre[0].reshape(1, D_MODEL)
    gpost = g_post[0].reshape(1, D_MODEL)
    wmain, wt, wout = _weights_call(w_in[0].T, w_out[0])
    bg = b_gate[0].reshape(2 * M_HEADS, 1)
    sinks = attn_sinks[0].reshape(1, ATT_HEADS)
    gmhb = jnp.broadcast_to(g_mh[0][:, :, None], (M_HEADS, M_DIM, ROWS))

    yp, wkp, wvp, cp, np_, mp = _prompt_call(x_prompt, gpre, wmain, wt, bg, sinks, gmhb, wout, gpost, PROMPT_BLOCK)

    ngroups = nseq // SAMPLE_GROUP
    m0 = jnp.broadcast_to(state_m[0].reshape(ngroups, SAMPLE_GROUP, 1, M_HEADS),
                          (ngroups, SAMPLE_GROUP, SAMPLE_SEQ, M_HEADS))
    m0 = jnp.pad(m0.reshape(ngroups, ROWS, M_HEADS).transpose(0, 2, 1), ((0, 0), (0, 8 - M_HEADS), (0, 0)))
    ys, wks, wvs, cs, ns, ms = _sample_call(
        x_sample.reshape(nseq * sseq, D_MODEL), _window_in(cache_win_k[0]), _window_in(cache_win_v[0]),
        state_C[0], state_n[0].transpose(1, 0, 2), m0,
        gpre, wmain, wt, bg, sinks, gmhb, wout, gpost)

    ms = ms[:, 0:M_HEADS, ::SAMPLE_SEQ].transpose(0, 2, 1).reshape(nseq, M_HEADS)
    return (yp, ys.reshape(nseq, sseq, D_MODEL), _window_out(wkp), _window_out(wvp),
            cp[None], np_[None], mp[:, 0:M_HEADS, 0][None],
            _window_out(wks), _window_out(wvs), cs[None], ns.transpose(1, 0, 2)[None], ms[None])
```

```python
import functools

import jax
import jax.numpy as jnp
from jax import lax
from jax.experimental import pallas as pl
from jax.experimental.pallas import tpu as pltpu

F32 = jnp.float32
BF16 = jnp.bfloat16

D_MODEL = 1024
ROWS = 128
ATT_HEADS, ATT_KV, ATT_GROUP, ATT_DIM = 8, 2, 4, 64
M_HEADS, M_DIM = 4, 128
NORM_EPS = 1e-6
NEG_BIG = -1e30
ATT_SCALE = ATT_DIM ** -0.5
K_SCALE = M_DIM ** -0.5

QA, KA, VA, ZA = 0, 512, 640, 768
QM, KM, VM, OM, ZM = 1280, 1792, 2304, 2816, 3328
GATES = 3840
D_IN = 3848

P_QA, P_KA, P_ZA, P_QM, P_KM, P_OM, P_ZM = 0, 512, 640, 1152, 1664, 2176, 2688
P_MAIN = 3200
T_VA, T_VM, T_GATES = 0, 128, 640
T_ROWS = 656
T_SPLIT = 336
STATE_ROWS = 144
ONES_ROWS = 16

SAMPLE_SEQ = 8
SAMPLE_GROUP = ROWS // SAMPLE_SEQ
SEQ_UNROLL = 4

VMEM_LIMIT_BYTES = 56 * 1024 * 1024
NT_DIMS = (((1,), (1,)), ((), ()))


def _rms(x, g):
    return x * lax.rsqrt(jnp.mean(x * x, axis=-1, keepdims=True) + NORM_EPS) * g


def _silu(x):
    return x * jax.nn.sigmoid(x)


def _log_sigmoid(x):
    return -(jnp.maximum(-x, 0.0) + jnp.log1p(jnp.exp(-jnp.abs(x))))


def _slope(head):
    return 2.0 ** -(head + 1)


def _bdot(a, b):
    return jnp.dot(a.astype(BF16), b.astype(BF16), preferred_element_type=F32)


def _bdot_nt(a, b):
    return lax.dot_general(a.astype(BF16), b.astype(BF16), NT_DIMS, preferred_element_type=F32)


def _exact_dot(x, m):
    hi = x.astype(BF16).astype(F32)
    mid = (x - hi).astype(BF16).astype(F32)
    lo = (x - hi - mid).astype(BF16).astype(F32)
    parts = jnp.dot(jnp.concatenate([hi, mid, lo, jnp.zeros_like(hi)], axis=0).astype(BF16), m,
                    preferred_element_type=F32)
    return parts[0:8] + parts[8:16] + parts[16:24]


def _gate_rows(x, m0, mask_t, last_sel):
    row = lax.broadcasted_iota(jnp.int32, (8, ROWS), 0)
    head_rows = row < M_HEADS
    ic = jnp.where(head_rows, x, 0.0)
    fc = jnp.where(head_rows, _log_sigmoid(pltpu.roll(x, M_HEADS, 0)), 0.0)
    b = _exact_dot(fc, mask_t.astype(BF16))
    a = ic - b
    a_cols = jnp.concatenate([a, jnp.zeros((ROWS - 8, ROWS), F32)], axis=0).T
    a_masked = [jnp.where(mask_t, a_cols[:, h:h + 1], -jnp.inf) for h in range(M_HEADS)]
    cm = jnp.concatenate([jnp.max(am, axis=0, keepdims=True) for am in a_masked]
                         + [jnp.zeros((8 - M_HEADS, ROWS), F32)], axis=0)
    m_t = jnp.maximum(b + m0, b + cm)
    if last_sel is None:
        b_last = jnp.broadcast_to(b[:, ROWS - 1:ROWS], b.shape)
        m_new = jnp.broadcast_to(m_t[:, ROWS - 1:ROWS], b.shape)
    else:
        both = _exact_dot(jnp.where(head_rows, b, pltpu.roll(m_t, M_HEADS, 0)), last_sel)
        b_last = jnp.where(head_rows, both, 0.0)
        m_new = jnp.where(head_rows, pltpu.roll(both, M_HEADS, 0), 0.0)
    return dict(a_masked=a_masked, bm=b - m_t, gexp=jnp.exp(b + m0 - m_t), enm=jnp.exp(-m_t), m_new=m_new,
                w=jnp.exp(b_last - b + ic - m_new), decay=jnp.exp(b_last + m0 - m_new))


def _out_tail(cat, x, wout_ref, gpost_ref):
    y = jnp.dot(cat, wout_ref[...], preferred_element_type=F32)
    return x + _rms(y, gpost_ref[...])


MAIN_BLOCKS = P_MAIN // 128


def _main_src_block(i):
    return jnp.where(i < 5, i, jnp.where(i < 17, i + 1, i + 5))


def _weights_kernel(main_ref, va_ref, vm0_ref, vm1_ref, vm2_ref, vm3_ref, gates_ref, wout_ref,
                    wmain_ref, wt_ref, woutb_ref):
    wmain_ref[...] = main_ref[...].T.astype(BF16)

    @pl.when(pl.program_id(0) == 0)
    def _():
        wt_ref[...] = jnp.concatenate(
            [va_ref[...], vm0_ref[...], vm1_ref[...], vm2_ref[...], vm3_ref[...], gates_ref[...],
             jnp.zeros((T_ROWS - T_GATES - 8, D_MODEL), F32)], axis=0).astype(BF16)
        woutb_ref[...] = wout_ref[...].astype(BF16)


def _weights_call(w_in_t, w_out):
    blk = lambda r: pl.BlockSpec((128, D_MODEL), lambda i, r=r: (r, 0))
    return pl.pallas_call(
        _weights_kernel,
        grid=(MAIN_BLOCKS,),
        in_specs=[
            pl.BlockSpec((128, D_MODEL), lambda i: (_main_src_block(i), 0)),
            blk(VA // 128), blk(VM // 128), blk(VM // 128 + 1), blk(VM // 128 + 2), blk(VM // 128 + 3),
            pl.BlockSpec((8, D_MODEL), lambda i: (GATES // 8, 0)),
            pl.BlockSpec((D_MODEL, D_MODEL), lambda i: (0, 0)),
        ],
        out_specs=(
            pl.BlockSpec((D_MODEL, 128), lambda i: (0, i)),
            pl.BlockSpec((T_ROWS, D_MODEL), lambda i: (0, 0)),
            pl.BlockSpec((D_MODEL, D_MODEL), lambda i: (0, 0)),
        ),
        out_shape=(
            jax.ShapeDtypeStruct((D_MODEL, P_MAIN), BF16),
            jax.ShapeDtypeStruct((T_ROWS, D_MODEL), BF16),
            jax.ShapeDtypeStruct((D_MODEL, D_MODEL), BF16),
        ),
        compiler_params=pltpu.CompilerParams(
            dimension_semantics=("arbitrary",), vmem_limit_bytes=VMEM_LIMIT_BYTES),
        name="layer_weights",
    )(w_in_t, w_in_t, w_in_t, w_in_t, w_in_t, w_in_t, w_in_t, w_out)


def _prompt_bias_t(kv, first):
    j = lax.broadcasted_iota(jnp.int32, (2 * ROWS, ROWS), 0)
    i = lax.broadcasted_iota(jnp.int32, (2 * ROWS, ROWS), 1)
    diff = ROWS + i - j
    valid = (diff >= 0) & (diff < ROWS)
    if first:
        valid = valid & (j >= ROWS)
    dfl = diff.astype(F32)
    return jnp.concatenate(
        [jnp.where(valid, -_slope(kv * ATT_GROUP + g) * dfl, NEG_BIG) for g in range(ATT_GROUP)], axis=1)


def _prompt_kernel(x_ref, xnext_ref, gpre_ref, wmain_ref, wt_ref, bg_ref, sink_ref, gmhb_ref, wout_ref, gpost_ref,
                   y_ref, wk_ref, wv_ref, c_ref, n_ref, m_ref,
                   proj_ref, projt_ref, cat_ref, xn_ref, kprev_ref, vtprev_ref, ct_ref, mst_ref, bias_ref,
                   *, tb, nt):
    bi = pl.program_id(0)
    j = pl.program_id(1)
    nchunks = tb // ROWS

    @pl.when((bi == 0) & (j == 0))
    def _():
        for first in range(2):
            for kv in range(ATT_KV):
                bias_ref[first * ATT_KV + kv] = _prompt_bias_t(kv, first)

    @pl.when(j == 0)
    def _():
        kprev_ref[...] = jnp.zeros_like(kprev_ref)
        vtprev_ref[...] = jnp.zeros_like(vtprev_ref)
        ct_ref[...] = jnp.zeros_like(ct_ref)
        mst_ref[...] = jnp.zeros_like(mst_ref)

    sub = tb // PROMPT_SUBBLOCKS
    sub_chunks = sub // ROWS

    def norm_rows(sb):
        rows_sb = slice(sb * sub, (sb + 1) * sub)
        xn_ref[rows_sb, :] = _rms(x_ref[0, rows_sb, :], gpre_ref[...]).astype(BF16)

    def proj_cols(sb, c0, c1):
        rows_sb = slice(sb * sub, (sb + 1) * sub)
        proj_ref[rows_sb, c0:c1] = jnp.dot(xn_ref[rows_sb, :], wmain_ref[:, c0:c1], preferred_element_type=F32)

    def proj_t(sb):
        for r0, r1 in ((0, T_SPLIT), (T_SPLIT, T_ROWS)):
            pt = lax.dot_general(wt_ref[r0:r1, :], xn_ref[sb * sub:(sb + 1) * sub, :], NT_DIMS,
                                 preferred_element_type=F32)
            for c in range(sub_chunks):
                projt_ref[sb * sub_chunks + c, r0:r1, :] = pt[:, c * ROWS:(c + 1) * ROWS]

    def out_rows(sb):
        rows_sb = slice(sb * sub, (sb + 1) * sub)
        y_ref[0, rows_sb, :] = _out_tail(cat_ref[rows_sb, :], x_ref[0, rows_sb, :], wout_ref, gpost_ref)

    def proj_pieces(sb):
        bounds = list(range(0, P_MAIN, PROJ_COL_STEP)) + [P_MAIN]
        pieces = [functools.partial(proj_cols, sb, c0, c1) for c0, c1 in zip(bounds[:-1], bounds[1:])]
        return pieces[:2] + [functools.partial(proj_t, sb)] + pieces[2:]

    ri = lax.broadcasted_iota(jnp.int32, (ROWS, ROWS), 0)
    ci = lax.broadcasted_iota(jnp.int32, (ROWS, ROWS), 1)
    mask_t = ri <= ci
    st_row = lax.broadcasted_iota(jnp.int32, (STATE_ROWS - M_DIM, ROWS), 0)

    def chunk(c):
        rows = pl.ds(c * ROWS, ROWS)
        first = ((j == 0) & (c == 0)).astype(jnp.int32)
        yield

        qa = proj_ref[rows, P_QA:P_QA + 512] * ATT_SCALE
        kcur = proj_ref[rows, P_KA:P_KA + 128].astype(BF16)
        vtcur = projt_ref[c, T_VA:T_VA + 128, :].astype(BF16)
        kcat = jnp.concatenate([kprev_ref[...], kcur], axis=0)
        vtcat = jnp.concatenate([vtprev_ref[...], vtcur], axis=1)
        r = _gate_rows(projt_ref[c, T_GATES:T_GATES + 8, :] + bg_ref[...], mst_ref[...], mask_t, None)

        scores, sinks = [], []
        for kv in range(ATT_KV):
            want_hi = kv == 1
            keep = (ci >= ATT_DIM) if want_hi else (ci < ATT_DIM)
            pieces = []
            for g in range(ATT_GROUP):
                hh = kv * ATT_GROUP + g
                blk = qa[:, (hh // 2) * 128:(hh // 2 + 1) * 128]
                if (hh % 2 == 1) != want_hi:
                    blk = pltpu.roll(blk, ATT_DIM, 1)
                pieces.append(jnp.where(keep, blk, 0.0))
            q4 = jnp.concatenate(pieces, axis=0)
            scores.append(_bdot_nt(kcat, q4) + bias_ref[first * ATT_KV + kv])
            sinks.append(jnp.concatenate(
                [jnp.broadcast_to(sink_ref[0:1, kv * ATT_GROUP + g:kv * ATT_GROUP + g + 1], (1, ROWS))
                 for g in range(ATT_GROUP)], axis=1))
        yield
        vts, r1s = [], []
        for h in range(M_HEADS):
            q = proj_ref[rows, P_QM + h * M_DIM:P_QM + (h + 1) * M_DIM].astype(BF16)
            k = (proj_ref[rows, P_KM + h * M_DIM:P_KM + (h + 1) * M_DIM] * K_SCALE).astype(BF16)
            vt = projt_ref[c, T_VM + h * M_DIM:T_VM + (h + 1) * M_DIM, :]
            ct = ct_ref[h]
            r1s.append(lax.dot_general(jnp.concatenate([k, ct.astype(BF16)], axis=0), q, NT_DIMS,
                                       preferred_element_type=F32))
            w_row = r["w"][h:h + 1, :]
            lhs2 = jnp.concatenate([vt * w_row, jnp.where(st_row == 0, w_row, 0.0)], axis=0)
            ct_ref[h] = r["decay"][h:h + 1, 0:1] * ct + _bdot(lhs2, k)
            vts.append(vt)
        mst_ref[...] = r["m_new"]
        yield

        outs, nums, dens = [], [], []
        for kv in range(ATT_KV):
            s, sink = scores[kv], sinks[kv]
            mx = jnp.maximum(jnp.max(s, axis=0, keepdims=True), sink)
            p = jnp.exp(s - mx).astype(BF16)
            lhs = jnp.concatenate([vtcat[kv * ATT_DIM:(kv + 1) * ATT_DIM, :],
                                   jnp.ones((ONES_ROWS, 2 * ROWS), BF16)], axis=0)
            o = jnp.dot(lhs, p, preferred_element_type=F32)
            outs.append((o, jnp.exp(sink - mx)))
        yield
        for h in range(M_HEADS):
            r1 = r1s[h]
            dm = jnp.exp(r["a_masked"][h] + r["bm"][h:h + 1, :])
            st = r1[0:ROWS] * dm
            g_row = r["gexp"][h:h + 1, :]
            nums.append(_bdot(vts[h], st) + g_row * r1[ROWS:ROWS + M_DIM])
            dens.append(jnp.sum(st, axis=0, keepdims=True) + g_row * r1[ROWS + M_DIM:ROWS + M_DIM + 1])
        yield

        att = []
        for kv in range(ATT_KV):
            o, esink = outs[kv]
            on = o[0:ATT_DIM, :] / (o[ATT_DIM:ATT_DIM + 1, :] + esink)
            for pair in range(2):
                two = jnp.concatenate([on[:, (2 * pair) * ROWS:(2 * pair + 1) * ROWS],
                                       on[:, (2 * pair + 1) * ROWS:(2 * pair + 2) * ROWS]], axis=0)
                att.append(two.T)
        a_out = jnp.concatenate(att, axis=1) * _silu(proj_ref[rows, P_ZA:P_ZA + 512])
        kprev_ref[...] = kcur
        vtprev_ref[...] = vtcur
        yield
        m_out = []
        for h in range(M_HEADS):
            ht = nums[h] / jnp.maximum(jnp.abs(dens[h]), r["enm"][h:h + 1, :])
            hn = ht * lax.rsqrt(jnp.mean(ht * ht, axis=0, keepdims=True) + NORM_EPS) * gmhb_ref[h]
            m_out.append(jax.nn.sigmoid(proj_ref[rows, P_OM + h * M_DIM:P_OM + (h + 1) * M_DIM]) * hn.T
                         * _silu(proj_ref[rows, P_ZM + h * M_DIM:P_ZM + (h + 1) * M_DIM]))

        cat_ref[rows, :] = jnp.concatenate([a_out] + m_out, axis=1).astype(BF16)
        yield

    @pl.when((bi == 0) & (j == 0))
    def _():
        norm_rows(0)
        for piece in proj_pieces(0):
            piece()

    def norm_next():
        xn_ref[0:sub, :] = _rms(xnext_ref[0], gpre_ref[...]).astype(BF16)

    def run_chunks(sb, fillers):
        n_fill, n_slots, slot = len(fillers), sub_chunks * CHUNK_STAGES, 0
        for c in range(sb * sub_chunks, (sb + 1) * sub_chunks):
            for _ in chunk(c):
                for _ in range(-(-(slot + 1) * n_fill // n_slots) + (-slot * n_fill // n_slots)):
                    fillers.pop(0)()
                slot += 1
        assert slot == n_slots and not fillers

    assert PROMPT_SUBBLOCKS == 2
    run_chunks(0, [functools.partial(norm_rows, 1)] + proj_pieces(1))
    run_chunks(1, [functools.partial(out_rows, 0), norm_next] + proj_pieces(0))
    out_rows(1)

    @pl.when(j == nt - 1)
    def _():
        wk_ref[0] = proj_ref[tb - ROWS:tb, P_KA:P_KA + 128].T
        wv_ref[0] = projt_ref[nchunks - 1, T_VA:T_VA + 128, :]
        for h in range(M_HEADS):
            ct = ct_ref[h]
            c_ref[0, h] = ct[0:M_DIM].T
            n_ref[0, h:h + 1, :] = ct[M_DIM:M_DIM + 1]
        m_ref[0] = mst_ref[...]


def _prompt_call(x, gpre, wmain, wt, bg, sinks, gmhb, wout, gpost, tb):
    bsz, seq, _ = x.shape
    nt = seq // tb
    full = lambda shape: pl.BlockSpec(shape, lambda b, j: (0,) * len(shape))

    def next_first_subblock(b, j):
        wrap = (j + 1 == nt).astype(jnp.int32)
        return (jnp.minimum(b + wrap, bsz - 1), (j + 1) * (1 - wrap) * PROMPT_SUBBLOCKS, 0)

    out_shapes = (
        jax.ShapeDtypeStruct((bsz, seq, D_MODEL), F32),
        jax.ShapeDtypeStruct((bsz, 128, ROWS), F32),
        jax.ShapeDtypeStruct((bsz, 128, ROWS), F32),
        jax.ShapeDtypeStruct((bsz, M_HEADS, M_DIM, M_DIM), F32),
        jax.ShapeDtypeStruct((bsz, M_HEADS, M_DIM), F32),
        jax.ShapeDtypeStruct((bsz, 8, ROWS), F32),
    )
    return pl.pallas_call(
        functools.partial(_prompt_kernel, tb=tb, nt=nt),
        grid=(bsz, nt),
        in_specs=[
            pl.BlockSpec((1, tb, D_MODEL), lambda b, j: (b, j, 0)),
            pl.BlockSpec((1, tb // PROMPT_SUBBLOCKS, D_MODEL), next_first_subblock),
            full((1, D_MODEL)), full((D_MODEL, P_MAIN)), full((T_ROWS, D_MODEL)), full((8, 1)),
            full((1, ATT_HEADS)), full((M_HEADS, M_DIM, ROWS)), full((D_MODEL, D_MODEL)), full((1, D_MODEL)),
        ],
        out_specs=(
            pl.BlockSpec((1, tb, D_MODEL), lambda b, j: (b, j, 0)),
            pl.BlockSpec((1, 128, ROWS), lambda b, j: (b, 0, 0)),
            pl.BlockSpec((1, 128, ROWS), lambda b, j: (b, 0, 0)),
            pl.BlockSpec((1, M_HEADS, M_DIM, M_DIM), lambda b, j: (b, 0, 0, 0)),
            pl.BlockSpec((1, M_HEADS, M_DIM), lambda b, j: (b, 0, 0)),
            pl.BlockSpec((1, 8, ROWS), lambda b, j: (b, 0, 0)),
        ),
        out_shape=out_shapes,
        scratch_shapes=[
            pltpu.VMEM((tb, P_MAIN), F32),
            pltpu.VMEM((tb // ROWS, T_ROWS, ROWS), F32),
            pltpu.VMEM((tb, D_MODEL), BF16),
            pltpu.VMEM((tb, D_MODEL), BF16),
            pltpu.VMEM((ROWS, 128), BF16),
            pltpu.VMEM((128, ROWS), BF16),
            pltpu.VMEM((M_HEADS, STATE_ROWS, M_DIM), F32),
            pltpu.VMEM((8, ROWS), F32),
            pltpu.VMEM((2 * ATT_KV, 2 * ROWS, ATT_GROUP * ROWS), F32),
        ],
        compiler_params=pltpu.CompilerParams(
            dimension_semantics=("arbitrary", "arbitrary"), vmem_limit_bytes=VMEM_LIMIT_BYTES),
        name="prompt_layer",
    )(x, x, gpre, wmain, wt, bg, sinks, gmhb, wout, gpost)


def _sample_bias_new(head):
    r = lax.broadcasted_iota(jnp.int32, (ROWS, ROWS), 0)
    c = lax.broadcasted_iota(jnp.int32, (ROWS, ROWS), 1)
    valid = ((r >> 3) == (c >> 3)) & (r >= c)
    return jnp.where(valid, -_slope(head) * (r - c).astype(F32), NEG_BIG)


def _sample_bias_cache(head):
    r = lax.broadcasted_iota(jnp.int32, (ROWS, ROWS), 0)
    c = lax.broadcasted_iota(jnp.int32, (ROWS, ROWS), 1)
    diff = (r & (SAMPLE_SEQ - 1)) + ROWS - c
    return jnp.where(diff < ROWS, -_slope(head) * diff.astype(F32), NEG_BIG)


def _sample_kernel(x_ref, kct_ref, vct_ref, cin_ref, nin_ref, m0_ref,
                   gpre_ref, wmain_ref, wt_ref, bg_ref, sink_ref, gmhb_ref, wout_ref, gpost_ref,
                   y_ref, kot_ref, vot_ref, cout_ref, nout_ref, mout_ref,
                   proj_ref, qh_ref, sc_ref, oc_ref, qc_ref, kat_ref, vat_ref, wv_ref, kt_ref, decb_ref,
                   bn_ref, bc_ref):
    @pl.when(pl.program_id(0) == 0)
    def _():
        for hh in range(ATT_HEADS):
            bn_ref[hh] = _sample_bias_new(hh)
            bc_ref[hh] = _sample_bias_cache(hh)

    x = x_ref[...]
    xn = _rms(x, gpre_ref[...]).astype(BF16)
    proj_ref[...] = jnp.dot(xn, wmain_ref[...], preferred_element_type=F32)
    pt = lax.dot_general(wt_ref[...], xn, NT_DIMS, preferred_element_type=F32)

    ri = lax.broadcasted_iota(jnp.int32, (ROWS, ROWS), 0)
    ci = lax.broadcasted_iota(jnp.int32, (ROWS, ROWS), 1)
    same_seq = (ri >> 3) == (ci >> 3)
    mask_t = same_seq & (ri <= ci)
    last_sel = (same_seq & ((ri & (SAMPLE_SEQ - 1)) == SAMPLE_SEQ - 1)).astype(BF16)
    row16 = lax.broadcasted_iota(jnp.int32, (SAMPLE_GROUP, ROWS), 0)
    lane16 = lax.broadcasted_iota(jnp.int32, (SAMPLE_GROUP, ROWS), 1)
    seq_of_lane = (lane16 >> 3) == row16
    ones_rows = jnp.ones((ONES_ROWS, ROWS), F32)

    qa = proj_ref[:, P_QA:P_QA + 512] * ATT_SCALE
    for hh in range(ATT_HEADS):
        blk = qa[:, (hh // 2) * 128:(hh // 2 + 1) * 128]
        if hh % 2 == 1:
            blk = pltpu.roll(blk, ATT_DIM, 1)
        qh_ref[hh] = blk[:, 0:ATT_DIM]
    ka = proj_ref[:, P_KA:P_KA + 128]
    kat = ka.T
    vat = pt[T_VA:T_VA + 128, :]
    kat_ref[...] = kat
    vat_ref[...] = vat

    def seq_scores(b, carry):
        rows = pl.ds(pl.multiple_of(b * SAMPLE_SEQ, SAMPLE_SEQ), SAMPLE_SEQ)
        kct = kct_ref[b]
        for kv in range(ATT_KV):
            lhs = jnp.concatenate([qh_ref[kv * ATT_GROUP + g, rows, :] for g in range(ATT_GROUP)], axis=0)
            res = _bdot(lhs, kct[kv * ATT_DIM:(kv + 1) * ATT_DIM, :])
            for g in range(ATT_GROUP):
                sc_ref[kv * ATT_GROUP + g, rows, :] = res[g * SAMPLE_SEQ:(g + 1) * SAMPLE_SEQ]
        for h in range(M_HEADS):
            qc_ref[h, rows, :] = _bdot(proj_ref[rows, P_QM + h * M_DIM:P_QM + (h + 1) * M_DIM], cin_ref[b, h])
        return carry

    lax.fori_loop(0, SAMPLE_GROUP, seq_scores, 0, unroll=SEQ_UNROLL)

    o_new, esinks = [], []
    for hh in range(ATT_HEADS):
        kv = hh // ATT_GROUP
        s_n = _bdot(qh_ref[hh], kat[kv * ATT_DIM:(kv + 1) * ATT_DIM, :]) + bn_ref[hh]
        s_c = sc_ref[hh] + bc_ref[hh]
        sink = sink_ref[0:1, hh:hh + 1]
        mx = jnp.maximum(jnp.maximum(jnp.max(s_n, axis=-1, keepdims=True),
                                     jnp.max(s_c, axis=-1, keepdims=True)), sink)
        p_n = jnp.exp(s_n - mx)
        sc_ref[hh] = jnp.exp(s_c - mx)
        vaug = jnp.concatenate([vat[kv * ATT_DIM:(kv + 1) * ATT_DIM, :], ones_rows], axis=0)
        o_new.append(_bdot_nt(p_n, vaug))
        esinks.append(jnp.exp(sink - mx))

    r = _gate_rows(pt[T_GATES:T_GATES + 8, :] + bg_ref[...], m0_ref[0], mask_t, last_sel)
    mout_ref[0] = r["m_new"]
    m_out = []
    for h in range(M_HEADS):
        q = proj_ref[:, P_QM + h * M_DIM:P_QM + (h + 1) * M_DIM].astype(BF16)
        kf = proj_ref[:, P_KM + h * M_DIM:P_KM + (h + 1) * M_DIM] * K_SCALE
        k = kf.astype(BF16)
        vt = pt[T_VM + h * M_DIM:T_VM + (h + 1) * M_DIM, :]
        n_h = nin_ref[h]
        r1 = lax.dot_general(jnp.concatenate([k, n_h.astype(BF16)], axis=0), q, NT_DIMS,
                             preferred_element_type=F32)
        st = r1[0:ROWS] * jnp.exp(r["a_masked"][h] + r["bm"][h:h + 1, :])
        q_n = jnp.sum(jnp.where(seq_of_lane, r1[ROWS:ROWS + SAMPLE_GROUP], 0.0), axis=0, keepdims=True)
        g_row = r["gexp"][h:h + 1, :]
        num = _bdot(vt, st) + g_row * qc_ref[h].T
        den = jnp.sum(st, axis=0, keepdims=True) + g_row * q_n
        ht = num / jnp.maximum(jnp.abs(den), r["enm"][h:h + 1, :])
        hn = ht * lax.rsqrt(jnp.mean(ht * ht, axis=0, keepdims=True) + NORM_EPS) * gmhb_ref[h]
        m_out.append(jax.nn.sigmoid(proj_ref[:, P_OM + h * M_DIM:P_OM + (h + 1) * M_DIM]) * hn.T
                     * _silu(proj_ref[:, P_ZM + h * M_DIM:P_ZM + (h + 1) * M_DIM]))
        w_row = r["w"][h:h + 1, :]
        dec16 = jnp.sum(jnp.where(lane16 == row16 * SAMPLE_SEQ, r["decay"][h:h + 1, :], 0.0),
                        axis=1, keepdims=True)
        nout_ref[h] = dec16 * n_h + _bdot(jnp.where(seq_of_lane, w_row, 0.0), k)
        decb_ref[h] = jnp.broadcast_to(dec16, (SAMPLE_GROUP, ROWS))
        wv_ref[h] = (vt * w_row).T
        kt_ref[h] = kf.T.astype(BF16)

    keep_new = ci >= ROWS - SAMPLE_SEQ

    def seq_update(b, carry):
        rows = pl.ds(pl.multiple_of(b * SAMPLE_SEQ, SAMPLE_SEQ), SAMPLE_SEQ)
        vct = vct_ref[b]
        for kv in range(ATT_KV):
            vaug = jnp.concatenate([vct[kv * ATT_DIM:(kv + 1) * ATT_DIM, :], ones_rows], axis=0)
            pl_ = jnp.concatenate([sc_ref[kv * ATT_GROUP + g, rows, :] for g in range(ATT_GROUP)], axis=0)
            res = _bdot_nt(pl_, vaug)
            for g in range(ATT_GROUP):
                oc_ref[kv * ATT_GROUP + g, rows, 0:ATT_DIM + ONES_ROWS] = res[g * SAMPLE_SEQ:(g + 1) * SAMPLE_SEQ]
        in_seq = (ri >> 3) == b
        for h in range(M_HEADS):
            upd = jnp.dot(kt_ref[h], jnp.where(in_seq, wv_ref[h], 0.0).astype(BF16), preferred_element_type=F32)
            cout_ref[b, h] = decb_ref[h, pl.ds(b, 1), :] * cin_ref[b, h] + upd
        shift = (ROWS - SAMPLE_SEQ - b * SAMPLE_SEQ) & (ROWS - 1)
        kot_ref[b] = jnp.where(keep_new, pltpu.roll(kat_ref[...], shift, 1),
                               pltpu.roll(kct_ref[b], ROWS - SAMPLE_SEQ, 1))
        vot_ref[b] = jnp.where(keep_new, pltpu.roll(vat_ref[...], shift, 1),
                               pltpu.roll(vct, ROWS - SAMPLE_SEQ, 1))
        return carry

    lax.fori_loop(0, SAMPLE_GROUP, seq_update, 0, unroll=SEQ_UNROLL)

    att = []
    for pair in range(ATT_HEADS // 2):
        halves = []
        for hh in (2 * pair, 2 * pair + 1):
            on, oc = o_new[hh], oc_ref[hh]
            den = on[:, ATT_DIM:ATT_DIM + 1] + oc[:, ATT_DIM:ATT_DIM + 1] + esinks[hh]
            halves.append((on[:, 0:ATT_DIM] + oc[:, 0:ATT_DIM]) / den)
        att.append(jnp.concatenate(halves, axis=1))
    a_out = jnp.concatenate(att, axis=1) * _silu(proj_ref[:, P_ZA:P_ZA + 512])
    cat = jnp.concatenate([a_out] + m_out, axis=1).astype(BF16)
    y_ref[...] = _out_tail(cat, x, wout_ref, gpost_ref)


def _sample_call(x, kct, vct, cin, nin, m0, gpre, wmain, wt, bg, sinks, gmhb, wout, gpost):
    nrows = x.shape[0]
    ngroups = nrows // ROWS
    nseq = ngroups * SAMPLE_GROUP
    full = lambda shape: pl.BlockSpec(shape, lambda i: (0,) * len(shape))
    grp = SAMPLE_GROUP
    out_shapes = (
        jax.ShapeDtypeStruct((nrows, D_MODEL), F32),
        jax.ShapeDtypeStruct((nseq, 128, ROWS), F32),
        jax.ShapeDtypeStruct((nseq, 128, ROWS), F32),
        jax.ShapeDtypeStruct((nseq, M_HEADS, M_DIM, M_DIM), F32),
        jax.ShapeDtypeStruct((M_HEADS, nseq, M_DIM), F32),
        jax.ShapeDtypeStruct((ngroups, 8, ROWS), F32),
    )
    return pl.pallas_call(
        _sample_kernel,
        grid=(ngroups,),
        in_specs=[
            pl.BlockSpec((ROWS, D_MODEL), lambda i: (i, 0)),
            pl.BlockSpec((grp, 128, ROWS), lambda i: (i, 0, 0)),
            pl.BlockSpec((grp, 128, ROWS), lambda i: (i, 0, 0)),
            pl.BlockSpec((grp, M_HEADS, M_DIM, M_DIM), lambda i: (i, 0, 0, 0)),
            pl.BlockSpec((M_HEADS, grp, M_DIM), lambda i: (0, i, 0)),
            pl.BlockSpec((1, 8, ROWS), lambda i: (i, 0, 0)),
            full((1, D_MODEL)), full((D_MODEL, P_MAIN)), full((T_ROWS, D_MODEL)), full((8, 1)),
            full((1, ATT_HEADS)), full((M_HEADS, M_DIM, ROWS)), full((D_MODEL, D_MODEL)), full((1, D_MODEL)),
        ],
        out_specs=(
            pl.BlockSpec((ROWS, D_MODEL), lambda i: (i, 0)),
            pl.BlockSpec((grp, 128, ROWS), lambda i: (i, 0, 0)),
            pl.BlockSpec((grp, 128, ROWS), lambda i: (i, 0, 0)),
            pl.BlockSpec((grp, M_HEADS, M_DIM, M_DIM), lambda i: (i, 0, 0, 0)),
            pl.BlockSpec((M_HEADS, grp, M_DIM), lambda i: (0, i, 0)),
            pl.BlockSpec((1, 8, ROWS), lambda i: (i, 0, 0)),
        ),
        out_shape=out_shapes,
        scratch_shapes=[
            pltpu.VMEM((ROWS, P_MAIN), F32),
            pltpu.VMEM((ATT_HEADS, ROWS, ATT_DIM), F32),
            pltpu.VMEM((ATT_HEADS, ROWS, ROWS), F32),
            pltpu.VMEM((ATT_HEADS, ROWS, ROWS), F32),
            pltpu.VMEM((M_HEADS, ROWS, M_DIM), F32),
            pltpu.VMEM((128, ROWS), F32),
            pltpu.VMEM((128, ROWS), F32),
            pltpu.VMEM((M_HEADS, ROWS, M_DIM), F32),
            pltpu.VMEM((M_HEADS, M_DIM, ROWS), BF16),
            pltpu.VMEM((M_HEADS, SAMPLE_GROUP, ROWS), F32),
            pltpu.VMEM((ATT_HEADS, ROWS, ROWS), F32),
            pltpu.VMEM((ATT_HEADS, ROWS, ROWS), F32),
        ],
        compiler_params=pltpu.CompilerParams(
            dimension_semantics=("arbitrary",), vmem_limit_bytes=VMEM_LIMIT_BYTES),
        name="sample_layer",
    )(x, kct, vct, cin, nin, m0, gpre, wmain, wt, bg, sinks, gmhb, wout, gpost)


PROMPT_BLOCK = 512
PROMPT_SUBBLOCKS = 2
PROJ_COL_STEP = 512
CHUNK_STAGES = 7


def _window_in(cache):
    nseq = cache.shape[0]
    return cache.transpose(0, 2, 3, 1).reshape(nseq, ATT_KV * ATT_DIM, ROWS)


def _window_out(win_t):
    nseq = win_t.shape[0]
    return win_t.reshape(nseq, ATT_KV, ATT_DIM, ROWS).transpose(0, 3, 1, 2)[None]


def kernel(x_prompt, x_sample, cache_win_k, cache_win_v, state_C, state_n, state_m,
           g_pre, w_in, b_gate, attn_sinks, g_mh, w_out, g_post):
    depth = g_pre.shape[0]
    assert depth == 1, "single-layer trunk"
    nseq, sseq, _ = x_sample.shape
    assert sseq == SAMPLE_SEQ and nseq % SAMPLE_GROUP == 0

    gpre = g_pre[0].reshape(1, D_MODEL)
    gpost = g_post[0].reshape(1, D_MODEL)
    wmain, wt, wout = _weights_call(w_in[0].T, w_out[0])
    bg = b_gate[0].reshape(2 * M_HEADS, 1)
    sinks = attn_sinks[0].reshape(1, ATT_HEADS)
    gmhb = jnp.broadcast_to(g_mh[0][:, :, None], (M_HEADS, M_DIM, ROWS))

    yp, wkp, wvp, cp, np_, mp = _prompt_call(x_prompt, gpre, wmain, wt, bg, sinks, gmhb, wout, gpost, PROMPT_BLOCK)

    ngroups = nseq // SAMPLE_GROUP
    m0 = jnp.broadcast_to(state_m[0].reshape(ngroups, SAMPLE_GROUP, 1, M_HEADS),
                          (ngroups, SAMPLE_GROUP, SAMPLE_SEQ, M_HEADS))
    m0 = jnp.pad(m0.reshape(ngroups, ROWS, M_HEADS).transpose(0, 2, 1), ((0, 0), (0, 8 - M_HEADS), (0, 0)))
    ys, wks, wvs, cs, ns, ms = _sample_call(
        x_sample.reshape(nseq * sseq, D_MODEL), _window_in(cache_win_k[0]), _window_in(cache_win_v[0]),
        state_C[0], state_n[0].transpose(1, 0, 2), m0,
        gpre, wmain, wt, bg, sinks, gmhb, wout, gpost)

    ms = ms[:, 0:M_HEADS, ::SAMPLE_SEQ].transpose(0, 2, 1).reshape(nseq, M_HEADS)
    return (yp, ys.reshape(nseq, sseq, D_MODEL), _window_out(wkp), _window_out(wvp),
            cp[None], np_[None], mp[:, 0:M_HEADS, 0][None],
            _window_out(wks), _window_out(wvs), cs[None], ns.transpose(1, 0, 2)[None], ms[None])
```

```python
import functools

import jax
import jax.numpy as jnp
from jax import lax
from jax.experimental import pallas as pl
from jax.experimental.pallas import tpu as pltpu

F32 = jnp.float32
BF16 = jnp.bfloat16

D_MODEL = 1024
ROWS = 128
ATT_HEADS, ATT_KV, ATT_GROUP, ATT_DIM = 8, 2, 4, 64
M_HEADS, M_DIM = 4, 128
NORM_EPS = 1e-6
NEG_BIG = -1e30
ATT_SCALE = ATT_DIM ** -0.5
K_SCALE = M_DIM ** -0.5

QA, KA, VA, ZA = 0, 512, 640, 768
QM, KM, VM, OM, ZM = 1280, 1792, 2304, 2816, 3328
GATES = 3840
D_IN = 3848

P_QA, P_ZA, P_QM, P_KM, P_OM, P_ZM, P_KA = 0, 512, 1024, 1536, 2048, 2560, 3072
P_MAIN = 3328
T_VA, T_VM, T_GATES = 0, 128, 640
T_ROWS = 656
T_SPLIT = 336
STATE_ROWS = 144
ONES_ROWS = 16

SAMPLE_SEQ = 8
SAMPLE_GROUP = ROWS // SAMPLE_SEQ
SEQ_UNROLL = 4
SAMPLE_PARTS = 2

VMEM_LIMIT_BYTES = 56 * 1024 * 1024
NT_DIMS = (((1,), (1,)), ((), ()))


def _rms(x, g):
    return x * lax.rsqrt(jnp.mean(x * x, axis=-1, keepdims=True) + NORM_EPS) * g


def _silu(x):
    return x * jax.nn.sigmoid(x)


def _log_sigmoid(x):
    return -(jnp.maximum(-x, 0.0) + jnp.log1p(jnp.exp(-jnp.abs(x))))


def _slope(head):
    return 2.0 ** -(head + 1)


def _bdot(a, b):
    return jnp.dot(a.astype(BF16), b.astype(BF16), preferred_element_type=F32)


def _bdot_nt(a, b):
    return lax.dot_general(a.astype(BF16), b.astype(BF16), NT_DIMS, preferred_element_type=F32)


def _exact_dot(x, m):
    hi = x.astype(BF16).astype(F32)
    mid = (x - hi).astype(BF16).astype(F32)
    lo = (x - hi - mid).astype(BF16).astype(F32)
    parts = jnp.dot(jnp.concatenate([hi, mid, lo, jnp.zeros_like(hi)], axis=0).astype(BF16), m,
                    preferred_element_type=F32)
    return parts[0:8] + parts[8:16] + parts[16:24]


def _gate_rows(x, m0, mask_t, last_sel):
    row = lax.broadcasted_iota(jnp.int32, (8, ROWS), 0)
    head_rows = row < M_HEADS
    ic = jnp.where(head_rows, x, 0.0)
    fc = jnp.where(head_rows, _log_sigmoid(pltpu.roll(x, M_HEADS, 0)), 0.0)
    b = _exact_dot(fc, mask_t.astype(BF16))
    a = ic - b
    a_cols = jnp.concatenate([a, jnp.zeros((ROWS - 8, ROWS), F32)], axis=0).T
    a_masked = [jnp.where(mask_t, a_cols[:, h:h + 1], -jnp.inf) for h in range(M_HEADS)]
    cm = jnp.concatenate([jnp.max(am, axis=0, keepdims=True) for am in a_masked]
                         + [jnp.zeros((8 - M_HEADS, ROWS), F32)], axis=0)
    m_t = jnp.maximum(b + m0, b + cm)
    if last_sel is None:
        b_last = jnp.broadcast_to(b[:, ROWS - 1:ROWS], b.shape)
        m_new = jnp.broadcast_to(m_t[:, ROWS - 1:ROWS], b.shape)
    else:
        both = _exact_dot(jnp.where(head_rows, b, pltpu.roll(m_t, M_HEADS, 0)), last_sel)
        b_last = jnp.where(head_rows, both, 0.0)
        m_new = jnp.where(head_rows, pltpu.roll(both, M_HEADS, 0), 0.0)
    return dict(a_masked=a_masked, bm=b - m_t, gexp=jnp.exp(b + m0 - m_t), enm=jnp.exp(-m_t), m_new=m_new,
                w=jnp.exp(b_last - b + ic - m_new), decay=jnp.exp(b_last + m0 - m_new))


def _out_tail(cat, x, wout_ref, gpost_ref):
    y = jnp.dot(cat, wout_ref[...], preferred_element_type=F32)
    return x + _rms(y, gpost_ref[...])


W_BLOCK = 256
MAIN_BLOCKS = P_MAIN // W_BLOCK


def _main_src_block(i):
    return jnp.where(i < 2, i, jnp.where(i < 8, i + 1, jnp.where(i < 12, i + 3, 2)))


def _weights_kernel(main_ref, va_ref, vm0_ref, vm1_ref, vm2_ref, vm3_ref, gates_ref, wout_ref,
                    wmain_ref, wt_ref, woutb_ref):
    wmain_ref[...] = main_ref[...].T.astype(BF16)

    @pl.when(pl.program_id(0) == 0)
    def _():
        wt_ref[...] = jnp.concatenate(
            [va_ref[...], vm0_ref[...], vm1_ref[...], vm2_ref[...], vm3_ref[...], gates_ref[...],
             jnp.zeros((T_ROWS - T_GATES - 8, D_MODEL), F32)], axis=0).astype(BF16)
        woutb_ref[...] = wout_ref[...].astype(BF16)


def _weights_call(w_in_t, w_out):
    blk = lambda r: pl.BlockSpec((128, D_MODEL), lambda i, r=r: (r, 0))
    return pl.pallas_call(
        _weights_kernel,
        grid=(MAIN_BLOCKS,),
        in_specs=[
            pl.BlockSpec((W_BLOCK, D_MODEL), lambda i: (_main_src_block(i), 0)),
            blk(VA // 128), blk(VM // 128), blk(VM // 128 + 1), blk(VM // 128 + 2), blk(VM // 128 + 3),
            pl.BlockSpec((8, D_MODEL), lambda i: (GATES // 8, 0)),
            pl.BlockSpec((D_MODEL, D_MODEL), lambda i: (0, 0)),
        ],
        out_specs=(
            pl.BlockSpec((D_MODEL, W_BLOCK), lambda i: (0, i)),
            pl.BlockSpec((T_ROWS, D_MODEL), lambda i: (0, 0)),
            pl.BlockSpec((D_MODEL, D_MODEL), lambda i: (0, 0)),
        ),
        out_shape=(
            jax.ShapeDtypeStruct((D_MODEL, P_MAIN), BF16),
            jax.ShapeDtypeStruct((T_ROWS, D_MODEL), BF16),
            jax.ShapeDtypeStruct((D_MODEL, D_MODEL), BF16),
        ),
        compiler_params=pltpu.CompilerParams(
            dimension_semantics=("arbitrary",), vmem_limit_bytes=VMEM_LIMIT_BYTES),
        name="layer_weights",
    )(w_in_t, w_in_t, w_in_t, w_in_t, w_in_t, w_in_t, w_in_t, w_out)


def _prompt_bias_t(kv, first):
    j = lax.broadcasted_iota(jnp.int32, (2 * ROWS, ROWS), 0)
    i = lax.broadcasted_iota(jnp.int32, (2 * ROWS, ROWS), 1)
    diff = ROWS + i - j
    valid = (diff >= 0) & (diff < ROWS)
    if first:
        valid = valid & (j >= ROWS)
    dfl = diff.astype(F32)
    return jnp.concatenate(
        [jnp.where(valid, -_slope(kv * ATT_GROUP + g) * dfl, NEG_BIG) for g in range(ATT_GROUP)], axis=1)


def _prompt_kernel(x_ref, xnext_ref, gpre_ref, wmain_ref, wt_ref, bg_ref, sink_ref, gmhb_ref, wout_ref, gpost_ref,
                   y_ref, wk_ref, wv_ref, c_ref, n_ref, m_ref,
                   proj_ref, projt_ref, cat_ref, xn_ref, kprev_ref, vtprev_ref, ct_ref, mst_ref, bias_ref,
                   *, tb, nt):
    bi = pl.program_id(0)
    j = pl.program_id(1)
    nchunks = tb // ROWS

    @pl.when((bi == 0) & (j == 0))
    def _():
        for first in range(2):
            for kv in range(ATT_KV):
                bias_ref[first * ATT_KV + kv] = _prompt_bias_t(kv, first)

    @pl.when(j == 0)
    def _():
        kprev_ref[...] = jnp.zeros_like(kprev_ref)
        vtprev_ref[...] = jnp.zeros_like(vtprev_ref)
        ct_ref[...] = jnp.zeros_like(ct_ref)
        mst_ref[...] = jnp.zeros_like(mst_ref)

    sub = tb // PROMPT_SUBBLOCKS
    sub_chunks = sub // ROWS

    def norm_rows(sb):
        rows_sb = slice(sb * sub, (sb + 1) * sub)
        xn_ref[rows_sb, :] = _rms(x_ref[0, rows_sb, :], gpre_ref[...]).astype(BF16)

    def proj_cols(sb, c0, c1):
        rows_sb = slice(sb * sub, (sb + 1) * sub)
        proj_ref[rows_sb, c0:c1] = jnp.dot(xn_ref[rows_sb, :], wmain_ref[:, c0:c1], preferred_element_type=F32)

    def proj_t(sb):
        for r0, r1 in ((0, T_SPLIT), (T_SPLIT, T_ROWS)):
            pt = lax.dot_general(wt_ref[r0:r1, :], xn_ref[sb * sub:(sb + 1) * sub, :], NT_DIMS,
                                 preferred_element_type=F32)
            for c in range(sub_chunks):
                projt_ref[sb * sub_chunks + c, r0:r1, :] = pt[:, c * ROWS:(c + 1) * ROWS]

    def out_rows(sb):
        rows_sb = slice(sb * sub, (sb + 1) * sub)
        y_ref[0, rows_sb, :] = _out_tail(cat_ref[rows_sb, :], x_ref[0, rows_sb, :], wout_ref, gpost_ref)

    def proj_pieces(sb):
        bounds = list(range(0, P_MAIN, PROJ_COL_STEP)) + [P_MAIN]
        pieces = [functools.partial(proj_cols, sb, c0, c1) for c0, c1 in zip(bounds[:-1], bounds[1:])]
        return pieces[:2] + [functools.partial(proj_t, sb)] + pieces[2:]

    ri = lax.broadcasted_iota(jnp.int32, (ROWS, ROWS), 0)
    ci = lax.broadcasted_iota(jnp.int32, (ROWS, ROWS), 1)
    mask_t = ri <= ci
    st_row = lax.broadcasted_iota(jnp.int32, (STATE_ROWS - M_DIM, ROWS), 0)

    def chunk(c):
        rows = pl.ds(c * ROWS, ROWS)
        first = ((j == 0) & (c == 0)).astype(jnp.int32)
        yield

        qa = proj_ref[rows, P_QA:P_QA + 512] * ATT_SCALE
        kcur = proj_ref[rows, P_KA:P_KA + 128].astype(BF16)
        vtcur = projt_ref[c, T_VA:T_VA + 128, :].astype(BF16)
        kcat = jnp.concatenate([kprev_ref[...], kcur], axis=0)
        vtcat = jnp.concatenate([vtprev_ref[...], vtcur], axis=1)
        r = _gate_rows(projt_ref[c, T_GATES:T_GATES + 8, :] + bg_ref[...], mst_ref[...], mask_t, None)

        scores, sinks = [], []
        for kv in range(ATT_KV):
            want_hi = kv == 1
            keep = (ci >= ATT_DIM) if want_hi else (ci < ATT_DIM)
            pieces = []
            for g in range(ATT_GROUP):
                hh = kv * ATT_GROUP + g
                blk = qa[:, (hh // 2) * 128:(hh // 2 + 1) * 128]
                if (hh % 2 == 1) != want_hi:
                    blk = pltpu.roll(blk, ATT_DIM, 1)
                pieces.append(jnp.where(keep, blk, 0.0))
            q4 = jnp.concatenate(pieces, axis=0)
            scores.append(_bdot_nt(kcat, q4) + bias_ref[first * ATT_KV + kv])
            sinks.append(jnp.concatenate(
                [jnp.broadcast_to(sink_ref[0:1, kv * ATT_GROUP + g:kv * ATT_GROUP + g + 1], (1, ROWS))
                 for g in range(ATT_GROUP)], axis=1))
        yield
        vts, r1s = [], []
        for h in range(M_HEADS):
            q = proj_ref[rows, P_QM + h * M_DIM:P_QM + (h + 1) * M_DIM].astype(BF16)
            k = (proj_ref[rows, P_KM + h * M_DIM:P_KM + (h + 1) * M_DIM] * K_SCALE).astype(BF16)
            vt = projt_ref[c, T_VM + h * M_DIM:T_VM + (h + 1) * M_DIM, :]
            ct = ct_ref[h]
            r1s.append(lax.dot_general(jnp.concatenate([k, ct.astype(BF16)], axis=0), q, NT_DIMS,
                                       preferred_element_type=F32))
            w_row = r["w"][h:h + 1, :]
            lhs2 = jnp.concatenate([vt * w_row, jnp.where(st_row == 0, w_row, 0.0)], axis=0)
            ct_ref[h] = r["decay"][h:h + 1, 0:1] * ct + _bdot(lhs2, k)
            vts.append(vt)
        mst_ref[...] = r["m_new"]
        yield

        outs, nums, dens = [], [], []
        for kv in range(ATT_KV):
            s, sink = scores[kv], sinks[kv]
            mx = jnp.maximum(jnp.max(s, axis=0, keepdims=True), sink)
            p = jnp.exp(s - mx).astype(BF16)
            lhs = jnp.concatenate([vtcat[kv * ATT_DIM:(kv + 1) * ATT_DIM, :],
                                   jnp.ones((ONES_ROWS, 2 * ROWS), BF16)], axis=0)
            o = jnp.dot(lhs, p, preferred_element_type=F32)
            outs.append((o, jnp.exp(sink - mx)))
        yield
        for h in range(M_HEADS):
            r1 = r1s[h]
            dm = jnp.exp(r["a_masked"][h] + r["bm"][h:h + 1, :])
            st = r1[0:ROWS] * dm
            g_row = r["gexp"][h:h + 1, :]
            nums.append(_bdot(vts[h], st) + g_row * r1[ROWS:ROWS + M_DIM])
            dens.append(jnp.sum(st, axis=0, keepdims=True) + g_row * r1[ROWS + M_DIM:ROWS + M_DIM + 1])
        yield

        att = []
        for kv in range(ATT_KV):
            o, esink = outs[kv]
            on = o[0:ATT_DIM, :] / (o[ATT_DIM:ATT_DIM + 1, :] + esink)
            for pair in range(2):
                two = jnp.concatenate([on[:, (2 * pair) * ROWS:(2 * pair + 1) * ROWS],
                                       on[:, (2 * pair + 1) * ROWS:(2 * pair + 2) * ROWS]], axis=0)
                att.append(two.T)
        a_out = jnp.concatenate(att, axis=1) * _silu(proj_ref[rows, P_ZA:P_ZA + 512])
        kprev_ref[...] = kcur
        vtprev_ref[...] = vtcur
        yield
        m_out = []
        for h in range(M_HEADS):
            ht = nums[h] / jnp.maximum(jnp.abs(dens[h]), r["enm"][h:h + 1, :])
            hn = ht * lax.rsqrt(jnp.mean(ht * ht, axis=0, keepdims=True) + NORM_EPS) * gmhb_ref[h]
            m_out.append(jax.nn.sigmoid(proj_ref[rows, P_OM + h * M_DIM:P_OM + (h + 1) * M_DIM]) * hn.T
                         * _silu(proj_ref[rows, P_ZM + h * M_DIM:P_ZM + (h + 1) * M_DIM]))

        cat_ref[rows, :] = jnp.concatenate([a_out] + m_out, axis=1).astype(BF16)
        yield

    @pl.when((bi == 0) & (j == 0))
    def _():
        norm_rows(0)
        for piece in proj_pieces(0):
            piece()

    def norm_next():
        xn_ref[0:sub, :] = _rms(xnext_ref[0], gpre_ref[...]).astype(BF16)

    def run_chunks(sb, fillers):
        n_fill, n_slots, slot = len(fillers), sub_chunks * CHUNK_STAGES, 0
        for c in range(sb * sub_chunks, (sb + 1) * sub_chunks):
            for _ in chunk(c):
                for _ in range(-(-(slot + 1) * n_fill // n_slots) + (-slot * n_fill // n_slots)):
                    fillers.pop(0)()
                slot += 1
        assert slot == n_slots and not fillers

    assert PROMPT_SUBBLOCKS == 2
    run_chunks(0, [functools.partial(norm_rows, 1)] + proj_pieces(1))
    run_chunks(1, [functools.partial(out_rows, 0), norm_next] + proj_pieces(0))
    out_rows(1)

    @pl.when(j == nt - 1)
    def _():
        wk_ref[0] = proj_ref[tb - ROWS:tb, P_KA:P_KA + 128].T
        wv_ref[0] = projt_ref[nchunks - 1, T_VA:T_VA + 128, :]
        for h in range(M_HEADS):
            ct = ct_ref[h]
            c_ref[0, h] = ct[0:M_DIM].T
            n_ref[0, h:h + 1, :] = ct[M_DIM:M_DIM + 1]
        m_ref[0] = mst_ref[...]


def _prompt_call(x, gpre, wmain, wt, bg, sinks, gmhb, wout, gpost, tb):
    bsz, seq, _ = x.shape
    nt = seq // tb
    full = lambda shape: pl.BlockSpec(shape, lambda b, j: (0,) * len(shape))

    def next_first_subblock(b, j):
        wrap = (j + 1 == nt).astype(jnp.int32)
        return (jnp.minimum(b + wrap, bsz - 1), (j + 1) * (1 - wrap) * PROMPT_SUBBLOCKS, 0)

    out_shapes = (
        jax.ShapeDtypeStruct((bsz, seq, D_MODEL), F32),
        jax.ShapeDtypeStruct((bsz, 128, ROWS), F32),
        jax.ShapeDtypeStruct((bsz, 128, ROWS), F32),
        jax.ShapeDtypeStruct((bsz, M_HEADS, M_DIM, M_DIM), F32),
        jax.ShapeDtypeStruct((bsz, M_HEADS, M_DIM), F32),
        jax.ShapeDtypeStruct((bsz, 8, ROWS), F32),
    )
    return pl.pallas_call(
        functools.partial(_prompt_kernel, tb=tb, nt=nt),
        grid=(bsz, nt),
        in_specs=[
            pl.BlockSpec((1, tb, D_MODEL), lambda b, j: (b, j, 0)),
            pl.BlockSpec((1, tb // PROMPT_SUBBLOCKS, D_MODEL), next_first_subblock),
            full((1, D_MODEL)), full((D_MODEL, P_MAIN)), full((T_ROWS, D_MODEL)), full((8, 1)),
            full((1, ATT_HEADS)), full((M_HEADS, M_DIM, ROWS)), full((D_MODEL, D_MODEL)), full((1, D_MODEL)),
        ],
        out_specs=(
            pl.BlockSpec((1, tb, D_MODEL), lambda b, j: (b, j, 0)),
            pl.BlockSpec((1, 128, ROWS), lambda b, j: (b, 0, 0)),
            pl.BlockSpec((1, 128, ROWS), lambda b, j: (b, 0, 0)),
            pl.BlockSpec((1, M_HEADS, M_DIM, M_DIM), lambda b, j: (b, 0, 0, 0)),
            pl.BlockSpec((1, M_HEADS, M_DIM), lambda b, j: (b, 0, 0)),
            pl.BlockSpec((1, 8, ROWS), lambda b, j: (b, 0, 0)),
        ),
        out_shape=out_shapes,
        scratch_shapes=[
            pltpu.VMEM((tb, P_MAIN), F32),
            pltpu.VMEM((tb // ROWS, T_ROWS, ROWS), F32),
            pltpu.VMEM((tb, D_MODEL), BF16),
            pltpu.VMEM((tb, D_MODEL), BF16),
            pltpu.VMEM((ROWS, 128), BF16),
            pltpu.VMEM((128, ROWS), BF16),
            pltpu.VMEM((M_HEADS, STATE_ROWS, M_DIM), F32),
            pltpu.VMEM((8, ROWS), F32),
            pltpu.VMEM((2 * ATT_KV, 2 * ROWS, ATT_GROUP * ROWS), F32),
        ],
        compiler_params=pltpu.CompilerParams(
            dimension_semantics=("arbitrary", "arbitrary"), vmem_limit_bytes=VMEM_LIMIT_BYTES),
        name="prompt_layer",
    )(x, x, gpre, wmain, wt, bg, sinks, gmhb, wout, gpost)


def _sample_bias_new(head):
    r = lax.broadcasted_iota(jnp.int32, (ROWS, ROWS), 0)
    c = lax.broadcasted_iota(jnp.int32, (ROWS, ROWS), 1)
    valid = ((r >> 3) == (c >> 3)) & (r >= c)
    return jnp.where(valid, -_slope(head) * (r - c).astype(F32), NEG_BIG)


def _sample_bias_cache(head):
    r = lax.broadcasted_iota(jnp.int32, (ROWS, ROWS), 0)
    c = lax.broadcasted_iota(jnp.int32, (ROWS, ROWS), 1)
    diff = (r & (SAMPLE_SEQ - 1)) + ROWS - c
    return jnp.where(diff < ROWS, -_slope(head) * diff.astype(F32), NEG_BIG)


def _sample_kernel(x_ref, kct_ref, vct_ref, cin_ref, nin_ref, m0_ref,
                   gpre_ref, wmain_ref, wt_ref, bg_ref, sink_ref, gmhb_ref, wout_ref, gpost_ref,
                   y_ref, kot_ref, vot_ref, cout_ref, nout_ref, mout_ref,
                   projfull_ref, projt_ref, cat_ref, qh_ref, sc_ref, oc_ref, qc_ref, kat_ref, vat_ref, wv_ref, kt_ref,
                   decb_ref, bn_ref, bc_ref):
    step, part = pl.program_id(0), pl.program_id(1)

    @pl.when((step == 0) & (part == 0))
    def _():
        for hh in range(ATT_HEADS):
            bn_ref[hh] = _sample_bias_new(hh)
            bc_ref[hh] = _sample_bias_cache(hh)

    @pl.when(part == 0)
    def _():
        xn = _rms(x_ref[...], gpre_ref[...]).astype(BF16)
        projfull_ref[...] = jnp.dot(xn, wmain_ref[...], preferred_element_type=F32)
        ptf = lax.dot_general(wt_ref[...], xn, NT_DIMS, preferred_element_type=F32)
        for g in range(SAMPLE_PARTS):
            projt_ref[g] = ptf[:, g * ROWS:(g + 1) * ROWS]

    part_rows = pl.ds(pl.multiple_of(part * ROWS, ROWS), ROWS)
    proj_ref = projfull_ref.at[part_rows]
    pt = projt_ref[part]

    ri = lax.broadcasted_iota(jnp.int32, (ROWS, ROWS), 0)
    ci = lax.broadcasted_iota(jnp.int32, (ROWS, ROWS), 1)
    same_seq = (ri >> 3) == (ci >> 3)
    mask_t = same_seq & (ri <= ci)
    last_sel = (same_seq & ((ri & (SAMPLE_SEQ - 1)) == SAMPLE_SEQ - 1)).astype(BF16)
    row16 = lax.broadcasted_iota(jnp.int32, (SAMPLE_GROUP, ROWS), 0)
    lane16 = lax.broadcasted_iota(jnp.int32, (SAMPLE_GROUP, ROWS), 1)
    seq_of_lane = (lane16 >> 3) == row16
    ones_rows = jnp.ones((ONES_ROWS, ROWS), F32)

    qa = proj_ref[:, P_QA:P_QA + 512] * ATT_SCALE
    for hh in range(ATT_HEADS):
        blk = qa[:, (hh // 2) * 128:(hh // 2 + 1) * 128]
        if hh % 2 == 1:
            blk = pltpu.roll(blk, ATT_DIM, 1)
        qh_ref[hh] = blk[:, 0:ATT_DIM]
    ka = proj_ref[:, P_KA:P_KA + 128]
    kat = ka.T
    vat = pt[T_VA:T_VA + 128, :]
    kat_ref[...] = kat
    vat_ref[...] = vat

    def seq_scores(b, carry):
        rows = pl.ds(pl.multiple_of(b * SAMPLE_SEQ, SAMPLE_SEQ), SAMPLE_SEQ)
        kct = kct_ref[b]
        for kv in range(ATT_KV):
            lhs = jnp.concatenate([qh_ref[kv * ATT_GROUP + g, rows, :] for g in range(ATT_GROUP)], axis=0)
            res = _bdot(lhs, kct[kv * ATT_DIM:(kv + 1) * ATT_DIM, :])
            for g in range(ATT_GROUP):
                sc_ref[kv * ATT_GROUP + g, rows, :] = res[g * SAMPLE_SEQ:(g + 1) * SAMPLE_SEQ]
        for h in range(M_HEADS):
            qc_ref[h, rows, :] = _bdot(proj_ref[rows, P_QM + h * M_DIM:P_QM + (h + 1) * M_DIM], cin_ref[b, h])
        return carry

    lax.fori_loop(0, SAMPLE_GROUP, seq_scores, 0, unroll=SEQ_UNROLL)

    o_new, esinks = [], []
    for hh in range(ATT_HEADS):
        kv = hh // ATT_GROUP
        s_n = _bdot(qh_ref[hh], kat[kv * ATT_DIM:(kv + 1) * ATT_DIM, :]) + bn_ref[hh]
        s_c = sc_ref[hh] + bc_ref[hh]
        sink = sink_ref[0:1, hh:hh + 1]
        mx = jnp.maximum(jnp.maximum(jnp.max(s_n, axis=-1, keepdims=True),
                                     jnp.max(s_c, axis=-1, keepdims=True)), sink)
        p_n = jnp.exp(s_n - mx)
        sc_ref[hh] = jnp.exp(s_c - mx)
        vaug = jnp.concatenate([vat[kv * ATT_DIM:(kv + 1) * ATT_DIM, :], ones_rows], axis=0)
        o_new.append(_bdot_nt(p_n, vaug))
        esinks.append(jnp.exp(sink - mx))

    r = _gate_rows(pt[T_GATES:T_GATES + 8, :] + bg_ref[...], m0_ref[0], mask_t, last_sel)
    mout_ref[0] = r["m_new"]
    m_out = []
    for h in range(M_HEADS):
        q = proj_ref[:, P_QM + h * M_DIM:P_QM + (h + 1) * M_DIM].astype(BF16)
        kf = proj_ref[:, P_KM + h * M_DIM:P_KM + (h + 1) * M_DIM] * K_SCALE
        k = kf.astype(BF16)
        vt = pt[T_VM + h * M_DIM:T_VM + (h + 1) * M_DIM, :]
        n_h = nin_ref[h]
        r1 = lax.dot_general(jnp.concatenate([k, n_h.astype(BF16)], axis=0), q, NT_DIMS,
                             preferred_element_type=F32)
        st = r1[0:ROWS] * jnp.exp(r["a_masked"][h] + r["bm"][h:h + 1, :])
        q_n = jnp.sum(jnp.where(seq_of_lane, r1[ROWS:ROWS + SAMPLE_GROUP], 0.0), axis=0, keepdims=True)
        g_row = r["gexp"][h:h + 1, :]
        num = _bdot(vt, st) + g_row * qc_ref[h].T
        den = jnp.sum(st, axis=0, keepdims=True) + g_row * q_n
        ht = num / jnp.maximum(jnp.abs(den), r["enm"][h:h + 1, :])
        hn = ht * lax.rsqrt(jnp.mean(ht * ht, axis=0, keepdims=True) + NORM_EPS) * gmhb_ref[h]
        m_out.append(jax.nn.sigmoid(proj_ref[:, P_OM + h * M_DIM:P_OM + (h + 1) * M_DIM]) * hn.T
                     * _silu(proj_ref[:, P_ZM + h * M_DIM:P_ZM + (h + 1) * M_DIM]))
        w_row = r["w"][h:h + 1, :]
        dec16 = jnp.sum(jnp.where(lane16 == row16 * SAMPLE_SEQ, r["decay"][h:h + 1, :], 0.0),
                        axis=1, keepdims=True)
        nout_ref[h] = dec16 * n_h + _bdot(jnp.where(seq_of_lane, w_row, 0.0), k)
        decb_ref[h] = jnp.broadcast_to(dec16, (SAMPLE_GROUP, ROWS))
        wv_ref[h] = (vt * w_row).T
        kt_ref[h] = kf.T.astype(BF16)

    keep_new = ci >= ROWS - SAMPLE_SEQ

    def seq_update(b, carry):
        rows = pl.ds(pl.multiple_of(b * SAMPLE_SEQ, SAMPLE_SEQ), SAMPLE_SEQ)
        vct = vct_ref[b]
        for kv in range(ATT_KV):
            vaug = jnp.concatenate([vct[kv * ATT_DIM:(kv + 1) * ATT_DIM, :], ones_rows], axis=0)
            pl_ = jnp.concatenate([sc_ref[kv * ATT_GROUP + g, rows, :] for g in range(ATT_GROUP)], axis=0)
            res = _bdot_nt(pl_, vaug)
            for g in range(ATT_GROUP):
                oc_ref[kv * ATT_GROUP + g, rows, 0:ATT_DIM + ONES_ROWS] = res[g * SAMPLE_SEQ:(g + 1) * SAMPLE_SEQ]
        in_seq = (ri >> 3) == b
        for h in range(M_HEADS):
            upd = jnp.dot(kt_ref[h], jnp.where(in_seq, wv_ref[h], 0.0).astype(BF16), preferred_element_type=F32)
            cout_ref[b, h] = decb_ref[h, pl.ds(b, 1), :] * cin_ref[b, h] + upd
        shift = (ROWS - SAMPLE_SEQ - b * SAMPLE_SEQ) & (ROWS - 1)
        kot_ref[b] = jnp.where(keep_new, pltpu.roll(kat_ref[...], shift, 1),
                               pltpu.roll(kct_ref[b], ROWS - SAMPLE_SEQ, 1))
        vot_ref[b] = jnp.where(keep_new, pltpu.roll(vat_ref[...], shift, 1),
                               pltpu.roll(vct, ROWS - SAMPLE_SEQ, 1))
        return carry

    lax.fori_loop(0, SAMPLE_GROUP, seq_update, 0, unroll=SEQ_UNROLL)

    att = []
    for pair in range(ATT_HEADS // 2):
        halves = []
        for hh in (2 * pair, 2 * pair + 1):
            on, oc = o_new[hh], oc_ref[hh]
            den = on[:, ATT_DIM:ATT_DIM + 1] + oc[:, ATT_DIM:ATT_DIM + 1] + esinks[hh]
            halves.append((on[:, 0:ATT_DIM] + oc[:, 0:ATT_DIM]) / den)
        att.append(jnp.concatenate(halves, axis=1))
    a_out = jnp.concatenate(att, axis=1) * _silu(proj_ref[:, P_ZA:P_ZA + 512])
    cat_ref[part_rows, :] = jnp.concatenate([a_out] + m_out, axis=1).astype(BF16)

    @pl.when(part == SAMPLE_PARTS - 1)
    def _():
        y_ref[...] = _out_tail(cat_ref[...], x_ref[...], wout_ref, gpost_ref)


def _sample_call(x, kct, vct, cin, nin, m0, gpre, wmain, wt, bg, sinks, gmhb, wout, gpost):
    nrows = x.shape[0]
    ngroups = nrows // ROWS
    nseq = ngroups * SAMPLE_GROUP
    parts = SAMPLE_PARTS
    assert ngroups % parts == 0
    full = lambda shape: pl.BlockSpec(shape, lambda i, p: (0,) * len(shape))
    grp = SAMPLE_GROUP
    group = lambda i, p: i * parts + p
    out_shapes = (
        jax.ShapeDtypeStruct((nrows, D_MODEL), F32),
        jax.ShapeDtypeStruct((nseq, 128, ROWS), F32),
        jax.ShapeDtypeStruct((nseq, 128, ROWS), F32),
        jax.ShapeDtypeStruct((nseq, M_HEADS, M_DIM, M_DIM), F32),
        jax.ShapeDtypeStruct((M_HEADS, nseq, M_DIM), F32),
        jax.ShapeDtypeStruct((ngroups, 8, ROWS), F32),
    )
    return pl.pallas_call(
        _sample_kernel,
        grid=(ngroups // parts, parts),
        in_specs=[
            pl.BlockSpec((parts * ROWS, D_MODEL), lambda i, p: (i, 0)),
            pl.BlockSpec((grp, 128, ROWS), lambda i, p: (group(i, p), 0, 0)),
            pl.BlockSpec((grp, 128, ROWS), lambda i, p: (group(i, p), 0, 0)),
            pl.BlockSpec((grp, M_HEADS, M_DIM, M_DIM), lambda i, p: (group(i, p), 0, 0, 0)),
            pl.BlockSpec((M_HEADS, grp, M_DIM), lambda i, p: (0, group(i, p), 0)),
            pl.BlockSpec((1, 8, ROWS), lambda i, p: (group(i, p), 0, 0)),
            full((1, D_MODEL)), full((D_MODEL, P_MAIN)), full((T_ROWS, D_MODEL)), full((8, 1)),
            full((1, ATT_HEADS)), full((M_HEADS, M_DIM, ROWS)), full((D_MODEL, D_MODEL)), full((1, D_MODEL)),
        ],
        out_specs=(
            pl.BlockSpec((parts * ROWS, D_MODEL), lambda i, p: (i, 0)),
            pl.BlockSpec((grp, 128, ROWS), lambda i, p: (group(i, p), 0, 0)),
            pl.BlockSpec((grp, 128, ROWS), lambda i, p: (group(i, p), 0, 0)),
            pl.BlockSpec((grp, M_HEADS, M_DIM, M_DIM), lambda i, p: (group(i, p), 0, 0, 0)),
            pl.BlockSpec((M_HEADS, grp, M_DIM), lambda i, p: (0, group(i, p), 0)),
            pl.BlockSpec((1, 8, ROWS), lambda i, p: (group(i, p), 0, 0)),
        ),
        out_shape=out_shapes,
        scratch_shapes=[
            pltpu.VMEM((parts * ROWS, P_MAIN), F32),
            pltpu.VMEM((parts, T_ROWS, ROWS), F32),
            pltpu.VMEM((parts * ROWS, D_MODEL), BF16),
            pltpu.VMEM((ATT_HEADS, ROWS, ATT_DIM), F32),
            pltpu.VMEM((ATT_HEADS, ROWS, ROWS), F32),
            pltpu.VMEM((ATT_HEADS, ROWS, ROWS), F32),
            pltpu.VMEM((M_HEADS, ROWS, M_DIM), F32),
            pltpu.VMEM((128, ROWS), F32),
            pltpu.VMEM((128, ROWS), F32),
            pltpu.VMEM((M_HEADS, ROWS, M_DIM), F32),
            pltpu.VMEM((M_HEADS, M_DIM, ROWS), BF16),
            pltpu.VMEM((M_HEADS, SAMPLE_GROUP, ROWS), F32),
            pltpu.VMEM((ATT_HEADS, ROWS, ROWS), F32),
            pltpu.VMEM((ATT_HEADS, ROWS, ROWS), F32),
        ],
        compiler_params=pltpu.CompilerParams(
            dimension_semantics=("arbitrary", "arbitrary"), vmem_limit_bytes=VMEM_LIMIT_BYTES),
        name="sample_layer",
    )(x, kct, vct, cin, nin, m0, gpre, wmain, wt, bg, sinks, gmhb, wout, gpost)


PROMPT_BLOCK = 512
PROMPT_SUBBLOCKS = 2
PROJ_COL_STEP = 512
CHUNK_STAGES = 7


def _window_in(cache):
    nseq = cache.shape[0]
    return cache.transpose(0, 2, 3, 1).reshape(nseq, ATT_KV * ATT_DIM, ROWS)


def _window_out(win_t):
    nseq = win_t.shape[0]
    return win_t.reshape(nseq, ATT_KV, ATT_DIM, ROWS).transpose(0, 3, 1, 2)[None]


def kernel(x_prompt, x_sample, cache_win_k, cache_win_v, state_C, state_n, state_m,
           g_pre, w_in, b_gate, attn_sinks, g_mh, w_out, g_post):
    depth = g_pre.shape[0]
    assert depth == 1, "single-layer trunk"
    nseq, sseq, _ = x_sample.shape
    assert sseq == SAMPLE_SEQ and nseq % SAMPLE_GROUP == 0

    gpre = g_pre[0].reshape(1, D_MODEL)
    gpost = g_post[0].reshape(1, D_MODEL)
    wmain, wt, wout = _weights_call(w_in[0].T, w_out[0])
    bg = b_gate[0].reshape(2 * M_HEADS, 1)
    sinks = attn_sinks[0].reshape(1, ATT_HEADS)
    gmhb = jnp.broadcast_to(g_mh[0][:, :, None], (M_HEADS, M_DIM, ROWS))

    yp, wkp, wvp, cp, np_, mp = _prompt_call(x_prompt, gpre, wmain, wt, bg, sinks, gmhb, wout, gpost, PROMPT_BLOCK)

    ngroups = nseq // SAMPLE_GROUP
    m0 = jnp.broadcast_to(state_m[0].reshape(ngroups, SAMPLE_GROUP, 1, M_HEADS),
                          (ngroups, SAMPLE_GROUP, SAMPLE_SEQ, M_HEADS))
    m0 = jnp.pad(m0.reshape(ngroups, ROWS, M_HEADS).transpose(0, 2, 1), ((0, 0), (0, 8 - M_HEADS), (0, 0)))
    ys, wks, wvs, cs, ns, ms = _sample_call(
        x_sample.reshape(nseq * sseq, D_MODEL), _window_in(cache_win_k[0]), _window_in(cache_win_v[0]),
        state_C[0], state_n[0].transpose(1, 0, 2), m0,
        gpre, wmain, wt, bg, sinks, gmhb, wout, gpost)

    ms = ms[:, 0:M_HEADS, ::SAMPLE_SEQ].transpose(0, 2, 1).reshape(nseq, M_HEADS)
    return (yp, ys.reshape(nseq, sseq, D_MODEL), _window_out(wkp), _window_out(wvp),
            cp[None], np_[None], mp[:, 0:M_HEADS, 0][None],
            _window_out(wks), _window_out(wvs), cs[None], ns.transpose(1, 0, 2)[None], ms[None])
```

```python
import functools

import jax
import jax.numpy as jnp
from jax import lax
from jax.experimental import pallas as pl
from jax.experimental.pallas import tpu as pltpu

F32 = jnp.float32
BF16 = jnp.bfloat16

D_MODEL = 1024
ROWS = 128
ATT_HEADS, ATT_KV, ATT_GROUP, ATT_DIM = 8, 2, 4, 64
M_HEADS, M_DIM = 4, 128
NORM_EPS = 1e-6
NEG_BIG = -1e30
ATT_SCALE = ATT_DIM ** -0.5
LOG2E = 1.4426950408889634
K_SCALE = M_DIM ** -0.5

QA, KA, VA, ZA = 0, 512, 640, 768
QM, KM, VM, OM, ZM = 1280, 1792, 2304, 2816, 3328
GATES = 3840
D_IN = 3848

P_QA, P_ZA, P_QM, P_KM, P_OM, P_ZM, P_KA = 0, 512, 1024, 1536, 2048, 2560, 3072
P_MAIN = 3328
T_VA, T_VM, T_GATES = 0, 128, 640
T_ROWS = 656
T_SPLIT = 336
STATE_ROWS = 144
ONES_ROWS = 16

SAMPLE_SEQ = 8
SAMPLE_GROUP = ROWS // SAMPLE_SEQ
SEQ_UNROLL = 4
SAMPLE_PARTS = 2

VMEM_LIMIT_BYTES = 56 * 1024 * 1024
NT_DIMS = (((1,), (1,)), ((), ()))


def _rms(x, g):
    return x * lax.rsqrt(jnp.mean(x * x, axis=-1, keepdims=True) + NORM_EPS) * g


def _sigmoid(x):
    return 0.5 + 0.5 * jnp.tanh(0.5 * x)


def _silu(x):
    h = 0.5 * x
    return h + h * jnp.tanh(h)


def _log_sigmoid(x):
    return -(jnp.maximum(-x, 0.0) + jnp.log1p(jnp.exp(-jnp.abs(x))))


def _slope(head):
    return 2.0 ** -(head + 1)


def _bdot(a, b):
    return jnp.dot(a.astype(BF16), b.astype(BF16), preferred_element_type=F32)


def _bdot_nt(a, b):
    return lax.dot_general(a.astype(BF16), b.astype(BF16), NT_DIMS, preferred_element_type=F32)


def _exact_dot(x, m):
    hi = x.astype(BF16).astype(F32)
    mid = (x - hi).astype(BF16).astype(F32)
    lo = (x - hi - mid).astype(BF16).astype(F32)
    parts = jnp.dot(jnp.concatenate([hi, mid, lo, jnp.zeros_like(hi)], axis=0).astype(BF16), m,
                    preferred_element_type=F32)
    return parts[0:8] + parts[8:16] + parts[16:24]


def _gate_rows(x, m0, mask_t, last_sel):
    row = lax.broadcasted_iota(jnp.int32, (8, ROWS), 0)
    head_rows = row < M_HEADS
    ic = jnp.where(head_rows, x, 0.0)
    fc = jnp.where(head_rows, _log_sigmoid(pltpu.roll(x, M_HEADS, 0)), 0.0)
    b = _exact_dot(fc, mask_t.astype(BF16))
    a = ic - b
    a_cols = jnp.concatenate([a, jnp.zeros((ROWS - 8, ROWS), F32)], axis=0).T
    a_masked = [jnp.where(mask_t, a_cols[:, h:h + 1], -jnp.inf) for h in range(M_HEADS)]
    cm = jnp.concatenate([jnp.max(am, axis=0, keepdims=True) for am in a_masked]
                         + [jnp.zeros((8 - M_HEADS, ROWS), F32)], axis=0)
    m_t = jnp.maximum(b + m0, b + cm)
    if last_sel is None:
        b_last = jnp.broadcast_to(b[:, ROWS - 1:ROWS], b.shape)
        m_new = jnp.broadcast_to(m_t[:, ROWS - 1:ROWS], b.shape)
    else:
        both = _exact_dot(jnp.where(head_rows, b, pltpu.roll(m_t, M_HEADS, 0)), last_sel)
        b_last = jnp.where(head_rows, both, 0.0)
        m_new = jnp.where(head_rows, pltpu.roll(both, M_HEADS, 0), 0.0)
    return dict(a_masked=a_masked, bm=b - m_t, gexp=jnp.exp(b + m0 - m_t), enm=jnp.exp(-m_t), m_new=m_new,
                w=jnp.exp(b_last - b + ic - m_new), decay=jnp.exp(b_last + m0 - m_new))


def _out_tail(cat, x, wout_ref, gpost_ref):
    y = jnp.dot(cat, wout_ref[...], preferred_element_type=F32)
    return x + _rms(y, gpost_ref[...])


W_BLOCK = 256
MAIN_BLOCKS = P_MAIN // W_BLOCK


def _main_src_block(i):
    return jnp.where(i < 2, i, jnp.where(i < 8, i + 1, jnp.where(i < 12, i + 3, 2)))


def _weights_kernel(main_ref, va_ref, vm0_ref, vm1_ref, vm2_ref, vm3_ref, gates_ref, wout_ref,
                    wmain_ref, wt_ref, woutb_ref):
    wmain_ref[...] = main_ref[...].T.astype(BF16)

    @pl.when(pl.program_id(0) == 0)
    def _():
        wt_ref[...] = jnp.concatenate(
            [va_ref[...], vm0_ref[...], vm1_ref[...], vm2_ref[...], vm3_ref[...], gates_ref[...],
             jnp.zeros((T_ROWS - T_GATES - 8, D_MODEL), F32)], axis=0).astype(BF16)
        woutb_ref[...] = wout_ref[...].astype(BF16)


def _weights_call(w_in_t, w_out):
    blk = lambda r: pl.BlockSpec((128, D_MODEL), lambda i, r=r: (r, 0))
    return pl.pallas_call(
        _weights_kernel,
        grid=(MAIN_BLOCKS,),
        in_specs=[
            pl.BlockSpec((W_BLOCK, D_MODEL), lambda i: (_main_src_block(i), 0)),
            blk(VA // 128), blk(VM // 128), blk(VM // 128 + 1), blk(VM // 128 + 2), blk(VM // 128 + 3),
            pl.BlockSpec((8, D_MODEL), lambda i: (GATES // 8, 0)),
            pl.BlockSpec((D_MODEL, D_MODEL), lambda i: (0, 0)),
        ],
        out_specs=(
            pl.BlockSpec((D_MODEL, W_BLOCK), lambda i: (0, i)),
            pl.BlockSpec((T_ROWS, D_MODEL), lambda i: (0, 0)),
            pl.BlockSpec((D_MODEL, D_MODEL), lambda i: (0, 0)),
        ),
        out_shape=(
            jax.ShapeDtypeStruct((D_MODEL, P_MAIN), BF16),
            jax.ShapeDtypeStruct((T_ROWS, D_MODEL), BF16),
            jax.ShapeDtypeStruct((D_MODEL, D_MODEL), BF16),
        ),
        compiler_params=pltpu.CompilerParams(
            dimension_semantics=("arbitrary",), vmem_limit_bytes=VMEM_LIMIT_BYTES),
        name="layer_weights",
    )(w_in_t, w_in_t, w_in_t, w_in_t, w_in_t, w_in_t, w_in_t, w_out)


def _prompt_bias_t(kv, first):
    j = lax.broadcasted_iota(jnp.int32, (2 * ROWS, ROWS), 0)
    i = lax.broadcasted_iota(jnp.int32, (2 * ROWS, ROWS), 1)
    diff = ROWS + i - j
    valid = (diff >= 0) & (diff < ROWS)
    if first:
        valid = valid & (j >= ROWS)
    dfl = diff.astype(F32)
    return jnp.concatenate(
        [jnp.where(valid, -_slope(kv * ATT_GROUP + g) * dfl, NEG_BIG) * LOG2E for g in range(ATT_GROUP)], axis=1)


def _prompt_kernel(x_ref, xnext_ref, gpre_ref, wmain_ref, wt_ref, bg_ref, sink_ref, gmhb_ref, wout_ref, gpost_ref,
                   y_ref, wk_ref, wv_ref, c_ref, n_ref, m_ref,
                   proj_ref, projt_ref, cat_ref, xn_ref, kprev_ref, vtprev_ref, ct_ref, mst_ref, bias_ref,
                   *, tb, nt):
    bi = pl.program_id(0)
    j = pl.program_id(1)
    nchunks = tb // ROWS

    @pl.when((bi == 0) & (j == 0))
    def _():
        for first in range(2):
            for kv in range(ATT_KV):
                bias_ref[first * ATT_KV + kv] = _prompt_bias_t(kv, first)

    @pl.when(j == 0)
    def _():
        kprev_ref[...] = jnp.zeros_like(kprev_ref)
        vtprev_ref[...] = jnp.zeros_like(vtprev_ref)
        ct_ref[...] = jnp.zeros_like(ct_ref)
        mst_ref[...] = jnp.zeros_like(mst_ref)

    sub = tb // PROMPT_SUBBLOCKS
    sub_chunks = sub // ROWS

    def norm_rows(sb):
        rows_sb = slice(sb * sub, (sb + 1) * sub)
        xn_ref[rows_sb, :] = _rms(x_ref[0, rows_sb, :], gpre_ref[...]).astype(BF16)

    def proj_cols(sb, c0, c1):
        rows_sb = slice(sb * sub, (sb + 1) * sub)
        proj_ref[rows_sb, c0:c1] = jnp.dot(xn_ref[rows_sb, :], wmain_ref[:, c0:c1], preferred_element_type=F32)

    def proj_t(sb):
        for r0, r1 in ((0, T_SPLIT), (T_SPLIT, T_ROWS)):
            pt = lax.dot_general(wt_ref[r0:r1, :], xn_ref[sb * sub:(sb + 1) * sub, :], NT_DIMS,
                                 preferred_element_type=F32)
            for c in range(sub_chunks):
                projt_ref[sb * sub_chunks + c, r0:r1, :] = pt[:, c * ROWS:(c + 1) * ROWS]

    def out_rows(sb):
        rows_sb = slice(sb * sub, (sb + 1) * sub)
        y_ref[0, rows_sb, :] = _out_tail(cat_ref[rows_sb, :], x_ref[0, rows_sb, :], wout_ref, gpost_ref)

    def proj_pieces(sb):
        bounds = list(range(0, P_MAIN, PROJ_COL_STEP)) + [P_MAIN]
        pieces = [functools.partial(proj_cols, sb, c0, c1) for c0, c1 in zip(bounds[:-1], bounds[1:])]
        return pieces[:2] + [functools.partial(proj_t, sb)] + pieces[2:]

    ri = lax.broadcasted_iota(jnp.int32, (ROWS, ROWS), 0)
    ci = lax.broadcasted_iota(jnp.int32, (ROWS, ROWS), 1)
    mask_t = ri <= ci
    st_row = lax.broadcasted_iota(jnp.int32, (STATE_ROWS - M_DIM, ROWS), 0)

    def chunk(c):
        rows = pl.ds(c * ROWS, ROWS)
        first = ((j == 0) & (c == 0)).astype(jnp.int32)
        yield

        qa = proj_ref[rows, P_QA:P_QA + 512] * (ATT_SCALE * LOG2E)
        kcur = proj_ref[rows, P_KA:P_KA + 128].astype(BF16)
        vtcur = projt_ref[c, T_VA:T_VA + 128, :].astype(BF16)
        kcat = jnp.concatenate([kprev_ref[...], kcur], axis=0)
        vtcat = jnp.concatenate([vtprev_ref[...], vtcur], axis=1)
        r = _gate_rows(projt_ref[c, T_GATES:T_GATES + 8, :] + bg_ref[...], mst_ref[...], mask_t, None)

        scores, sinks = [], []
        for kv in range(ATT_KV):
            want_hi = kv == 1
            keep = (ci >= ATT_DIM) if want_hi else (ci < ATT_DIM)
            pieces = []
            for g in range(ATT_GROUP):
                hh = kv * ATT_GROUP + g
                blk = qa[:, (hh // 2) * 128:(hh // 2 + 1) * 128]
                if (hh % 2 == 1) != want_hi:
                    blk = pltpu.roll(blk, ATT_DIM, 1)
                pieces.append(jnp.where(keep, blk, 0.0))
            q4 = jnp.concatenate(pieces, axis=0)
            scores.append(_bdot_nt(kcat, q4) + bias_ref[first * ATT_KV + kv])
            sinks.append(jnp.concatenate(
                [jnp.broadcast_to(sink_ref[0:1, kv * ATT_GROUP + g:kv * ATT_GROUP + g + 1] * LOG2E, (1, ROWS))
                 for g in range(ATT_GROUP)], axis=1))
        yield
        vts, r1s = [], []
        for h in range(M_HEADS):
            q = proj_ref[rows, P_QM + h * M_DIM:P_QM + (h + 1) * M_DIM].astype(BF16)
            k = (proj_ref[rows, P_KM + h * M_DIM:P_KM + (h + 1) * M_DIM] * K_SCALE).astype(BF16)
            vt = projt_ref[c, T_VM + h * M_DIM:T_VM + (h + 1) * M_DIM, :]
            ct = ct_ref[h]
            r1s.append(lax.dot_general(jnp.concatenate([k, ct.astype(BF16)], axis=0), q, NT_DIMS,
                                       preferred_element_type=F32))
            w_row = r["w"][h:h + 1, :]
            lhs2 = jnp.concatenate([vt * w_row, jnp.where(st_row == 0, w_row, 0.0)], axis=0)
            ct_ref[h] = r["decay"][h:h + 1, 0:1] * ct + _bdot(lhs2, k)
            vts.append(vt)
        mst_ref[...] = r["m_new"]
        yield

        outs, nums, dens = [], [], []
        for kv in range(ATT_KV):
            s, sink = scores[kv], sinks[kv]
            mx = jnp.maximum(jnp.max(s, axis=0, keepdims=True), sink)
            p = jnp.exp2(s - mx).astype(BF16)
            lhs = jnp.concatenate([vtcat[kv * ATT_DIM:(kv + 1) * ATT_DIM, :],
                                   jnp.ones((ONES_ROWS, 2 * ROWS), BF16)], axis=0)
            o = jnp.dot(lhs, p, preferred_element_type=F32)
            outs.append((o, jnp.exp2(sink - mx)))
        yield
        for h in range(M_HEADS):
            r1 = r1s[h]
            dm = jnp.exp(r["a_masked"][h] + r["bm"][h:h + 1, :])
            st = r1[0:ROWS] * dm
            g_row = r["gexp"][h:h + 1, :]
            nums.append(_bdot(vts[h], st) + g_row * r1[ROWS:ROWS + M_DIM])
            dens.append(jnp.sum(st, axis=0, keepdims=True) + g_row * r1[ROWS + M_DIM:ROWS + M_DIM + 1])
        yield

        att = []
        for kv in range(ATT_KV):
            o, esink = outs[kv]
            on = o[0:ATT_DIM, :] * (1.0 / (o[ATT_DIM:ATT_DIM + 1, :] + esink))
            for pair in range(2):
                two = jnp.concatenate([on[:, (2 * pair) * ROWS:(2 * pair + 1) * ROWS],
                                       on[:, (2 * pair + 1) * ROWS:(2 * pair + 2) * ROWS]], axis=0)
                att.append(two.T)
        a_out = jnp.concatenate(att, axis=1) * _silu(proj_ref[rows, P_ZA:P_ZA + 512])
        kprev_ref[...] = kcur
        vtprev_ref[...] = vtcur
        yield
        m_out = []
        for h in range(M_HEADS):
            ht = nums[h] * (1.0 / jnp.maximum(jnp.abs(dens[h]), r["enm"][h:h + 1, :]))
            hn = ht * lax.rsqrt(jnp.mean(ht * ht, axis=0, keepdims=True) + NORM_EPS) * gmhb_ref[h]
            m_out.append(_sigmoid(proj_ref[rows, P_OM + h * M_DIM:P_OM + (h + 1) * M_DIM]) * hn.T
                         * _silu(proj_ref[rows, P_ZM + h * M_DIM:P_ZM + (h + 1) * M_DIM]))

        cat_ref[rows, :] = jnp.concatenate([a_out] + m_out, axis=1).astype(BF16)
        yield

    @pl.when((bi == 0) & (j == 0))
    def _():
        norm_rows(0)
        for piece in proj_pieces(0):
            piece()

    def norm_next():
        xn_ref[0:sub, :] = _rms(xnext_ref[0], gpre_ref[...]).astype(BF16)

    def run_chunks(sb, fillers):
        n_fill, n_slots, slot = len(fillers), sub_chunks * CHUNK_STAGES, 0
        for c in range(sb * sub_chunks, (sb + 1) * sub_chunks):
            for _ in chunk(c):
                for _ in range(-(-(slot + 1) * n_fill // n_slots) + (-slot * n_fill // n_slots)):
                    fillers.pop(0)()
                slot += 1
        assert slot == n_slots and not fillers

    assert PROMPT_SUBBLOCKS == 2
    run_chunks(0, [functools.partial(norm_rows, 1)] + proj_pieces(1))
    run_chunks(1, [functools.partial(out_rows, 0), norm_next] + proj_pieces(0))
    out_rows(1)

    @pl.when(j == nt - 1)
    def _():
        wk_ref[0] = proj_ref[tb - ROWS:tb, P_KA:P_KA + 128].T
        wv_ref[0] = projt_ref[nchunks - 1, T_VA:T_VA + 128, :]
        for h in range(M_HEADS):
            ct = ct_ref[h]
            c_ref[0, h] = ct[0:M_DIM].T
            n_ref[0, h:h + 1, :] = ct[M_DIM:M_DIM + 1]
        m_ref[0] = mst_ref[...]


def _prompt_call(x, gpre, wmain, wt, bg, sinks, gmhb, wout, gpost, tb):
    bsz, seq, _ = x.shape
    nt = seq // tb
    full = lambda shape: pl.BlockSpec(shape, lambda b, j: (0,) * len(shape))

    def next_first_subblock(b, j):
        wrap = (j + 1 == nt).astype(jnp.int32)
        return (jnp.minimum(b + wrap, bsz - 1), (j + 1) * (1 - wrap) * PROMPT_SUBBLOCKS, 0)

    out_shapes = (
        jax.ShapeDtypeStruct((bsz, seq, D_MODEL), F32),
        jax.ShapeDtypeStruct((bsz, 128, ROWS), F32),
        jax.ShapeDtypeStruct((bsz, 128, ROWS), F32),
        jax.ShapeDtypeStruct((bsz, M_HEADS, M_DIM, M_DIM), F32),
        jax.ShapeDtypeStruct((bsz, M_HEADS, M_DIM), F32),
        jax.ShapeDtypeStruct((bsz, 8, ROWS), F32),
    )
    return pl.pallas_call(
        functools.partial(_prompt_kernel, tb=tb, nt=nt),
        grid=(bsz, nt),
        in_specs=[
            pl.BlockSpec((1, tb, D_MODEL), lambda b, j: (b, j, 0)),
            pl.BlockSpec((1, tb // PROMPT_SUBBLOCKS, D_MODEL), next_first_subblock),
            full((1, D_MODEL)), full((D_MODEL, P_MAIN)), full((T_ROWS, D_MODEL)), full((8, 1)),
            full((1, ATT_HEADS)), full((M_HEADS, M_DIM, ROWS)), full((D_MODEL, D_MODEL)), full((1, D_MODEL)),
        ],
        out_specs=(
            pl.BlockSpec((1, tb, D_MODEL), lambda b, j: (b, j, 0)),
            pl.BlockSpec((1, 128, ROWS), lambda b, j: (b, 0, 0)),
            pl.BlockSpec((1, 128, ROWS), lambda b, j: (b, 0, 0)),
            pl.BlockSpec((1, M_HEADS, M_DIM, M_DIM), lambda b, j: (b, 0, 0, 0)),
            pl.BlockSpec((1, M_HEADS, M_DIM), lambda b, j: (b, 0, 0)),
            pl.BlockSpec((1, 8, ROWS), lambda b, j: (b, 0, 0)),
        ),
        out_shape=out_shapes,
        scratch_shapes=[
            pltpu.VMEM((tb, P_MAIN), F32),
            pltpu.VMEM((tb // ROWS, T_ROWS, ROWS), F32),
            pltpu.VMEM((tb, D_MODEL), BF16),
            pltpu.VMEM((tb, D_MODEL), BF16),
            pltpu.VMEM((ROWS, 128), BF16),
            pltpu.VMEM((128, ROWS), BF16),
            pltpu.VMEM((M_HEADS, STATE_ROWS, M_DIM), F32),
            pltpu.VMEM((8, ROWS), F32),
            pltpu.VMEM((2 * ATT_KV, 2 * ROWS, ATT_GROUP * ROWS), F32),
        ],
        compiler_params=pltpu.CompilerParams(
            dimension_semantics=("arbitrary", "arbitrary"), vmem_limit_bytes=VMEM_LIMIT_BYTES),
        name="prompt_layer",
    )(x, x, gpre, wmain, wt, bg, sinks, gmhb, wout, gpost)


def _sample_bias_new(head):
    r = lax.broadcasted_iota(jnp.int32, (ROWS, ROWS), 0)
    c = lax.broadcasted_iota(jnp.int32, (ROWS, ROWS), 1)
    valid = ((r >> 3) == (c >> 3)) & (r >= c)
    return jnp.where(valid, -_slope(head) * (r - c).astype(F32), NEG_BIG)


def _sample_bias_cache(head):
    r = lax.broadcasted_iota(jnp.int32, (ROWS, ROWS), 0)
    c = lax.broadcasted_iota(jnp.int32, (ROWS, ROWS), 1)
    diff = (r & (SAMPLE_SEQ - 1)) + ROWS - c
    return jnp.where(diff < ROWS, -_slope(head) * diff.astype(F32), NEG_BIG)


def _sample_kernel(x_ref, kct_ref, vct_ref, cin_ref, nin_ref, m0_ref,
                   gpre_ref, wmain_ref, wt_ref, bg_ref, sink_ref, gmhb_ref, wout_ref, gpost_ref,
                   y_ref, kot_ref, vot_ref, cout_ref, nout_ref, mout_ref,
                   projfull_ref, projt_ref, cat_ref, qh_ref, sc_ref, oc_ref, qc_ref, kat_ref, vat_ref, wv_ref, kt_ref,
                   decb_ref, bn_ref, bc_ref):
    step, part = pl.program_id(0), pl.program_id(1)

    @pl.when((step == 0) & (part == 0))
    def _():
        for hh in range(ATT_HEADS):
            bn_ref[hh] = _sample_bias_new(hh)
            bc_ref[hh] = _sample_bias_cache(hh)

    @pl.when(part == 0)
    def _():
        xn = _rms(x_ref[...], gpre_ref[...]).astype(BF16)
        projfull_ref[...] = jnp.dot(xn, wmain_ref[...], preferred_element_type=F32)
        ptf = lax.dot_general(wt_ref[...], xn, NT_DIMS, preferred_element_type=F32)
        for g in range(SAMPLE_PARTS):
            projt_ref[g] = ptf[:, g * ROWS:(g + 1) * ROWS]

    part_rows = pl.ds(pl.multiple_of(part * ROWS, ROWS), ROWS)
    proj_ref = projfull_ref.at[part_rows]
    pt = projt_ref[part]

    ri = lax.broadcasted_iota(jnp.int32, (ROWS, ROWS), 0)
    ci = lax.broadcasted_iota(jnp.int32, (ROWS, ROWS), 1)
    same_seq = (ri >> 3) == (ci >> 3)
    mask_t = same_seq & (ri <= ci)
    last_sel = (same_seq & ((ri & (SAMPLE_SEQ - 1)) == SAMPLE_SEQ - 1)).astype(BF16)
    row16 = lax.broadcasted_iota(jnp.int32, (SAMPLE_GROUP, ROWS), 0)
    lane16 = lax.broadcasted_iota(jnp.int32, (SAMPLE_GROUP, ROWS), 1)
    seq_of_lane = (lane16 >> 3) == row16
    ones_rows = jnp.ones((ONES_ROWS, ROWS), F32)

    qa = proj_ref[:, P_QA:P_QA + 512] * ATT_SCALE
    for hh in range(ATT_HEADS):
        blk = qa[:, (hh // 2) * 128:(hh // 2 + 1) * 128]
        if hh % 2 == 1:
            blk = pltpu.roll(blk, ATT_DIM, 1)
        qh_ref[hh] = blk[:, 0:ATT_DIM]
    ka = proj_ref[:, P_KA:P_KA + 128]
    kat = ka.T
    vat = pt[T_VA:T_VA + 128, :]
    kat_ref[...] = kat
    vat_ref[...] = vat

    def seq_scores(b, carry):
        rows = pl.ds(pl.multiple_of(b * SAMPLE_SEQ, SAMPLE_SEQ), SAMPLE_SEQ)
        kct = kct_ref[b]
        for kv in range(ATT_KV):
            lhs = jnp.concatenate([qh_ref[kv * ATT_GROUP + g, rows, :] for g in range(ATT_GROUP)], axis=0)
            res = _bdot(lhs, kct[kv * ATT_DIM:(kv + 1) * ATT_DIM, :])
            for g in range(ATT_GROUP):
                sc_ref[kv * ATT_GROUP + g, rows, :] = res[g * SAMPLE_SEQ:(g + 1) * SAMPLE_SEQ]
        for h in range(M_HEADS):
            qc_ref[h, rows, :] = _bdot(proj_ref[rows, P_QM + h * M_DIM:P_QM + (h + 1) * M_DIM], cin_ref[b, h])
        return carry

    lax.fori_loop(0, SAMPLE_GROUP, seq_scores, 0, unroll=SEQ_UNROLL)

    o_new, esinks = [], []
    for hh in range(ATT_HEADS):
        kv = hh // ATT_GROUP
        s_n = _bdot(qh_ref[hh], kat[kv * ATT_DIM:(kv + 1) * ATT_DIM, :]) + bn_ref[hh]
        s_c = sc_ref[hh] + bc_ref[hh]
        sink = sink_ref[0:1, hh:hh + 1]
        mx = jnp.maximum(jnp.maximum(jnp.max(s_n, axis=-1, keepdims=True),
                                     jnp.max(s_c, axis=-1, keepdims=True)), sink)
        p_n = jnp.exp(s_n - mx)
        sc_ref[hh] = jnp.exp(s_c - mx)
        vaug = jnp.concatenate([vat[kv * ATT_DIM:(kv + 1) * ATT_DIM, :], ones_rows], axis=0)
        o_new.append(_bdot_nt(p_n, vaug))
        esinks.append(jnp.exp(sink - mx))

    r = _gate_rows(pt[T_GATES:T_GATES + 8, :] + bg_ref[...], m0_ref[0], mask_t, last_sel)
    mout_ref[0] = r["m_new"]
    m_out = []
    for h in range(M_HEADS):
        q = proj_ref[:, P_QM + h * M_DIM:P_QM + (h + 1) * M_DIM].astype(BF16)
        kf = proj_ref[:, P_KM + h * M_DIM:P_KM + (h + 1) * M_DIM] * K_SCALE
        k = kf.astype(BF16)
        vt = pt[T_VM + h * M_DIM:T_VM + (h + 1) * M_DIM, :]
        n_h = nin_ref[h]
        r1 = lax.dot_general(jnp.concatenate([k, n_h.astype(BF16)], axis=0), q, NT_DIMS,
                             preferred_element_type=F32)
        st = r1[0:ROWS] * jnp.exp(r["a_masked"][h] + r["bm"][h:h + 1, :])
        q_n = jnp.sum(jnp.where(seq_of_lane, r1[ROWS:ROWS + SAMPLE_GROUP], 0.0), axis=0, keepdims=True)
        g_row = r["gexp"][h:h + 1, :]
        num = _bdot(vt, st) + g_row * qc_ref[h].T
        den = jnp.sum(st, axis=0, keepdims=True) + g_row * q_n
        ht = num * (1.0 / jnp.maximum(jnp.abs(den), r["enm"][h:h + 1, :]))
        hn = ht * lax.rsqrt(jnp.mean(ht * ht, axis=0, keepdims=True) + NORM_EPS) * gmhb_ref[h]
        m_out.append(_sigmoid(proj_ref[:, P_OM + h * M_DIM:P_OM + (h + 1) * M_DIM]) * hn.T
                     * _silu(proj_ref[:, P_ZM + h * M_DIM:P_ZM + (h + 1) * M_DIM]))
        w_row = r["w"][h:h + 1, :]
        dec16 = jnp.sum(jnp.where(lane16 == row16 * SAMPLE_SEQ, r["decay"][h:h + 1, :], 0.0),
                        axis=1, keepdims=True)
        nout_ref[h] = dec16 * n_h + _bdot(jnp.where(seq_of_lane, w_row, 0.0), k)
        decb_ref[h] = jnp.broadcast_to(dec16, (SAMPLE_GROUP, ROWS))
        wv_ref[h] = (vt * w_row).T
        kt_ref[h] = kf.T.astype(BF16)

    keep_new = ci >= ROWS - SAMPLE_SEQ

    def seq_update(b, carry):
        rows = pl.ds(pl.multiple_of(b * SAMPLE_SEQ, SAMPLE_SEQ), SAMPLE_SEQ)
        vct = vct_ref[b]
        for kv in range(ATT_KV):
            vaug = jnp.concatenate([vct[kv * ATT_DIM:(kv + 1) * ATT_DIM, :], ones_rows], axis=0)
            pl_ = jnp.concatenate([sc_ref[kv * ATT_GROUP + g, rows, :] for g in range(ATT_GROUP)], axis=0)
            res = _bdot_nt(pl_, vaug)
            for g in range(ATT_GROUP):
                oc_ref[kv * ATT_GROUP + g, rows, 0:ATT_DIM + ONES_ROWS] = res[g * SAMPLE_SEQ:(g + 1) * SAMPLE_SEQ]
        in_seq = (ri >> 3) == b
        for h in range(M_HEADS):
            upd = jnp.dot(kt_ref[h], jnp.where(in_seq, wv_ref[h], 0.0).astype(BF16), preferred_element_type=F32)
            cout_ref[b, h] = decb_ref[h, pl.ds(b, 1), :] * cin_ref[b, h] + upd
        shift = (ROWS - SAMPLE_SEQ - b * SAMPLE_SEQ) & (ROWS - 1)
        kot_ref[b] = jnp.where(keep_new, pltpu.roll(kat_ref[...], shift, 1),
                               pltpu.roll(kct_ref[b], ROWS - SAMPLE_SEQ, 1))
        vot_ref[b] = jnp.where(keep_new, pltpu.roll(vat_ref[...], shift, 1),
                               pltpu.roll(vct, ROWS - SAMPLE_SEQ, 1))
        return carry

    lax.fori_loop(0, SAMPLE_GROUP, seq_update, 0, unroll=SEQ_UNROLL)

    att = []
    for pair in range(ATT_HEADS // 2):
        halves = []
        for hh in (2 * pair, 2 * pair + 1):
            on, oc = o_new[hh], oc_ref[hh]
            den = on[:, ATT_DIM:ATT_DIM + 1] + oc[:, ATT_DIM:ATT_DIM + 1] + esinks[hh]
            halves.append((on[:, 0:ATT_DIM] + oc[:, 0:ATT_DIM]) * (1.0 / den))
        att.append(jnp.concatenate(halves, axis=1))
    a_out = jnp.concatenate(att, axis=1) * _silu(proj_ref[:, P_ZA:P_ZA + 512])
    cat_ref[part_rows, :] = jnp.concatenate([a_out] + m_out, axis=1).astype(BF16)

    @pl.when(part == SAMPLE_PARTS - 1)
    def _():
        y_ref[...] = _out_tail(cat_ref[...], x_ref[...], wout_ref, gpost_ref)


def _sample_call(x, kct, vct, cin, nin, m0, gpre, wmain, wt, bg, sinks, gmhb, wout, gpost):
    nrows = x.shape[0]
    ngroups = nrows // ROWS
    nseq = ngroups * SAMPLE_GROUP
    parts = SAMPLE_PARTS
    assert ngroups % parts == 0
    full = lambda shape: pl.BlockSpec(shape, lambda i, p: (0,) * len(shape))
    grp = SAMPLE_GROUP
    group = lambda i, p: i * parts + p
    out_shapes = (
        jax.ShapeDtypeStruct((nrows, D_MODEL), F32),
        jax.ShapeDtypeStruct((nseq, 128, ROWS), F32),
        jax.ShapeDtypeStruct((nseq, 128, ROWS), F32),
        jax.ShapeDtypeStruct((nseq, M_HEADS, M_DIM, M_DIM), F32),
        jax.ShapeDtypeStruct((M_HEADS, nseq, M_DIM), F32),
        jax.ShapeDtypeStruct((ngroups, 8, ROWS), F32),
    )
    return pl.pallas_call(
        _sample_kernel,
        grid=(ngroups // parts, parts),
        in_specs=[
            pl.BlockSpec((parts * ROWS, D_MODEL), lambda i, p: (i, 0)),
            pl.BlockSpec((grp, 128, ROWS), lambda i, p: (group(i, p), 0, 0)),
            pl.BlockSpec((grp, 128, ROWS), lambda i, p: (group(i, p), 0, 0)),
            pl.BlockSpec((grp, M_HEADS, M_DIM, M_DIM), lambda i, p: (group(i, p), 0, 0, 0)),
            pl.BlockSpec((M_HEADS, grp, M_DIM), lambda i, p: (0, group(i, p), 0)),
            pl.BlockSpec((1, 8, ROWS), lambda i, p: (group(i, p), 0, 0)),
            full((1, D_MODEL)), full((D_MODEL, P_MAIN)), full((T_ROWS, D_MODEL)), full((8, 1)),
            full((1, ATT_HEADS)), full((M_HEADS, M_DIM, ROWS)), full((D_MODEL, D_MODEL)), full((1, D_MODEL)),
        ],
        out_specs=(
            pl.BlockSpec((parts * ROWS, D_MODEL), lambda i, p: (i, 0)),
            pl.BlockSpec((grp, 128, ROWS), lambda i, p: (group(i, p), 0, 0)),
            pl.BlockSpec((grp, 128, ROWS), lambda i, p: (group(i, p), 0, 0)),
            pl.BlockSpec((grp, M_HEADS, M_DIM, M_DIM), lambda i, p: (group(i, p), 0, 0, 0)),
            pl.BlockSpec((M_HEADS, grp, M_DIM), lambda i, p: (0, group(i, p), 0)),
            pl.BlockSpec((1, 8, ROWS), lambda i, p: (group(i, p), 0, 0)),
        ),
        out_shape=out_shapes,
        scratch_shapes=[
            pltpu.VMEM((parts * ROWS, P_MAIN), F32),
            pltpu.VMEM((parts, T_ROWS, ROWS), F32),
            pltpu.VMEM((parts * ROWS, D_MODEL), BF16),
            pltpu.VMEM((ATT_HEADS, ROWS, ATT_DIM), F32),
            pltpu.VMEM((ATT_HEADS, ROWS, ROWS), F32),
            pltpu.VMEM((ATT_HEADS, ROWS, ROWS), F32),
            pltpu.VMEM((M_HEADS, ROWS, M_DIM), F32),
            pltpu.VMEM((128, ROWS), F32),
            pltpu.VMEM((128, ROWS), F32),
            pltpu.VMEM((M_HEADS, ROWS, M_DIM), F32),
            pltpu.VMEM((M_HEADS, M_DIM, ROWS), BF16),
            pltpu.VMEM((M_HEADS, SAMPLE_GROUP, ROWS), F32),
            pltpu.VMEM((ATT_HEADS, ROWS, ROWS), F32),
            pltpu.VMEM((ATT_HEADS, ROWS, ROWS), F32),
        ],
        compiler_params=pltpu.CompilerParams(
            dimension_semantics=("arbitrary", "arbitrary"), vmem_limit_bytes=VMEM_LIMIT_BYTES),
        name="sample_layer",
    )(x, kct, vct, cin, nin, m0, gpre, wmain, wt, bg, sinks, gmhb, wout, gpost)


PROMPT_BLOCK = 512
PROMPT_SUBBLOCKS = 2
PROJ_COL_STEP = 512
CHUNK_STAGES = 7


def _window_in(cache):
    nseq = cache.shape[0]
    return cache.transpose(0, 2, 3, 1).reshape(nseq, ATT_KV * ATT_DIM, ROWS)


def _window_out(win_t):
    nseq = win_t.shape[0]
    return win_t.reshape(nseq, ATT_KV, ATT_DIM, ROWS).transpose(0, 3, 1, 2)[None]


def kernel(x_prompt, x_sample, cache_win_k, cache_win_v, state_C, state_n, state_m,
           g_pre, w_in, b_gate, attn_sinks, g_mh, w_out, g_post):
    depth = g_pre.shape[0]
    assert depth == 1, "single-layer trunk"
    nseq, sseq, _ = x_sample.shape
    assert sseq == SAMPLE_SEQ and nseq % SAMPLE_GROUP == 0

    gpre = g_pre[0].reshape(1, D_MODEL)
    gpost = g_post[0].reshape(1, D_MODEL)
    wmain, wt, wout = _weights_call(w_in[0].T, w_out[0])
    bg = b_gate[0].reshape(2 * M_HEADS, 1)
    sinks = attn_sinks[0].reshape(1, ATT_HEADS)
    gmhb = jnp.broadcast_to(g_mh[0][:, :, None], (M_HEADS, M_DIM, ROWS))

    yp, wkp, wvp, cp, np_, mp = _prompt_call(x_prompt, gpre, wmain, wt, bg, sinks, gmhb, wout, gpost, PROMPT_BLOCK)

    ngroups = nseq // SAMPLE_GROUP
    m0 = jnp.broadcast_to(state_m[0].reshape(ngroups, SAMPLE_GROUP, 1, M_HEADS),
                          (ngroups, SAMPLE_GROUP, SAMPLE_SEQ, M_HEADS))
    m0 = jnp.pad(m0.reshape(ngroups, ROWS, M_HEADS).transpose(0, 2, 1), ((0, 0), (0, 8 - M_HEADS), (0, 0)))
    ys, wks, wvs, cs, ns, ms = _sample_call(
        x_sample.reshape(nseq * sseq, D_MODEL), _window_in(cache_win_k[0]), _window_in(cache_win_v[0]),
        state_C[0], state_n[0].transpose(1, 0, 2), m0,
        gpre, wmain, wt, bg, sinks, gmhb, wout, gpost)

    ms = ms[:, 0:M_HEADS, ::SAMPLE_SEQ].transpose(0, 2, 1).reshape(nseq, M_HEADS)
    return (yp, ys.reshape(nseq, sseq, D_MODEL), _window_out(wkp), _window_out(wvp),
            cp[None], np_[None], mp[:, 0:M_HEADS, 0][None],
            _window_out(wks), _window_out(wvs), cs[None], ns.transpose(1, 0, 2)[None], ms[None])
```

```python
import functools

import jax
import jax.numpy as jnp
from jax import lax
from jax.experimental import pallas as pl
from jax.experimental.pallas import tpu as pltpu

F32 = jnp.float32
BF16 = jnp.bfloat16

D_MODEL = 1024
ROWS = 128
ATT_HEADS, ATT_KV, ATT_GROUP, ATT_DIM = 8, 2, 4, 64
M_HEADS, M_DIM = 4, 128
NORM_EPS = 1e-6
NEG_BIG = -1e30
ATT_SCALE = ATT_DIM ** -0.5
LOG2E = 1.4426950408889634
K_SCALE = M_DIM ** -0.5

QA, KA, VA, ZA = 0, 512, 640, 768
QM, KM, VM, OM, ZM = 1280, 1792, 2304, 2816, 3328
GATES = 3840
D_IN = 3848

P_QA, P_ZA, P_QM, P_KM, P_OM, P_ZM, P_KA = 0, 512, 1024, 1536, 2048, 2560, 3072
P_MAIN = 3328
T_VA, T_VM, T_GATES = 0, 128, 640
T_ROWS = 656
T_SPLIT = 336
STATE_ROWS = 144
ONES_ROWS = 16

SAMPLE_SEQ = 8
SAMPLE_GROUP = ROWS // SAMPLE_SEQ
SEQ_UNROLL = 16
SAMPLE_PARTS = 2

VMEM_LIMIT_BYTES = 56 * 1024 * 1024
NT_DIMS = (((1,), (1,)), ((), ()))


def _rms(x, g):
    return x * lax.rsqrt(jnp.mean(x * x, axis=-1, keepdims=True) + NORM_EPS) * g


def _sigmoid(x):
    return 0.5 + 0.5 * jnp.tanh(0.5 * x)


def _silu(x):
    h = 0.5 * x
    return h + h * jnp.tanh(h)


def _log_sigmoid(x):
    return -(jnp.maximum(-x, 0.0) + jnp.log1p(jnp.exp(-jnp.abs(x))))


def _slope(head):
    return 2.0 ** -(head + 1)


def _bdot(a, b):
    return jnp.dot(a.astype(BF16), b.astype(BF16), preferred_element_type=F32)


def _bdot_nt(a, b):
    return lax.dot_general(a.astype(BF16), b.astype(BF16), NT_DIMS, preferred_element_type=F32)


def _exact_dot(x, m):
    hi = x.astype(BF16).astype(F32)
    mid = (x - hi).astype(BF16).astype(F32)
    lo = (x - hi - mid).astype(BF16).astype(F32)
    parts = jnp.dot(jnp.concatenate([hi, mid, lo, jnp.zeros_like(hi)], axis=0).astype(BF16), m,
                    preferred_element_type=F32)
    return parts[0:8] + parts[8:16] + parts[16:24]


def _gate_rows(x, m0, mask_t, last_sel):
    row = lax.broadcasted_iota(jnp.int32, (8, ROWS), 0)
    head_rows = row < M_HEADS
    ic = jnp.where(head_rows, x, 0.0)
    fc = jnp.where(head_rows, _log_sigmoid(pltpu.roll(x, M_HEADS, 0)), 0.0)
    b = _exact_dot(fc, mask_t.astype(BF16))
    a = ic - b
    a_cols = jnp.concatenate([a, jnp.zeros((ROWS - 8, ROWS), F32)], axis=0).T
    a_masked = [jnp.where(mask_t, a_cols[:, h:h + 1], -jnp.inf) for h in range(M_HEADS)]
    cm = jnp.concatenate([jnp.max(am, axis=0, keepdims=True) for am in a_masked]
                         + [jnp.zeros((8 - M_HEADS, ROWS), F32)], axis=0)
    m_t = jnp.maximum(b + m0, b + cm)
    if last_sel is None:
        b_last = jnp.broadcast_to(b[:, ROWS - 1:ROWS], b.shape)
        m_new = jnp.broadcast_to(m_t[:, ROWS - 1:ROWS], b.shape)
    else:
        both = _exact_dot(jnp.where(head_rows, b, pltpu.roll(m_t, M_HEADS, 0)), last_sel)
        b_last = jnp.where(head_rows, both, 0.0)
        m_new = jnp.where(head_rows, pltpu.roll(both, M_HEADS, 0), 0.0)
    return dict(a_masked=a_masked, bm=b - m_t, gexp=jnp.exp(b + m0 - m_t), enm=jnp.exp(-m_t), m_new=m_new,
                w=jnp.exp(b_last - b + ic - m_new), decay=jnp.exp(b_last + m0 - m_new))


def _out_tail(cat, x, wout_ref, gpost_ref):
    y = jnp.dot(cat, wout_ref[...], preferred_element_type=F32)
    return x + _rms(y, gpost_ref[...])


W_BLOCK = 256
MAIN_BLOCKS = P_MAIN // W_BLOCK


def _main_src_block(i):
    return jnp.where(i < 2, i, jnp.where(i < 8, i + 1, jnp.where(i < 12, i + 3, 2)))


def _weights_kernel(main_ref, va_ref, vm0_ref, vm1_ref, vm2_ref, vm3_ref, gates_ref, wout_ref,
                    wmain_ref, wt_ref, woutb_ref):
    wmain_ref[...] = main_ref[...].T.astype(BF16)

    @pl.when(pl.program_id(0) == 0)
    def _():
        wt_ref[...] = jnp.concatenate(
            [va_ref[...], vm0_ref[...], vm1_ref[...], vm2_ref[...], vm3_ref[...], gates_ref[...],
             jnp.zeros((T_ROWS - T_GATES - 8, D_MODEL), F32)], axis=0).astype(BF16)
        woutb_ref[...] = wout_ref[...].astype(BF16)


def _weights_call(w_in_t, w_out):
    blk = lambda r: pl.BlockSpec((128, D_MODEL), lambda i, r=r: (r, 0))
    return pl.pallas_call(
        _weights_kernel,
        grid=(MAIN_BLOCKS,),
        in_specs=[
            pl.BlockSpec((W_BLOCK, D_MODEL), lambda i: (_main_src_block(i), 0)),
            blk(VA // 128), blk(VM // 128), blk(VM // 128 + 1), blk(VM // 128 + 2), blk(VM // 128 + 3),
            pl.BlockSpec((8, D_MODEL), lambda i: (GATES // 8, 0)),
            pl.BlockSpec((D_MODEL, D_MODEL), lambda i: (0, 0)),
        ],
        out_specs=(
            pl.BlockSpec((D_MODEL, W_BLOCK), lambda i: (0, i)),
            pl.BlockSpec((T_ROWS, D_MODEL), lambda i: (0, 0)),
            pl.BlockSpec((D_MODEL, D_MODEL), lambda i: (0, 0)),
        ),
        out_shape=(
            jax.ShapeDtypeStruct((D_MODEL, P_MAIN), BF16),
            jax.ShapeDtypeStruct((T_ROWS, D_MODEL), BF16),
            jax.ShapeDtypeStruct((D_MODEL, D_MODEL), BF16),
        ),
        compiler_params=pltpu.CompilerParams(
            dimension_semantics=("arbitrary",), vmem_limit_bytes=VMEM_LIMIT_BYTES),
        name="layer_weights",
    )(w_in_t, w_in_t, w_in_t, w_in_t, w_in_t, w_in_t, w_in_t, w_out)


def _prompt_bias_t(kv, first):
    j = lax.broadcasted_iota(jnp.int32, (2 * ROWS, ROWS), 0)
    i = lax.broadcasted_iota(jnp.int32, (2 * ROWS, ROWS), 1)
    diff = ROWS + i - j
    valid = (diff >= 0) & (diff < ROWS)
    if first:
        valid = valid & (j >= ROWS)
    dfl = diff.astype(F32)
    return jnp.concatenate(
        [jnp.where(valid, -_slope(kv * ATT_GROUP + g) * dfl, NEG_BIG) * LOG2E for g in range(ATT_GROUP)], axis=1)


def _prompt_kernel(x_ref, xnext_ref, gpre_ref, wmain_ref, wt_ref, bg_ref, sink_ref, gmhb_ref, wout_ref, gpost_ref,
                   y_ref, wk_ref, wv_ref, c_ref, n_ref, m_ref,
                   proj_ref, projt_ref, cat_ref, xn_ref, kprev_ref, vtprev_ref, ct_ref, mst_ref, bias_ref,
                   *, tb, nt):
    bi = pl.program_id(0)
    j = pl.program_id(1)
    nchunks = tb // ROWS

    @pl.when((bi == 0) & (j == 0))
    def _():
        for first in range(2):
            for kv in range(ATT_KV):
                bias_ref[first * ATT_KV + kv] = _prompt_bias_t(kv, first)

    @pl.when(j == 0)
    def _():
        kprev_ref[...] = jnp.zeros_like(kprev_ref)
        vtprev_ref[...] = jnp.zeros_like(vtprev_ref)
        ct_ref[...] = jnp.zeros_like(ct_ref)
        mst_ref[...] = jnp.zeros_like(mst_ref)

    sub = tb // PROMPT_SUBBLOCKS
    sub_chunks = sub // ROWS

    def norm_rows(sb):
        rows_sb = slice(sb * sub, (sb + 1) * sub)
        xn_ref[rows_sb, :] = _rms(x_ref[0, rows_sb, :], gpre_ref[...]).astype(BF16)

    def proj_cols(sb, c0, c1):
        rows_sb = slice(sb * sub, (sb + 1) * sub)
        proj_ref[rows_sb, c0:c1] = jnp.dot(xn_ref[rows_sb, :], wmain_ref[:, c0:c1], preferred_element_type=F32)

    def proj_t(sb):
        for r0, r1 in ((0, T_SPLIT), (T_SPLIT, T_ROWS)):
            pt = lax.dot_general(wt_ref[r0:r1, :], xn_ref[sb * sub:(sb + 1) * sub, :], NT_DIMS,
                                 preferred_element_type=F32)
            for c in range(sub_chunks):
                projt_ref[sb * sub_chunks + c, r0:r1, :] = pt[:, c * ROWS:(c + 1) * ROWS]

    def out_rows(sb):
        rows_sb = slice(sb * sub, (sb + 1) * sub)
        y_ref[0, rows_sb, :] = _out_tail(cat_ref[rows_sb, :], x_ref[0, rows_sb, :], wout_ref, gpost_ref)

    def proj_pieces(sb):
        bounds = list(range(0, P_MAIN, PROJ_COL_STEP)) + [P_MAIN]
        pieces = [functools.partial(proj_cols, sb, c0, c1) for c0, c1 in zip(bounds[:-1], bounds[1:])]
        return pieces[:2] + [functools.partial(proj_t, sb)] + pieces[2:]

    ri = lax.broadcasted_iota(jnp.int32, (ROWS, ROWS), 0)
    ci = lax.broadcasted_iota(jnp.int32, (ROWS, ROWS), 1)
    mask_t = ri <= ci
    st_row = lax.broadcasted_iota(jnp.int32, (STATE_ROWS - M_DIM, ROWS), 0)

    def chunk(c):
        rows = pl.ds(c * ROWS, ROWS)
        first = ((j == 0) & (c == 0)).astype(jnp.int32)
        yield

        qa = proj_ref[rows, P_QA:P_QA + 512] * (ATT_SCALE * LOG2E)
        kcur = proj_ref[rows, P_KA:P_KA + 128].astype(BF16)
        vtcur = projt_ref[c, T_VA:T_VA + 128, :].astype(BF16)
        kcat = jnp.concatenate([kprev_ref[...], kcur], axis=0)
        vtcat = jnp.concatenate([vtprev_ref[...], vtcur], axis=1)
        r = _gate_rows(projt_ref[c, T_GATES:T_GATES + 8, :] + bg_ref[...], mst_ref[...], mask_t, None)

        scores, sinks = [], []
        for kv in range(ATT_KV):
            want_hi = kv == 1
            keep = (ci >= ATT_DIM) if want_hi else (ci < ATT_DIM)
            pieces = []
            for g in range(ATT_GROUP):
                hh = kv * ATT_GROUP + g
                blk = qa[:, (hh // 2) * 128:(hh // 2 + 1) * 128]
                if (hh % 2 == 1) != want_hi:
                    blk = pltpu.roll(blk, ATT_DIM, 1)
                pieces.append(jnp.where(keep, blk, 0.0))
            q4 = jnp.concatenate(pieces, axis=0)
            scores.append(_bdot_nt(kcat, q4) + bias_ref[first * ATT_KV + kv])
            sinks.append(jnp.concatenate(
                [jnp.broadcast_to(sink_ref[0:1, kv * ATT_GROUP + g:kv * ATT_GROUP + g + 1] * LOG2E, (1, ROWS))
                 for g in range(ATT_GROUP)], axis=1))
        yield
        vts, r1s = [], []
        for h in range(M_HEADS):
            q = proj_ref[rows, P_QM + h * M_DIM:P_QM + (h + 1) * M_DIM].astype(BF16)
            k = (proj_ref[rows, P_KM + h * M_DIM:P_KM + (h + 1) * M_DIM] * K_SCALE).astype(BF16)
            vt = projt_ref[c, T_VM + h * M_DIM:T_VM + (h + 1) * M_DIM, :]
            ct = ct_ref[h]
            r1s.append(lax.dot_general(jnp.concatenate([k, ct.astype(BF16)], axis=0), q, NT_DIMS,
                                       preferred_element_type=F32))
            w_row = r["w"][h:h + 1, :]
            lhs2 = jnp.concatenate([vt * w_row, jnp.where(st_row == 0, w_row, 0.0)], axis=0)
            ct_ref[h] = r["decay"][h:h + 1, 0:1] * ct + _bdot(lhs2, k)
            vts.append(vt)
        mst_ref[...] = r["m_new"]
        yield

        outs, nums, dens = [], [], []
        for kv in range(ATT_KV):
            s, sink = scores[kv], sinks[kv]
            mx = jnp.maximum(jnp.max(s, axis=0, keepdims=True), sink)
            p = jnp.exp2(s - mx).astype(BF16)
            lhs = jnp.concatenate([vtcat[kv * ATT_DIM:(kv + 1) * ATT_DIM, :],
                                   jnp.ones((ONES_ROWS, 2 * ROWS), BF16)], axis=0)
            o = jnp.dot(lhs, p, preferred_element_type=F32)
            outs.append((o, jnp.exp2(sink - mx)))
        yield
        for h in range(M_HEADS):
            r1 = r1s[h]
            dm = jnp.exp(r["a_masked"][h] + r["bm"][h:h + 1, :])
            st = r1[0:ROWS] * dm
            g_row = r["gexp"][h:h + 1, :]
            nums.append(_bdot(vts[h], st) + g_row * r1[ROWS:ROWS + M_DIM])
            dens.append(jnp.sum(st, axis=0, keepdims=True) + g_row * r1[ROWS + M_DIM:ROWS + M_DIM + 1])
        yield

        att = []
        for kv in range(ATT_KV):
            o, esink = outs[kv]
            on = o[0:ATT_DIM, :] * (1.0 / (o[ATT_DIM:ATT_DIM + 1, :] + esink))
            for pair in range(2):
                two = jnp.concatenate([on[:, (2 * pair) * ROWS:(2 * pair + 1) * ROWS],
                                       on[:, (2 * pair + 1) * ROWS:(2 * pair + 2) * ROWS]], axis=0)
                att.append(two.T)
        a_out = jnp.concatenate(att, axis=1) * _silu(proj_ref[rows, P_ZA:P_ZA + 512])
        kprev_ref[...] = kcur
        vtprev_ref[...] = vtcur
        yield
        m_out = []
        for h in range(M_HEADS):
            ht = nums[h] * (1.0 / jnp.maximum(jnp.abs(dens[h]), r["enm"][h:h + 1, :]))
            hn = ht * lax.rsqrt(jnp.mean(ht * ht, axis=0, keepdims=True) + NORM_EPS) * gmhb_ref[h]
            m_out.append(_sigmoid(proj_ref[rows, P_OM + h * M_DIM:P_OM + (h + 1) * M_DIM]) * hn.T
                         * _silu(proj_ref[rows, P_ZM + h * M_DIM:P_ZM + (h + 1) * M_DIM]))

        cat_ref[rows, :] = jnp.concatenate([a_out] + m_out, axis=1).astype(BF16)
        yield

    @pl.when((bi == 0) & (j == 0))
    def _():
        norm_rows(0)
        for piece in proj_pieces(0):
            piece()

    def norm_next():
        xn_ref[0:sub, :] = _rms(xnext_ref[0], gpre_ref[...]).astype(BF16)

    def run_chunks(sb, fillers):
        n_fill, n_slots, slot = len(fillers), sub_chunks * CHUNK_STAGES, 0
        for c in range(sb * sub_chunks, (sb + 1) * sub_chunks):
            for _ in chunk(c):
                for _ in range(-(-(slot + 1) * n_fill // n_slots) + (-slot * n_fill // n_slots)):
                    fillers.pop(0)()
                slot += 1
        assert slot == n_slots and not fillers

    assert PROMPT_SUBBLOCKS == 2
    run_chunks(0, [functools.partial(norm_rows, 1)] + proj_pieces(1))
    run_chunks(1, [functools.partial(out_rows, 0), norm_next] + proj_pieces(0))
    out_rows(1)

    @pl.when(j == nt - 1)
    def _():
        wk_ref[0] = proj_ref[tb - ROWS:tb, P_KA:P_KA + 128].T
        wv_ref[0] = projt_ref[nchunks - 1, T_VA:T_VA + 128, :]
        for h in range(M_HEADS):
            ct = ct_ref[h]
            c_ref[0, h] = ct[0:M_DIM].T
            n_ref[0, h:h + 1, :] = ct[M_DIM:M_DIM + 1]
        m_ref[0] = mst_ref[...]


def _prompt_call(x, gpre, wmain, wt, bg, sinks, gmhb, wout, gpost, tb):
    bsz, seq, _ = x.shape
    nt = seq // tb
    full = lambda shape: pl.BlockSpec(shape, lambda b, j: (0,) * len(shape))

    def next_first_subblock(b, j):
        wrap = (j + 1 == nt).astype(jnp.int32)
        return (jnp.minimum(b + wrap, bsz - 1), (j + 1) * (1 - wrap) * PROMPT_SUBBLOCKS, 0)

    out_shapes = (
        jax.ShapeDtypeStruct((bsz, seq, D_MODEL), F32),
        jax.ShapeDtypeStruct((bsz, 128, ROWS), F32),
        jax.ShapeDtypeStruct((bsz, 128, ROWS), F32),
        jax.ShapeDtypeStruct((bsz, M_HEADS, M_DIM, M_DIM), F32),
        jax.ShapeDtypeStruct((bsz, M_HEADS, M_DIM), F32),
        jax.ShapeDtypeStruct((bsz, 8, ROWS), F32),
    )
    return pl.pallas_call(
        functools.partial(_prompt_kernel, tb=tb, nt=nt),
        grid=(bsz, nt),
        in_specs=[
            pl.BlockSpec((1, tb, D_MODEL), lambda b, j: (b, j, 0)),
            pl.BlockSpec((1, tb // PROMPT_SUBBLOCKS, D_MODEL), next_first_subblock),
            full((1, D_MODEL)), full((D_MODEL, P_MAIN)), full((T_ROWS, D_MODEL)), full((8, 1)),
            full((1, ATT_HEADS)), full((M_HEADS, M_DIM, ROWS)), full((D_MODEL, D_MODEL)), full((1, D_MODEL)),
        ],
        out_specs=(
            pl.BlockSpec((1, tb, D_MODEL), lambda b, j: (b, j, 0)),
            pl.BlockSpec((1, 128, ROWS), lambda b, j: (b, 0, 0)),
            pl.BlockSpec((1, 128, ROWS), lambda b, j: (b, 0, 0)),
            pl.BlockSpec((1, M_HEADS, M_DIM, M_DIM), lambda b, j: (b, 0, 0, 0)),
            pl.BlockSpec((1, M_HEADS, M_DIM), lambda b, j: (b, 0, 0)),
            pl.BlockSpec((1, 8, ROWS), lambda b, j: (b, 0, 0)),
        ),
        out_shape=out_shapes,
        scratch_shapes=[
            pltpu.VMEM((tb, P_MAIN), F32),
            pltpu.VMEM((tb // ROWS, T_ROWS, ROWS), F32),
            pltpu.VMEM((tb, D_MODEL), BF16),
            pltpu.VMEM((tb, D_MODEL), BF16),
            pltpu.VMEM((ROWS, 128), BF16),
            pltpu.VMEM((128, ROWS), BF16),
            pltpu.VMEM((M_HEADS, STATE_ROWS, M_DIM), F32),
            pltpu.VMEM((8, ROWS), F32),
            pltpu.VMEM((2 * ATT_KV, 2 * ROWS, ATT_GROUP * ROWS), F32),
        ],
        compiler_params=pltpu.CompilerParams(
            dimension_semantics=("arbitrary", "arbitrary"), vmem_limit_bytes=VMEM_LIMIT_BYTES),
        name="prompt_layer",
    )(x, x, gpre, wmain, wt, bg, sinks, gmhb, wout, gpost)


def _sample_bias_new(head):
    r = lax.broadcasted_iota(jnp.int32, (ROWS, ROWS), 0)
    c = lax.broadcasted_iota(jnp.int32, (ROWS, ROWS), 1)
    valid = ((r >> 3) == (c >> 3)) & (r >= c)
    return jnp.where(valid, -_slope(head) * (r - c).astype(F32), NEG_BIG)


def _sample_bias_cache(head):
    r = lax.broadcasted_iota(jnp.int32, (ROWS, ROWS), 0)
    c = lax.broadcasted_iota(jnp.int32, (ROWS, ROWS), 1)
    diff = (r & (SAMPLE_SEQ - 1)) + ROWS - c
    return jnp.where(diff < ROWS, -_slope(head) * diff.astype(F32), NEG_BIG)


def _sample_kernel(x_ref, kct_ref, vct_ref, cin_ref, nin_ref, m0_ref,
                   gpre_ref, wmain_ref, wt_ref, bg_ref, sink_ref, gmhb_ref, wout_ref, gpost_ref,
                   y_ref, kot_ref, vot_ref, cout_ref, nout_ref, mout_ref,
                   projfull_ref, projt_ref, cat_ref, qh_ref, sc_ref, oc_ref, qc_ref, kat_ref, vat_ref, wv_ref, kt_ref,
                   decb_ref, bn_ref, bc_ref):
    step, part = pl.program_id(0), pl.program_id(1)

    @pl.when((step == 0) & (part == 0))
    def _():
        for hh in range(ATT_HEADS):
            bn_ref[hh] = _sample_bias_new(hh)
            bc_ref[hh] = _sample_bias_cache(hh)

    @pl.when(part == 0)
    def _():
        xn = _rms(x_ref[...], gpre_ref[...]).astype(BF16)
        projfull_ref[...] = jnp.dot(xn, wmain_ref[...], preferred_element_type=F32)
        ptf = lax.dot_general(wt_ref[...], xn, NT_DIMS, preferred_element_type=F32)
        for g in range(SAMPLE_PARTS):
            projt_ref[g] = ptf[:, g * ROWS:(g + 1) * ROWS]

    part_rows = pl.ds(pl.multiple_of(part * ROWS, ROWS), ROWS)
    proj_ref = projfull_ref.at[part_rows]
    pt = projt_ref[part]

    ri = lax.broadcasted_iota(jnp.int32, (ROWS, ROWS), 0)
    ci = lax.broadcasted_iota(jnp.int32, (ROWS, ROWS), 1)
    same_seq = (ri >> 3) == (ci >> 3)
    mask_t = same_seq & (ri <= ci)
    last_sel = (same_seq & ((ri & (SAMPLE_SEQ - 1)) == SAMPLE_SEQ - 1)).astype(BF16)
    row16 = lax.broadcasted_iota(jnp.int32, (SAMPLE_GROUP, ROWS), 0)
    lane16 = lax.broadcasted_iota(jnp.int32, (SAMPLE_GROUP, ROWS), 1)
    seq_of_lane = (lane16 >> 3) == row16
    ones_rows = jnp.ones((ONES_ROWS, ROWS), F32)

    qa = proj_ref[:, P_QA:P_QA + 512] * ATT_SCALE
    for hh in range(ATT_HEADS):
        blk = qa[:, (hh // 2) * 128:(hh // 2 + 1) * 128]
        if hh % 2 == 1:
            blk = pltpu.roll(blk, ATT_DIM, 1)
        qh_ref[hh] = blk[:, 0:ATT_DIM]
    ka = proj_ref[:, P_KA:P_KA + 128]
    kat = ka.T
    vat = pt[T_VA:T_VA + 128, :]
    kat_ref[...] = kat
    vat_ref[...] = vat

    def seq_scores(b, carry):
        rows = pl.ds(pl.multiple_of(b * SAMPLE_SEQ, SAMPLE_SEQ), SAMPLE_SEQ)
        kct = kct_ref[b]
        for kv in range(ATT_KV):
            lhs = jnp.concatenate([qh_ref[kv * ATT_GROUP + g, rows, :] for g in range(ATT_GROUP)], axis=0)
            res = _bdot(lhs, kct[kv * ATT_DIM:(kv + 1) * ATT_DIM, :])
            for g in range(ATT_GROUP):
                sc_ref[kv * ATT_GROUP + g, rows, :] = res[g * SAMPLE_SEQ:(g + 1) * SAMPLE_SEQ]
        for h in range(M_HEADS):
            qc_ref[h, rows, :] = _bdot(proj_ref[rows, P_QM + h * M_DIM:P_QM + (h + 1) * M_DIM], cin_ref[b, h])
        return carry

    lax.fori_loop(0, SAMPLE_GROUP, seq_scores, 0, unroll=SEQ_UNROLL)

    o_new, esinks = [], []
    for hh in range(ATT_HEADS):
        kv = hh // ATT_GROUP
        s_n = _bdot(qh_ref[hh], kat[kv * ATT_DIM:(kv + 1) * ATT_DIM, :]) + bn_ref[hh]
        s_c = sc_ref[hh] + bc_ref[hh]
        sink = sink_ref[0:1, hh:hh + 1]
        mx = jnp.maximum(jnp.maximum(jnp.max(s_n, axis=-1, keepdims=True),
                                     jnp.max(s_c, axis=-1, keepdims=True)), sink)
        p_n = jnp.exp(s_n - mx)
        sc_ref[hh] = jnp.exp(s_c - mx)
        vaug = jnp.concatenate([vat[kv * ATT_DIM:(kv + 1) * ATT_DIM, :], ones_rows], axis=0)
        o_new.append(_bdot_nt(p_n, vaug))
        esinks.append(jnp.exp(sink - mx))

    r = _gate_rows(pt[T_GATES:T_GATES + 8, :] + bg_ref[...], m0_ref[0], mask_t, last_sel)
    mout_ref[0] = r["m_new"]
    m_out = []
    for h in range(M_HEADS):
        q = proj_ref[:, P_QM + h * M_DIM:P_QM + (h + 1) * M_DIM].astype(BF16)
        kf = proj_ref[:, P_KM + h * M_DIM:P_KM + (h + 1) * M_DIM] * K_SCALE
        k = kf.astype(BF16)
        vt = pt[T_VM + h * M_DIM:T_VM + (h + 1) * M_DIM, :]
        n_h = nin_ref[h]
        r1 = lax.dot_general(jnp.concatenate([k, n_h.astype(BF16)], axis=0), q, NT_DIMS,
                             preferred_element_type=F32)
        st = r1[0:ROWS] * jnp.exp(r["a_masked"][h] + r["bm"][h:h + 1, :])
        q_n = jnp.sum(jnp.where(seq_of_lane, r1[ROWS:ROWS + SAMPLE_GROUP], 0.0), axis=0, keepdims=True)
        g_row = r["gexp"][h:h + 1, :]
        num = _bdot(vt, st) + g_row * qc_ref[h].T
        den = jnp.sum(st, axis=0, keepdims=True) + g_row * q_n
        ht = num * (1.0 / jnp.maximum(jnp.abs(den), r["enm"][h:h + 1, :]))
        hn = ht * lax.rsqrt(jnp.mean(ht * ht, axis=0, keepdims=True) + NORM_EPS) * gmhb_ref[h]
        m_out.append(_sigmoid(proj_ref[:, P_OM + h * M_DIM:P_OM + (h + 1) * M_DIM]) * hn.T
                     * _silu(proj_ref[:, P_ZM + h * M_DIM:P_ZM + (h + 1) * M_DIM]))
        w_row = r["w"][h:h + 1, :]
        dec16 = jnp.sum(jnp.where(lane16 == row16 * SAMPLE_SEQ, r["decay"][h:h + 1, :], 0.0),
                        axis=1, keepdims=True)
        nout_ref[h] = dec16 * n_h + _bdot(jnp.where(seq_of_lane, w_row, 0.0), k)
        decb_ref[h] = jnp.broadcast_to(dec16, (SAMPLE_GROUP, ROWS))
        wv_ref[h] = (vt * w_row).T
        kt_ref[h] = kf.T.astype(BF16)

    keep_new = ci >= ROWS - SAMPLE_SEQ

    def seq_update(b, carry):
        rows = pl.ds(pl.multiple_of(b * SAMPLE_SEQ, SAMPLE_SEQ), SAMPLE_SEQ)
        vct = vct_ref[b]
        for kv in range(ATT_KV):
            vaug = jnp.concatenate([vct[kv * ATT_DIM:(kv + 1) * ATT_DIM, :], ones_rows], axis=0)
            pl_ = jnp.concatenate([sc_ref[kv * ATT_GROUP + g, rows, :] for g in range(ATT_GROUP)], axis=0)
            res = _bdot_nt(pl_, vaug)
            for g in range(ATT_GROUP):
                oc_ref[kv * ATT_GROUP + g, rows, 0:ATT_DIM + ONES_ROWS] = res[g * SAMPLE_SEQ:(g + 1) * SAMPLE_SEQ]
        in_seq = (ri >> 3) == b
        for h in range(M_HEADS):
            upd = jnp.dot(kt_ref[h], jnp.where(in_seq, wv_ref[h], 0.0).astype(BF16), preferred_element_type=F32)
            cout_ref[b, h] = decb_ref[h, pl.ds(b, 1), :] * cin_ref[b, h] + upd
        shift = (ROWS - SAMPLE_SEQ - b * SAMPLE_SEQ) & (ROWS - 1)
        kot_ref[b] = jnp.where(keep_new, pltpu.roll(kat_ref[...], shift, 1),
                               pltpu.roll(kct_ref[b], ROWS - SAMPLE_SEQ, 1))
        vot_ref[b] = jnp.where(keep_new, pltpu.roll(vat_ref[...], shift, 1),
                               pltpu.roll(vct, ROWS - SAMPLE_SEQ, 1))
        return carry

    lax.fori_loop(0, SAMPLE_GROUP, seq_update, 0, unroll=SEQ_UNROLL)

    att = []
    for pair in range(ATT_HEADS // 2):
        halves = []
        for hh in (2 * pair, 2 * pair + 1):
            on, oc = o_new[hh], oc_ref[hh]
            den = on[:, ATT_DIM:ATT_DIM + 1] + oc[:, ATT_DIM:ATT_DIM + 1] + esinks[hh]
            halves.append((on[:, 0:ATT_DIM] + oc[:, 0:ATT_DIM]) * (1.0 / den))
        att.append(jnp.concatenate(halves, axis=1))
    a_out = jnp.concatenate(att, axis=1) * _silu(proj_ref[:, P_ZA:P_ZA + 512])
    cat_ref[part_rows, :] = jnp.concatenate([a_out] + m_out, axis=1).astype(BF16)

    @pl.when(part == SAMPLE_PARTS - 1)
    def _():
        y_ref[...] = _out_tail(cat_ref[...], x_ref[...], wout_ref, gpost_ref)


def _sample_call(x, kct, vct, cin, nin, m0, gpre, wmain, wt, bg, sinks, gmhb, wout, gpost):
    nrows = x.shape[0]
    ngroups = nrows // ROWS
    nseq = ngroups * SAMPLE_GROUP
    parts = SAMPLE_PARTS
    assert ngroups % parts == 0
    full = lambda shape: pl.BlockSpec(shape, lambda i, p: (0,) * len(shape))
    grp = SAMPLE_GROUP
    group = lambda i, p: i * parts + p
    out_shapes = (
        jax.ShapeDtypeStruct((nrows, D_MODEL), F32),
        jax.ShapeDtypeStruct((nseq, 128, ROWS), F32),
        jax.ShapeDtypeStruct((nseq, 128, ROWS), F32),
        jax.ShapeDtypeStruct((nseq, M_HEADS, M_DIM, M_DIM), F32),
        jax.ShapeDtypeStruct((M_HEADS, nseq, M_DIM), F32),
        jax.ShapeDtypeStruct((ngroups, 8, ROWS), F32),
    )
    return pl.pallas_call(
        _sample_kernel,
        grid=(ngroups // parts, parts),
        in_specs=[
            pl.BlockSpec((parts * ROWS, D_MODEL), lambda i, p: (i, 0)),
            pl.BlockSpec((grp, 128, ROWS), lambda i, p: (group(i, p), 0, 0)),
            pl.BlockSpec((grp, 128, ROWS), lambda i, p: (group(i, p), 0, 0)),
            pl.BlockSpec((grp, M_HEADS, M_DIM, M_DIM), lambda i, p: (group(i, p), 0, 0, 0)),
            pl.BlockSpec((M_HEADS, grp, M_DIM), lambda i, p: (0, group(i, p), 0)),
            pl.BlockSpec((1, 8, ROWS), lambda i, p: (group(i, p), 0, 0)),
            full((1, D_MODEL)), full((D_MODEL, P_MAIN)), full((T_ROWS, D_MODEL)), full((8, 1)),
            full((1, ATT_HEADS)), full((M_HEADS, M_DIM, ROWS)), full((D_MODEL, D_MODEL)), full((1, D_MODEL)),
        ],
        out_specs=(
            pl.BlockSpec((parts * ROWS, D_MODEL), lambda i, p: (i, 0)),
            pl.BlockSpec((grp, 128, ROWS), lambda i, p: (group(i, p), 0, 0)),
            pl.BlockSpec((grp, 128, ROWS), lambda i, p: (group(i, p), 0, 0)),
            pl.BlockSpec((grp, M_HEADS, M_DIM, M_DIM), lambda i, p: (group(i, p), 0, 0, 0)),
            pl.BlockSpec((M_HEADS, grp, M_DIM), lambda i, p: (0, group(i, p), 0)),
            pl.BlockSpec((1, 8, ROWS), lambda i, p: (group(i, p), 0, 0)),
        ),
        out_shape=out_shapes,
        scratch_shapes=[
            pltpu.VMEM((parts * ROWS, P_MAIN), F32),
            pltpu.VMEM((parts, T_ROWS, ROWS), F32),
            pltpu.VMEM((parts * ROWS, D_MODEL), BF16),
            pltpu.VMEM((ATT_HEADS, ROWS, ATT_DIM), F32),
            pltpu.VMEM((ATT_HEADS, ROWS, ROWS), F32),
            pltpu.VMEM((ATT_HEADS, ROWS, ROWS), F32),
            pltpu.VMEM((M_HEADS, ROWS, M_DIM), F32),
            pltpu.VMEM((128, ROWS), F32),
            pltpu.VMEM((128, ROWS), F32),
            pltpu.VMEM((M_HEADS, ROWS, M_DIM), F32),
            pltpu.VMEM((M_HEADS, M_DIM, ROWS), BF16),
            pltpu.VMEM((M_HEADS, SAMPLE_GROUP, ROWS), F32),
            pltpu.VMEM((ATT_HEADS, ROWS, ROWS), F32),
            pltpu.VMEM((ATT_HEADS, ROWS, ROWS), F32),
        ],
        compiler_params=pltpu.CompilerParams(
            dimension_semantics=("arbitrary", "arbitrary"), vmem_limit_bytes=VMEM_LIMIT_BYTES),
        name="sample_layer",
    )(x, kct, vct, cin, nin, m0, gpre, wmain, wt, bg, sinks, gmhb, wout, gpost)


PROMPT_BLOCK = 512
PROMPT_SUBBLOCKS = 2
PROJ_COL_STEP = 512
CHUNK_STAGES = 7


def _window_in(cache):
    nseq = cache.shape[0]
    return cache.transpose(0, 2, 3, 1).reshape(nseq, ATT_KV * ATT_DIM, ROWS)


def _window_out(win_t):
    nseq = win_t.shape[0]
    return win_t.reshape(nseq, ATT_KV, ATT_DIM, ROWS).transpose(0, 3, 1, 2)[None]


def kernel(x_prompt, x_sample, cache_win_k, cache_win_v, state_C, state_n, state_m,
           g_pre, w_in, b_gate, attn_sinks, g_mh, w_out, g_post):
    depth = g_pre.shape[0]
    assert depth == 1, "single-layer trunk"
    nseq, sseq, _ = x_sample.shape
    assert sseq == SAMPLE_SEQ and nseq % SAMPLE_GROUP == 0

    gpre = g_pre[0].reshape(1, D_MODEL)
    gpost = g_post[0].reshape(1, D_MODEL)
    wmain, wt, wout = _weights_call(w_in[0].T, w_out[0])
    bg = b_gate[0].reshape(2 * M_HEADS, 1)
    sinks = attn_sinks[0].reshape(1, ATT_HEADS)
    gmhb = jnp.broadcast_to(g_mh[0][:, :, None], (M_HEADS, M_DIM, ROWS))

    yp, wkp, wvp, cp, np_, mp = _prompt_call(x_prompt, gpre, wmain, wt, bg, sinks, gmhb, wout, gpost, PROMPT_BLOCK)

    ngroups = nseq // SAMPLE_GROUP
    m0 = jnp.broadcast_to(state_m[0].reshape(ngroups, SAMPLE_GROUP, 1, M_HEADS),
                          (ngroups, SAMPLE_GROUP, SAMPLE_SEQ, M_HEADS))
    m0 = jnp.pad(m0.reshape(ngroups, ROWS, M_HEADS).transpose(0, 2, 1), ((0, 0), (0, 8 - M_HEADS), (0, 0)))
    ys, wks, wvs, cs, ns, ms = _sample_call(
        x_sample.reshape(nseq * sseq, D_MODEL), _window_in(cache_win_k[0]), _window_in(cache_win_v[0]),
        state_C[0], state_n[0].transpose(1, 0, 2), m0,
        gpre, wmain, wt, bg, sinks, gmhb, wout, gpost)

    ms = ms[:, 0:M_HEADS, ::SAMPLE_SEQ].transpose(0, 2, 1).reshape(nseq, M_HEADS)
    return (yp, ys.reshape(nseq, sseq, D_MODEL), _window_out(wkp), _window_out(wvp),
            cp[None], np_[None], mp[:, 0:M_HEADS, 0][None],
            _window_out(wks), _window_out(wvs), cs[None], ns.transpose(1, 0, 2)[None], ms[None])
```

```python
import functools

import jax
import jax.numpy as jnp
from jax import lax
from jax.experimental import pallas as pl
from jax.experimental.pallas import tpu as pltpu

F32 = jnp.float32
BF16 = jnp.bfloat16

D_MODEL = 1024
ROWS = 128
ATT_HEADS, ATT_KV, ATT_GROUP, ATT_DIM = 8, 2, 4, 64
M_HEADS, M_DIM = 4, 128
NORM_EPS = 1e-6
NEG_BIG = -1e30
ATT_SCALE = ATT_DIM ** -0.5
LOG2E = 1.4426950408889634
K_SCALE = M_DIM ** -0.5

QA, KA, VA, ZA = 0, 512, 640, 768
QM, KM, VM, OM, ZM = 1280, 1792, 2304, 2816, 3328
GATES = 3840
D_IN = 3848

P_QA, P_ZA, P_QM, P_KM, P_OM, P_ZM, P_KA = 0, 512, 1024, 1536, 2048, 2560, 3072
P_MAIN = 3328
T_VA, T_VM, T_GATES = 0, 128, 640
T_ROWS = 656
T_SPLIT = 336
STATE_ROWS = 144
ONES_ROWS = 16

SAMPLE_SEQ = 8
SAMPLE_GROUP = ROWS // SAMPLE_SEQ
SEQ_UNROLL = 16
SAMPLE_PARTS = 2

VMEM_LIMIT_BYTES = 56 * 1024 * 1024
NT_DIMS = (((1,), (1,)), ((), ()))


def _rms(x, g):
    return x * lax.rsqrt(jnp.mean(x * x, axis=-1, keepdims=True) + NORM_EPS) * g


def _sigmoid(x):
    return 0.5 + 0.5 * jnp.tanh(0.5 * x)


def _silu(x):
    h = 0.5 * x
    return h + h * jnp.tanh(h)


def _log_sigmoid(x):
    return -(jnp.maximum(-x, 0.0) + jnp.log1p(jnp.exp(-jnp.abs(x))))


def _slope(head):
    return 2.0 ** -(head + 1)


def _bdot(a, b):
    return jnp.dot(a.astype(BF16), b.astype(BF16), preferred_element_type=F32)


def _bdot_nt(a, b):
    return lax.dot_general(a.astype(BF16), b.astype(BF16), NT_DIMS, preferred_element_type=F32)


def _exact_dot(x, m):
    hi = x.astype(BF16).astype(F32)
    mid = (x - hi).astype(BF16).astype(F32)
    lo = (x - hi - mid).astype(BF16).astype(F32)
    parts = jnp.dot(jnp.concatenate([hi, mid, lo, jnp.zeros_like(hi)], axis=0).astype(BF16), m,
                    preferred_element_type=F32)
    return parts[0:8] + parts[8:16] + parts[16:24]


def _gate_rows(x, m0, mask_t, last_sel):
    row = lax.broadcasted_iota(jnp.int32, (8, ROWS), 0)
    head_rows = row < M_HEADS
    ic = jnp.where(head_rows, x, 0.0)
    fc = jnp.where(head_rows, _log_sigmoid(pltpu.roll(x, M_HEADS, 0)), 0.0)
    b = _exact_dot(fc, mask_t.astype(BF16))
    a = ic - b
    a_cols = jnp.concatenate([a, jnp.zeros((ROWS - 8, ROWS), F32)], axis=0).T
    a_masked = [jnp.where(mask_t, a_cols[:, h:h + 1], -jnp.inf) for h in range(M_HEADS)]
    cm = jnp.concatenate([jnp.max(am, axis=0, keepdims=True) for am in a_masked]
                         + [jnp.zeros((8 - M_HEADS, ROWS), F32)], axis=0)
    m_t = jnp.maximum(b + m0, b + cm)
    if last_sel is None:
        b_last = jnp.broadcast_to(b[:, ROWS - 1:ROWS], b.shape)
        m_new = jnp.broadcast_to(m_t[:, ROWS - 1:ROWS], b.shape)
    else:
        both = _exact_dot(jnp.where(head_rows, b, pltpu.roll(m_t, M_HEADS, 0)), last_sel)
        b_last = jnp.where(head_rows, both, 0.0)
        m_new = jnp.where(head_rows, pltpu.roll(both, M_HEADS, 0), 0.0)
    return dict(a_masked=a_masked, bm=b - m_t, gexp=jnp.exp(b + m0 - m_t), enm=jnp.exp(-m_t), m_new=m_new,
                w=jnp.exp(b_last - b + ic - m_new), decay=jnp.exp(b_last + m0 - m_new))


def _out_tail(cat, x, wout_ref, gpost_ref):
    y = jnp.dot(cat, wout_ref[...], preferred_element_type=F32)
    return x + _rms(y, gpost_ref[...])


W_BLOCK = 256
MAIN_BLOCKS = P_MAIN // W_BLOCK


def _main_src_block(i):
    return jnp.where(i < 2, i, jnp.where(i < 8, i + 1, jnp.where(i < 12, i + 3, 2)))


def _weights_kernel(main_ref, va_ref, vm0_ref, vm1_ref, vm2_ref, vm3_ref, gates_ref, wout_ref, gmh_ref, bgate_ref,
                    wmain_ref, wt_ref, woutb_ref, gmhb_ref, bgb_ref):
    wmain_ref[...] = main_ref[...].T.astype(BF16)

    @pl.when(pl.program_id(0) == 0)
    def _():
        wt_ref[...] = jnp.concatenate(
            [va_ref[...], vm0_ref[...], vm1_ref[...], vm2_ref[...], vm3_ref[...], gates_ref[...],
             jnp.zeros((T_ROWS - T_GATES - 8, D_MODEL), F32)], axis=0).astype(BF16)
        woutb_ref[...] = wout_ref[...].astype(BF16)
        for h in range(M_HEADS):
            gmhb_ref[h] = jnp.broadcast_to(gmh_ref[h:h + 1, :], (M_DIM, ROWS)).T
        bg_row = jnp.concatenate([bgate_ref[...], jnp.zeros((1, ROWS - 2 * M_HEADS), F32)], axis=1)
        bgb_ref[...] = jnp.broadcast_to(bg_row, (ROWS, ROWS)).T[0:2 * M_HEADS, :]


def _weights_call(w_in_t, w_out, gmh, bgate):
    blk = lambda r: pl.BlockSpec((128, D_MODEL), lambda i, r=r: (r, 0))
    const = lambda shape: pl.BlockSpec(shape, lambda i: (0,) * len(shape))
    return pl.pallas_call(
        _weights_kernel,
        grid=(MAIN_BLOCKS,),
        in_specs=[
            pl.BlockSpec((W_BLOCK, D_MODEL), lambda i: (_main_src_block(i), 0)),
            blk(VA // 128), blk(VM // 128), blk(VM // 128 + 1), blk(VM // 128 + 2), blk(VM // 128 + 3),
            pl.BlockSpec((8, D_MODEL), lambda i: (GATES // 8, 0)),
            const((D_MODEL, D_MODEL)), const((M_HEADS, M_DIM)), const((1, 2 * M_HEADS)),
        ],
        out_specs=(
            pl.BlockSpec((D_MODEL, W_BLOCK), lambda i: (0, i)),
            const((T_ROWS, D_MODEL)), const((D_MODEL, D_MODEL)),
            const((M_HEADS, M_DIM, ROWS)), const((2 * M_HEADS, ROWS)),
        ),
        out_shape=(
            jax.ShapeDtypeStruct((D_MODEL, P_MAIN), BF16),
            jax.ShapeDtypeStruct((T_ROWS, D_MODEL), BF16),
            jax.ShapeDtypeStruct((D_MODEL, D_MODEL), BF16),
            jax.ShapeDtypeStruct((M_HEADS, M_DIM, ROWS), F32),
            jax.ShapeDtypeStruct((2 * M_HEADS, ROWS), F32),
        ),
        compiler_params=pltpu.CompilerParams(
            dimension_semantics=("arbitrary",), vmem_limit_bytes=VMEM_LIMIT_BYTES),
        name="layer_weights",
    )(w_in_t, w_in_t, w_in_t, w_in_t, w_in_t, w_in_t, w_in_t, w_out, gmh, bgate)


def _prompt_bias_t(kv, first):
    j = lax.broadcasted_iota(jnp.int32, (2 * ROWS, ROWS), 0)
    i = lax.broadcasted_iota(jnp.int32, (2 * ROWS, ROWS), 1)
    diff = ROWS + i - j
    valid = (diff >= 0) & (diff < ROWS)
    if first:
        valid = valid & (j >= ROWS)
    dfl = diff.astype(F32)
    return jnp.concatenate(
        [jnp.where(valid, -_slope(kv * ATT_GROUP + g) * dfl, NEG_BIG) * LOG2E for g in range(ATT_GROUP)], axis=1)


def _prompt_kernel(x_ref, xnext_ref, gpre_ref, wmain_ref, wt_ref, bg_ref, sink_ref, gmhb_ref, wout_ref, gpost_ref,
                   y_ref, wk_ref, wv_ref, c_ref, n_ref, m_ref,
                   proj_ref, projt_ref, cat_ref, xn_ref, kprev_ref, vtprev_ref, ct_ref, mst_ref, bias_ref,
                   *, tb, nt):
    bi = pl.program_id(0)
    j = pl.program_id(1)
    nchunks = tb // ROWS

    @pl.when((bi == 0) & (j == 0))
    def _():
        for first in range(2):
            for kv in range(ATT_KV):
                bias_ref[first * ATT_KV + kv] = _prompt_bias_t(kv, first)
        m_ref[...] = jnp.zeros_like(m_ref)

    @pl.when(j == 0)
    def _():
        kprev_ref[...] = jnp.zeros_like(kprev_ref)
        vtprev_ref[...] = jnp.zeros_like(vtprev_ref)
        ct_ref[...] = jnp.zeros_like(ct_ref)
        mst_ref[...] = jnp.zeros_like(mst_ref)

    sub = tb // PROMPT_SUBBLOCKS
    sub_chunks = sub // ROWS

    def norm_rows(sb):
        rows_sb = slice(sb * sub, (sb + 1) * sub)
        xn_ref[rows_sb, :] = _rms(x_ref[0, rows_sb, :], gpre_ref[...]).astype(BF16)

    def proj_cols(sb, c0, c1):
        rows_sb = slice(sb * sub, (sb + 1) * sub)
        proj_ref[rows_sb, c0:c1] = jnp.dot(xn_ref[rows_sb, :], wmain_ref[:, c0:c1], preferred_element_type=F32)

    def proj_t(sb):
        for r0, r1 in ((0, T_SPLIT), (T_SPLIT, T_ROWS)):
            pt = lax.dot_general(wt_ref[r0:r1, :], xn_ref[sb * sub:(sb + 1) * sub, :], NT_DIMS,
                                 preferred_element_type=F32)
            for c in range(sub_chunks):
                projt_ref[sb * sub_chunks + c, r0:r1, :] = pt[:, c * ROWS:(c + 1) * ROWS]

    def out_rows(sb):
        rows_sb = slice(sb * sub, (sb + 1) * sub)
        y_ref[0, rows_sb, :] = _out_tail(cat_ref[rows_sb, :], x_ref[0, rows_sb, :], wout_ref, gpost_ref)

    def proj_pieces(sb):
        bounds = list(range(0, P_MAIN, PROJ_COL_STEP)) + [P_MAIN]
        pieces = [functools.partial(proj_cols, sb, c0, c1) for c0, c1 in zip(bounds[:-1], bounds[1:])]
        return pieces[:2] + [functools.partial(proj_t, sb)] + pieces[2:]

    ri = lax.broadcasted_iota(jnp.int32, (ROWS, ROWS), 0)
    ci = lax.broadcasted_iota(jnp.int32, (ROWS, ROWS), 1)
    mask_t = ri <= ci
    st_row = lax.broadcasted_iota(jnp.int32, (STATE_ROWS - M_DIM, ROWS), 0)

    def chunk(c):
        rows = pl.ds(c * ROWS, ROWS)
        first = ((j == 0) & (c == 0)).astype(jnp.int32)
        yield

        qa = proj_ref[rows, P_QA:P_QA + 512] * (ATT_SCALE * LOG2E)
        kcur = proj_ref[rows, P_KA:P_KA + 128].astype(BF16)
        vtcur = projt_ref[c, T_VA:T_VA + 128, :].astype(BF16)
        kcat = jnp.concatenate([kprev_ref[...], kcur], axis=0)
        vtcat = jnp.concatenate([vtprev_ref[...], vtcur], axis=1)
        r = _gate_rows(projt_ref[c, T_GATES:T_GATES + 8, :] + bg_ref[...], mst_ref[...], mask_t, None)

        scores, sinks = [], []
        for kv in range(ATT_KV):
            want_hi = kv == 1
            keep = (ci >= ATT_DIM) if want_hi else (ci < ATT_DIM)
            pieces = []
            for g in range(ATT_GROUP):
                hh = kv * ATT_GROUP + g
                blk = qa[:, (hh // 2) * 128:(hh // 2 + 1) * 128]
                if (hh % 2 == 1) != want_hi:
                    blk = pltpu.roll(blk, ATT_DIM, 1)
                pieces.append(jnp.where(keep, blk, 0.0))
            q4 = jnp.concatenate(pieces, axis=0)
            scores.append(_bdot_nt(kcat, q4) + bias_ref[first * ATT_KV + kv])
            sinks.append(jnp.concatenate(
                [jnp.broadcast_to(sink_ref[0:1, kv * ATT_GROUP + g:kv * ATT_GROUP + g + 1] * LOG2E, (1, ROWS))
                 for g in range(ATT_GROUP)], axis=1))
        yield
        vts, r1s = [], []
        for h in range(M_HEADS):
            q = proj_ref[rows, P_QM + h * M_DIM:P_QM + (h + 1) * M_DIM].astype(BF16)
            k = (proj_ref[rows, P_KM + h * M_DIM:P_KM + (h + 1) * M_DIM] * K_SCALE).astype(BF16)
            vt = projt_ref[c, T_VM + h * M_DIM:T_VM + (h + 1) * M_DIM, :]
            ct = ct_ref[h]
            r1s.append(lax.dot_general(jnp.concatenate([k, ct.astype(BF16)], axis=0), q, NT_DIMS,
                                       preferred_element_type=F32))
            w_row = r["w"][h:h + 1, :]
            lhs2 = jnp.concatenate([vt * w_row, jnp.where(st_row == 0, w_row, 0.0)], axis=0)
            ct_ref[h] = r["decay"][h:h + 1, 0:1] * ct + _bdot(lhs2, k)
            vts.append(vt)
        mst_ref[...] = r["m_new"]
        yield

        outs, nums, dens = [], [], []
        for kv in range(ATT_KV):
            s, sink = scores[kv], sinks[kv]
            mx = jnp.maximum(jnp.max(s, axis=0, keepdims=True), sink)
            p = jnp.exp2(s - mx).astype(BF16)
            lhs = jnp.concatenate([vtcat[kv * ATT_DIM:(kv + 1) * ATT_DIM, :],
                                   jnp.ones((ONES_ROWS, 2 * ROWS), BF16)], axis=0)
            o = jnp.dot(lhs, p, preferred_element_type=F32)
            outs.append((o, jnp.exp2(sink - mx)))
        yield
        for h in range(M_HEADS):
            r1 = r1s[h]
            dm = jnp.exp(r["a_masked"][h] + r["bm"][h:h + 1, :])
            st = r1[0:ROWS] * dm
            g_row = r["gexp"][h:h + 1, :]
            nums.append(_bdot(vts[h], st) + g_row * r1[ROWS:ROWS + M_DIM])
            dens.append(jnp.sum(st, axis=0, keepdims=True) + g_row * r1[ROWS + M_DIM:ROWS + M_DIM + 1])
        yield

        att = []
        for kv in range(ATT_KV):
            o, esink = outs[kv]
            on = o[0:ATT_DIM, :] * (1.0 / (o[ATT_DIM:ATT_DIM + 1, :] + esink))
            for pair in range(2):
                two = jnp.concatenate([on[:, (2 * pair) * ROWS:(2 * pair + 1) * ROWS],
                                       on[:, (2 * pair + 1) * ROWS:(2 * pair + 2) * ROWS]], axis=0)
                att.append(two.T)
        a_out = jnp.concatenate(att, axis=1) * _silu(proj_ref[rows, P_ZA:P_ZA + 512])
        kprev_ref[...] = kcur
        vtprev_ref[...] = vtcur
        yield
        m_out = []
        for h in range(M_HEADS):
            ht = nums[h] * (1.0 / jnp.maximum(jnp.abs(dens[h]), r["enm"][h:h + 1, :]))
            hn = ht * lax.rsqrt(jnp.mean(ht * ht, axis=0, keepdims=True) + NORM_EPS) * gmhb_ref[h]
            m_out.append(_sigmoid(proj_ref[rows, P_OM + h * M_DIM:P_OM + (h + 1) * M_DIM]) * hn.T
                         * _silu(proj_ref[rows, P_ZM + h * M_DIM:P_ZM + (h + 1) * M_DIM]))

        cat_ref[rows, :] = jnp.concatenate([a_out] + m_out, axis=1).astype(BF16)
        yield

    @pl.when((bi == 0) & (j == 0))
    def _():
        norm_rows(0)
        for piece in proj_pieces(0):
            piece()

    def norm_next():
        xn_ref[0:sub, :] = _rms(xnext_ref[0], gpre_ref[...]).astype(BF16)

    def run_chunks(sb, fillers):
        n_fill, n_slots, slot = len(fillers), sub_chunks * CHUNK_STAGES, 0
        for c in range(sb * sub_chunks, (sb + 1) * sub_chunks):
            for _ in chunk(c):
                for _ in range(-(-(slot + 1) * n_fill // n_slots) + (-slot * n_fill // n_slots)):
                    fillers.pop(0)()
                slot += 1
        assert slot == n_slots and not fillers

    assert PROMPT_SUBBLOCKS == 2
    run_chunks(0, [functools.partial(norm_rows, 1)] + proj_pieces(1))
    run_chunks(1, [functools.partial(out_rows, 0), norm_next] + proj_pieces(0))
    out_rows(1)

    @pl.when(j == nt - 1)
    def _():
        wk_ref[0] = proj_ref[tb - ROWS:tb, P_KA:P_KA + 128].T
        wv_ref[0] = projt_ref[nchunks - 1, T_VA:T_VA + 128, :]
        for h in range(M_HEADS):
            ct = ct_ref[h]
            c_ref[0, h] = ct[0:M_DIM].T
            n_ref[0, h:h + 1, :] = ct[M_DIM:M_DIM + 1]
        seq_lane = lax.broadcasted_iota(jnp.int32, m_ref.shape, 1) == bi
        m_ref[...] = jnp.where(seq_lane, mst_ref[0:M_HEADS, 0:m_ref.shape[1]], m_ref[...])


def _prompt_call(x, gpre, wmain, wt, bg, sinks, gmhb, wout, gpost, tb):
    bsz, seq, _ = x.shape
    nt = seq // tb
    full = lambda shape: pl.BlockSpec(shape, lambda b, j: (0,) * len(shape))

    def next_first_subblock(b, j):
        wrap = (j + 1 == nt).astype(jnp.int32)
        return (jnp.minimum(b + wrap, bsz - 1), (j + 1) * (1 - wrap) * PROMPT_SUBBLOCKS, 0)

    out_shapes = (
        jax.ShapeDtypeStruct((bsz, seq, D_MODEL), F32),
        jax.ShapeDtypeStruct((bsz, 128, ROWS), F32),
        jax.ShapeDtypeStruct((bsz, 128, ROWS), F32),
        jax.ShapeDtypeStruct((bsz, M_HEADS, M_DIM, M_DIM), F32),
        jax.ShapeDtypeStruct((bsz, M_HEADS, M_DIM), F32),
        jax.ShapeDtypeStruct((M_HEADS, bsz), F32),
    )
    return pl.pallas_call(
        functools.partial(_prompt_kernel, tb=tb, nt=nt),
        grid=(bsz, nt),
        in_specs=[
            pl.BlockSpec((1, tb, D_MODEL), lambda b, j: (b, j, 0)),
            pl.BlockSpec((1, tb // PROMPT_SUBBLOCKS, D_MODEL), next_first_subblock),
            full((1, D_MODEL)), full((D_MODEL, P_MAIN)), full((T_ROWS, D_MODEL)), full((8, ROWS)),
            full((1, ATT_HEADS)), full((M_HEADS, M_DIM, ROWS)), full((D_MODEL, D_MODEL)), full((1, D_MODEL)),
        ],
        out_specs=(
            pl.BlockSpec((1, tb, D_MODEL), lambda b, j: (b, j, 0)),
            pl.BlockSpec((1, 128, ROWS), lambda b, j: (b, 0, 0)),
            pl.BlockSpec((1, 128, ROWS), lambda b, j: (b, 0, 0)),
            pl.BlockSpec((1, M_HEADS, M_DIM, M_DIM), lambda b, j: (b, 0, 0, 0)),
            pl.BlockSpec((1, M_HEADS, M_DIM), lambda b, j: (b, 0, 0)),
            pl.BlockSpec((M_HEADS, bsz), lambda b, j: (0, 0)),
        ),
        out_shape=out_shapes,
        scratch_shapes=[
            pltpu.VMEM((tb, P_MAIN), F32),
            pltpu.VMEM((tb // ROWS, T_ROWS, ROWS), F32),
            pltpu.VMEM((tb, D_MODEL), BF16),
            pltpu.VMEM((tb, D_MODEL), BF16),
            pltpu.VMEM((ROWS, 128), BF16),
            pltpu.VMEM((128, ROWS), BF16),
            pltpu.VMEM((M_HEADS, STATE_ROWS, M_DIM), F32),
            pltpu.VMEM((8, ROWS), F32),
            pltpu.VMEM((2 * ATT_KV, 2 * ROWS, ATT_GROUP * ROWS), F32),
        ],
        compiler_params=pltpu.CompilerParams(
            dimension_semantics=("arbitrary", "arbitrary"), vmem_limit_bytes=VMEM_LIMIT_BYTES),
        name="prompt_layer",
    )(x, x, gpre, wmain, wt, bg, sinks, gmhb, wout, gpost)


def _sample_bias_new(head):
    r = lax.broadcasted_iota(jnp.int32, (ROWS, ROWS), 0)
    c = lax.broadcasted_iota(jnp.int32, (ROWS, ROWS), 1)
    valid = ((r >> 3) == (c >> 3)) & (r >= c)
    return jnp.where(valid, -_slope(head) * (r - c).astype(F32), NEG_BIG)


def _sample_bias_cache(head):
    r = lax.broadcasted_iota(jnp.int32, (ROWS, ROWS), 0)
    c = lax.broadcasted_iota(jnp.int32, (ROWS, ROWS), 1)
    diff = (r & (SAMPLE_SEQ - 1)) + ROWS - c
    return jnp.where(diff < ROWS, -_slope(head) * diff.astype(F32), NEG_BIG)


def _sample_kernel(x_ref, kct_ref, vct_ref, cin_ref, nin_ref, m0_ref,
                   gpre_ref, wmain_ref, wt_ref, bg_ref, sink_ref, gmhb_ref, wout_ref, gpost_ref,
                   y_ref, kot_ref, vot_ref, cout_ref, nout_ref, mout_ref,
                   projfull_ref, projt_ref, cat_ref, qh_ref, sc_ref, oc_ref, qc_ref, kat_ref, vat_ref, wv_ref, kt_ref,
                   decb_ref, bn_ref, bc_ref):
    step, part = pl.program_id(0), pl.program_id(1)

    @pl.when((step == 0) & (part == 0))
    def _():
        for hh in range(ATT_HEADS):
            bn_ref[hh] = _sample_bias_new(hh)
            bc_ref[hh] = _sample_bias_cache(hh)
        mout_ref[...] = jnp.zeros_like(mout_ref)

    @pl.when(part == 0)
    def _():
        xn = _rms(x_ref[...], gpre_ref[...]).astype(BF16)
        projfull_ref[...] = jnp.dot(xn, wmain_ref[...], preferred_element_type=F32)
        ptf = lax.dot_general(wt_ref[...], xn, NT_DIMS, preferred_element_type=F32)
        for g in range(SAMPLE_PARTS):
            projt_ref[g] = ptf[:, g * ROWS:(g + 1) * ROWS]

    part_rows = pl.ds(pl.multiple_of(part * ROWS, ROWS), ROWS)
    proj_ref = projfull_ref.at[part_rows]
    pt = projt_ref[part]

    ri = lax.broadcasted_iota(jnp.int32, (ROWS, ROWS), 0)
    ci = lax.broadcasted_iota(jnp.int32, (ROWS, ROWS), 1)
    same_seq = (ri >> 3) == (ci >> 3)
    mask_t = same_seq & (ri <= ci)
    last_sel = (same_seq & ((ri & (SAMPLE_SEQ - 1)) == SAMPLE_SEQ - 1)).astype(BF16)
    row16 = lax.broadcasted_iota(jnp.int32, (SAMPLE_GROUP, ROWS), 0)
    lane16 = lax.broadcasted_iota(jnp.int32, (SAMPLE_GROUP, ROWS), 1)
    seq_of_lane = (lane16 >> 3) == row16
    ones_rows = jnp.ones((ONES_ROWS, ROWS), F32)

    qa = proj_ref[:, P_QA:P_QA + 512] * ATT_SCALE
    for hh in range(ATT_HEADS):
        blk = qa[:, (hh // 2) * 128:(hh // 2 + 1) * 128]
        if hh % 2 == 1:
            blk = pltpu.roll(blk, ATT_DIM, 1)
        qh_ref[hh] = blk[:, 0:ATT_DIM]
    ka = proj_ref[:, P_KA:P_KA + 128]
    kat = ka.T
    vat = pt[T_VA:T_VA + 128, :]
    kat_ref[...] = kat
    vat_ref[...] = vat

    def seq_scores(b, carry):
        rows = pl.ds(pl.multiple_of(b * SAMPLE_SEQ, SAMPLE_SEQ), SAMPLE_SEQ)
        kct = kct_ref[b]
        for kv in range(ATT_KV):
            lhs = jnp.concatenate([qh_ref[kv * ATT_GROUP + g, rows, :] for g in range(ATT_GROUP)], axis=0)
            res = _bdot(lhs, kct[kv * ATT_DIM:(kv + 1) * ATT_DIM, :])
            for g in range(ATT_GROUP):
                sc_ref[kv * ATT_GROUP + g, rows, :] = res[g * SAMPLE_SEQ:(g + 1) * SAMPLE_SEQ]
        for h in range(M_HEADS):
            qc_ref[h, rows, :] = _bdot(proj_ref[rows, P_QM + h * M_DIM:P_QM + (h + 1) * M_DIM], cin_ref[b, h])
        return carry

    lax.fori_loop(0, SAMPLE_GROUP, seq_scores, 0, unroll=SEQ_UNROLL)

    o_new, esinks = [], []
    for hh in range(ATT_HEADS):
        kv = hh // ATT_GROUP
        s_n = _bdot(qh_ref[hh], kat[kv * ATT_DIM:(kv + 1) * ATT_DIM, :]) + bn_ref[hh]
        s_c = sc_ref[hh] + bc_ref[hh]
        sink = sink_ref[0:1, hh:hh + 1]
        mx = jnp.maximum(jnp.max(jnp.maximum(s_n, s_c), axis=-1, keepdims=True), sink)
        mx_b = jnp.broadcast_to(mx, (ROWS, ROWS))
        p_n = jnp.exp(s_n - mx_b)
        sc_ref[hh] = jnp.exp(s_c - mx_b)
        vaug = jnp.concatenate([vat[kv * ATT_DIM:(kv + 1) * ATT_DIM, :], ones_rows], axis=0)
        o_new.append(_bdot_nt(p_n, vaug))
        esinks.append(jnp.exp(sink - mx))

    first_seq = (step * SAMPLE_PARTS + part) * SAMPLE_GROUP
    seq_step = ri == first_seq + (ci >> 3)
    m0 = _exact_dot(jnp.concatenate([m0_ref[...], jnp.zeros((8 - M_HEADS, ROWS), F32)], axis=0),
                    seq_step.astype(BF16))
    r = _gate_rows(pt[T_GATES:T_GATES + 8, :] + bg_ref[...], m0, mask_t, last_sel)
    step_seq = (ci == first_seq + (ri >> 3)) & ((ri & (SAMPLE_SEQ - 1)) == 0)
    mout_ref[...] += _exact_dot(r["m_new"], step_seq.astype(BF16))[0:M_HEADS]
    m_out = []
    for h in range(M_HEADS):
        q = proj_ref[:, P_QM + h * M_DIM:P_QM + (h + 1) * M_DIM].astype(BF16)
        kf = proj_ref[:, P_KM + h * M_DIM:P_KM + (h + 1) * M_DIM] * K_SCALE
        k = kf.astype(BF16)
        vt = pt[T_VM + h * M_DIM:T_VM + (h + 1) * M_DIM, :]
        n_h = nin_ref[h]
        r1 = lax.dot_general(jnp.concatenate([k, n_h.astype(BF16)], axis=0), q, NT_DIMS,
                             preferred_element_type=F32)
        st = r1[0:ROWS] * jnp.exp(r["a_masked"][h] + r["bm"][h:h + 1, :])
        q_n = jnp.sum(jnp.where(seq_of_lane, r1[ROWS:ROWS + SAMPLE_GROUP], 0.0), axis=0, keepdims=True)
        g_row = r["gexp"][h:h + 1, :]
        num = _bdot(vt, st) + g_row * qc_ref[h].T
        den = jnp.sum(st, axis=0, keepdims=True) + g_row * q_n
        ht = num * (1.0 / jnp.maximum(jnp.abs(den), r["enm"][h:h + 1, :]))
        hn = ht * lax.rsqrt(jnp.mean(ht * ht, axis=0, keepdims=True) + NORM_EPS) * gmhb_ref[h]
        m_out.append(_sigmoid(proj_ref[:, P_OM + h * M_DIM:P_OM + (h + 1) * M_DIM]) * hn.T
                     * _silu(proj_ref[:, P_ZM + h * M_DIM:P_ZM + (h + 1) * M_DIM]))
        w_row = r["w"][h:h + 1, :]
        dec16 = jnp.sum(jnp.where(lane16 == row16 * SAMPLE_SEQ, r["decay"][h:h + 1, :], 0.0),
                        axis=1, keepdims=True)
        nout_ref[h] = dec16 * n_h + _bdot(jnp.where(seq_of_lane, w_row, 0.0), k)
        decb_ref[h] = jnp.broadcast_to(dec16, (SAMPLE_GROUP, ROWS))
        wv_ref[h] = (vt * w_row).T
        kt_ref[h] = kf.T.astype(BF16)

    keep_new = ci >= ROWS - SAMPLE_SEQ

    def seq_update(b, carry):
        rows = pl.ds(pl.multiple_of(b * SAMPLE_SEQ, SAMPLE_SEQ), SAMPLE_SEQ)
        vct = vct_ref[b]
        for kv in range(ATT_KV):
            vaug = jnp.concatenate([vct[kv * ATT_DIM:(kv + 1) * ATT_DIM, :], ones_rows], axis=0)
            pl_ = jnp.concatenate([sc_ref[kv * ATT_GROUP + g, rows, :] for g in range(ATT_GROUP)], axis=0)
            res = _bdot_nt(pl_, vaug)
            for g in range(ATT_GROUP):
                oc_ref[kv * ATT_GROUP + g, rows, 0:ATT_DIM + ONES_ROWS] = res[g * SAMPLE_SEQ:(g + 1) * SAMPLE_SEQ]
        in_seq = (ri >> 3) == b
        for h in range(M_HEADS):
            upd = jnp.dot(kt_ref[h], jnp.where(in_seq, wv_ref[h], 0.0).astype(BF16), preferred_element_type=F32)
            cout_ref[b, h] = decb_ref[h, pl.ds(b, 1), :] * cin_ref[b, h] + upd
        shift = (ROWS - SAMPLE_SEQ - b * SAMPLE_SEQ) & (ROWS - 1)
        kot_ref[b] = jnp.where(keep_new, pltpu.roll(kat_ref[...], shift, 1),
                               pltpu.roll(kct_ref[b], ROWS - SAMPLE_SEQ, 1))
        vot_ref[b] = jnp.where(keep_new, pltpu.roll(vat_ref[...], shift, 1),
                               pltpu.roll(vct, ROWS - SAMPLE_SEQ, 1))
        return carry

    lax.fori_loop(0, SAMPLE_GROUP, seq_update, 0, unroll=SEQ_UNROLL)

    att = []
    for pair in range(ATT_HEADS // 2):
        halves = []
        for hh in (2 * pair, 2 * pair + 1):
            on, oc = o_new[hh], oc_ref[hh]
            den = on[:, ATT_DIM:ATT_DIM + 1] + oc[:, ATT_DIM:ATT_DIM + 1] + esinks[hh]
            halves.append((on[:, 0:ATT_DIM] + oc[:, 0:ATT_DIM]) * (1.0 / den))
        att.append(jnp.concatenate(halves, axis=1))
    a_out = jnp.concatenate(att, axis=1) * _silu(proj_ref[:, P_ZA:P_ZA + 512])
    cat_ref[part_rows, :] = jnp.concatenate([a_out] + m_out, axis=1).astype(BF16)

    @pl.when(part == SAMPLE_PARTS - 1)
    def _():
        y_ref[...] = _out_tail(cat_ref[...], x_ref[...], wout_ref, gpost_ref)


def _sample_call(x, kct, vct, cin, nin, m0, gpre, wmain, wt, bg, sinks, gmhb, wout, gpost):
    nrows = x.shape[0]
    ngroups = nrows // ROWS
    nseq = ngroups * SAMPLE_GROUP
    parts = SAMPLE_PARTS
    assert ngroups % parts == 0
    assert nseq == ROWS, "the per-sequence stabiliser state is handled as one 128-lane row per head"
    full = lambda shape: pl.BlockSpec(shape, lambda i, p: (0,) * len(shape))
    grp = SAMPLE_GROUP
    group = lambda i, p: i * parts + p
    out_shapes = (
        jax.ShapeDtypeStruct((nrows, D_MODEL), F32),
        jax.ShapeDtypeStruct((nseq, 128, ROWS), F32),
        jax.ShapeDtypeStruct((nseq, 128, ROWS), F32),
        jax.ShapeDtypeStruct((nseq, M_HEADS, M_DIM, M_DIM), F32),
        jax.ShapeDtypeStruct((M_HEADS, nseq, M_DIM), F32),
        jax.ShapeDtypeStruct((M_HEADS, nseq), F32),
    )
    return pl.pallas_call(
        _sample_kernel,
        grid=(ngroups // parts, parts),
        in_specs=[
            pl.BlockSpec((parts * ROWS, D_MODEL), lambda i, p: (i, 0)),
            pl.BlockSpec((grp, 128, ROWS), lambda i, p: (group(i, p), 0, 0)),
            pl.BlockSpec((grp, 128, ROWS), lambda i, p: (group(i, p), 0, 0)),
            pl.BlockSpec((grp, M_HEADS, M_DIM, M_DIM), lambda i, p: (group(i, p), 0, 0, 0)),
            pl.BlockSpec((M_HEADS, grp, M_DIM), lambda i, p: (0, group(i, p), 0)),
            full((M_HEADS, nseq)),
            full((1, D_MODEL)), full((D_MODEL, P_MAIN)), full((T_ROWS, D_MODEL)), full((8, ROWS)),
            full((1, ATT_HEADS)), full((M_HEADS, M_DIM, ROWS)), full((D_MODEL, D_MODEL)), full((1, D_MODEL)),
        ],
        out_specs=(
            pl.BlockSpec((parts * ROWS, D_MODEL), lambda i, p: (i, 0)),
            pl.BlockSpec((grp, 128, ROWS), lambda i, p: (group(i, p), 0, 0)),
            pl.BlockSpec((grp, 128, ROWS), lambda i, p: (group(i, p), 0, 0)),
            pl.BlockSpec((grp, M_HEADS, M_DIM, M_DIM), lambda i, p: (group(i, p), 0, 0, 0)),
            pl.BlockSpec((M_HEADS, grp, M_DIM), lambda i, p: (0, group(i, p), 0)),
            full((M_HEADS, nseq)),
        ),
        out_shape=out_shapes,
        scratch_shapes=[
            pltpu.VMEM((parts * ROWS, P_MAIN), F32),
            pltpu.VMEM((parts, T_ROWS, ROWS), F32),
            pltpu.VMEM((parts * ROWS, D_MODEL), BF16),
            pltpu.VMEM((ATT_HEADS, ROWS, ATT_DIM), F32),
            pltpu.VMEM((ATT_HEADS, ROWS, ROWS), F32),
            pltpu.VMEM((ATT_HEADS, ROWS, ROWS), F32),
            pltpu.VMEM((M_HEADS, ROWS, M_DIM), F32),
            pltpu.VMEM((128, ROWS), F32),
            pltpu.VMEM((128, ROWS), F32),
            pltpu.VMEM((M_HEADS, ROWS, M_DIM), F32),
            pltpu.VMEM((M_HEADS, M_DIM, ROWS), BF16),
            pltpu.VMEM((M_HEADS, SAMPLE_GROUP, ROWS), F32),
            pltpu.VMEM((ATT_HEADS, ROWS, ROWS), F32),
            pltpu.VMEM((ATT_HEADS, ROWS, ROWS), F32),
        ],
        compiler_params=pltpu.CompilerParams(
            dimension_semantics=("arbitrary", "arbitrary"), vmem_limit_bytes=VMEM_LIMIT_BYTES),
        name="sample_layer",
    )(x, kct, vct, cin, nin, m0, gpre, wmain, wt, bg, sinks, gmhb, wout, gpost)


PROMPT_BLOCK = 512
PROMPT_SUBBLOCKS = 2
PROJ_COL_STEP = 512
CHUNK_STAGES = 7


def _window_in(cache):
    nseq = cache.shape[0]
    return cache.transpose(0, 2, 3, 1).reshape(nseq, ATT_KV * ATT_DIM, ROWS)


def _window_out(win_t):
    nseq = win_t.shape[0]
    return win_t.reshape(nseq, ATT_KV, ATT_DIM, ROWS).transpose(0, 3, 1, 2)[None]


def kernel(x_prompt, x_sample, cache_win_k, cache_win_v, state_C, state_n, state_m,
           g_pre, w_in, b_gate, attn_sinks, g_mh, w_out, g_post):
    depth = g_pre.shape[0]
    assert depth == 1, "single-layer trunk"
    nseq, sseq, _ = x_sample.shape
    assert sseq == SAMPLE_SEQ and nseq % SAMPLE_GROUP == 0

    gpre = g_pre[0].reshape(1, D_MODEL)
    gpost = g_post[0].reshape(1, D_MODEL)
    wmain, wt, wout, gmhb, bg = _weights_call(w_in[0].T, w_out[0], g_mh[0], b_gate)
    sinks = attn_sinks[0].reshape(1, ATT_HEADS)

    yp, wkp, wvp, cp, np_, mp = _prompt_call(x_prompt, gpre, wmain, wt, bg, sinks, gmhb, wout, gpost, PROMPT_BLOCK)

    ys, wks, wvs, cs, ns, ms = _sample_call(
        x_sample.reshape(nseq * sseq, D_MODEL), _window_in(cache_win_k[0]), _window_in(cache_win_v[0]),
        state_C[0], state_n[0].transpose(1, 0, 2), state_m[0].T,
        gpre, wmain, wt, bg, sinks, gmhb, wout, gpost)

    return (yp, ys.reshape(nseq, sseq, D_MODEL), _window_out(wkp), _window_out(wvp),
            cp[None], np_[None], mp.T[None],
            _window_out(wks), _window_out(wvs), cs[None], ns.transpose(1, 0, 2)[None], ms.T[None])
```

```python
import functools

import jax
import jax.numpy as jnp
from jax import lax
from jax.experimental import pallas as pl
from jax.experimental.pallas import tpu as pltpu

F32 = jnp.float32
BF16 = jnp.bfloat16

D_MODEL = 1024
ROWS = 128
ATT_HEADS, ATT_KV, ATT_GROUP, ATT_DIM = 8, 2, 4, 64
M_HEADS, M_DIM = 4, 128
NORM_EPS = 1e-6
NEG_BIG = -1e30
ATT_SCALE = ATT_DIM ** -0.5
LOG2E = 1.4426950408889634
K_SCALE = M_DIM ** -0.5

QA, KA, VA, ZA = 0, 512, 640, 768
QM, KM, VM, OM, ZM = 1280, 1792, 2304, 2816, 3328
GATES = 3840
D_IN = 3848

P_QA, P_ZA, P_QM, P_KM, P_OM, P_ZM, P_KA, P_VA = 0, 512, 1024, 1536, 2048, 2560, 3072, 3200
P_MAIN = 3328
T_VM, T_GATES = 0, 512
T_ROWS = 528
T_SPLIT = 272
STATE_ROWS = 144
ONES_ROWS = 16

SAMPLE_SEQ = 8
SAMPLE_GROUP = ROWS // SAMPLE_SEQ
SEQ_UNROLL = 16
SAMPLE_PARTS = 2

VMEM_LIMIT_BYTES = 56 * 1024 * 1024
NT_DIMS = (((1,), (1,)), ((), ()))


def _rms(x, g):
    return x * lax.rsqrt(jnp.mean(x * x, axis=-1, keepdims=True) + NORM_EPS) * g


def _sigmoid(x):
    return 0.5 + 0.5 * jnp.tanh(0.5 * x)


def _silu(x):
    h = 0.5 * x
    return h + h * jnp.tanh(h)


def _log_sigmoid(x):
    return -(jnp.maximum(-x, 0.0) + jnp.log1p(jnp.exp(-jnp.abs(x))))


def _slope(head):
    return 2.0 ** -(head + 1)


def _bdot(a, b):
    return jnp.dot(a.astype(BF16), b.astype(BF16), preferred_element_type=F32)


def _bdot_nt(a, b):
    return lax.dot_general(a.astype(BF16), b.astype(BF16), NT_DIMS, preferred_element_type=F32)


def _exact_dot(x, m):
    hi = x.astype(BF16).astype(F32)
    mid = (x - hi).astype(BF16).astype(F32)
    lo = (x - hi - mid).astype(BF16).astype(F32)
    parts = jnp.dot(jnp.concatenate([hi, mid, lo, jnp.zeros_like(hi)], axis=0).astype(BF16), m,
                    preferred_element_type=F32)
    return parts[0:8] + parts[8:16] + parts[16:24]


def _gate_rows(x, m0, mask_t, last_sel):
    row = lax.broadcasted_iota(jnp.int32, (8, ROWS), 0)
    head_rows = row < M_HEADS
    ic = jnp.where(head_rows, x, 0.0)
    fc = jnp.where(head_rows, _log_sigmoid(pltpu.roll(x, M_HEADS, 0)), 0.0)
    b = _exact_dot(fc, mask_t.astype(BF16))
    a = ic - b
    a_cols = jnp.concatenate([a, jnp.zeros((ROWS - 8, ROWS), F32)], axis=0).T
    a_masked = [jnp.where(mask_t, a_cols[:, h:h + 1], -jnp.inf) for h in range(M_HEADS)]
    cm = jnp.concatenate([jnp.max(am, axis=0, keepdims=True) for am in a_masked]
                         + [jnp.zeros((8 - M_HEADS, ROWS), F32)], axis=0)
    m_t = jnp.maximum(b + m0, b + cm)
    if last_sel is None:
        b_last = jnp.broadcast_to(b[:, ROWS - 1:ROWS], b.shape)
        m_new = jnp.broadcast_to(m_t[:, ROWS - 1:ROWS], b.shape)
    else:
        both = _exact_dot(jnp.where(head_rows, b, pltpu.roll(m_t, M_HEADS, 0)), last_sel)
        b_last = jnp.where(head_rows, both, 0.0)
        m_new = jnp.where(head_rows, pltpu.roll(both, M_HEADS, 0), 0.0)
    return dict(a_masked=a_masked, bm=b - m_t, gexp=jnp.exp(b + m0 - m_t), enm=jnp.exp(-m_t), m_new=m_new,
                w=jnp.exp(b_last - b + ic - m_new), decay=jnp.exp(b_last + m0 - m_new))


def _out_tail(cat, x, wout_ref, gpost_ref):
    y = jnp.dot(cat, wout_ref[...], preferred_element_type=F32)
    return x + _rms(y, gpost_ref[...])


W_BLOCK = 256
MAIN_BLOCKS = P_MAIN // W_BLOCK


def _main_src_block(i):
    return jnp.where(i < 2, i, jnp.where(i < 8, i + 1, jnp.where(i < 12, i + 3, 2)))


def _weights_kernel(main_ref, vm0_ref, vm1_ref, vm2_ref, vm3_ref, gates_ref, wout_ref, gmh_ref, bgate_ref,
                    wmain_ref, wt_ref, woutb_ref, gmhb_ref, bgb_ref):
    wmain_ref[...] = main_ref[...].T.astype(BF16)

    @pl.when(pl.program_id(0) == 0)
    def _():
        wt_ref[...] = jnp.concatenate(
            [vm0_ref[...], vm1_ref[...], vm2_ref[...], vm3_ref[...], gates_ref[...],
             jnp.zeros((T_ROWS - T_GATES - 8, D_MODEL), F32)], axis=0).astype(BF16)
        woutb_ref[...] = wout_ref[...].astype(BF16)
        for h in range(M_HEADS):
            gmhb_ref[h] = jnp.broadcast_to(gmh_ref[h:h + 1, :], (M_DIM, ROWS)).T
        bg_row = jnp.concatenate([bgate_ref[...], jnp.zeros((1, ROWS - 2 * M_HEADS), F32)], axis=1)
        bgb_ref[...] = jnp.broadcast_to(bg_row, (ROWS, ROWS)).T[0:2 * M_HEADS, :]


def _weights_call(w_in_t, w_out, gmh, bgate):
    blk = lambda r: pl.BlockSpec((128, D_MODEL), lambda i, r=r: (r, 0))
    const = lambda shape: pl.BlockSpec(shape, lambda i: (0,) * len(shape))
    return pl.pallas_call(
        _weights_kernel,
        grid=(MAIN_BLOCKS,),
        in_specs=[
            pl.BlockSpec((W_BLOCK, D_MODEL), lambda i: (_main_src_block(i), 0)),
            blk(VM // 128), blk(VM // 128 + 1), blk(VM // 128 + 2), blk(VM // 128 + 3),
            pl.BlockSpec((8, D_MODEL), lambda i: (GATES // 8, 0)),
            const((D_MODEL, D_MODEL)), const((M_HEADS, M_DIM)), const((1, 2 * M_HEADS)),
        ],
        out_specs=(
            pl.BlockSpec((D_MODEL, W_BLOCK), lambda i: (0, i)),
            const((T_ROWS, D_MODEL)), const((D_MODEL, D_MODEL)),
            const((M_HEADS, M_DIM, ROWS)), const((2 * M_HEADS, ROWS)),
        ),
        out_shape=(
            jax.ShapeDtypeStruct((D_MODEL, P_MAIN), BF16),
            jax.ShapeDtypeStruct((T_ROWS, D_MODEL), BF16),
            jax.ShapeDtypeStruct((D_MODEL, D_MODEL), BF16),
            jax.ShapeDtypeStruct((M_HEADS, M_DIM, ROWS), F32),
            jax.ShapeDtypeStruct((2 * M_HEADS, ROWS), F32),
        ),
        compiler_params=pltpu.CompilerParams(
            dimension_semantics=("arbitrary",), vmem_limit_bytes=VMEM_LIMIT_BYTES),
        name="layer_weights",
    )(w_in_t, w_in_t, w_in_t, w_in_t, w_in_t, w_in_t, w_out, gmh, bgate)


def _prompt_bias_t(kv, first):
    j = lax.broadcasted_iota(jnp.int32, (2 * ROWS, ROWS), 0)
    i = lax.broadcasted_iota(jnp.int32, (2 * ROWS, ROWS), 1)
    diff = ROWS + i - j
    valid = (diff >= 0) & (diff < ROWS)
    if first:
        valid = valid & (j >= ROWS)
    dfl = diff.astype(F32)
    return jnp.concatenate(
        [jnp.where(valid, -_slope(kv * ATT_GROUP + g) * dfl, NEG_BIG) * LOG2E for g in range(ATT_GROUP)], axis=1)


def _prompt_kernel(x_ref, xnext_ref, gpre_ref, wmain_ref, wt_ref, bg_ref, sink_ref, gmhb_ref, wout_ref, gpost_ref,
                   y_ref, wk_ref, wv_ref, c_ref, n_ref, m_ref,
                   proj_ref, projt_ref, cat_ref, xn_ref, kprev_ref, vtprev_ref, ct_ref, mst_ref, bias_ref,
                   *, tb, nt):
    bi = pl.program_id(0)
    j = pl.program_id(1)
    nchunks = tb // ROWS

    @pl.when((bi == 0) & (j == 0))
    def _():
        for first in range(2):
            for kv in range(ATT_KV):
                bias_ref[first * ATT_KV + kv] = _prompt_bias_t(kv, first)
        m_ref[...] = jnp.zeros_like(m_ref)

    @pl.when(j == 0)
    def _():
        kprev_ref[...] = jnp.zeros_like(kprev_ref)
        vtprev_ref[...] = jnp.zeros_like(vtprev_ref)
        ct_ref[...] = jnp.zeros_like(ct_ref)
        mst_ref[...] = jnp.zeros_like(mst_ref)

    sub = tb // PROMPT_SUBBLOCKS
    sub_chunks = sub // ROWS

    def norm_rows(sb):
        rows_sb = slice(sb * sub, (sb + 1) * sub)
        xn_ref[rows_sb, :] = _rms(x_ref[0, rows_sb, :], gpre_ref[...]).astype(BF16)

    def proj_cols(sb, c0, c1):
        rows_sb = slice(sb * sub, (sb + 1) * sub)
        proj_ref[rows_sb, c0:c1] = jnp.dot(xn_ref[rows_sb, :], wmain_ref[:, c0:c1], preferred_element_type=F32)

    def proj_t(sb):
        for r0, r1 in ((0, T_SPLIT), (T_SPLIT, T_ROWS)):
            pt = lax.dot_general(wt_ref[r0:r1, :], xn_ref[sb * sub:(sb + 1) * sub, :], NT_DIMS,
                                 preferred_element_type=F32)
            for c in range(sub_chunks):
                projt_ref[sb * sub_chunks + c, r0:r1, :] = pt[:, c * ROWS:(c + 1) * ROWS]

    def out_rows(sb):
        rows_sb = slice(sb * sub, (sb + 1) * sub)
        y_ref[0, rows_sb, :] = _out_tail(cat_ref[rows_sb, :], x_ref[0, rows_sb, :], wout_ref, gpost_ref)

    def proj_pieces(sb):
        bounds = list(range(0, P_MAIN, PROJ_COL_STEP)) + [P_MAIN]
        pieces = [functools.partial(proj_cols, sb, c0, c1) for c0, c1 in zip(bounds[:-1], bounds[1:])]
        return pieces[:2] + [functools.partial(proj_t, sb)] + pieces[2:]

    ri = lax.broadcasted_iota(jnp.int32, (ROWS, ROWS), 0)
    ci = lax.broadcasted_iota(jnp.int32, (ROWS, ROWS), 1)
    mask_t = ri <= ci
    st_row = lax.broadcasted_iota(jnp.int32, (STATE_ROWS - M_DIM, ROWS), 0)

    def chunk(c):
        rows = pl.ds(c * ROWS, ROWS)
        first = ((j == 0) & (c == 0)).astype(jnp.int32)
        yield

        qa = proj_ref[rows, P_QA:P_QA + 512] * (ATT_SCALE * LOG2E)
        kcur = proj_ref[rows, P_KA:P_KA + 128].astype(BF16)
        vtcur = proj_ref[rows, P_VA:P_VA + 128].T.astype(BF16)
        kcat = jnp.concatenate([kprev_ref[...], kcur], axis=0)
        vtcat = jnp.concatenate([vtprev_ref[...], vtcur], axis=1)
        r = _gate_rows(projt_ref[c, T_GATES:T_GATES + 8, :] + bg_ref[...], mst_ref[...], mask_t, None)

        scores, sinks = [], []
        for kv in range(ATT_KV):
            want_hi = kv == 1
            keep = (ci >= ATT_DIM) if want_hi else (ci < ATT_DIM)
            pieces = []
            for g in range(ATT_GROUP):
                hh = kv * ATT_GROUP + g
                blk = qa[:, (hh // 2) * 128:(hh // 2 + 1) * 128]
                if (hh % 2 == 1) != want_hi:
                    blk = pltpu.roll(blk, ATT_DIM, 1)
                pieces.append(jnp.where(keep, blk, 0.0))
            q4 = jnp.concatenate(pieces, axis=0)
            scores.append(_bdot_nt(kcat, q4) + bias_ref[first * ATT_KV + kv])
            sinks.append(jnp.concatenate(
                [jnp.broadcast_to(sink_ref[0:1, kv * ATT_GROUP + g:kv * ATT_GROUP + g + 1] * LOG2E, (1, ROWS))
                 for g in range(ATT_GROUP)], axis=1))
        yield
        vts, r1s = [], []
        for h in range(M_HEADS):
            q = proj_ref[rows, P_QM + h * M_DIM:P_QM + (h + 1) * M_DIM].astype(BF16)
            k = (proj_ref[rows, P_KM + h * M_DIM:P_KM + (h + 1) * M_DIM] * K_SCALE).astype(BF16)
            vt = projt_ref[c, T_VM + h * M_DIM:T_VM + (h + 1) * M_DIM, :]
            ct = ct_ref[h]
            r1s.append(lax.dot_general(jnp.concatenate([k, ct.astype(BF16)], axis=0), q, NT_DIMS,
                                       preferred_element_type=F32))
            w_row = r["w"][h:h + 1, :]
            lhs2 = jnp.concatenate([vt * w_row, jnp.where(st_row == 0, w_row, 0.0)], axis=0)
            ct_ref[h] = r["decay"][h:h + 1, 0:1] * ct + _bdot(lhs2, k)
            vts.append(vt)
        mst_ref[...] = r["m_new"]
        yield

        outs, nums, dens = [], [], []
        for kv in range(ATT_KV):
            s, sink = scores[kv], sinks[kv]
            mx = jnp.maximum(jnp.max(s, axis=0, keepdims=True), sink)
            p = jnp.exp2(s - mx).astype(BF16)
            lhs = jnp.concatenate([vtcat[kv * ATT_DIM:(kv + 1) * ATT_DIM, :],
                                   jnp.ones((ONES_ROWS, 2 * ROWS), BF16)], axis=0)
            o = jnp.dot(lhs, p, preferred_element_type=F32)
            outs.append((o, jnp.exp2(sink - mx)))
        yield
        for h in range(M_HEADS):
            r1 = r1s[h]
            dm = jnp.exp(r["a_masked"][h] + r["bm"][h:h + 1, :])
            st = r1[0:ROWS] * dm
            g_row = r["gexp"][h:h + 1, :]
            nums.append(_bdot(vts[h], st) + g_row * r1[ROWS:ROWS + M_DIM])
            dens.append(jnp.sum(st, axis=0, keepdims=True) + g_row * r1[ROWS + M_DIM:ROWS + M_DIM + 1])
        yield

        att = []
        for kv in range(ATT_KV):
            o, esink = outs[kv]
            on = o[0:ATT_DIM, :] * (1.0 / (o[ATT_DIM:ATT_DIM + 1, :] + esink))
            for pair in range(2):
                two = jnp.concatenate([on[:, (2 * pair) * ROWS:(2 * pair + 1) * ROWS],
                                       on[:, (2 * pair + 1) * ROWS:(2 * pair + 2) * ROWS]], axis=0)
                att.append(two.T)
        a_out = jnp.concatenate(att, axis=1) * _silu(proj_ref[rows, P_ZA:P_ZA + 512])
        kprev_ref[...] = kcur
        vtprev_ref[...] = vtcur
        yield
        m_out = []
        for h in range(M_HEADS):
            ht = nums[h] * (1.0 / jnp.maximum(jnp.abs(dens[h]), r["enm"][h:h + 1, :]))
            hn = ht * lax.rsqrt(jnp.mean(ht * ht, axis=0, keepdims=True) + NORM_EPS) * gmhb_ref[h]
            m_out.append(_sigmoid(proj_ref[rows, P_OM + h * M_DIM:P_OM + (h + 1) * M_DIM]) * hn.T
                         * _silu(proj_ref[rows, P_ZM + h * M_DIM:P_ZM + (h + 1) * M_DIM]))

        cat_ref[rows, :] = jnp.concatenate([a_out] + m_out, axis=1).astype(BF16)
        yield

    @pl.when((bi == 0) & (j == 0))
    def _():
        for sb in range(PROMPT_SUBBLOCKS):
            norm_rows(sb)
        for piece in proj_pieces(0):
            piece()

    def norm_next():
        xn_ref[...] = _rms(xnext_ref[0], gpre_ref[...]).astype(BF16)

    def run_chunks(sb, fillers):
        n_fill, n_slots, slot = len(fillers), sub_chunks * CHUNK_STAGES, 0
        for c in range(sb * sub_chunks, (sb + 1) * sub_chunks):
            for _ in chunk(c):
                for _ in range(-(-(slot + 1) * n_fill // n_slots) + (-slot * n_fill // n_slots)):
                    fillers.pop(0)()
                slot += 1
        assert slot == n_slots and not fillers

    assert PROMPT_SUBBLOCKS == 2
    run_chunks(0, proj_pieces(1))
    nxt = proj_pieces(0)
    run_chunks(1, [functools.partial(out_rows, 0), norm_next] + nxt[:-TAIL_PIECES])
    out_rows(1)
    for piece in nxt[-TAIL_PIECES:]:
        piece()

    @pl.when(j == nt - 1)
    def _():
        wk_ref[0] = proj_ref[tb - ROWS:tb, P_KA:P_KA + 128].T
        wv_ref[0] = proj_ref[tb - ROWS:tb, P_VA:P_VA + 128].T
        for h in range(M_HEADS):
            ct = ct_ref[h]
            c_ref[0, h] = ct[0:M_DIM].T
            n_ref[0, h:h + 1, :] = ct[M_DIM:M_DIM + 1]
        seq_lane = lax.broadcasted_iota(jnp.int32, m_ref.shape, 1) == bi
        m_ref[...] = jnp.where(seq_lane, mst_ref[0:M_HEADS, 0:m_ref.shape[1]], m_ref[...])


def _prompt_call(x, gpre, wmain, wt, bg, sinks, gmhb, wout, gpost, tb):
    bsz, seq, _ = x.shape
    nt = seq // tb
    full = lambda shape: pl.BlockSpec(shape, lambda b, j: (0,) * len(shape))

    def next_block(b, j):
        wrap = (j + 1 == nt).astype(jnp.int32)
        return (jnp.minimum(b + wrap, bsz - 1), (j + 1) * (1 - wrap), 0)

    out_shapes = (
        jax.ShapeDtypeStruct((bsz, seq, D_MODEL), F32),
        jax.ShapeDtypeStruct((bsz, 128, ROWS), F32),
        jax.ShapeDtypeStruct((bsz, 128, ROWS), F32),
        jax.ShapeDtypeStruct((bsz, M_HEADS, M_DIM, M_DIM), F32),
        jax.ShapeDtypeStruct((bsz, M_HEADS, M_DIM), F32),
        jax.ShapeDtypeStruct((M_HEADS, bsz), F32),
    )
    return pl.pallas_call(
        functools.partial(_prompt_kernel, tb=tb, nt=nt),
        grid=(bsz, nt),
        in_specs=[
            pl.BlockSpec((1, tb, D_MODEL), lambda b, j: (b, j, 0)),
            pl.BlockSpec((1, tb, D_MODEL), next_block),
            full((1, D_MODEL)), full((D_MODEL, P_MAIN)), full((T_ROWS, D_MODEL)), full((8, ROWS)),
            full((1, ATT_HEADS)), full((M_HEADS, M_DIM, ROWS)), full((D_MODEL, D_MODEL)), full((1, D_MODEL)),
        ],
        out_specs=(
            pl.BlockSpec((1, tb, D_MODEL), lambda b, j: (b, j, 0)),
            pl.BlockSpec((1, 128, ROWS), lambda b, j: (b, 0, 0)),
            pl.BlockSpec((1, 128, ROWS), lambda b, j: (b, 0, 0)),
            pl.BlockSpec((1, M_HEADS, M_DIM, M_DIM), lambda b, j: (b, 0, 0, 0)),
            pl.BlockSpec((1, M_HEADS, M_DIM), lambda b, j: (b, 0, 0)),
            pl.BlockSpec((M_HEADS, bsz), lambda b, j: (0, 0)),
        ),
        out_shape=out_shapes,
        scratch_shapes=[
            pltpu.VMEM((tb, P_MAIN), F32),
            pltpu.VMEM((tb // ROWS, T_ROWS, ROWS), F32),
            pltpu.VMEM((tb, D_MODEL), BF16),
            pltpu.VMEM((tb, D_MODEL), BF16),
            pltpu.VMEM((ROWS, 128), BF16),
            pltpu.VMEM((128, ROWS), BF16),
            pltpu.VMEM((M_HEADS, STATE_ROWS, M_DIM), F32),
            pltpu.VMEM((8, ROWS), F32),
            pltpu.VMEM((2 * ATT_KV, 2 * ROWS, ATT_GROUP * ROWS), F32),
        ],
        compiler_params=pltpu.CompilerParams(
            dimension_semantics=("arbitrary", "arbitrary"), vmem_limit_bytes=VMEM_LIMIT_BYTES),
        name="prompt_layer",
    )(x, x, gpre, wmain, wt, bg, sinks, gmhb, wout, gpost)


def _sample_bias_new(head):
    r = lax.broadcasted_iota(jnp.int32, (ROWS, ROWS), 0)
    c = lax.broadcasted_iota(jnp.int32, (ROWS, ROWS), 1)
    valid = ((r >> 3) == (c >> 3)) & (r >= c)
    return jnp.where(valid, -_slope(head) * (r - c).astype(F32), NEG_BIG)


def _sample_bias_cache(head):
    r = lax.broadcasted_iota(jnp.int32, (ROWS, ROWS), 0)
    c = lax.broadcasted_iota(jnp.int32, (ROWS, ROWS), 1)
    diff = (r & (SAMPLE_SEQ - 1)) + ROWS - c
    return jnp.where(diff < ROWS, -_slope(head) * diff.astype(F32), NEG_BIG)


def _sample_kernel(x_ref, kct_ref, vct_ref, cin_ref, nin_ref, m0_ref,
                   gpre_ref, wmain_ref, wt_ref, bg_ref, sink_ref, gmhb_ref, wout_ref, gpost_ref,
                   y_ref, kot_ref, vot_ref, cout_ref, nout_ref, mout_ref,
                   projfull_ref, projt_ref, cat_ref, qh_ref, sc_ref, oc_ref, qc_ref, kat_ref, vat_ref, wv_ref, kt_ref,
                   decb_ref, bn_ref, bc_ref):
    step, part = pl.program_id(0), pl.program_id(1)

    @pl.when((step == 0) & (part == 0))
    def _():
        for hh in range(ATT_HEADS):
            bn_ref[hh] = _sample_bias_new(hh)
            bc_ref[hh] = _sample_bias_cache(hh)
        mout_ref[...] = jnp.zeros_like(mout_ref)

    @pl.when(part == 0)
    def _():
        xn = _rms(x_ref[...], gpre_ref[...]).astype(BF16)
        projfull_ref[...] = jnp.dot(xn, wmain_ref[...], preferred_element_type=F32)
        ptf = lax.dot_general(wt_ref[...], xn, NT_DIMS, preferred_element_type=F32)
        for g in range(SAMPLE_PARTS):
            projt_ref[g] = ptf[:, g * ROWS:(g + 1) * ROWS]

    part_rows = pl.ds(pl.multiple_of(part * ROWS, ROWS), ROWS)
    proj_ref = projfull_ref.at[part_rows]
    pt = projt_ref[part]

    ri = lax.broadcasted_iota(jnp.int32, (ROWS, ROWS), 0)
    ci = lax.broadcasted_iota(jnp.int32, (ROWS, ROWS), 1)
    same_seq = (ri >> 3) == (ci >> 3)
    mask_t = same_seq & (ri <= ci)
    last_sel = (same_seq & ((ri & (SAMPLE_SEQ - 1)) == SAMPLE_SEQ - 1)).astype(BF16)
    row16 = lax.broadcasted_iota(jnp.int32, (SAMPLE_GROUP, ROWS), 0)
    lane16 = lax.broadcasted_iota(jnp.int32, (SAMPLE_GROUP, ROWS), 1)
    seq_of_lane = (lane16 >> 3) == row16
    ones_rows = jnp.ones((ONES_ROWS, ROWS), F32)

    qa = proj_ref[:, P_QA:P_QA + 512] * ATT_SCALE
    for hh in range(ATT_HEADS):
        blk = qa[:, (hh // 2) * 128:(hh // 2 + 1) * 128]
        if hh % 2 == 1:
            blk = pltpu.roll(blk, ATT_DIM, 1)
        qh_ref[hh] = blk[:, 0:ATT_DIM]
    ka = proj_ref[:, P_KA:P_KA + 128]
    kat = ka.T
    vat = proj_ref[:, P_VA:P_VA + 128].T
    kat_ref[...] = kat
    vat_ref[...] = vat

    def seq_scores(b, carry):
        rows = pl.ds(pl.multiple_of(b * SAMPLE_SEQ, SAMPLE_SEQ), SAMPLE_SEQ)
        kct = kct_ref[b]
        for kv in range(ATT_KV):
            lhs = jnp.concatenate([qh_ref[kv * ATT_GROUP + g, rows, :] for g in range(ATT_GROUP)], axis=0)
            res = _bdot(lhs, kct[kv * ATT_DIM:(kv + 1) * ATT_DIM, :])
            for g in range(ATT_GROUP):
                sc_ref[kv * ATT_GROUP + g, rows, :] = res[g * SAMPLE_SEQ:(g + 1) * SAMPLE_SEQ]
        for h in range(M_HEADS):
            qc_ref[h, rows, :] = _bdot(proj_ref[rows, P_QM + h * M_DIM:P_QM + (h + 1) * M_DIM], cin_ref[b, h])
        return carry

    lax.fori_loop(0, SAMPLE_GROUP, seq_scores, 0, unroll=SEQ_UNROLL)

    o_new, esinks = [], []
    for hh in range(ATT_HEADS):
        kv = hh // ATT_GROUP
        s_n = _bdot(qh_ref[hh], kat[kv * ATT_DIM:(kv + 1) * ATT_DIM, :]) + bn_ref[hh]
        s_c = sc_ref[hh] + bc_ref[hh]
        sink = sink_ref[0:1, hh:hh + 1]
        mx = jnp.maximum(jnp.max(jnp.maximum(s_n, s_c), axis=-1, keepdims=True), sink)
        mx_b = jnp.broadcast_to(mx, (ROWS, ROWS))
        p_n = jnp.exp(s_n - mx_b)
        sc_ref[hh] = jnp.exp(s_c - mx_b)
        vaug = jnp.concatenate([vat[kv * ATT_DIM:(kv + 1) * ATT_DIM, :], ones_rows], axis=0)
        o_new.append(_bdot_nt(p_n, vaug))
        esinks.append(jnp.exp(sink - mx))

    first_seq = (step * SAMPLE_PARTS + part) * SAMPLE_GROUP
    seq_step = ri == first_seq + (ci >> 3)
    m0 = _exact_dot(jnp.concatenate([m0_ref[...], jnp.zeros((8 - M_HEADS, ROWS), F32)], axis=0),
                    seq_step.astype(BF16))
    r = _gate_rows(pt[T_GATES:T_GATES + 8, :] + bg_ref[...], m0, mask_t, last_sel)
    step_seq = (ci == first_seq + (ri >> 3)) & ((ri & (SAMPLE_SEQ - 1)) == 0)
    mout_ref[...] += _exact_dot(r["m_new"], step_seq.astype(BF16))[0:M_HEADS]
    m_out = []
    for h in range(M_HEADS):
        q = proj_ref[:, P_QM + h * M_DIM:P_QM + (h + 1) * M_DIM].astype(BF16)
        kf = proj_ref[:, P_KM + h * M_DIM:P_KM + (h + 1) * M_DIM] * K_SCALE
        k = kf.astype(BF16)
        vt = pt[T_VM + h * M_DIM:T_VM + (h + 1) * M_DIM, :]
        n_h = nin_ref[h]
        r1 = lax.dot_general(jnp.concatenate([k, n_h.astype(BF16)], axis=0), q, NT_DIMS,
                             preferred_element_type=F32)
        st = r1[0:ROWS] * jnp.exp(r["a_masked"][h] + r["bm"][h:h + 1, :])
        q_n = jnp.sum(jnp.where(seq_of_lane, r1[ROWS:ROWS + SAMPLE_GROUP], 0.0), axis=0, keepdims=True)
        g_row = r["gexp"][h:h + 1, :]
        num = _bdot(vt, st) + g_row * qc_ref[h].T
        den = jnp.sum(st, axis=0, keepdims=True) + g_row * q_n
        ht = num * (1.0 / jnp.maximum(jnp.abs(den), r["enm"][h:h + 1, :]))
        hn = ht * lax.rsqrt(jnp.mean(ht * ht, axis=0, keepdims=True) + NORM_EPS) * gmhb_ref[h]
        m_out.append(_sigmoid(proj_ref[:, P_OM + h * M_DIM:P_OM + (h + 1) * M_DIM]) * hn.T
                     * _silu(proj_ref[:, P_ZM + h * M_DIM:P_ZM + (h + 1) * M_DIM]))
        w_row = r["w"][h:h + 1, :]
        dec16 = jnp.sum(jnp.where(lane16 == row16 * SAMPLE_SEQ, r["decay"][h:h + 1, :], 0.0),
                        axis=1, keepdims=True)
        nout_ref[h] = dec16 * n_h + _bdot(jnp.where(seq_of_lane, w_row, 0.0), k)
        decb_ref[h] = jnp.broadcast_to(dec16, (SAMPLE_GROUP, ROWS))
        wv_ref[h] = (vt * w_row).T
        kt_ref[h] = kf.T.astype(BF16)

    keep_new = ci >= ROWS - SAMPLE_SEQ

    def seq_update(b, carry):
        rows = pl.ds(pl.multiple_of(b * SAMPLE_SEQ, SAMPLE_SEQ), SAMPLE_SEQ)
        vct = vct_ref[b]
        for kv in range(ATT_KV):
            vaug = jnp.concatenate([vct[kv * ATT_DIM:(kv + 1) * ATT_DIM, :], ones_rows], axis=0)
            pl_ = jnp.concatenate([sc_ref[kv * ATT_GROUP + g, rows, :] for g in range(ATT_GROUP)], axis=0)
            res = _bdot_nt(pl_, vaug)
            for g in range(ATT_GROUP):
                oc_ref[kv * ATT_GROUP + g, rows, 0:ATT_DIM + ONES_ROWS] = res[g * SAMPLE_SEQ:(g + 1) * SAMPLE_SEQ]
        in_seq = (ri >> 3) == b
        for h in range(M_HEADS):
            upd = jnp.dot(kt_ref[h], jnp.where(in_seq, wv_ref[h], 0.0).astype(BF16), preferred_element_type=F32)
            cout_ref[b, h] = decb_ref[h, pl.ds(b, 1), :] * cin_ref[b, h] + upd
        shift = (ROWS - SAMPLE_SEQ - b * SAMPLE_SEQ) & (ROWS - 1)
        kot_ref[b] = jnp.where(keep_new, pltpu.roll(kat_ref[...], shift, 1),
                               pltpu.roll(kct_ref[b], ROWS - SAMPLE_SEQ, 1))
        vot_ref[b] = jnp.where(keep_new, pltpu.roll(vat_ref[...], shift, 1),
                               pltpu.roll(vct, ROWS - SAMPLE_SEQ, 1))
        return carry

    lax.fori_loop(0, SAMPLE_GROUP, seq_update, 0, unroll=SEQ_UNROLL)

    att = []
    for pair in range(ATT_HEADS // 2):
        halves = []
        for hh in (2 * pair, 2 * pair + 1):
            on, oc = o_new[hh], oc_ref[hh]
            den = on[:, ATT_DIM:ATT_DIM + 1] + oc[:, ATT_DIM:ATT_DIM + 1] + esinks[hh]
            halves.append((on[:, 0:ATT_DIM] + oc[:, 0:ATT_DIM]) * (1.0 / den))
        att.append(jnp.concatenate(halves, axis=1))
    a_out = jnp.concatenate(att, axis=1) * _silu(proj_ref[:, P_ZA:P_ZA + 512])
    cat_ref[part_rows, :] = jnp.concatenate([a_out] + m_out, axis=1).astype(BF16)

    @pl.when(part == SAMPLE_PARTS - 1)
    def _():
        y_ref[...] = _out_tail(cat_ref[...], x_ref[...], wout_ref, gpost_ref)


def _sample_call(x, kct, vct, cin, nin, m0, gpre, wmain, wt, bg, sinks, gmhb, wout, gpost):
    nrows = x.shape[0]
    ngroups = nrows // ROWS
    nseq = ngroups * SAMPLE_GROUP
    parts = SAMPLE_PARTS
    assert ngroups % parts == 0
    assert nseq == ROWS, "the per-sequence stabiliser state is handled as one 128-lane row per head"
    full = lambda shape: pl.BlockSpec(shape, lambda i, p: (0,) * len(shape))
    grp = SAMPLE_GROUP
    group = lambda i, p: i * parts + p
    out_shapes = (
        jax.ShapeDtypeStruct((nrows, D_MODEL), F32),
        jax.ShapeDtypeStruct((nseq, 128, ROWS), F32),
        jax.ShapeDtypeStruct((nseq, 128, ROWS), F32),
        jax.ShapeDtypeStruct((nseq, M_HEADS, M_DIM, M_DIM), F32),
        jax.ShapeDtypeStruct((M_HEADS, nseq, M_DIM), F32),
        jax.ShapeDtypeStruct((M_HEADS, nseq), F32),
    )
    return pl.pallas_call(
        _sample_kernel,
        grid=(ngroups // parts, parts),
        in_specs=[
            pl.BlockSpec((parts * ROWS, D_MODEL), lambda i, p: (i, 0)),
            pl.BlockSpec((grp, 128, ROWS), lambda i, p: (group(i, p), 0, 0)),
            pl.BlockSpec((grp, 128, ROWS), lambda i, p: (group(i, p), 0, 0)),
            pl.BlockSpec((grp, M_HEADS, M_DIM, M_DIM), lambda i, p: (group(i, p), 0, 0, 0)),
            pl.BlockSpec((M_HEADS, grp, M_DIM), lambda i, p: (0, group(i, p), 0)),
            full((M_HEADS, nseq)),
            full((1, D_MODEL)), full((D_MODEL, P_MAIN)), full((T_ROWS, D_MODEL)), full((8, ROWS)),
            full((1, ATT_HEADS)), full((M_HEADS, M_DIM, ROWS)), full((D_MODEL, D_MODEL)), full((1, D_MODEL)),
        ],
        out_specs=(
            pl.BlockSpec((parts * ROWS, D_MODEL), lambda i, p: (i, 0)),
            pl.BlockSpec((grp, 128, ROWS), lambda i, p: (group(i, p), 0, 0)),
            pl.BlockSpec((grp, 128, ROWS), lambda i, p: (group(i, p), 0, 0)),
            pl.BlockSpec((grp, M_HEADS, M_DIM, M_DIM), lambda i, p: (group(i, p), 0, 0, 0)),
            pl.BlockSpec((M_HEADS, grp, M_DIM), lambda i, p: (0, group(i, p), 0)),
            full((M_HEADS, nseq)),
        ),
        out_shape=out_shapes,
        scratch_shapes=[
            pltpu.VMEM((parts * ROWS, P_MAIN), F32),
            pltpu.VMEM((parts, T_ROWS, ROWS), F32),
            pltpu.VMEM((parts * ROWS, D_MODEL), BF16),
            pltpu.VMEM((ATT_HEADS, ROWS, ATT_DIM), F32),
            pltpu.VMEM((ATT_HEADS, ROWS, ROWS), F32),
            pltpu.VMEM((ATT_HEADS, ROWS, ROWS), F32),
            pltpu.VMEM((M_HEADS, ROWS, M_DIM), F32),
            pltpu.VMEM((128, ROWS), F32),
            pltpu.VMEM((128, ROWS), F32),
            pltpu.VMEM((M_HEADS, ROWS, M_DIM), F32),
            pltpu.VMEM((M_HEADS, M_DIM, ROWS), BF16),
            pltpu.VMEM((M_HEADS, SAMPLE_GROUP, ROWS), F32),
            pltpu.VMEM((ATT_HEADS, ROWS, ROWS), F32),
            pltpu.VMEM((ATT_HEADS, ROWS, ROWS), F32),
        ],
        compiler_params=pltpu.CompilerParams(
            dimension_semantics=("arbitrary", "arbitrary"), vmem_limit_bytes=VMEM_LIMIT_BYTES),
        name="sample_layer",
    )(x, kct, vct, cin, nin, m0, gpre, wmain, wt, bg, sinks, gmhb, wout, gpost)


PROMPT_BLOCK = 512
PROMPT_SUBBLOCKS = 2
PROJ_COL_STEP = 512
CHUNK_STAGES = 7
TAIL_PIECES = 3


def _window_in(cache):
    nseq = cache.shape[0]
    return cache.transpose(0, 2, 3, 1).reshape(nseq, ATT_KV * ATT_DIM, ROWS)


def _window_out(win_t):
    nseq = win_t.shape[0]
    return win_t.reshape(nseq, ATT_KV, ATT_DIM, ROWS).transpose(0, 3, 1, 2)[None]


def kernel(x_prompt, x_sample, cache_win_k, cache_win_v, state_C, state_n, state_m,
           g_pre, w_in, b_gate, attn_sinks, g_mh, w_out, g_post):
    depth = g_pre.shape[0]
    assert depth == 1, "single-layer trunk"
    nseq, sseq, _ = x_sample.shape
    assert sseq == SAMPLE_SEQ and nseq % SAMPLE_GROUP == 0

    gpre = g_pre[0].reshape(1, D_MODEL)
    gpost = g_post[0].reshape(1, D_MODEL)
    wmain, wt, wout, gmhb, bg = _weights_call(w_in[0].T, w_out[0], g_mh[0], b_gate)
    sinks = attn_sinks[0].reshape(1, ATT_HEADS)

    yp, wkp, wvp, cp, np_, mp = _prompt_call(x_prompt, gpre, wmain, wt, bg, sinks, gmhb, wout, gpost, PROMPT_BLOCK)

    ys, wks, wvs, cs, ns, ms = _sample_call(
        x_sample.reshape(nseq * sseq, D_MODEL), _window_in(cache_win_k[0]), _window_in(cache_win_v[0]),
        state_C[0], state_n[0].transpose(1, 0, 2), state_m[0].T,
        gpre, wmain, wt, bg, sinks, gmhb, wout, gpost)

    return (yp, ys.reshape(nseq, sseq, D_MODEL), _window_out(wkp), _window_out(wvp),
            cp[None], np_[None], mp.T[None],
            _window_out(wks), _window_out(wvs), cs[None], ns.transpose(1, 0, 2)[None], ms.T[None])
```

```python
import functools

import jax
import jax.numpy as jnp
from jax import lax
from jax.experimental import pallas as pl
from jax.experimental.pallas import tpu as pltpu

F32 = jnp.float32
BF16 = jnp.bfloat16

D_MODEL = 1024
ROWS = 128
ATT_HEADS, ATT_KV, ATT_GROUP, ATT_DIM = 8, 2, 4, 64
M_HEADS, M_DIM = 4, 128
NORM_EPS = 1e-6
NEG_BIG = -1e30
ATT_SCALE = ATT_DIM ** -0.5
LOG2E = 1.4426950408889634
K_SCALE = M_DIM ** -0.5

VM, GATES = 2304, 3840

P_QA, P_ZA, P_QM, P_KM, P_OM, P_ZM, P_KA, P_VA = 0, 512, 1024, 1536, 2048, 2560, 3072, 3200
P_MAIN = 3328
T_VM, T_GATES = 0, 512
T_ROWS = 528
T_SPLIT = 272
STATE_ROWS = 144
ONES_ROWS = 16

SAMPLE_SEQ = 8
SAMPLE_GROUP = ROWS // SAMPLE_SEQ
SAMPLE_PARTS = 2

VMEM_BYTES_V7X = 64 * 1024 * 1024
VMEM_LIMIT_BYTES = VMEM_BYTES_V7X * 7 // 8
NT_DIMS = (((1,), (1,)), ((), ()))


def _rms(x, g):
    return x * lax.rsqrt(jnp.mean(x * x, axis=-1, keepdims=True) + NORM_EPS) * g


def _sigmoid(x):
    return 0.5 + 0.5 * jnp.tanh(0.5 * x)


def _silu(x):
    h = 0.5 * x
    return h + h * jnp.tanh(h)


def _log_sigmoid(x):
    return -(jnp.maximum(-x, 0.0) + jnp.log1p(jnp.exp(-jnp.abs(x))))


def _slope(head):
    return 2.0 ** -(head + 1)


def _bdot(a, b):
    return jnp.dot(a.astype(BF16), b.astype(BF16), preferred_element_type=F32)


def _bdot_nt(a, b):
    return lax.dot_general(a.astype(BF16), b.astype(BF16), NT_DIMS, preferred_element_type=F32)


def _exact_dot(x, m):
    hi = x.astype(BF16).astype(F32)
    mid = (x - hi).astype(BF16).astype(F32)
    lo = (x - hi - mid).astype(BF16).astype(F32)
    parts = jnp.dot(jnp.concatenate([hi, mid, lo, jnp.zeros_like(hi)], axis=0).astype(BF16), m,
                    preferred_element_type=F32)
    return parts[0:8] + parts[8:16] + parts[16:24]


def _gate_rows(x, m0, mask_t, last_sel):
    row = lax.broadcasted_iota(jnp.int32, (8, ROWS), 0)
    head_rows = row < M_HEADS
    ic = jnp.where(head_rows, x, 0.0)
    fc = jnp.where(head_rows, _log_sigmoid(pltpu.roll(x, M_HEADS, 0)), 0.0)
    b = _exact_dot(fc, mask_t.astype(BF16))
    a = ic - b
    a_cols = jnp.concatenate([a, jnp.zeros((ROWS - 8, ROWS), F32)], axis=0).T
    a_masked = [jnp.where(mask_t, a_cols[:, h:h + 1], -jnp.inf) for h in range(M_HEADS)]
    cm = jnp.concatenate([jnp.max(am, axis=0, keepdims=True) for am in a_masked]
                         + [jnp.zeros((8 - M_HEADS, ROWS), F32)], axis=0)
    m_t = jnp.maximum(b + m0, b + cm)
    if last_sel is None:
        b_last = jnp.broadcast_to(b[:, ROWS - 1:ROWS], b.shape)
        m_new = jnp.broadcast_to(m_t[:, ROWS - 1:ROWS], b.shape)
    else:
        both = _exact_dot(jnp.where(head_rows, b, pltpu.roll(m_t, M_HEADS, 0)), last_sel)
        b_last = jnp.where(head_rows, both, 0.0)
        m_new = jnp.where(head_rows, pltpu.roll(both, M_HEADS, 0), 0.0)
    return dict(a_masked=a_masked, bm=b - m_t, gexp=jnp.exp(b + m0 - m_t), enm=jnp.exp(-m_t), m_new=m_new,
                w=jnp.exp(b_last - b + ic - m_new), decay=jnp.exp(b_last + m0 - m_new))


def _out_tail(cat, x, wout_ref, gpost_ref):
    y = jnp.dot(cat, wout_ref[...], preferred_element_type=F32)
    return x + _rms(y, gpost_ref[...])


W_BLOCK = 256
MAIN_BLOCKS = P_MAIN // W_BLOCK


def _main_src_block(i):
    return jnp.where(i < 2, i, jnp.where(i < 8, i + 1, jnp.where(i < 12, i + 3, 2)))


def _weights_kernel(main_ref, vm0_ref, vm1_ref, vm2_ref, vm3_ref, gates_ref, wout_ref, gmh_ref, bgate_ref,
                    wmain_ref, wt_ref, woutb_ref, gmhb_ref, bgb_ref):
    wmain_ref[...] = main_ref[...].T.astype(BF16)

    @pl.when(pl.program_id(0) == 0)
    def _():
        wt_ref[...] = jnp.concatenate(
            [vm0_ref[...], vm1_ref[...], vm2_ref[...], vm3_ref[...], gates_ref[...],
             jnp.zeros((T_ROWS - T_GATES - 8, D_MODEL), F32)], axis=0).astype(BF16)
        woutb_ref[...] = wout_ref[...].astype(BF16)
        for h in range(M_HEADS):
            gmhb_ref[h] = jnp.broadcast_to(gmh_ref[h:h + 1, :], (M_DIM, ROWS)).T
        bg_row = jnp.concatenate([bgate_ref[...], jnp.zeros((1, ROWS - 2 * M_HEADS), F32)], axis=1)
        bgb_ref[...] = jnp.broadcast_to(bg_row, (ROWS, ROWS)).T[0:2 * M_HEADS, :]


def _weights_call(w_in_t, w_out, gmh, bgate):
    blk = lambda r: pl.BlockSpec((128, D_MODEL), lambda i, r=r: (r, 0))
    const = lambda shape: pl.BlockSpec(shape, lambda i: (0,) * len(shape))
    return pl.pallas_call(
        _weights_kernel,
        grid=(MAIN_BLOCKS,),
        in_specs=[
            pl.BlockSpec((W_BLOCK, D_MODEL), lambda i: (_main_src_block(i), 0)),
            blk(VM // 128), blk(VM // 128 + 1), blk(VM // 128 + 2), blk(VM // 128 + 3),
            pl.BlockSpec((8, D_MODEL), lambda i: (GATES // 8, 0)),
            const((D_MODEL, D_MODEL)), const((M_HEADS, M_DIM)), const((1, 2 * M_HEADS)),
        ],
        out_specs=(
            pl.BlockSpec((D_MODEL, W_BLOCK), lambda i: (0, i)),
            const((T_ROWS, D_MODEL)), const((D_MODEL, D_MODEL)),
            const((M_HEADS, M_DIM, ROWS)), const((2 * M_HEADS, ROWS)),
        ),
        out_shape=(
            jax.ShapeDtypeStruct((D_MODEL, P_MAIN), BF16),
            jax.ShapeDtypeStruct((T_ROWS, D_MODEL), BF16),
            jax.ShapeDtypeStruct((D_MODEL, D_MODEL), BF16),
            jax.ShapeDtypeStruct((M_HEADS, M_DIM, ROWS), F32),
            jax.ShapeDtypeStruct((2 * M_HEADS, ROWS), F32),
        ),
        compiler_params=pltpu.CompilerParams(
            dimension_semantics=("arbitrary",), vmem_limit_bytes=VMEM_LIMIT_BYTES),
        name="layer_weights",
    )(w_in_t, w_in_t, w_in_t, w_in_t, w_in_t, w_in_t, w_out, gmh, bgate)


def _prompt_bias_t(kv, first):
    j = lax.broadcasted_iota(jnp.int32, (2 * ROWS, ROWS), 0)
    i = lax.broadcasted_iota(jnp.int32, (2 * ROWS, ROWS), 1)
    diff = ROWS + i - j
    valid = (diff >= 0) & (diff < ROWS)
    if first:
        valid = valid & (j >= ROWS)
    dfl = diff.astype(F32)
    return jnp.concatenate(
        [jnp.where(valid, -_slope(kv * ATT_GROUP + g) * dfl, NEG_BIG) * LOG2E for g in range(ATT_GROUP)], axis=1)


def _prompt_kernel(x_ref, xnext_ref, gpre_ref, wmain_ref, wt_ref, bg_ref, sink_ref, gmhb_ref, wout_ref, gpost_ref,
                   y_ref, wk_ref, wv_ref, c_ref, n_ref, m_ref,
                   proj_ref, projt_ref, cat_ref, xn_ref, kprev_ref, vtprev_ref, ct_ref, mst_ref, bias_ref,
                   *, tb, nt):
    bi = pl.program_id(0)
    j = pl.program_id(1)
    nchunks = tb // ROWS

    @pl.when((bi == 0) & (j == 0))
    def _():
        for first in range(2):
            for kv in range(ATT_KV):
                bias_ref[first * ATT_KV + kv] = _prompt_bias_t(kv, first)
        m_ref[...] = jnp.zeros_like(m_ref)

    @pl.when(j == 0)
    def _():
        kprev_ref[...] = jnp.zeros_like(kprev_ref)
        vtprev_ref[...] = jnp.zeros_like(vtprev_ref)
        ct_ref[...] = jnp.zeros_like(ct_ref)
        mst_ref[...] = jnp.zeros_like(mst_ref)

    sub = tb // PROMPT_SUBBLOCKS
    sub_chunks = sub // ROWS

    def norm_rows(sb):
        rows_sb = slice(sb * sub, (sb + 1) * sub)
        xn_ref[rows_sb, :] = _rms(x_ref[0, rows_sb, :], gpre_ref[...]).astype(BF16)

    def proj_cols(sb, c0, c1):
        rows_sb = slice(sb * sub, (sb + 1) * sub)
        proj_ref[rows_sb, c0:c1] = jnp.dot(xn_ref[rows_sb, :], wmain_ref[:, c0:c1], preferred_element_type=F32)

    def proj_t(sb):
        for r0, r1 in ((0, T_SPLIT), (T_SPLIT, T_ROWS)):
            pt = lax.dot_general(wt_ref[r0:r1, :], xn_ref[sb * sub:(sb + 1) * sub, :], NT_DIMS,
                                 preferred_element_type=F32)
            for c in range(sub_chunks):
                projt_ref[sb * sub_chunks + c, r0:r1, :] = pt[:, c * ROWS:(c + 1) * ROWS]

    def out_rows(sb):
        rows_sb = slice(sb * sub, (sb + 1) * sub)
        y_ref[0, rows_sb, :] = _out_tail(cat_ref[rows_sb, :], x_ref[0, rows_sb, :], wout_ref, gpost_ref)

    def proj_pieces(sb):
        bounds = list(range(0, P_MAIN, PROJ_COL_STEP)) + [P_MAIN]
        pieces = [functools.partial(proj_cols, sb, c0, c1) for c0, c1 in zip(bounds[:-1], bounds[1:])]
        return pieces[:2] + [functools.partial(proj_t, sb)] + pieces[2:]

    ri = lax.broadcasted_iota(jnp.int32, (ROWS, ROWS), 0)
    ci = lax.broadcasted_iota(jnp.int32, (ROWS, ROWS), 1)
    mask_t = ri <= ci
    st_row = lax.broadcasted_iota(jnp.int32, (STATE_ROWS - M_DIM, ROWS), 0)

    def chunk(c):
        rows = pl.ds(c * ROWS, ROWS)
        first = ((j == 0) & (c == 0)).astype(jnp.int32)
        yield

        qa = proj_ref[rows, P_QA:P_QA + 512] * (ATT_SCALE * LOG2E)
        kcur = proj_ref[rows, P_KA:P_KA + 128].astype(BF16)
        vtcur = proj_ref[rows, P_VA:P_VA + 128].T.astype(BF16)
        kcat = jnp.concatenate([kprev_ref[...], kcur], axis=0)
        vtcat = jnp.concatenate([vtprev_ref[...], vtcur], axis=1)
        r = _gate_rows(projt_ref[c, T_GATES:T_GATES + 8, :] + bg_ref[...], mst_ref[...], mask_t, None)

        scores, sinks = [], []
        for kv in range(ATT_KV):
            want_hi = kv == 1
            keep = (ci >= ATT_DIM) if want_hi else (ci < ATT_DIM)
            pieces = []
            for g in range(ATT_GROUP):
                hh = kv * ATT_GROUP + g
                blk = qa[:, (hh // 2) * 128:(hh // 2 + 1) * 128]
                if (hh % 2 == 1) != want_hi:
                    blk = pltpu.roll(blk, ATT_DIM, 1)
                pieces.append(jnp.where(keep, blk, 0.0))
            q4 = jnp.concatenate(pieces, axis=0)
            scores.append(_bdot_nt(kcat, q4) + bias_ref[first * ATT_KV + kv])
            sinks.append(jnp.concatenate(
                [jnp.broadcast_to(sink_ref[0:1, kv * ATT_GROUP + g:kv * ATT_GROUP + g + 1] * LOG2E, (1, ROWS))
                 for g in range(ATT_GROUP)], axis=1))
        yield
        vts, r1s = [], []
        for h in range(M_HEADS):
            q = proj_ref[rows, P_QM + h * M_DIM:P_QM + (h + 1) * M_DIM].astype(BF16)
            k = (proj_ref[rows, P_KM + h * M_DIM:P_KM + (h + 1) * M_DIM] * K_SCALE).astype(BF16)
            vt = projt_ref[c, T_VM + h * M_DIM:T_VM + (h + 1) * M_DIM, :]
            ct = ct_ref[h]
            r1s.append(lax.dot_general(jnp.concatenate([k, ct.astype(BF16)], axis=0), q, NT_DIMS,
                                       preferred_element_type=F32))
            w_row = r["w"][h:h + 1, :]
            lhs2 = jnp.concatenate([vt * w_row, jnp.where(st_row == 0, w_row, 0.0)], axis=0)
            ct_ref[h] = r["decay"][h:h + 1, 0:1] * ct + _bdot(lhs2, k)
            vts.append(vt)
        mst_ref[...] = r["m_new"]
        yield

        outs, nums, dens = [], [], []
        for kv in range(ATT_KV):
            s, sink = scores[kv], sinks[kv]
            mx = jnp.maximum(jnp.max(s, axis=0, keepdims=True), sink)
            p = jnp.exp2(s - mx).astype(BF16)
            lhs = jnp.concatenate([vtcat[kv * ATT_DIM:(kv + 1) * ATT_DIM, :],
                                   jnp.ones((ONES_ROWS, 2 * ROWS), BF16)], axis=0)
            o = jnp.dot(lhs, p, preferred_element_type=F32)
            outs.append((o, jnp.exp2(sink - mx)))
        yield
        for h in range(M_HEADS):
            r1 = r1s[h]
            dm = jnp.exp(r["a_masked"][h] + r["bm"][h:h + 1, :])
            st = r1[0:ROWS] * dm
            g_row = r["gexp"][h:h + 1, :]
            nums.append(_bdot(vts[h], st) + g_row * r1[ROWS:ROWS + M_DIM])
            dens.append(jnp.sum(st, axis=0, keepdims=True) + g_row * r1[ROWS + M_DIM:ROWS + M_DIM + 1])
        yield

        att = []
        for kv in range(ATT_KV):
            o, esink = outs[kv]
            on = o[0:ATT_DIM, :] * (1.0 / (o[ATT_DIM:ATT_DIM + 1, :] + esink))
            for pair in range(2):
                two = jnp.concatenate([on[:, (2 * pair) * ROWS:(2 * pair + 1) * ROWS],
                                       on[:, (2 * pair + 1) * ROWS:(2 * pair + 2) * ROWS]], axis=0)
                att.append(two.T)
        a_out = jnp.concatenate(att, axis=1) * _silu(proj_ref[rows, P_ZA:P_ZA + 512])
        kprev_ref[...] = kcur
        vtprev_ref[...] = vtcur
        yield
        m_out = []
        for h in range(M_HEADS):
            ht = nums[h] * (1.0 / jnp.maximum(jnp.abs(dens[h]), r["enm"][h:h + 1, :]))
            hn = ht * lax.rsqrt(jnp.mean(ht * ht, axis=0, keepdims=True) + NORM_EPS) * gmhb_ref[h]
            m_out.append(_sigmoid(proj_ref[rows, P_OM + h * M_DIM:P_OM + (h + 1) * M_DIM]) * hn.T
                         * _silu(proj_ref[rows, P_ZM + h * M_DIM:P_ZM + (h + 1) * M_DIM]))

        cat_ref[rows, :] = jnp.concatenate([a_out] + m_out, axis=1).astype(BF16)
        yield

    @pl.when((bi == 0) & (j == 0))
    def _():
        for sb in range(PROMPT_SUBBLOCKS):
            norm_rows(sb)
        for piece in proj_pieces(0):
            piece()

    def norm_next():
        xn_ref[...] = _rms(xnext_ref[0], gpre_ref[...]).astype(BF16)

    def run_chunks(sb, fillers):
        n_fill, n_slots, slot = len(fillers), sub_chunks * CHUNK_STAGES, 0
        for c in range(sb * sub_chunks, (sb + 1) * sub_chunks):
            for _ in chunk(c):
                for _ in range(-(-(slot + 1) * n_fill // n_slots) + (-slot * n_fill // n_slots)):
                    fillers.pop(0)()
                slot += 1
        assert slot == n_slots and not fillers

    assert PROMPT_SUBBLOCKS == 2
    run_chunks(0, proj_pieces(1))
    nxt = proj_pieces(0)
    run_chunks(1, [functools.partial(out_rows, 0), norm_next] + nxt[:-TAIL_PIECES])
    out_rows(1)
    for piece in nxt[-TAIL_PIECES:]:
        piece()

    @pl.when(j == nt - 1)
    def _():
        wk_ref[0] = proj_ref[tb - ROWS:tb, P_KA:P_KA + 128].T
        wv_ref[0] = proj_ref[tb - ROWS:tb, P_VA:P_VA + 128].T
        for h in range(M_HEADS):
            ct = ct_ref[h]
            c_ref[0, h] = ct[0:M_DIM].T
            n_ref[0, h:h + 1, :] = ct[M_DIM:M_DIM + 1]
        seq_lane = lax.broadcasted_iota(jnp.int32, m_ref.shape, 1) == bi
        m_ref[...] = jnp.where(seq_lane, mst_ref[0:M_HEADS, 0:m_ref.shape[1]], m_ref[...])


def _prompt_call(x, gpre, wmain, wt, bg, sinks, gmhb, wout, gpost, tb):
    bsz, seq, _ = x.shape
    nt = seq // tb
    full = lambda shape: pl.BlockSpec(shape, lambda b, j: (0,) * len(shape))

    def next_block(b, j):
        wrap = (j + 1 == nt).astype(jnp.int32)
        return (jnp.minimum(b + wrap, bsz - 1), (j + 1) * (1 - wrap), 0)

    out_shapes = (
        jax.ShapeDtypeStruct((bsz, seq, D_MODEL), F32),
        jax.ShapeDtypeStruct((bsz, 128, ROWS), F32),
        jax.ShapeDtypeStruct((bsz, 128, ROWS), F32),
        jax.ShapeDtypeStruct((bsz, M_HEADS, M_DIM, M_DIM), F32),
        jax.ShapeDtypeStruct((bsz, M_HEADS, M_DIM), F32),
        jax.ShapeDtypeStruct((M_HEADS, bsz), F32),
    )
    return pl.pallas_call(
        functools.partial(_prompt_kernel, tb=tb, nt=nt),
        grid=(bsz, nt),
        in_specs=[
            pl.BlockSpec((1, tb, D_MODEL), lambda b, j: (b, j, 0)),
            pl.BlockSpec((1, tb, D_MODEL), next_block),
            full((1, D_MODEL)), full((D_MODEL, P_MAIN)), full((T_ROWS, D_MODEL)), full((8, ROWS)),
            full((1, ATT_HEADS)), full((M_HEADS, M_DIM, ROWS)), full((D_MODEL, D_MODEL)), full((1, D_MODEL)),
        ],
        out_specs=(
            pl.BlockSpec((1, tb, D_MODEL), lambda b, j: (b, j, 0)),
            pl.BlockSpec((1, 128, ROWS), lambda b, j: (b, 0, 0)),
            pl.BlockSpec((1, 128, ROWS), lambda b, j: (b, 0, 0)),
            pl.BlockSpec((1, M_HEADS, M_DIM, M_DIM), lambda b, j: (b, 0, 0, 0)),
            pl.BlockSpec((1, M_HEADS, M_DIM), lambda b, j: (b, 0, 0)),
            pl.BlockSpec((M_HEADS, bsz), lambda b, j: (0, 0)),
        ),
        out_shape=out_shapes,
        scratch_shapes=[
            pltpu.VMEM((tb, P_MAIN), F32),
            pltpu.VMEM((tb // ROWS, T_ROWS, ROWS), F32),
            pltpu.VMEM((tb, D_MODEL), BF16),
            pltpu.VMEM((tb, D_MODEL), BF16),
            pltpu.VMEM((ROWS, 128), BF16),
            pltpu.VMEM((128, ROWS), BF16),
            pltpu.VMEM((M_HEADS, STATE_ROWS, M_DIM), F32),
            pltpu.VMEM((8, ROWS), F32),
            pltpu.VMEM((2 * ATT_KV, 2 * ROWS, ATT_GROUP * ROWS), F32),
        ],
        compiler_params=pltpu.CompilerParams(
            dimension_semantics=("arbitrary", "arbitrary"), vmem_limit_bytes=VMEM_LIMIT_BYTES),
        name="prompt_layer",
    )(x, x, gpre, wmain, wt, bg, sinks, gmhb, wout, gpost)


def _sample_bias_new(head):
    r = lax.broadcasted_iota(jnp.int32, (ROWS, ROWS), 0)
    c = lax.broadcasted_iota(jnp.int32, (ROWS, ROWS), 1)
    valid = ((r >> 3) == (c >> 3)) & (r >= c)
    return jnp.where(valid, -_slope(head) * (r - c).astype(F32), NEG_BIG)


def _sample_bias_cache(head):
    r = lax.broadcasted_iota(jnp.int32, (ROWS, ROWS), 0)
    c = lax.broadcasted_iota(jnp.int32, (ROWS, ROWS), 1)
    diff = (r & (SAMPLE_SEQ - 1)) + ROWS - c
    return jnp.where(diff < ROWS, -_slope(head) * diff.astype(F32), NEG_BIG)


def _sample_kernel(x_ref, kct_ref, vct_ref, cin_ref, nin_ref, m0_ref,
                   gpre_ref, wmain_ref, wt_ref, bg_ref, sink_ref, gmhb_ref, wout_ref, gpost_ref,
                   y_ref, kot_ref, vot_ref, cout_ref, nout_ref, mout_ref,
                   projfull_ref, projt_ref, cat_ref, qh_ref, sc_ref, oc_ref, qc_ref, wv_ref, kt_ref,
                   decb_ref, bn_ref, bc_ref):
    step, part = pl.program_id(0), pl.program_id(1)

    @pl.when((step == 0) & (part == 0))
    def _():
        for hh in range(ATT_HEADS):
            bn_ref[hh] = _sample_bias_new(hh)
            bc_ref[hh] = _sample_bias_cache(hh)
        mout_ref[...] = jnp.zeros_like(mout_ref)

    @pl.when(part == 0)
    def _():
        xn = _rms(x_ref[...], gpre_ref[...]).astype(BF16)
        projfull_ref[...] = jnp.dot(xn, wmain_ref[...], preferred_element_type=F32)
        ptf = lax.dot_general(wt_ref[...], xn, NT_DIMS, preferred_element_type=F32)
        for g in range(SAMPLE_PARTS):
            projt_ref[g] = ptf[:, g * ROWS:(g + 1) * ROWS]

    part_rows = pl.ds(pl.multiple_of(part * ROWS, ROWS), ROWS)
    proj_ref = projfull_ref.at[part_rows]
    pt = projt_ref[part]

    ri = lax.broadcasted_iota(jnp.int32, (ROWS, ROWS), 0)
    ci = lax.broadcasted_iota(jnp.int32, (ROWS, ROWS), 1)
    same_seq = (ri >> 3) == (ci >> 3)
    mask_t = same_seq & (ri <= ci)
    last_sel = (same_seq & ((ri & (SAMPLE_SEQ - 1)) == SAMPLE_SEQ - 1)).astype(BF16)
    row16 = lax.broadcasted_iota(jnp.int32, (SAMPLE_GROUP, ROWS), 0)
    lane16 = lax.broadcasted_iota(jnp.int32, (SAMPLE_GROUP, ROWS), 1)
    seq_of_lane = (lane16 >> 3) == row16
    ones_rows = jnp.ones((ONES_ROWS, ROWS), F32)

    qa = proj_ref[:, P_QA:P_QA + 512] * ATT_SCALE
    for hh in range(ATT_HEADS):
        blk = qa[:, (hh // 2) * 128:(hh // 2 + 1) * 128]
        if hh % 2 == 1:
            blk = pltpu.roll(blk, ATT_DIM, 1)
        qh_ref[hh] = blk[:, 0:ATT_DIM]
    ka = proj_ref[:, P_KA:P_KA + 128]
    kat = ka.T
    vat = proj_ref[:, P_VA:P_VA + 128].T

    keep_new = ci >= ROWS - SAMPLE_SEQ

    for b in range(SAMPLE_GROUP):
        rows = pl.ds(b * SAMPLE_SEQ, SAMPLE_SEQ)
        kct = kct_ref[b]
        shift = (ROWS - SAMPLE_SEQ - b * SAMPLE_SEQ) % ROWS
        new_k, new_v = (kat, vat) if shift == 0 else (pltpu.roll(kat, shift, 1), pltpu.roll(vat, shift, 1))
        kot_ref[b] = jnp.where(keep_new, new_k, pltpu.roll(kct, ROWS - SAMPLE_SEQ, 1))
        vot_ref[b] = jnp.where(keep_new, new_v, pltpu.roll(vct_ref[b], ROWS - SAMPLE_SEQ, 1))
        for kv in range(ATT_KV):
            lhs = jnp.concatenate([qh_ref[kv * ATT_GROUP + g, rows, :] for g in range(ATT_GROUP)], axis=0)
            res = _bdot(lhs, kct[kv * ATT_DIM:(kv + 1) * ATT_DIM, :])
            for g in range(ATT_GROUP):
                sc_ref[kv * ATT_GROUP + g, rows, :] = res[g * SAMPLE_SEQ:(g + 1) * SAMPLE_SEQ]
        for h in range(M_HEADS):
            qc_ref[h, rows, :] = _bdot(proj_ref[rows, P_QM + h * M_DIM:P_QM + (h + 1) * M_DIM], cin_ref[b, h])

    o_new, esinks = [], []
    for hh in range(ATT_HEADS):
        kv = hh // ATT_GROUP
        s_n = _bdot(qh_ref[hh], kat[kv * ATT_DIM:(kv + 1) * ATT_DIM, :]) + bn_ref[hh]
        s_c = sc_ref[hh] + bc_ref[hh]
        sink = sink_ref[0:1, hh:hh + 1]
        mx = jnp.maximum(jnp.max(jnp.maximum(s_n, s_c), axis=-1, keepdims=True), sink)
        mx_b = jnp.broadcast_to(mx, (ROWS, ROWS))
        p_n = jnp.exp(s_n - mx_b)
        sc_ref[hh] = jnp.exp(s_c - mx_b)
        vaug = jnp.concatenate([vat[kv * ATT_DIM:(kv + 1) * ATT_DIM, :], ones_rows], axis=0)
        o_new.append(_bdot_nt(p_n, vaug))
        esinks.append(jnp.exp(sink - mx))

    first_seq = (step * SAMPLE_PARTS + part) * SAMPLE_GROUP
    seq_step = ri == first_seq + (ci >> 3)
    m0 = _exact_dot(jnp.concatenate([m0_ref[...], jnp.zeros((8 - M_HEADS, ROWS), F32)], axis=0),
                    seq_step.astype(BF16))
    r = _gate_rows(pt[T_GATES:T_GATES + 8, :] + bg_ref[...], m0, mask_t, last_sel)
    step_seq = (ci == first_seq + (ri >> 3)) & ((ri & (SAMPLE_SEQ - 1)) == 0)
    mout_ref[...] += _exact_dot(r["m_new"], step_seq.astype(BF16))[0:M_HEADS]
    m_out = []
    for h in range(M_HEADS):
        q = proj_ref[:, P_QM + h * M_DIM:P_QM + (h + 1) * M_DIM].astype(BF16)
        kf = proj_ref[:, P_KM + h * M_DIM:P_KM + (h + 1) * M_DIM] * K_SCALE
        k = kf.astype(BF16)
        vt = pt[T_VM + h * M_DIM:T_VM + (h + 1) * M_DIM, :]
        n_h = nin_ref[h]
        r1 = lax.dot_general(jnp.concatenate([k, n_h.astype(BF16)], axis=0), q, NT_DIMS,
                             preferred_element_type=F32)
        st = r1[0:ROWS] * jnp.exp(r["a_masked"][h] + r["bm"][h:h + 1, :])
        q_n = jnp.sum(jnp.where(seq_of_lane, r1[ROWS:ROWS + SAMPLE_GROUP], 0.0), axis=0, keepdims=True)
        g_row = r["gexp"][h:h + 1, :]
        num = _bdot(vt, st) + g_row * qc_ref[h].T
        den = jnp.sum(st, axis=0, keepdims=True) + g_row * q_n
        ht = num * (1.0 / jnp.maximum(jnp.abs(den), r["enm"][h:h + 1, :]))
        hn = ht * lax.rsqrt(jnp.mean(ht * ht, axis=0, keepdims=True) + NORM_EPS) * gmhb_ref[h]
        m_out.append(_sigmoid(proj_ref[:, P_OM + h * M_DIM:P_OM + (h + 1) * M_DIM]) * hn.T
                     * _silu(proj_ref[:, P_ZM + h * M_DIM:P_ZM + (h + 1) * M_DIM]))
        w_row = r["w"][h:h + 1, :]
        dec16 = jnp.sum(jnp.where(lane16 == row16 * SAMPLE_SEQ, r["decay"][h:h + 1, :], 0.0),
                        axis=1, keepdims=True)
        nout_ref[h] = dec16 * n_h + _bdot(jnp.where(seq_of_lane, w_row, 0.0), k)
        decb_ref[h] = jnp.broadcast_to(dec16, (SAMPLE_GROUP, ROWS))
        wv_ref[h] = (vt * w_row).T
        kt_ref[h] = kf.T.astype(BF16)

    for b in range(SAMPLE_GROUP):
        rows = pl.ds(b * SAMPLE_SEQ, SAMPLE_SEQ)
        vct = vct_ref[b]
        for kv in range(ATT_KV):
            vaug = jnp.concatenate([vct[kv * ATT_DIM:(kv + 1) * ATT_DIM, :], ones_rows], axis=0)
            pl_ = jnp.concatenate([sc_ref[kv * ATT_GROUP + g, rows, :] for g in range(ATT_GROUP)], axis=0)
            res = _bdot_nt(pl_, vaug)
            for g in range(ATT_GROUP):
                oc_ref[kv * ATT_GROUP + g, rows, 0:ATT_DIM + ONES_ROWS] = res[g * SAMPLE_SEQ:(g + 1) * SAMPLE_SEQ]
        in_seq = (ri >> 3) == b
        for h in range(M_HEADS):
            upd = jnp.dot(kt_ref[h], jnp.where(in_seq, wv_ref[h], 0.0).astype(BF16), preferred_element_type=F32)
            cout_ref[b, h] = decb_ref[h, b:b + 1, :] * cin_ref[b, h] + upd

    att = []
    for pair in range(ATT_HEADS // 2):
        halves = []
        for hh in (2 * pair, 2 * pair + 1):
            on, oc = o_new[hh], oc_ref[hh]
            den = on[:, ATT_DIM:ATT_DIM + 1] + oc[:, ATT_DIM:ATT_DIM + 1] + esinks[hh]
            halves.append((on[:, 0:ATT_DIM] + oc[:, 0:ATT_DIM]) * (1.0 / den))
        att.append(jnp.concatenate(halves, axis=1))
    a_out = jnp.concatenate(att, axis=1) * _silu(proj_ref[:, P_ZA:P_ZA + 512])
    cat_ref[part_rows, :] = jnp.concatenate([a_out] + m_out, axis=1).astype(BF16)

    @pl.when(part == SAMPLE_PARTS - 1)
    def _():
        y_ref[...] = _out_tail(cat_ref[...], x_ref[...], wout_ref, gpost_ref)


def _sample_call(x, kct, vct, cin, nin, m0, gpre, wmain, wt, bg, sinks, gmhb, wout, gpost):
    nrows = x.shape[0]
    ngroups = nrows // ROWS
    nseq = ngroups * SAMPLE_GROUP
    parts = SAMPLE_PARTS
    assert ngroups % parts == 0
    assert nseq == ROWS, "the per-sequence stabiliser state is handled as one 128-lane row per head"
    full = lambda shape: pl.BlockSpec(shape, lambda i, p: (0,) * len(shape))
    grp = SAMPLE_GROUP
    group = lambda i, p: i * parts + p
    out_shapes = (
        jax.ShapeDtypeStruct((nrows, D_MODEL), F32),
        jax.ShapeDtypeStruct((nseq, 128, ROWS), F32),
        jax.ShapeDtypeStruct((nseq, 128, ROWS), F32),
        jax.ShapeDtypeStruct((nseq, M_HEADS, M_DIM, M_DIM), F32),
        jax.ShapeDtypeStruct((M_HEADS, nseq, M_DIM), F32),
        jax.ShapeDtypeStruct((M_HEADS, nseq), F32),
    )
    return pl.pallas_call(
        _sample_kernel,
        grid=(ngroups // parts, parts),
        in_specs=[
            pl.BlockSpec((parts * ROWS, D_MODEL), lambda i, p: (i, 0)),
            pl.BlockSpec((grp, 128, ROWS), lambda i, p: (group(i, p), 0, 0)),
            pl.BlockSpec((grp, 128, ROWS), lambda i, p: (group(i, p), 0, 0)),
            pl.BlockSpec((grp, M_HEADS, M_DIM, M_DIM), lambda i, p: (group(i, p), 0, 0, 0)),
            pl.BlockSpec((M_HEADS, grp, M_DIM), lambda i, p: (0, group(i, p), 0)),
            full((M_HEADS, nseq)),
            full((1, D_MODEL)), full((D_MODEL, P_MAIN)), full((T_ROWS, D_MODEL)), full((8, ROWS)),
            full((1, ATT_HEADS)), full((M_HEADS, M_DIM, ROWS)), full((D_MODEL, D_MODEL)), full((1, D_MODEL)),
        ],
        out_specs=(
            pl.BlockSpec((parts * ROWS, D_MODEL), lambda i, p: (i, 0)),
            pl.BlockSpec((grp, 128, ROWS), lambda i, p: (group(i, p), 0, 0)),
            pl.BlockSpec((grp, 128, ROWS), lambda i, p: (group(i, p), 0, 0)),
            pl.BlockSpec((grp, M_HEADS, M_DIM, M_DIM), lambda i, p: (group(i, p), 0, 0, 0)),
            pl.BlockSpec((M_HEADS, grp, M_DIM), lambda i, p: (0, group(i, p), 0)),
            full((M_HEADS, nseq)),
        ),
        out_shape=out_shapes,
        scratch_shapes=[
            pltpu.VMEM((parts * ROWS, P_MAIN), F32),
            pltpu.VMEM((parts, T_ROWS, ROWS), F32),
            pltpu.VMEM((parts * ROWS, D_MODEL), BF16),
            pltpu.VMEM((ATT_HEADS, ROWS, ATT_DIM), F32),
            pltpu.VMEM((ATT_HEADS, ROWS, ROWS), F32),
            pltpu.VMEM((ATT_HEADS, ROWS, ROWS), F32),
            pltpu.VMEM((M_HEADS, ROWS, M_DIM), F32),
            pltpu.VMEM((M_HEADS, ROWS, M_DIM), F32),
            pltpu.VMEM((M_HEADS, M_DIM, ROWS), BF16),
            pltpu.VMEM((M_HEADS, SAMPLE_GROUP, ROWS), F32),
            pltpu.VMEM((ATT_HEADS, ROWS, ROWS), F32),
            pltpu.VMEM((ATT_HEADS, ROWS, ROWS), F32),
        ],
        compiler_params=pltpu.CompilerParams(
            dimension_semantics=("arbitrary", "arbitrary"), vmem_limit_bytes=VMEM_LIMIT_BYTES),
        name="sample_layer",
    )(x, kct, vct, cin, nin, m0, gpre, wmain, wt, bg, sinks, gmhb, wout, gpost)


PROMPT_BLOCK = 512
PROMPT_SUBBLOCKS = 2
PROJ_COL_STEP = 512
CHUNK_STAGES = 7
TAIL_PIECES = 3


def _window_in(cache):
    nseq = cache.shape[0]
    return cache.transpose(0, 2, 3, 1).reshape(nseq, ATT_KV * ATT_DIM, ROWS)


def _window_out(win_t):
    nseq = win_t.shape[0]
    return win_t.reshape(nseq, ATT_KV, ATT_DIM, ROWS).transpose(0, 3, 1, 2)[None]


def kernel(x_prompt, x_sample, cache_win_k, cache_win_v, state_C, state_n, state_m,
           g_pre, w_in, b_gate, attn_sinks, g_mh, w_out, g_post):
    depth = g_pre.shape[0]
    assert depth == 1, "single-layer trunk"
    nseq, sseq, _ = x_sample.shape
    assert sseq == SAMPLE_SEQ and nseq % SAMPLE_GROUP == 0

    gpre = g_pre[0].reshape(1, D_MODEL)
    gpost = g_post[0].reshape(1, D_MODEL)
    wmain, wt, wout, gmhb, bg = _weights_call(w_in[0].T, w_out[0], g_mh[0], b_gate)
    sinks = attn_sinks[0].reshape(1, ATT_HEADS)

    yp, wkp, wvp, cp, np_, mp = _prompt_call(x_prompt, gpre, wmain, wt, bg, sinks, gmhb, wout, gpost, PROMPT_BLOCK)

    ys, wks, wvs, cs, ns, ms = _sample_call(
        x_sample.reshape(nseq * sseq, D_MODEL), _window_in(cache_win_k[0]), _window_in(cache_win_v[0]),
        state_C[0], state_n[0].transpose(1, 0, 2), state_m[0].T,
        gpre, wmain, wt, bg, sinks, gmhb, wout, gpost)

    return (yp, ys.reshape(nseq, sseq, D_MODEL), _window_out(wkp), _window_out(wvp),
            cp[None], np_[None], mp.T[None],
            _window_out(wks), _window_out(wvs), cs[None], ns.transpose(1, 0, 2)[None], ms.T[None])
```

```python
import functools

import jax
import jax.numpy as jnp
from jax import lax
from jax.experimental import pallas as pl
from jax.experimental.pallas import tpu as pltpu

F32 = jnp.float32
BF16 = jnp.bfloat16

D_MODEL = 1024
ROWS = 128
ATT_HEADS, ATT_KV, ATT_GROUP, ATT_DIM = 8, 2, 4, 64
M_HEADS, M_DIM = 4, 128
NORM_EPS = 1e-6
NEG_BIG = -1e30
ATT_SCALE = ATT_DIM ** -0.5
LOG2E = 1.4426950408889634
K_SCALE = M_DIM ** -0.5

VM, GATES = 2304, 3840

P_QA, P_ZA, P_QM, P_KM, P_OM, P_ZM, P_KA, P_VA = 0, 512, 1024, 1536, 2048, 2560, 3072, 3200
P_MAIN = 3328
T_VM, T_GATES = 0, 512
T_ROWS = 528
T_SPLIT = 272
STATE_ROWS = 144
ONES_ROWS = 16

SAMPLE_SEQ = 8
SAMPLE_GROUP = ROWS // SAMPLE_SEQ
SAMPLE_PARTS = 2

VMEM_BYTES_V7X = 64 * 1024 * 1024
VMEM_LIMIT_BYTES = VMEM_BYTES_V7X * 7 // 8
NT_DIMS = (((1,), (1,)), ((), ()))


def _rms(x, g):
    return x * lax.rsqrt(jnp.mean(x * x, axis=-1, keepdims=True) + NORM_EPS) * g


def _sigmoid(x):
    return 0.5 + 0.5 * jnp.tanh(0.5 * x)


def _silu(x):
    h = 0.5 * x
    return h + h * jnp.tanh(h)


def _log_sigmoid(x):
    return -(jnp.maximum(-x, 0.0) + jnp.log1p(jnp.exp(-jnp.abs(x))))


def _slope(head):
    return 2.0 ** -(head + 1)


def _bdot(a, b):
    return jnp.dot(a.astype(BF16), b.astype(BF16), preferred_element_type=F32)


def _bdot_nt(a, b):
    return lax.dot_general(a.astype(BF16), b.astype(BF16), NT_DIMS, preferred_element_type=F32)


def _exact_dot(x, m):
    hi = x.astype(BF16).astype(F32)
    mid = (x - hi).astype(BF16).astype(F32)
    lo = (x - hi - mid).astype(BF16).astype(F32)
    parts = jnp.dot(jnp.concatenate([hi, mid, lo, jnp.zeros_like(hi)], axis=0).astype(BF16), m,
                    preferred_element_type=F32)
    return parts[0:8] + parts[8:16] + parts[16:24]


def _gate_rows(x, m0, mask_t, last_sel):
    row = lax.broadcasted_iota(jnp.int32, (8, ROWS), 0)
    head_rows = row < M_HEADS
    ic = jnp.where(head_rows, x, 0.0)
    fc = jnp.where(head_rows, _log_sigmoid(pltpu.roll(x, M_HEADS, 0)), 0.0)
    b = _exact_dot(fc, mask_t.astype(BF16))
    a = ic - b
    a_cols = jnp.concatenate([a, jnp.zeros((ROWS - 8, ROWS), F32)], axis=0).T
    a_masked = [jnp.where(mask_t, a_cols[:, h:h + 1], -jnp.inf) for h in range(M_HEADS)]
    cm = jnp.concatenate([jnp.max(am, axis=0, keepdims=True) for am in a_masked]
                         + [jnp.zeros((8 - M_HEADS, ROWS), F32)], axis=0)
    m_t = jnp.maximum(b + m0, b + cm)
    if last_sel is None:
        b_last = jnp.broadcast_to(b[:, ROWS - 1:ROWS], b.shape)
        m_new = jnp.broadcast_to(m_t[:, ROWS - 1:ROWS], b.shape)
    else:
        both = _exact_dot(jnp.where(head_rows, b, pltpu.roll(m_t, M_HEADS, 0)), last_sel)
        b_last = jnp.where(head_rows, both, 0.0)
        m_new = jnp.where(head_rows, pltpu.roll(both, M_HEADS, 0), 0.0)
    return dict(a_masked=a_masked, bm=b - m_t, gexp=jnp.exp(b + m0 - m_t), enm=jnp.exp(-m_t), m_new=m_new,
                w=jnp.exp(b_last - b + ic - m_new), decay=jnp.exp(b_last + m0 - m_new))


def _out_tail(cat, x, wout_ref, gpost_ref):
    y = jnp.dot(cat, wout_ref[...], preferred_element_type=F32)
    return x + _rms(y, gpost_ref[...])


W_BLOCK = 256
MAIN_BLOCKS = P_MAIN // W_BLOCK


def _main_src_block(i):
    return jnp.where(i < 2, i, jnp.where(i < 8, i + 1, jnp.where(i < 12, i + 3, 2)))


def _weights_kernel(main_ref, vm0_ref, vm1_ref, vm2_ref, vm3_ref, gates_ref, wout_ref, gmh_ref, bgate_ref,
                    wmain_ref, wt_ref, woutb_ref, gmhb_ref, bgb_ref):
    wmain_ref[...] = main_ref[...].T.astype(BF16)

    @pl.when(pl.program_id(0) == 0)
    def _():
        wt_ref[...] = jnp.concatenate(
            [vm0_ref[...], vm1_ref[...], vm2_ref[...], vm3_ref[...], gates_ref[...],
             jnp.zeros((T_ROWS - T_GATES - 8, D_MODEL), F32)], axis=0).astype(BF16)
        woutb_ref[...] = wout_ref[...].astype(BF16)
        for h in range(M_HEADS):
            gmhb_ref[h] = jnp.broadcast_to(gmh_ref[h:h + 1, :], (M_DIM, ROWS)).T
        bg_row = jnp.concatenate([bgate_ref[...], jnp.zeros((1, ROWS - 2 * M_HEADS), F32)], axis=1)
        bgb_ref[...] = jnp.broadcast_to(bg_row, (ROWS, ROWS)).T[0:2 * M_HEADS, :]


def _weights_call(w_in_t, w_out, gmh, bgate):
    blk = lambda r: pl.BlockSpec((128, D_MODEL), lambda i, r=r: (r, 0))
    const = lambda shape: pl.BlockSpec(shape, lambda i: (0,) * len(shape))
    return pl.pallas_call(
        _weights_kernel,
        grid=(MAIN_BLOCKS,),
        in_specs=[
            pl.BlockSpec((W_BLOCK, D_MODEL), lambda i: (_main_src_block(i), 0)),
            blk(VM // 128), blk(VM // 128 + 1), blk(VM // 128 + 2), blk(VM // 128 + 3),
            pl.BlockSpec((8, D_MODEL), lambda i: (GATES // 8, 0)),
            const((D_MODEL, D_MODEL)), const((M_HEADS, M_DIM)), const((1, 2 * M_HEADS)),
        ],
        out_specs=(
            pl.BlockSpec((D_MODEL, W_BLOCK), lambda i: (0, i)),
            const((T_ROWS, D_MODEL)), const((D_MODEL, D_MODEL)),
            const((M_HEADS, M_DIM, ROWS)), const((2 * M_HEADS, ROWS)),
        ),
        out_shape=(
            jax.ShapeDtypeStruct((D_MODEL, P_MAIN), BF16),
            jax.ShapeDtypeStruct((T_ROWS, D_MODEL), BF16),
            jax.ShapeDtypeStruct((D_MODEL, D_MODEL), BF16),
            jax.ShapeDtypeStruct((M_HEADS, M_DIM, ROWS), F32),
            jax.ShapeDtypeStruct((2 * M_HEADS, ROWS), F32),
        ),
        compiler_params=pltpu.CompilerParams(
            dimension_semantics=("arbitrary",), vmem_limit_bytes=VMEM_LIMIT_BYTES),
        name="layer_weights",
    )(w_in_t, w_in_t, w_in_t, w_in_t, w_in_t, w_in_t, w_out, gmh, bgate)


def _prompt_bias_t(kv, first):
    j = lax.broadcasted_iota(jnp.int32, (2 * ROWS, ROWS), 0)
    i = lax.broadcasted_iota(jnp.int32, (2 * ROWS, ROWS), 1)
    diff = ROWS + i - j
    valid = (diff >= 0) & (diff < ROWS)
    if first:
        valid = valid & (j >= ROWS)
    dfl = diff.astype(F32)
    return jnp.concatenate(
        [jnp.where(valid, -_slope(kv * ATT_GROUP + g) * dfl, NEG_BIG) * LOG2E for g in range(ATT_GROUP)], axis=1)


def _prompt_kernel(x_ref, xnext_ref, gpre_ref, wmain_ref, wt_ref, bg_ref, sink_ref, gmhb_ref, wout_ref, gpost_ref,
                   y_ref, wk_ref, wv_ref, c_ref, n_ref, m_ref,
                   proj_ref, projt_ref, cat_ref, xn_ref, kprev_ref, vtprev_ref, ct_ref, mst_ref, bias_ref,
                   *, tb, nt):
    bi = pl.program_id(0)
    j = pl.program_id(1)
    nchunks = tb // ROWS

    @pl.when((bi == 0) & (j == 0))
    def _():
        for first in range(2):
            for kv in range(ATT_KV):
                bias_ref[first * ATT_KV + kv] = _prompt_bias_t(kv, first)
        m_ref[...] = jnp.zeros_like(m_ref)

    @pl.when(j == 0)
    def _():
        kprev_ref[...] = jnp.zeros_like(kprev_ref)
        vtprev_ref[...] = jnp.zeros_like(vtprev_ref)
        ct_ref[...] = jnp.zeros_like(ct_ref)
        mst_ref[...] = jnp.zeros_like(mst_ref)

    sub = tb // PROMPT_SUBBLOCKS
    sub_chunks = sub // ROWS

    def norm_rows(sb):
        rows_sb = slice(sb * sub, (sb + 1) * sub)
        xn_ref[rows_sb, :] = _rms(x_ref[0, rows_sb, :], gpre_ref[...]).astype(BF16)

    def proj_cols(sb, c0, c1):
        rows_sb = slice(sb * sub, (sb + 1) * sub)
        proj_ref[rows_sb, c0:c1] = jnp.dot(xn_ref[rows_sb, :], wmain_ref[:, c0:c1], preferred_element_type=F32)

    def proj_t(sb):
        for r0, r1 in ((0, T_SPLIT), (T_SPLIT, T_ROWS)):
            pt = lax.dot_general(wt_ref[r0:r1, :], xn_ref[sb * sub:(sb + 1) * sub, :], NT_DIMS,
                                 preferred_element_type=F32)
            for c in range(sub_chunks):
                projt_ref[sb * sub_chunks + c, r0:r1, :] = pt[:, c * ROWS:(c + 1) * ROWS]

    def out_rows(sb):
        rows_sb = slice(sb * sub, (sb + 1) * sub)
        y_ref[0, rows_sb, :] = _out_tail(cat_ref[rows_sb, :], x_ref[0, rows_sb, :], wout_ref, gpost_ref)

    def proj_pieces(sb):
        bounds = list(range(0, P_MAIN, PROJ_COL_STEP)) + [P_MAIN]
        pieces = [functools.partial(proj_cols, sb, c0, c1) for c0, c1 in zip(bounds[:-1], bounds[1:])]
        return pieces[:2] + [functools.partial(proj_t, sb)] + pieces[2:]

    ri = lax.broadcasted_iota(jnp.int32, (ROWS, ROWS), 0)
    ci = lax.broadcasted_iota(jnp.int32, (ROWS, ROWS), 1)
    mask_t = ri <= ci
    st_row = lax.broadcasted_iota(jnp.int32, (STATE_ROWS - M_DIM, ROWS), 0)

    def chunk(c):
        rows = pl.ds(c * ROWS, ROWS)
        first = ((j == 0) & (c == 0)).astype(jnp.int32)
        yield

        qa = proj_ref[rows, P_QA:P_QA + 512] * (ATT_SCALE * LOG2E)
        kcur = proj_ref[rows, P_KA:P_KA + 128].astype(BF16)
        vtcur = proj_ref[rows, P_VA:P_VA + 128].T.astype(BF16)
        kcat = jnp.concatenate([kprev_ref[...], kcur], axis=0)
        vtcat = jnp.concatenate([vtprev_ref[...], vtcur], axis=1)
        r = _gate_rows(projt_ref[c, T_GATES:T_GATES + 8, :] + bg_ref[...], mst_ref[...], mask_t, None)

        scores, sinks = [], []
        for kv in range(ATT_KV):
            want_hi = kv == 1
            keep = (ci >= ATT_DIM) if want_hi else (ci < ATT_DIM)
            pieces = []
            for g in range(ATT_GROUP):
                hh = kv * ATT_GROUP + g
                blk = qa[:, (hh // 2) * 128:(hh // 2 + 1) * 128]
                if (hh % 2 == 1) != want_hi:
                    blk = pltpu.roll(blk, ATT_DIM, 1)
                pieces.append(jnp.where(keep, blk, 0.0))
            q4 = jnp.concatenate(pieces, axis=0)
            scores.append(_bdot_nt(kcat, q4) + bias_ref[first * ATT_KV + kv])
            sinks.append(jnp.concatenate(
                [jnp.broadcast_to(sink_ref[0:1, kv * ATT_GROUP + g:kv * ATT_GROUP + g + 1] * LOG2E, (1, ROWS))
                 for g in range(ATT_GROUP)], axis=1))
        yield
        g_cols = jnp.concatenate([r["gexp"], jnp.zeros((ROWS - 8, ROWS), F32)], axis=0).T
        lhs_nums, qgts, sts = [], [], []
        for h in range(M_HEADS):
            qf = proj_ref[rows, P_QM + h * M_DIM:P_QM + (h + 1) * M_DIM]
            q = qf.astype(BF16)
            k = (proj_ref[rows, P_KM + h * M_DIM:P_KM + (h + 1) * M_DIM] * K_SCALE).astype(BF16)
            vt = projt_ref[c, T_VM + h * M_DIM:T_VM + (h + 1) * M_DIM, :]
            ct = ct_ref[h]
            sts.append(_bdot_nt(k, q))
            qgts.append((qf * g_cols[:, h:h + 1]).T.astype(BF16))
            lhs_nums.append(jnp.concatenate(
                [jnp.concatenate([vt, ct[0:M_DIM]], axis=1),
                 jnp.concatenate([jnp.zeros((STATE_ROWS - M_DIM, ROWS), F32), ct[M_DIM:STATE_ROWS]], axis=1)],
                axis=0).astype(BF16))
            w_row = r["w"][h:h + 1, :]
            lhs2 = jnp.concatenate([vt * w_row, jnp.where(st_row == 0, w_row, 0.0)], axis=0)
            ct_ref[h] = r["decay"][h:h + 1, 0:1] * ct + _bdot(lhs2, k)
        mst_ref[...] = r["m_new"]
        yield

        outs, nums, dens = [], [], []
        for kv in range(ATT_KV):
            s, sink = scores[kv], sinks[kv]
            mx = jnp.maximum(jnp.max(s, axis=0, keepdims=True), sink)
            p = jnp.exp2(s - mx).astype(BF16)
            lhs = jnp.concatenate([vtcat[kv * ATT_DIM:(kv + 1) * ATT_DIM, :],
                                   jnp.ones((ONES_ROWS, 2 * ROWS), BF16)], axis=0)
            o = jnp.dot(lhs, p, preferred_element_type=F32)
            outs.append((o, jnp.exp2(sink - mx)))
        yield
        for h in range(M_HEADS):
            st = sts[h] * jnp.exp(r["a_masked"][h] + r["bm"][h:h + 1, :])
            res = jnp.dot(lhs_nums[h], jnp.concatenate([st.astype(BF16), qgts[h]], axis=0),
                          preferred_element_type=F32)
            nums.append(res[0:M_DIM])
            dens.append(jnp.sum(st, axis=0, keepdims=True) + res[M_DIM:M_DIM + 1])
        yield

        att = []
        for kv in range(ATT_KV):
            o, esink = outs[kv]
            on = o[0:ATT_DIM, :] * (1.0 / (o[ATT_DIM:ATT_DIM + 1, :] + esink))
            for pair in range(2):
                two = jnp.concatenate([on[:, (2 * pair) * ROWS:(2 * pair + 1) * ROWS],
                                       on[:, (2 * pair + 1) * ROWS:(2 * pair + 2) * ROWS]], axis=0)
                att.append(two.T)
        a_out = jnp.concatenate(att, axis=1) * _silu(proj_ref[rows, P_ZA:P_ZA + 512])
        kprev_ref[...] = kcur
        vtprev_ref[...] = vtcur
        yield
        m_out = []
        for h in range(M_HEADS):
            ht = nums[h] * (1.0 / jnp.maximum(jnp.abs(dens[h]), r["enm"][h:h + 1, :]))
            hn = ht * lax.rsqrt(jnp.mean(ht * ht, axis=0, keepdims=True) + NORM_EPS) * gmhb_ref[h]
            m_out.append(_sigmoid(proj_ref[rows, P_OM + h * M_DIM:P_OM + (h + 1) * M_DIM]) * hn.T
                         * _silu(proj_ref[rows, P_ZM + h * M_DIM:P_ZM + (h + 1) * M_DIM]))

        cat_ref[rows, :] = jnp.concatenate([a_out] + m_out, axis=1).astype(BF16)
        yield

    @pl.when((bi == 0) & (j == 0))
    def _():
        for sb in range(PROMPT_SUBBLOCKS):
            norm_rows(sb)
        for piece in proj_pieces(0):
            piece()

    def norm_next():
        xn_ref[...] = _rms(xnext_ref[0], gpre_ref[...]).astype(BF16)

    def run_chunks(sb, fillers):
        n_fill, n_slots, slot = len(fillers), sub_chunks * CHUNK_STAGES, 0
        for c in range(sb * sub_chunks, (sb + 1) * sub_chunks):
            for _ in chunk(c):
                for _ in range(-(-(slot + 1) * n_fill // n_slots) + (-slot * n_fill // n_slots)):
                    fillers.pop(0)()
                slot += 1
        assert slot == n_slots and not fillers

    assert PROMPT_SUBBLOCKS == 2
    run_chunks(0, proj_pieces(1))
    nxt = proj_pieces(0)
    run_chunks(1, [functools.partial(out_rows, 0), norm_next] + nxt[:-TAIL_PIECES])
    out_rows(1)
    for piece in nxt[-TAIL_PIECES:]:
        piece()

    @pl.when(j == nt - 1)
    def _():
        wk_ref[0] = proj_ref[tb - ROWS:tb, P_KA:P_KA + 128].T
        wv_ref[0] = proj_ref[tb - ROWS:tb, P_VA:P_VA + 128].T
        for h in range(M_HEADS):
            ct = ct_ref[h]
            c_ref[0, h] = ct[0:M_DIM].T
            n_ref[0, h:h + 1, :] = ct[M_DIM:M_DIM + 1]
        seq_lane = lax.broadcasted_iota(jnp.int32, m_ref.shape, 1) == bi
        m_ref[...] = jnp.where(seq_lane, mst_ref[0:M_HEADS, 0:m_ref.shape[1]], m_ref[...])


def _prompt_call(x, gpre, wmain, wt, bg, sinks, gmhb, wout, gpost, tb):
    bsz, seq, _ = x.shape
    nt = seq // tb
    full = lambda shape: pl.BlockSpec(shape, lambda b, j: (0,) * len(shape))

    def next_block(b, j):
        wrap = (j + 1 == nt).astype(jnp.int32)
        return (jnp.minimum(b + wrap, bsz - 1), (j + 1) * (1 - wrap), 0)

    out_shapes = (
        jax.ShapeDtypeStruct((bsz, seq, D_MODEL), F32),
        jax.ShapeDtypeStruct((bsz, 128, ROWS), F32),
        jax.ShapeDtypeStruct((bsz, 128, ROWS), F32),
        jax.ShapeDtypeStruct((bsz, M_HEADS, M_DIM, M_DIM), F32),
        jax.ShapeDtypeStruct((bsz, M_HEADS, M_DIM), F32),
        jax.ShapeDtypeStruct((M_HEADS, bsz), F32),
    )
    return pl.pallas_call(
        functools.partial(_prompt_kernel, tb=tb, nt=nt),
        grid=(bsz, nt),
        in_specs=[
            pl.BlockSpec((1, tb, D_MODEL), lambda b, j: (b, j, 0)),
            pl.BlockSpec((1, tb, D_MODEL), next_block),
            full((1, D_MODEL)), full((D_MODEL, P_MAIN)), full((T_ROWS, D_MODEL)), full((8, ROWS)),
            full((1, ATT_HEADS)), full((M_HEADS, M_DIM, ROWS)), full((D_MODEL, D_MODEL)), full((1, D_MODEL)),
        ],
        out_specs=(
            pl.BlockSpec((1, tb, D_MODEL), lambda b, j: (b, j, 0)),
            pl.BlockSpec((1, 128, ROWS), lambda b, j: (b, 0, 0)),
            pl.BlockSpec((1, 128, ROWS), lambda b, j: (b, 0, 0)),
            pl.BlockSpec((1, M_HEADS, M_DIM, M_DIM), lambda b, j: (b, 0, 0, 0)),
            pl.BlockSpec((1, M_HEADS, M_DIM), lambda b, j: (b, 0, 0)),
            pl.BlockSpec((M_HEADS, bsz), lambda b, j: (0, 0)),
        ),
        out_shape=out_shapes,
        scratch_shapes=[
            pltpu.VMEM((tb, P_MAIN), F32),
            pltpu.VMEM((tb // ROWS, T_ROWS, ROWS), F32),
            pltpu.VMEM((tb, D_MODEL), BF16),
            pltpu.VMEM((tb, D_MODEL), BF16),
            pltpu.VMEM((ROWS, 128), BF16),
            pltpu.VMEM((128, ROWS), BF16),
            pltpu.VMEM((M_HEADS, STATE_ROWS, M_DIM), F32),
            pltpu.VMEM((8, ROWS), F32),
            pltpu.VMEM((2 * ATT_KV, 2 * ROWS, ATT_GROUP * ROWS), F32),
        ],
        compiler_params=pltpu.CompilerParams(
            dimension_semantics=("arbitrary", "arbitrary"), vmem_limit_bytes=VMEM_LIMIT_BYTES),
        name="prompt_layer",
    )(x, x, gpre, wmain, wt, bg, sinks, gmhb, wout, gpost)


def _sample_bias_new(head):
    r = lax.broadcasted_iota(jnp.int32, (ROWS, ROWS), 0)
    c = lax.broadcasted_iota(jnp.int32, (ROWS, ROWS), 1)
    valid = ((r >> 3) == (c >> 3)) & (r >= c)
    return jnp.where(valid, -_slope(head) * (r - c).astype(F32), NEG_BIG)


def _sample_bias_cache(head):
    r = lax.broadcasted_iota(jnp.int32, (ROWS, ROWS), 0)
    c = lax.broadcasted_iota(jnp.int32, (ROWS, ROWS), 1)
    diff = (r & (SAMPLE_SEQ - 1)) + ROWS - c
    return jnp.where(diff < ROWS, -_slope(head) * diff.astype(F32), NEG_BIG)


def _sample_kernel(x_ref, kct_ref, vct_ref, cin_ref, nin_ref, m0_ref,
                   gpre_ref, wmain_ref, wt_ref, bg_ref, sink_ref, gmhb_ref, wout_ref, gpost_ref,
                   y_ref, kot_ref, vot_ref, cout_ref, nout_ref, mout_ref,
                   projfull_ref, projt_ref, cat_ref, qh_ref, sc_ref, oc_ref, qc_ref, wv_ref, kt_ref,
                   decb_ref, bn_ref, bc_ref):
    step, part = pl.program_id(0), pl.program_id(1)

    @pl.when((step == 0) & (part == 0))
    def _():
        for hh in range(ATT_HEADS):
            bn_ref[hh] = _sample_bias_new(hh)
            bc_ref[hh] = _sample_bias_cache(hh)
        mout_ref[...] = jnp.zeros_like(mout_ref)

    @pl.when(part == 0)
    def _():
        xn = _rms(x_ref[...], gpre_ref[...]).astype(BF16)
        projfull_ref[...] = jnp.dot(xn, wmain_ref[...], preferred_element_type=F32)
        ptf = lax.dot_general(wt_ref[...], xn, NT_DIMS, preferred_element_type=F32)
        for g in range(SAMPLE_PARTS):
            projt_ref[g] = ptf[:, g * ROWS:(g + 1) * ROWS]

    part_rows = pl.ds(pl.multiple_of(part * ROWS, ROWS), ROWS)
    proj_ref = projfull_ref.at[part_rows]
    pt = projt_ref[part]

    ri = lax.broadcasted_iota(jnp.int32, (ROWS, ROWS), 0)
    ci = lax.broadcasted_iota(jnp.int32, (ROWS, ROWS), 1)
    same_seq = (ri >> 3) == (ci >> 3)
    mask_t = same_seq & (ri <= ci)
    last_sel = (same_seq & ((ri & (SAMPLE_SEQ - 1)) == SAMPLE_SEQ - 1)).astype(BF16)
    row16 = lax.broadcasted_iota(jnp.int32, (SAMPLE_GROUP, ROWS), 0)
    lane16 = lax.broadcasted_iota(jnp.int32, (SAMPLE_GROUP, ROWS), 1)
    seq_of_lane = (lane16 >> 3) == row16
    ones_rows = jnp.ones((ONES_ROWS, ROWS), F32)

    qa = proj_ref[:, P_QA:P_QA + 512] * ATT_SCALE
    for hh in range(ATT_HEADS):
        blk = qa[:, (hh // 2) * 128:(hh // 2 + 1) * 128]
        if hh % 2 == 1:
            blk = pltpu.roll(blk, ATT_DIM, 1)
        qh_ref[hh] = blk[:, 0:ATT_DIM]
    ka = proj_ref[:, P_KA:P_KA + 128]
    kat = ka.T
    vat = proj_ref[:, P_VA:P_VA + 128].T

    keep_new = ci >= ROWS - SAMPLE_SEQ

    for b in range(SAMPLE_GROUP):
        rows = pl.ds(b * SAMPLE_SEQ, SAMPLE_SEQ)
        kct = kct_ref[b]
        shift = (ROWS - SAMPLE_SEQ - b * SAMPLE_SEQ) % ROWS
        new_k, new_v = (kat, vat) if shift == 0 else (pltpu.roll(kat, shift, 1), pltpu.roll(vat, shift, 1))
        kot_ref[b] = jnp.where(keep_new, new_k, pltpu.roll(kct, ROWS - SAMPLE_SEQ, 1))
        vot_ref[b] = jnp.where(keep_new, new_v, pltpu.roll(vct_ref[b], ROWS - SAMPLE_SEQ, 1))
        for kv in range(ATT_KV):
            lhs = jnp.concatenate([qh_ref[kv * ATT_GROUP + g, rows, :] for g in range(ATT_GROUP)], axis=0)
            res = _bdot(lhs, kct[kv * ATT_DIM:(kv + 1) * ATT_DIM, :])
            for g in range(ATT_GROUP):
                sc_ref[kv * ATT_GROUP + g, rows, :] = res[g * SAMPLE_SEQ:(g + 1) * SAMPLE_SEQ]
        for h in range(M_HEADS):
            qc_ref[h, rows, :] = _bdot(proj_ref[rows, P_QM + h * M_DIM:P_QM + (h + 1) * M_DIM], cin_ref[b, h])

    o_new, esinks = [], []
    for hh in range(ATT_HEADS):
        kv = hh // ATT_GROUP
        s_n = _bdot(qh_ref[hh], kat[kv * ATT_DIM:(kv + 1) * ATT_DIM, :]) + bn_ref[hh]
        s_c = sc_ref[hh] + bc_ref[hh]
        sink = sink_ref[0:1, hh:hh + 1]
        mx = jnp.maximum(jnp.max(jnp.maximum(s_n, s_c), axis=-1, keepdims=True), sink)
        mx_b = jnp.broadcast_to(mx, (ROWS, ROWS))
        p_n = jnp.exp(s_n - mx_b)
        sc_ref[hh] = jnp.exp(s_c - mx_b)
        vaug = jnp.concatenate([vat[kv * ATT_DIM:(kv + 1) * ATT_DIM, :], ones_rows], axis=0)
        o_new.append(_bdot_nt(p_n, vaug))
        esinks.append(jnp.exp(sink - mx))

    first_seq = (step * SAMPLE_PARTS + part) * SAMPLE_GROUP
    seq_step = ri == first_seq + (ci >> 3)
    m0 = _exact_dot(jnp.concatenate([m0_ref[...], jnp.zeros((8 - M_HEADS, ROWS), F32)], axis=0),
                    seq_step.astype(BF16))
    r = _gate_rows(pt[T_GATES:T_GATES + 8, :] + bg_ref[...], m0, mask_t, last_sel)
    step_seq = (ci == first_seq + (ri >> 3)) & ((ri & (SAMPLE_SEQ - 1)) == 0)
    mout_ref[...] += _exact_dot(r["m_new"], step_seq.astype(BF16))[0:M_HEADS]
    m_out = []
    for h in range(M_HEADS):
        q = proj_ref[:, P_QM + h * M_DIM:P_QM + (h + 1) * M_DIM].astype(BF16)
        kf = proj_ref[:, P_KM + h * M_DIM:P_KM + (h + 1) * M_DIM] * K_SCALE
        k = kf.astype(BF16)
        vt = pt[T_VM + h * M_DIM:T_VM + (h + 1) * M_DIM, :]
        n_h = nin_ref[h]
        r1 = lax.dot_general(jnp.concatenate([k, n_h.astype(BF16)], axis=0), q, NT_DIMS,
                             preferred_element_type=F32)
        st = r1[0:ROWS] * jnp.exp(r["a_masked"][h] + r["bm"][h:h + 1, :])
        q_n = jnp.sum(jnp.where(seq_of_lane, r1[ROWS:ROWS + SAMPLE_GROUP], 0.0), axis=0, keepdims=True)
        g_row = r["gexp"][h:h + 1, :]
        num = _bdot(vt, st) + g_row * qc_ref[h].T
        den = jnp.sum(st, axis=0, keepdims=True) + g_row * q_n
        ht = num * (1.0 / jnp.maximum(jnp.abs(den), r["enm"][h:h + 1, :]))
        hn = ht * lax.rsqrt(jnp.mean(ht * ht, axis=0, keepdims=True) + NORM_EPS) * gmhb_ref[h]
        m_out.append(_sigmoid(proj_ref[:, P_OM + h * M_DIM:P_OM + (h + 1) * M_DIM]) * hn.T
                     * _silu(proj_ref[:, P_ZM + h * M_DIM:P_ZM + (h + 1) * M_DIM]))
        w_row = r["w"][h:h + 1, :]
        dec16 = jnp.sum(jnp.where(lane16 == row16 * SAMPLE_SEQ, r["decay"][h:h + 1, :], 0.0),
                        axis=1, keepdims=True)
        nout_ref[h] = dec16 * n_h + _bdot(jnp.where(seq_of_lane, w_row, 0.0), k)
        decb_ref[h] = jnp.broadcast_to(dec16, (SAMPLE_GROUP, ROWS))
        wv_ref[h] = (vt * w_row).T
        kt_ref[h] = kf.T.astype(BF16)

    for b in range(SAMPLE_GROUP):
        rows = pl.ds(b * SAMPLE_SEQ, SAMPLE_SEQ)
        vct = vct_ref[b]
        for kv in range(ATT_KV):
            vaug = jnp.concatenate([vct[kv * ATT_DIM:(kv + 1) * ATT_DIM, :], ones_rows], axis=0)
            pl_ = jnp.concatenate([sc_ref[kv * ATT_GROUP + g, rows, :] for g in range(ATT_GROUP)], axis=0)
            res = _bdot_nt(pl_, vaug)
            for g in range(ATT_GROUP):
                oc_ref[kv * ATT_GROUP + g, rows, 0:ATT_DIM + ONES_ROWS] = res[g * SAMPLE_SEQ:(g + 1) * SAMPLE_SEQ]
        in_seq = (ri >> 3) == b
        for h in range(M_HEADS):
            upd = jnp.dot(kt_ref[h], jnp.where(in_seq, wv_ref[h], 0.0).astype(BF16), preferred_element_type=F32)
            cout_ref[b, h] = decb_ref[h, b:b + 1, :] * cin_ref[b, h] + upd

    att = []
    for pair in range(ATT_HEADS // 2):
        halves = []
        for hh in (2 * pair, 2 * pair + 1):
            on, oc = o_new[hh], oc_ref[hh]
            den = on[:, ATT_DIM:ATT_DIM + 1] + oc[:, ATT_DIM:ATT_DIM + 1] + esinks[hh]
            halves.append((on[:, 0:ATT_DIM] + oc[:, 0:ATT_DIM]) * (1.0 / den))
        att.append(jnp.concatenate(halves, axis=1))
    a_out = jnp.concatenate(att, axis=1) * _silu(proj_ref[:, P_ZA:P_ZA + 512])
    cat_ref[part_rows, :] = jnp.concatenate([a_out] + m_out, axis=1).astype(BF16)

    @pl.when(part == SAMPLE_PARTS - 1)
    def _():
        y_ref[...] = _out_tail(cat_ref[...], x_ref[...], wout_ref, gpost_ref)


def _sample_call(x, kct, vct, cin, nin, m0, gpre, wmain, wt, bg, sinks, gmhb, wout, gpost):
    nrows = x.shape[0]
    ngroups = nrows // ROWS
    nseq = ngroups * SAMPLE_GROUP
    parts = SAMPLE_PARTS
    assert ngroups % parts == 0
    assert nseq == ROWS, "the per-sequence stabiliser state is handled as one 128-lane row per head"
    full = lambda shape: pl.BlockSpec(shape, lambda i, p: (0,) * len(shape))
    grp = SAMPLE_GROUP
    group = lambda i, p: i * parts + p
    out_shapes = (
        jax.ShapeDtypeStruct((nrows, D_MODEL), F32),
        jax.ShapeDtypeStruct((nseq, 128, ROWS), F32),
        jax.ShapeDtypeStruct((nseq, 128, ROWS), F32),
        jax.ShapeDtypeStruct((nseq, M_HEADS, M_DIM, M_DIM), F32),
        jax.ShapeDtypeStruct((M_HEADS, nseq, M_DIM), F32),
        jax.ShapeDtypeStruct((M_HEADS, nseq), F32),
    )
    return pl.pallas_call(
        _sample_kernel,
        grid=(ngroups // parts, parts),
        in_specs=[
            pl.BlockSpec((parts * ROWS, D_MODEL), lambda i, p: (i, 0)),
            pl.BlockSpec((grp, 128, ROWS), lambda i, p: (group(i, p), 0, 0)),
            pl.BlockSpec((grp, 128, ROWS), lambda i, p: (group(i, p), 0, 0)),
            pl.BlockSpec((grp, M_HEADS, M_DIM, M_DIM), lambda i, p: (group(i, p), 0, 0, 0)),
            pl.BlockSpec((M_HEADS, grp, M_DIM), lambda i, p: (0, group(i, p), 0)),
            full((M_HEADS, nseq)),
            full((1, D_MODEL)), full((D_MODEL, P_MAIN)), full((T_ROWS, D_MODEL)), full((8, ROWS)),
            full((1, ATT_HEADS)), full((M_HEADS, M_DIM, ROWS)), full((D_MODEL, D_MODEL)), full((1, D_MODEL)),
        ],
        out_specs=(
            pl.BlockSpec((parts * ROWS, D_MODEL), lambda i, p: (i, 0)),
            pl.BlockSpec((grp, 128, ROWS), lambda i, p: (group(i, p), 0, 0)),
            pl.BlockSpec((grp, 128, ROWS), lambda i, p: (group(i, p), 0, 0)),
            pl.BlockSpec((grp, M_HEADS, M_DIM, M_DIM), lambda i, p: (group(i, p), 0, 0, 0)),
            pl.BlockSpec((M_HEADS, grp, M_DIM), lambda i, p: (0, group(i, p), 0)),
            full((M_HEADS, nseq)),
        ),
        out_shape=out_shapes,
        scratch_shapes=[
            pltpu.VMEM((parts * ROWS, P_MAIN), F32),
            pltpu.VMEM((parts, T_ROWS, ROWS), F32),
            pltpu.VMEM((parts * ROWS, D_MODEL), BF16),
            pltpu.VMEM((ATT_HEADS, ROWS, ATT_DIM), F32),
            pltpu.VMEM((ATT_HEADS, ROWS, ROWS), F32),
            pltpu.VMEM((ATT_HEADS, ROWS, ROWS), F32),
            pltpu.VMEM((M_HEADS, ROWS, M_DIM), F32),
            pltpu.VMEM((M_HEADS, ROWS, M_DIM), F32),
            pltpu.VMEM((M_HEADS, M_DIM, ROWS), BF16),
            pltpu.VMEM((M_HEADS, SAMPLE_GROUP, ROWS), F32),
            pltpu.VMEM((ATT_HEADS, ROWS, ROWS), F32),
            pltpu.VMEM((ATT_HEADS, ROWS, ROWS), F32),
        ],
        compiler_params=pltpu.CompilerParams(
            dimension_semantics=("arbitrary", "arbitrary"), vmem_limit_bytes=VMEM_LIMIT_BYTES),
        name="sample_layer",
    )(x, kct, vct, cin, nin, m0, gpre, wmain, wt, bg, sinks, gmhb, wout, gpost)


PROMPT_BLOCK = 512
PROMPT_SUBBLOCKS = 2
PROJ_COL_STEP = 512
CHUNK_STAGES = 7
TAIL_PIECES = 3


def _window_in(cache):
    nseq = cache.shape[0]
    return cache.transpose(0, 2, 3, 1).reshape(nseq, ATT_KV * ATT_DIM, ROWS)


def _window_out(win_t):
    nseq = win_t.shape[0]
    return win_t.reshape(nseq, ATT_KV, ATT_DIM, ROWS).transpose(0, 3, 1, 2)[None]


def kernel(x_prompt, x_sample, cache_win_k, cache_win_v, state_C, state_n, state_m,
           g_pre, w_in, b_gate, attn_sinks, g_mh, w_out, g_post):
    depth = g_pre.shape[0]
    assert depth == 1, "single-layer trunk"
    nseq, sseq, _ = x_sample.shape
    assert sseq == SAMPLE_SEQ and nseq % SAMPLE_GROUP == 0

    gpre = g_pre[0].reshape(1, D_MODEL)
    gpost = g_post[0].reshape(1, D_MODEL)
    wmain, wt, wout, gmhb, bg = _weights_call(w_in[0].T, w_out[0], g_mh[0], b_gate)
    sinks = attn_sinks[0].reshape(1, ATT_HEADS)

    yp, wkp, wvp, cp, np_, mp = _prompt_call(x_prompt, gpre, wmain, wt, bg, sinks, gmhb, wout, gpost, PROMPT_BLOCK)

    ys, wks, wvs, cs, ns, ms = _sample_call(
        x_sample.reshape(nseq * sseq, D_MODEL), _window_in(cache_win_k[0]), _window_in(cache_win_v[0]),
        state_C[0], state_n[0].transpose(1, 0, 2), state_m[0].T,
        gpre, wmain, wt, bg, sinks, gmhb, wout, gpost)

    return (yp, ys.reshape(nseq, sseq, D_MODEL), _window_out(wkp), _window_out(wvp),
            cp[None], np_[None], mp.T[None],
            _window_out(wks), _window_out(wvs), cs[None], ns.transpose(1, 0, 2)[None], ms.T[None])
```

```python
import functools

import jax
import jax.numpy as jnp
from jax import lax
from jax.experimental import pallas as pl
from jax.experimental.pallas import tpu as pltpu

F32 = jnp.float32
BF16 = jnp.bfloat16

D_MODEL = 1024
ROWS = 128
ATT_HEADS, ATT_KV, ATT_GROUP, ATT_DIM = 8, 2, 4, 64
M_HEADS, M_DIM = 4, 128
NORM_EPS = 1e-6
NEG_BIG = -1e30
ATT_SCALE = ATT_DIM ** -0.5
LOG2E = 1.4426950408889634
K_SCALE = M_DIM ** -0.5

VM, GATES = 2304, 3840

P_QA, P_ZA, P_QM, P_KM, P_OM, P_ZM, P_KA, P_VA = 0, 512, 1024, 1536, 2048, 2560, 3072, 3200
P_MAIN = 3328
T_VM, T_GATES = 0, 512
T_ROWS = 528
T_SPLIT = 272
STATE_ROWS = 144
ONES_ROWS = 16

SAMPLE_SEQ = 8
SAMPLE_GROUP = ROWS // SAMPLE_SEQ
SAMPLE_PARTS = 2

VMEM_BYTES_V7X = 64 * 1024 * 1024
VMEM_LIMIT_BYTES = VMEM_BYTES_V7X * 7 // 8
NT_DIMS = (((1,), (1,)), ((), ()))


def _rms(x, g):
    return x * lax.rsqrt(jnp.mean(x * x, axis=-1, keepdims=True) + NORM_EPS) * g


def _sigmoid(x):
    return 0.5 + 0.5 * jnp.tanh(0.5 * x)


def _silu(x):
    h = 0.5 * x
    return h + h * jnp.tanh(h)


def _log_sigmoid(x):
    return -(jnp.maximum(-x, 0.0) + jnp.log1p(jnp.exp(-jnp.abs(x))))


def _slope(head):
    return 2.0 ** -(head + 1)


def _bdot(a, b):
    return jnp.dot(a.astype(BF16), b.astype(BF16), preferred_element_type=F32)


def _bdot_nt(a, b):
    return lax.dot_general(a.astype(BF16), b.astype(BF16), NT_DIMS, preferred_element_type=F32)


def _exact_dot(x, m):
    hi = x.astype(BF16).astype(F32)
    mid = (x - hi).astype(BF16).astype(F32)
    lo = (x - hi - mid).astype(BF16).astype(F32)
    parts = jnp.dot(jnp.concatenate([hi, mid, lo, jnp.zeros_like(hi)], axis=0).astype(BF16), m,
                    preferred_element_type=F32)
    return parts[0:8] + parts[8:16] + parts[16:24]


def _gate_rows(x, m0, mask_t, last_sel):
    row = lax.broadcasted_iota(jnp.int32, (8, ROWS), 0)
    head_rows = row < M_HEADS
    ic = jnp.where(head_rows, x, 0.0)
    fc = jnp.where(head_rows, _log_sigmoid(pltpu.roll(x, M_HEADS, 0)), 0.0)
    b = _exact_dot(fc, mask_t.astype(BF16))
    a = ic - b
    a_cols = jnp.concatenate([a, jnp.zeros((ROWS - 8, ROWS), F32)], axis=0).T
    a_masked = [jnp.where(mask_t, a_cols[:, h:h + 1], -jnp.inf) for h in range(M_HEADS)]
    cm = jnp.concatenate([jnp.max(am, axis=0, keepdims=True) for am in a_masked]
                         + [jnp.zeros((8 - M_HEADS, ROWS), F32)], axis=0)
    m_t = jnp.maximum(b + m0, b + cm)
    if last_sel is None:
        b_last = jnp.broadcast_to(b[:, ROWS - 1:ROWS], b.shape)
        m_new = jnp.broadcast_to(m_t[:, ROWS - 1:ROWS], b.shape)
    else:
        both = _exact_dot(jnp.where(head_rows, b, pltpu.roll(m_t, M_HEADS, 0)), last_sel)
        b_last = jnp.where(head_rows, both, 0.0)
        m_new = jnp.where(head_rows, pltpu.roll(both, M_HEADS, 0), 0.0)
    return dict(a_masked=a_masked, bm=b - m_t, gexp=jnp.exp(b + m0 - m_t), enm=jnp.exp(-m_t), m_new=m_new,
                w=jnp.exp(b_last - b + ic - m_new), decay=jnp.exp(b_last + m0 - m_new))


def _out_tail(cat, x, wout_ref, gpost_ref):
    y = jnp.dot(cat, wout_ref[...], preferred_element_type=F32)
    return x + _rms(y, gpost_ref[...])


W_BLOCK = 256
MAIN_BLOCKS = P_MAIN // W_BLOCK


def _main_src_block(i):
    return jnp.where(i < 2, i, jnp.where(i < 8, i + 1, jnp.where(i < 12, i + 3, 2)))


def _weights_kernel(main_ref, vm0_ref, vm1_ref, vm2_ref, vm3_ref, gates_ref, wout_ref, gmh_ref, bgate_ref,
                    wmain_ref, wt_ref, woutb_ref, gmhb_ref, bgb_ref):
    wmain_ref[...] = main_ref[...].T.astype(BF16)

    @pl.when(pl.program_id(0) == 0)
    def _():
        wt_ref[...] = jnp.concatenate(
            [vm0_ref[...], vm1_ref[...], vm2_ref[...], vm3_ref[...], gates_ref[...],
             jnp.zeros((T_ROWS - T_GATES - 8, D_MODEL), F32)], axis=0).astype(BF16)
        woutb_ref[...] = wout_ref[...].astype(BF16)
        for h in range(M_HEADS):
            gmhb_ref[h] = jnp.broadcast_to(gmh_ref[h:h + 1, :], (M_DIM, ROWS)).T
        bg_row = jnp.concatenate([bgate_ref[...], jnp.zeros((1, ROWS - 2 * M_HEADS), F32)], axis=1)
        bgb_ref[...] = jnp.broadcast_to(bg_row, (ROWS, ROWS)).T[0:2 * M_HEADS, :]


def _weights_call(w_in_t, w_out, gmh, bgate):
    blk = lambda r: pl.BlockSpec((128, D_MODEL), lambda i, r=r: (r, 0))
    const = lambda shape: pl.BlockSpec(shape, lambda i: (0,) * len(shape))
    return pl.pallas_call(
        _weights_kernel,
        grid=(MAIN_BLOCKS,),
        in_specs=[
            pl.BlockSpec((W_BLOCK, D_MODEL), lambda i: (_main_src_block(i), 0)),
            blk(VM // 128), blk(VM // 128 + 1), blk(VM // 128 + 2), blk(VM // 128 + 3),
            pl.BlockSpec((8, D_MODEL), lambda i: (GATES // 8, 0)),
            const((D_MODEL, D_MODEL)), const((M_HEADS, M_DIM)), const((1, 2 * M_HEADS)),
        ],
        out_specs=(
            pl.BlockSpec((D_MODEL, W_BLOCK), lambda i: (0, i)),
            const((T_ROWS, D_MODEL)), const((D_MODEL, D_MODEL)),
            const((M_HEADS, M_DIM, ROWS)), const((2 * M_HEADS, ROWS)),
        ),
        out_shape=(
            jax.ShapeDtypeStruct((D_MODEL, P_MAIN), BF16),
            jax.ShapeDtypeStruct((T_ROWS, D_MODEL), BF16),
            jax.ShapeDtypeStruct((D_MODEL, D_MODEL), BF16),
            jax.ShapeDtypeStruct((M_HEADS, M_DIM, ROWS), F32),
            jax.ShapeDtypeStruct((2 * M_HEADS, ROWS), F32),
        ),
        compiler_params=pltpu.CompilerParams(
            dimension_semantics=("arbitrary",), vmem_limit_bytes=VMEM_LIMIT_BYTES),
        name="layer_weights",
    )(w_in_t, w_in_t, w_in_t, w_in_t, w_in_t, w_in_t, w_out, gmh, bgate)


def _prompt_bias_t(kv, first):
    j = lax.broadcasted_iota(jnp.int32, (2 * ROWS, ROWS), 0)
    i = lax.broadcasted_iota(jnp.int32, (2 * ROWS, ROWS), 1)
    diff = ROWS + i - j
    valid = (diff >= 0) & (diff < ROWS)
    if first:
        valid = valid & (j >= ROWS)
    dfl = diff.astype(F32)
    return jnp.concatenate(
        [jnp.where(valid, -_slope(kv * ATT_GROUP + g) * dfl, NEG_BIG) * LOG2E for g in range(ATT_GROUP)], axis=1)


def _prompt_kernel(x_ref, xnext_ref, gpre_ref, wmain_ref, wt_ref, bg_ref, sink_ref, gmhb_ref, wout_ref, gpost_ref,
                   y_ref, wk_ref, wv_ref, c_ref, n_ref, m_ref,
                   proj_ref, projt_ref, cat_ref, xn_ref, kprev_ref, vtprev_ref, ct_ref, mst_ref, bias_ref,
                   *, tb, nt):
    bi = pl.program_id(0)
    j = pl.program_id(1)
    nchunks = tb // ROWS

    @pl.when((bi == 0) & (j == 0))
    def _():
        for first in range(2):
            for kv in range(ATT_KV):
                bias_ref[first * ATT_KV + kv] = _prompt_bias_t(kv, first)
        m_ref[...] = jnp.zeros_like(m_ref)

    @pl.when(j == 0)
    def _():
        kprev_ref[...] = jnp.zeros_like(kprev_ref)
        vtprev_ref[...] = jnp.zeros_like(vtprev_ref)
        ct_ref[...] = jnp.zeros_like(ct_ref)
        mst_ref[...] = jnp.zeros_like(mst_ref)

    sub = tb // PROMPT_SUBBLOCKS
    sub_chunks = sub // ROWS

    def norm_rows(sb):
        rows_sb = slice(sb * sub, (sb + 1) * sub)
        xn_ref[rows_sb, :] = _rms(x_ref[0, rows_sb, :], gpre_ref[...]).astype(BF16)

    def proj_cols(sb, c0, c1):
        rows_sb = slice(sb * sub, (sb + 1) * sub)
        proj_ref[rows_sb, c0:c1] = jnp.dot(xn_ref[rows_sb, :], wmain_ref[:, c0:c1], preferred_element_type=F32)

    def proj_t(sb):
        for r0, r1 in ((0, T_SPLIT), (T_SPLIT, T_ROWS)):
            pt = lax.dot_general(wt_ref[r0:r1, :], xn_ref[sb * sub:(sb + 1) * sub, :], NT_DIMS,
                                 preferred_element_type=F32)
            for c in range(sub_chunks):
                projt_ref[sb * sub_chunks + c, r0:r1, :] = pt[:, c * ROWS:(c + 1) * ROWS]

    def out_rows(sb):
        rows_sb = slice(sb * sub, (sb + 1) * sub)
        y_ref[0, rows_sb, :] = _out_tail(cat_ref[rows_sb, :], x_ref[0, rows_sb, :], wout_ref, gpost_ref)

    def proj_pieces(sb):
        bounds = list(range(0, P_MAIN, PROJ_COL_STEP)) + [P_MAIN]
        pieces = [functools.partial(proj_cols, sb, c0, c1) for c0, c1 in zip(bounds[:-1], bounds[1:])]
        return pieces[:2] + [functools.partial(proj_t, sb)] + pieces[2:]

    ri = lax.broadcasted_iota(jnp.int32, (ROWS, ROWS), 0)
    ci = lax.broadcasted_iota(jnp.int32, (ROWS, ROWS), 1)
    mask_t = ri <= ci
    row8 = lax.broadcasted_iota(jnp.int32, (8, ROWS), 0)
    tail_row = lax.broadcasted_iota(jnp.int32, (STATE_ROWS - M_DIM, 2 * ROWS), 0)
    tail_lane = lax.broadcasted_iota(jnp.int32, (STATE_ROWS - M_DIM, 2 * ROWS), 1)

    def chunk(c):
        rows = pl.ds(c * ROWS, ROWS)
        first = ((j == 0) & (c == 0)).astype(jnp.int32)
        yield

        qa = proj_ref[rows, P_QA:P_QA + 512] * (ATT_SCALE * LOG2E)
        kcur = proj_ref[rows, P_KA:P_KA + 128].astype(BF16)
        vtcur = proj_ref[rows, P_VA:P_VA + 128].T.astype(BF16)
        kcat = jnp.concatenate([kprev_ref[...], kcur], axis=0)
        vtcat = jnp.concatenate([vtprev_ref[...], vtcur], axis=1)
        r = _gate_rows(projt_ref[c, T_GATES:T_GATES + 8, :] + bg_ref[...], mst_ref[...], mask_t, None)

        scores, sinks = [], []
        for kv in range(ATT_KV):
            want_hi = kv == 1
            keep = (ci >= ATT_DIM) if want_hi else (ci < ATT_DIM)
            pieces = []
            for g in range(ATT_GROUP):
                hh = kv * ATT_GROUP + g
                blk = qa[:, (hh // 2) * 128:(hh // 2 + 1) * 128]
                if (hh % 2 == 1) != want_hi:
                    blk = pltpu.roll(blk, ATT_DIM, 1)
                pieces.append(jnp.where(keep, blk, 0.0))
            q4 = jnp.concatenate(pieces, axis=0)
            scores.append(_bdot_nt(kcat, q4) + bias_ref[first * ATT_KV + kv])
            sinks.append(jnp.concatenate(
                [jnp.broadcast_to(sink_ref[0:1, kv * ATT_GROUP + g:kv * ATT_GROUP + g + 1] * LOG2E, (1, ROWS))
                 for g in range(ATT_GROUP)], axis=1))
        yield
        gw = jnp.where(row8 < M_HEADS, r["gexp"], pltpu.roll(r["w"], M_HEADS, 0))
        gw_cols = jnp.concatenate([gw, jnp.zeros((ROWS - 8, ROWS), F32)], axis=0).T
        lhs_nums, rhs_lows, sts = [], [], []
        for h in range(M_HEADS):
            qf = proj_ref[rows, P_QM + h * M_DIM:P_QM + (h + 1) * M_DIM]
            kf = proj_ref[rows, P_KM + h * M_DIM:P_KM + (h + 1) * M_DIM] * K_SCALE
            vt = projt_ref[c, T_VM + h * M_DIM:T_VM + (h + 1) * M_DIM, :]
            ct = ct_ref[h]
            sts.append((_bdot_nt(kf, qf), (kf * gw_cols[:, M_HEADS + h:M_HEADS + h + 1]).astype(BF16)))
            rhs_lows.append(jnp.concatenate([(qf * gw_cols[:, h:h + 1]).T.astype(BF16),
                                             jnp.zeros((M_DIM, M_DIM), BF16)], axis=1))
            tail = jnp.concatenate([jnp.zeros((STATE_ROWS - M_DIM, ROWS), F32), ct[M_DIM:STATE_ROWS]], axis=1)
            tail = jnp.where((tail_row == 1) & (tail_lane < ROWS), 1.0, tail)
            lhs_nums.append(jnp.concatenate([jnp.concatenate([vt, ct[0:M_DIM]], axis=1), tail], axis=0).astype(BF16))
        mst_ref[...] = r["m_new"]
        yield

        outs, nums, dens = [], [], []
        for kv in range(ATT_KV):
            s, sink = scores[kv], sinks[kv]
            mx = jnp.maximum(jnp.max(s, axis=0, keepdims=True), sink)
            p = jnp.exp2(s - mx).astype(BF16)
            lhs = jnp.concatenate([vtcat[kv * ATT_DIM:(kv + 1) * ATT_DIM, :],
                                   jnp.ones((ONES_ROWS, 2 * ROWS), BF16)], axis=0)
            o = jnp.dot(lhs, p, preferred_element_type=F32)
            outs.append((o, jnp.exp2(sink - mx)))
        yield
        for h in range(M_HEADS):
            raw, wk = sts[h]
            st = raw * jnp.exp(r["a_masked"][h] + r["bm"][h:h + 1, :])
            rhs = jnp.concatenate([jnp.concatenate([st.astype(BF16), wk], axis=1), rhs_lows[h]], axis=0)
            res = jnp.dot(lhs_nums[h], rhs, preferred_element_type=F32)
            nums.append(res[0:M_DIM, 0:ROWS])
            dens.append(jnp.sum(st, axis=0, keepdims=True) + res[M_DIM:M_DIM + 1, 0:ROWS])
            dec = r["decay"][h:h + 1, 0:1]
            ct_ref[h, 0:M_DIM, :] = dec * ct_ref[h, 0:M_DIM, :] + res[0:M_DIM, ROWS:2 * ROWS]
            ct_ref[h, M_DIM:M_DIM + 1, :] = dec * ct_ref[h, M_DIM:M_DIM + 1, :] + res[M_DIM + 1:M_DIM + 2, ROWS:2 * ROWS]
        yield

        att = []
        for kv in range(ATT_KV):
            o, esink = outs[kv]
            on = o[0:ATT_DIM, :] * (1.0 / (o[ATT_DIM:ATT_DIM + 1, :] + esink))
            for pair in range(2):
                two = jnp.concatenate([on[:, (2 * pair) * ROWS:(2 * pair + 1) * ROWS],
                                       on[:, (2 * pair + 1) * ROWS:(2 * pair + 2) * ROWS]], axis=0)
                att.append(two.T)
        a_out = jnp.concatenate(att, axis=1) * _silu(proj_ref[rows, P_ZA:P_ZA + 512])
        kprev_ref[...] = kcur
        vtprev_ref[...] = vtcur
        yield
        m_out = []
        for h in range(M_HEADS):
            ht = nums[h] * (1.0 / jnp.maximum(jnp.abs(dens[h]), r["enm"][h:h + 1, :]))
            hn = ht * lax.rsqrt(jnp.mean(ht * ht, axis=0, keepdims=True) + NORM_EPS) * gmhb_ref[h]
            m_out.append(_sigmoid(proj_ref[rows, P_OM + h * M_DIM:P_OM + (h + 1) * M_DIM]) * hn.T
                         * _silu(proj_ref[rows, P_ZM + h * M_DIM:P_ZM + (h + 1) * M_DIM]))

        cat_ref[rows, :] = jnp.concatenate([a_out] + m_out, axis=1).astype(BF16)
        yield

    @pl.when((bi == 0) & (j == 0))
    def _():
        for sb in range(PROMPT_SUBBLOCKS):
            norm_rows(sb)
        for piece in proj_pieces(0):
            piece()

    def norm_next():
        xn_ref[...] = _rms(xnext_ref[0], gpre_ref[...]).astype(BF16)

    def run_chunks(sb, fillers):
        n_fill, n_slots, slot = len(fillers), sub_chunks * CHUNK_STAGES, 0
        for c in range(sb * sub_chunks, (sb + 1) * sub_chunks):
            for _ in chunk(c):
                for _ in range(-(-(slot + 1) * n_fill // n_slots) + (-slot * n_fill // n_slots)):
                    fillers.pop(0)()
                slot += 1
        assert slot == n_slots and not fillers

    assert PROMPT_SUBBLOCKS == 2
    run_chunks(0, proj_pieces(1))
    nxt = proj_pieces(0)
    run_chunks(1, [functools.partial(out_rows, 0), norm_next] + nxt[:-TAIL_PIECES])
    out_rows(1)
    for piece in nxt[-TAIL_PIECES:]:
        piece()

    @pl.when(j == nt - 1)
    def _():
        wk_ref[0] = proj_ref[tb - ROWS:tb, P_KA:P_KA + 128].T
        wv_ref[0] = proj_ref[tb - ROWS:tb, P_VA:P_VA + 128].T
        for h in range(M_HEADS):
            ct = ct_ref[h]
            c_ref[0, h] = ct[0:M_DIM].T
            n_ref[0, h:h + 1, :] = ct[M_DIM:M_DIM + 1]
        seq_lane = lax.broadcasted_iota(jnp.int32, m_ref.shape, 1) == bi
        m_ref[...] = jnp.where(seq_lane, mst_ref[0:M_HEADS, 0:m_ref.shape[1]], m_ref[...])


def _prompt_call(x, gpre, wmain, wt, bg, sinks, gmhb, wout, gpost, tb):
    bsz, seq, _ = x.shape
    nt = seq // tb
    full = lambda shape: pl.BlockSpec(shape, lambda b, j: (0,) * len(shape))

    def next_block(b, j):
        wrap = (j + 1 == nt).astype(jnp.int32)
        return (jnp.minimum(b + wrap, bsz - 1), (j + 1) * (1 - wrap), 0)

    out_shapes = (
        jax.ShapeDtypeStruct((bsz, seq, D_MODEL), F32),
        jax.ShapeDtypeStruct((bsz, 128, ROWS), F32),
        jax.ShapeDtypeStruct((bsz, 128, ROWS), F32),
        jax.ShapeDtypeStruct((bsz, M_HEADS, M_DIM, M_DIM), F32),
        jax.ShapeDtypeStruct((bsz, M_HEADS, M_DIM), F32),
        jax.ShapeDtypeStruct((M_HEADS, bsz), F32),
    )
    return pl.pallas_call(
        functools.partial(_prompt_kernel, tb=tb, nt=nt),
        grid=(bsz, nt),
        in_specs=[
            pl.BlockSpec((1, tb, D_MODEL), lambda b, j: (b, j, 0)),
            pl.BlockSpec((1, tb, D_MODEL), next_block),
            full((1, D_MODEL)), full((D_MODEL, P_MAIN)), full((T_ROWS, D_MODEL)), full((8, ROWS)),
            full((1, ATT_HEADS)), full((M_HEADS, M_DIM, ROWS)), full((D_MODEL, D_MODEL)), full((1, D_MODEL)),
        ],
        out_specs=(
            pl.BlockSpec((1, tb, D_MODEL), lambda b, j: (b, j, 0)),
            pl.BlockSpec((1, 128, ROWS), lambda b, j: (b, 0, 0)),
            pl.BlockSpec((1, 128, ROWS), lambda b, j: (b, 0, 0)),
            pl.BlockSpec((1, M_HEADS, M_DIM, M_DIM), lambda b, j: (b, 0, 0, 0)),
            pl.BlockSpec((1, M_HEADS, M_DIM), lambda b, j: (b, 0, 0)),
            pl.BlockSpec((M_HEADS, bsz), lambda b, j: (0, 0)),
        ),
        out_shape=out_shapes,
        scratch_shapes=[
            pltpu.VMEM((tb, P_MAIN), F32),
            pltpu.VMEM((tb // ROWS, T_ROWS, ROWS), F32),
            pltpu.VMEM((tb, D_MODEL), BF16),
            pltpu.VMEM((tb, D_MODEL), BF16),
            pltpu.VMEM((ROWS, 128), BF16),
            pltpu.VMEM((128, ROWS), BF16),
            pltpu.VMEM((M_HEADS, STATE_ROWS, M_DIM), F32),
            pltpu.VMEM((8, ROWS), F32),
            pltpu.VMEM((2 * ATT_KV, 2 * ROWS, ATT_GROUP * ROWS), F32),
        ],
        compiler_params=pltpu.CompilerParams(
            dimension_semantics=("arbitrary", "arbitrary"), vmem_limit_bytes=VMEM_LIMIT_BYTES),
        name="prompt_layer",
    )(x, x, gpre, wmain, wt, bg, sinks, gmhb, wout, gpost)


def _sample_bias_new(head):
    r = lax.broadcasted_iota(jnp.int32, (ROWS, ROWS), 0)
    c = lax.broadcasted_iota(jnp.int32, (ROWS, ROWS), 1)
    valid = ((r >> 3) == (c >> 3)) & (r >= c)
    return jnp.where(valid, -_slope(head) * (r - c).astype(F32), NEG_BIG)


def _sample_bias_cache(head):
    r = lax.broadcasted_iota(jnp.int32, (ROWS, ROWS), 0)
    c = lax.broadcasted_iota(jnp.int32, (ROWS, ROWS), 1)
    diff = (r & (SAMPLE_SEQ - 1)) + ROWS - c
    return jnp.where(diff < ROWS, -_slope(head) * diff.astype(F32), NEG_BIG)


def _sample_kernel(x_ref, kct_ref, vct_ref, cin_ref, nin_ref, m0_ref,
                   gpre_ref, wmain_ref, wt_ref, bg_ref, sink_ref, gmhb_ref, wout_ref, gpost_ref,
                   y_ref, kot_ref, vot_ref, cout_ref, nout_ref, mout_ref,
                   projfull_ref, projt_ref, cat_ref, qh_ref, sc_ref, oc_ref, qc_ref, wv_ref, kt_ref,
                   decb_ref, bn_ref, bc_ref):
    step, part = pl.program_id(0), pl.program_id(1)

    @pl.when((step == 0) & (part == 0))
    def _():
        for hh in range(ATT_HEADS):
            bn_ref[hh] = _sample_bias_new(hh)
            bc_ref[hh] = _sample_bias_cache(hh)
        mout_ref[...] = jnp.zeros_like(mout_ref)

    @pl.when(part == 0)
    def _():
        xn = _rms(x_ref[...], gpre_ref[...]).astype(BF16)
        projfull_ref[...] = jnp.dot(xn, wmain_ref[...], preferred_element_type=F32)
        ptf = lax.dot_general(wt_ref[...], xn, NT_DIMS, preferred_element_type=F32)
        for g in range(SAMPLE_PARTS):
            projt_ref[g] = ptf[:, g * ROWS:(g + 1) * ROWS]

    part_rows = pl.ds(pl.multiple_of(part * ROWS, ROWS), ROWS)
    proj_ref = projfull_ref.at[part_rows]
    pt = projt_ref[part]

    ri = lax.broadcasted_iota(jnp.int32, (ROWS, ROWS), 0)
    ci = lax.broadcasted_iota(jnp.int32, (ROWS, ROWS), 1)
    same_seq = (ri >> 3) == (ci >> 3)
    mask_t = same_seq & (ri <= ci)
    last_sel = (same_seq & ((ri & (SAMPLE_SEQ - 1)) == SAMPLE_SEQ - 1)).astype(BF16)
    row16 = lax.broadcasted_iota(jnp.int32, (SAMPLE_GROUP, ROWS), 0)
    lane16 = lax.broadcasted_iota(jnp.int32, (SAMPLE_GROUP, ROWS), 1)
    seq_of_lane = (lane16 >> 3) == row16
    ones_rows = jnp.ones((ONES_ROWS, ROWS), F32)

    qa = proj_ref[:, P_QA:P_QA + 512] * ATT_SCALE
    for hh in range(ATT_HEADS):
        blk = qa[:, (hh // 2) * 128:(hh // 2 + 1) * 128]
        if hh % 2 == 1:
            blk = pltpu.roll(blk, ATT_DIM, 1)
        qh_ref[hh] = blk[:, 0:ATT_DIM]
    ka = proj_ref[:, P_KA:P_KA + 128]
    kat = ka.T
    vat = proj_ref[:, P_VA:P_VA + 128].T

    keep_new = ci >= ROWS - SAMPLE_SEQ

    for b in range(SAMPLE_GROUP):
        rows = pl.ds(b * SAMPLE_SEQ, SAMPLE_SEQ)
        kct = kct_ref[b]
        shift = (ROWS - SAMPLE_SEQ - b * SAMPLE_SEQ) % ROWS
        new_k, new_v = (kat, vat) if shift == 0 else (pltpu.roll(kat, shift, 1), pltpu.roll(vat, shift, 1))
        kot_ref[b] = jnp.where(keep_new, new_k, pltpu.roll(kct, ROWS - SAMPLE_SEQ, 1))
        vot_ref[b] = jnp.where(keep_new, new_v, pltpu.roll(vct_ref[b], ROWS - SAMPLE_SEQ, 1))
        for kv in range(ATT_KV):
            lhs = jnp.concatenate([qh_ref[kv * ATT_GROUP + g, rows, :] for g in range(ATT_GROUP)], axis=0)
            res = _bdot(lhs, kct[kv * ATT_DIM:(kv + 1) * ATT_DIM, :])
            for g in range(ATT_GROUP):
                sc_ref[kv * ATT_GROUP + g, rows, :] = res[g * SAMPLE_SEQ:(g + 1) * SAMPLE_SEQ]
        for h in range(M_HEADS):
            qc_ref[h, rows, :] = _bdot(proj_ref[rows, P_QM + h * M_DIM:P_QM + (h + 1) * M_DIM], cin_ref[b, h])

    o_new, esinks = [], []
    for hh in range(ATT_HEADS):
        kv = hh // ATT_GROUP
        s_n = _bdot(qh_ref[hh], kat[kv * ATT_DIM:(kv + 1) * ATT_DIM, :]) + bn_ref[hh]
        s_c = sc_ref[hh] + bc_ref[hh]
        sink = sink_ref[0:1, hh:hh + 1]
        mx = jnp.maximum(jnp.max(jnp.maximum(s_n, s_c), axis=-1, keepdims=True), sink)
        mx_b = jnp.broadcast_to(mx, (ROWS, ROWS))
        p_n = jnp.exp(s_n - mx_b)
        sc_ref[hh] = jnp.exp(s_c - mx_b)
        vaug = jnp.concatenate([vat[kv * ATT_DIM:(kv + 1) * ATT_DIM, :], ones_rows], axis=0)
        o_new.append(_bdot_nt(p_n, vaug))
        esinks.append(jnp.exp(sink - mx))

    first_seq = (step * SAMPLE_PARTS + part) * SAMPLE_GROUP
    seq_step = ri == first_seq + (ci >> 3)
    m0 = _exact_dot(jnp.concatenate([m0_ref[...], jnp.zeros((8 - M_HEADS, ROWS), F32)], axis=0),
                    seq_step.astype(BF16))
    r = _gate_rows(pt[T_GATES:T_GATES + 8, :] + bg_ref[...], m0, mask_t, last_sel)
    step_seq = (ci == first_seq + (ri >> 3)) & ((ri & (SAMPLE_SEQ - 1)) == 0)
    mout_ref[...] += _exact_dot(r["m_new"], step_seq.astype(BF16))[0:M_HEADS]
    m_out = []
    for h in range(M_HEADS):
        q = proj_ref[:, P_QM + h * M_DIM:P_QM + (h + 1) * M_DIM].astype(BF16)
        kf = proj_ref[:, P_KM + h * M_DIM:P_KM + (h + 1) * M_DIM] * K_SCALE
        k = kf.astype(BF16)
        vt = pt[T_VM + h * M_DIM:T_VM + (h + 1) * M_DIM, :]
        n_h = nin_ref[h]
        r1 = lax.dot_general(jnp.concatenate([k, n_h.astype(BF16)], axis=0), q, NT_DIMS,
                             preferred_element_type=F32)
        st = r1[0:ROWS] * jnp.exp(r["a_masked"][h] + r["bm"][h:h + 1, :])
        q_n = jnp.sum(jnp.where(seq_of_lane, r1[ROWS:ROWS + SAMPLE_GROUP], 0.0), axis=0, keepdims=True)
        g_row = r["gexp"][h:h + 1, :]
        num = _bdot(vt, st) + g_row * qc_ref[h].T
        den = jnp.sum(st, axis=0, keepdims=True) + g_row * q_n
        ht = num * (1.0 / jnp.maximum(jnp.abs(den), r["enm"][h:h + 1, :]))
        hn = ht * lax.rsqrt(jnp.mean(ht * ht, axis=0, keepdims=True) + NORM_EPS) * gmhb_ref[h]
        m_out.append(_sigmoid(proj_ref[:, P_OM + h * M_DIM:P_OM + (h + 1) * M_DIM]) * hn.T
                     * _silu(proj_ref[:, P_ZM + h * M_DIM:P_ZM + (h + 1) * M_DIM]))
        w_row = r["w"][h:h + 1, :]
        dec16 = jnp.sum(jnp.where(lane16 == row16 * SAMPLE_SEQ, r["decay"][h:h + 1, :], 0.0),
                        axis=1, keepdims=True)
        nout_ref[h] = dec16 * n_h + _bdot(jnp.where(seq_of_lane, w_row, 0.0), k)
        decb_ref[h] = jnp.broadcast_to(dec16, (SAMPLE_GROUP, ROWS))
        wv_ref[h] = (vt * w_row).T
        kt_ref[h] = kf.T.astype(BF16)

    for b in range(SAMPLE_GROUP):
        rows = pl.ds(b * SAMPLE_SEQ, SAMPLE_SEQ)
        vct = vct_ref[b]
        for kv in range(ATT_KV):
            vaug = jnp.concatenate([vct[kv * ATT_DIM:(kv + 1) * ATT_DIM, :], ones_rows], axis=0)
            pl_ = jnp.concatenate([sc_ref[kv * ATT_GROUP + g, rows, :] for g in range(ATT_GROUP)], axis=0)
            res = _bdot_nt(pl_, vaug)
            for g in range(ATT_GROUP):
                oc_ref[kv * ATT_GROUP + g, rows, 0:ATT_DIM + ONES_ROWS] = res[g * SAMPLE_SEQ:(g + 1) * SAMPLE_SEQ]
        in_seq = (ri >> 3) == b
        for h in range(M_HEADS):
            upd = jnp.dot(kt_ref[h], jnp.where(in_seq, wv_ref[h], 0.0).astype(BF16), preferred_element_type=F32)
            cout_ref[b, h] = decb_ref[h, b:b + 1, :] * cin_ref[b, h] + upd

    att = []
    for pair in range(ATT_HEADS // 2):
        halves = []
        for hh in (2 * pair, 2 * pair + 1):
            on, oc = o_new[hh], oc_ref[hh]
            den = on[:, ATT_DIM:ATT_DIM + 1] + oc[:, ATT_DIM:ATT_DIM + 1] + esinks[hh]
            halves.append((on[:, 0:ATT_DIM] + oc[:, 0:ATT_DIM]) * (1.0 / den))
        att.append(jnp.concatenate(halves, axis=1))
    a_out = jnp.concatenate(att, axis=1) * _silu(proj_ref[:, P_ZA:P_ZA + 512])
    cat_ref[part_rows, :] = jnp.concatenate([a_out] + m_out, axis=1).astype(BF16)

    @pl.when(part == SAMPLE_PARTS - 1)
    def _():
        y_ref[...] = _out_tail(cat_ref[...], x_ref[...], wout_ref, gpost_ref)


def _sample_call(x, kct, vct, cin, nin, m0, gpre, wmain, wt, bg, sinks, gmhb, wout, gpost):
    nrows = x.shape[0]
    ngroups = nrows // ROWS
    nseq = ngroups * SAMPLE_GROUP
    parts = SAMPLE_PARTS
    assert ngroups % parts == 0
    assert nseq == ROWS, "the per-sequence stabiliser state is handled as one 128-lane row per head"
    full = lambda shape: pl.BlockSpec(shape, lambda i, p: (0,) * len(shape))
    grp = SAMPLE_GROUP
    group = lambda i, p: i * parts + p
    out_shapes = (
        jax.ShapeDtypeStruct((nrows, D_MODEL), F32),
        jax.ShapeDtypeStruct((nseq, 128, ROWS), F32),
        jax.ShapeDtypeStruct((nseq, 128, ROWS), F32),
        jax.ShapeDtypeStruct((nseq, M_HEADS, M_DIM, M_DIM), F32),
        jax.ShapeDtypeStruct((M_HEADS, nseq, M_DIM), F32),
        jax.ShapeDtypeStruct((M_HEADS, nseq), F32),
    )
    return pl.pallas_call(
        _sample_kernel,
        grid=(ngroups // parts, parts),
        in_specs=[
            pl.BlockSpec((parts * ROWS, D_MODEL), lambda i, p: (i, 0)),
            pl.BlockSpec((grp, 128, ROWS), lambda i, p: (group(i, p), 0, 0)),
            pl.BlockSpec((grp, 128, ROWS), lambda i, p: (group(i, p), 0, 0)),
            pl.BlockSpec((grp, M_HEADS, M_DIM, M_DIM), lambda i, p: (group(i, p), 0, 0, 0)),
            pl.BlockSpec((M_HEADS, grp, M_DIM), lambda i, p: (0, group(i, p), 0)),
            full((M_HEADS, nseq)),
            full((1, D_MODEL)), full((D_MODEL, P_MAIN)), full((T_ROWS, D_MODEL)), full((8, ROWS)),
            full((1, ATT_HEADS)), full((M_HEADS, M_DIM, ROWS)), full((D_MODEL, D_MODEL)), full((1, D_MODEL)),
        ],
        out_specs=(
            pl.BlockSpec((parts * ROWS, D_MODEL), lambda i, p: (i, 0)),
            pl.BlockSpec((grp, 128, ROWS), lambda i, p: (group(i, p), 0, 0)),
            pl.BlockSpec((grp, 128, ROWS), lambda i, p: (group(i, p), 0, 0)),
            pl.BlockSpec((grp, M_HEADS, M_DIM, M_DIM), lambda i, p: (group(i, p), 0, 0, 0)),
            pl.BlockSpec((M_HEADS, grp, M_DIM), lambda i, p: (0, group(i, p), 0)),
            full((M_HEADS, nseq)),
        ),
        out_shape=out_shapes,
        scratch_shapes=[
            pltpu.VMEM((parts * ROWS, P_MAIN), F32),
            pltpu.VMEM((parts, T_ROWS, ROWS), F32),
            pltpu.VMEM((parts * ROWS, D_MODEL), BF16),
            pltpu.VMEM((ATT_HEADS, ROWS, ATT_DIM), F32),
            pltpu.VMEM((ATT_HEADS, ROWS, ROWS), F32),
            pltpu.VMEM((ATT_HEADS, ROWS, ROWS), F32),
            pltpu.VMEM((M_HEADS, ROWS, M_DIM), F32),
            pltpu.VMEM((M_HEADS, ROWS, M_DIM), F32),
            pltpu.VMEM((M_HEADS, M_DIM, ROWS), BF16),
            pltpu.VMEM((M_HEADS, SAMPLE_GROUP, ROWS), F32),
            pltpu.VMEM((ATT_HEADS, ROWS, ROWS), F32),
            pltpu.VMEM((ATT_HEADS, ROWS, ROWS), F32),
        ],
        compiler_params=pltpu.CompilerParams(
            dimension_semantics=("arbitrary", "arbitrary"), vmem_limit_bytes=VMEM_LIMIT_BYTES),
        name="sample_layer",
    )(x, kct, vct, cin, nin, m0, gpre, wmain, wt, bg, sinks, gmhb, wout, gpost)


PROMPT_BLOCK = 512
PROMPT_SUBBLOCKS = 2
PROJ_COL_STEP = 512
CHUNK_STAGES = 7
TAIL_PIECES = 3


def _window_in(cache):
    nseq = cache.shape[0]
    return cache.transpose(0, 2, 3, 1).reshape(nseq, ATT_KV * ATT_DIM, ROWS)


def _window_out(win_t):
    nseq = win_t.shape[0]
    return win_t.reshape(nseq, ATT_KV, ATT_DIM, ROWS).transpose(0, 3, 1, 2)[None]


def kernel(x_prompt, x_sample, cache_win_k, cache_win_v, state_C, state_n, state_m,
           g_pre, w_in, b_gate, attn_sinks, g_mh, w_out, g_post):
    depth = g_pre.shape[0]
    assert depth == 1, "single-layer trunk"
    nseq, sseq, _ = x_sample.shape
    assert sseq == SAMPLE_SEQ and nseq % SAMPLE_GROUP == 0

    gpre = g_pre[0].reshape(1, D_MODEL)
    gpost = g_post[0].reshape(1, D_MODEL)
    wmain, wt, wout, gmhb, bg = _weights_call(w_in[0].T, w_out[0], g_mh[0], b_gate)
    sinks = attn_sinks[0].reshape(1, ATT_HEADS)

    yp, wkp, wvp, cp, np_, mp = _prompt_call(x_prompt, gpre, wmain, wt, bg, sinks, gmhb, wout, gpost, PROMPT_BLOCK)

    ys, wks, wvs, cs, ns, ms = _sample_call(
        x_sample.reshape(nseq * sseq, D_MODEL), _window_in(cache_win_k[0]), _window_in(cache_win_v[0]),
        state_C[0], state_n[0].transpose(1, 0, 2), state_m[0].T,
        gpre, wmain, wt, bg, sinks, gmhb, wout, gpost)

    return (yp, ys.reshape(nseq, sseq, D_MODEL), _window_out(wkp), _window_out(wvp),
            cp[None], np_[None], mp.T[None],
            _window_out(wks), _window_out(wvs), cs[None], ns.transpose(1, 0, 2)[None], ms.T[None])
```

```python
import functools

import jax
import jax.numpy as jnp
from jax import lax
from jax.experimental import pallas as pl
from jax.experimental.pallas import tpu as pltpu

F32 = jnp.float32
BF16 = jnp.bfloat16

D_MODEL = 1024
ROWS = 128
ATT_HEADS, ATT_KV, ATT_GROUP, ATT_DIM = 8, 2, 4, 64
M_HEADS, M_DIM = 4, 128
NORM_EPS = 1e-6
NEG_BIG = -1e30
ATT_SCALE = ATT_DIM ** -0.5
LOG2E = 1.4426950408889634
K_SCALE = M_DIM ** -0.5

VM, GATES = 2304, 3840

P_QA, P_ZA, P_QM, P_KM, P_OM, P_ZM, P_KA, P_VA = 0, 512, 1024, 1536, 2048, 2560, 3072, 3200
P_MAIN = 3328
T_VM, T_GATES = 0, 512
T_ROWS = 528
T_SPLIT = 272
STATE_ROWS = 144
ONES_ROWS = 16

SAMPLE_SEQ = 8
SAMPLE_GROUP = ROWS // SAMPLE_SEQ
SAMPLE_PARTS = 2

VMEM_BYTES_V7X = 64 * 1024 * 1024
VMEM_LIMIT_BYTES = VMEM_BYTES_V7X * 7 // 8
NT_DIMS = (((1,), (1,)), ((), ()))


def _rms(x, g):
    return x * lax.rsqrt(jnp.mean(x * x, axis=-1, keepdims=True) + NORM_EPS) * g


def _sigmoid(x):
    return 0.5 + 0.5 * jnp.tanh(0.5 * x)


def _silu(x):
    h = 0.5 * x
    return h + h * jnp.tanh(h)


def _log_sigmoid(x):
    return -(jnp.maximum(-x, 0.0) + jnp.log1p(jnp.exp(-jnp.abs(x))))


def _slope(head):
    return 2.0 ** -(head + 1)


def _bdot(a, b):
    return jnp.dot(a.astype(BF16), b.astype(BF16), preferred_element_type=F32)


def _bdot_nt(a, b):
    return lax.dot_general(a.astype(BF16), b.astype(BF16), NT_DIMS, preferred_element_type=F32)


def _exact_dot(x, m):
    hi = x.astype(BF16).astype(F32)
    mid = (x - hi).astype(BF16).astype(F32)
    lo = (x - hi - mid).astype(BF16).astype(F32)
    parts = jnp.dot(jnp.concatenate([hi, mid, lo, jnp.zeros_like(hi)], axis=0).astype(BF16), m,
                    preferred_element_type=F32)
    return parts[0:8] + parts[8:16] + parts[16:24]


def _gate_rows(x, m0, mask_t, last_sel):
    row = lax.broadcasted_iota(jnp.int32, (8, ROWS), 0)
    head_rows = row < M_HEADS
    ic = jnp.where(head_rows, x, 0.0)
    fc = jnp.where(head_rows, _log_sigmoid(pltpu.roll(x, M_HEADS, 0)), 0.0)
    b = _exact_dot(fc, mask_t.astype(BF16))
    a = ic - b
    a_cols = jnp.concatenate([a, jnp.zeros((ROWS - 8, ROWS), F32)], axis=0).T
    a_masked = [jnp.where(mask_t, a_cols[:, h:h + 1], -jnp.inf) for h in range(M_HEADS)]
    cm = jnp.concatenate([jnp.max(am, axis=0, keepdims=True) for am in a_masked]
                         + [jnp.zeros((8 - M_HEADS, ROWS), F32)], axis=0)
    m_t = jnp.maximum(b + m0, b + cm)
    if last_sel is None:
        b_last = jnp.broadcast_to(b[:, ROWS - 1:ROWS], b.shape)
        m_new = jnp.broadcast_to(m_t[:, ROWS - 1:ROWS], b.shape)
    else:
        both = _exact_dot(jnp.where(head_rows, b, pltpu.roll(m_t, M_HEADS, 0)), last_sel)
        b_last = jnp.where(head_rows, both, 0.0)
        m_new = jnp.where(head_rows, pltpu.roll(both, M_HEADS, 0), 0.0)
    return dict(a_masked=a_masked, bm=b - m_t, gexp=jnp.exp(b + m0 - m_t), enm=jnp.exp(-m_t), m_new=m_new,
                w=jnp.exp(b_last - b + ic - m_new), decay=jnp.exp(b_last + m0 - m_new))


def _out_tail(cat, x, wout_ref, gpost_ref):
    y = jnp.dot(cat, wout_ref[...], preferred_element_type=F32)
    return x + _rms(y, gpost_ref[...])


W_BLOCK = 256
MAIN_BLOCKS = P_MAIN // W_BLOCK


def _main_src_block(i):
    return jnp.where(i < 2, i, jnp.where(i < 8, i + 1, jnp.where(i < 12, i + 3, 2)))


def _weights_kernel(main_ref, vm0_ref, vm1_ref, vm2_ref, vm3_ref, gates_ref, wout_ref, gmh_ref, bgate_ref,
                    wmain_ref, wt_ref, woutb_ref, gmhb_ref, bgb_ref):
    wmain_ref[...] = main_ref[...].T.astype(BF16)

    @pl.when(pl.program_id(0) == 0)
    def _():
        wt_ref[...] = jnp.concatenate(
            [vm0_ref[...], vm1_ref[...], vm2_ref[...], vm3_ref[...], gates_ref[...],
             jnp.zeros((T_ROWS - T_GATES - 8, D_MODEL), F32)], axis=0).astype(BF16)
        woutb_ref[...] = wout_ref[...].astype(BF16)
        for h in range(M_HEADS):
            gmhb_ref[h] = jnp.broadcast_to(gmh_ref[h:h + 1, :], (M_DIM, ROWS)).T
        bg_row = jnp.concatenate([bgate_ref[...], jnp.zeros((1, ROWS - 2 * M_HEADS), F32)], axis=1)
        bgb_ref[...] = jnp.broadcast_to(bg_row, (ROWS, ROWS)).T[0:2 * M_HEADS, :]


def _weights_call(w_in_t, w_out, gmh, bgate):
    blk = lambda r: pl.BlockSpec((128, D_MODEL), lambda i, r=r: (r, 0))
    const = lambda shape: pl.BlockSpec(shape, lambda i: (0,) * len(shape))
    return pl.pallas_call(
        _weights_kernel,
        grid=(MAIN_BLOCKS,),
        in_specs=[
            pl.BlockSpec((W_BLOCK, D_MODEL), lambda i: (_main_src_block(i), 0)),
            blk(VM // 128), blk(VM // 128 + 1), blk(VM // 128 + 2), blk(VM // 128 + 3),
            pl.BlockSpec((8, D_MODEL), lambda i: (GATES // 8, 0)),
            const((D_MODEL, D_MODEL)), const((M_HEADS, M_DIM)), const((1, 2 * M_HEADS)),
        ],
        out_specs=(
            pl.BlockSpec((D_MODEL, W_BLOCK), lambda i: (0, i)),
            const((T_ROWS, D_MODEL)), const((D_MODEL, D_MODEL)),
            const((M_HEADS, M_DIM, ROWS)), const((2 * M_HEADS, ROWS)),
        ),
        out_shape=(
            jax.ShapeDtypeStruct((D_MODEL, P_MAIN), BF16),
            jax.ShapeDtypeStruct((T_ROWS, D_MODEL), BF16),
            jax.ShapeDtypeStruct((D_MODEL, D_MODEL), BF16),
            jax.ShapeDtypeStruct((M_HEADS, M_DIM, ROWS), F32),
            jax.ShapeDtypeStruct((2 * M_HEADS, ROWS), F32),
        ),
        compiler_params=pltpu.CompilerParams(
            dimension_semantics=("arbitrary",), vmem_limit_bytes=VMEM_LIMIT_BYTES),
        name="layer_weights",
    )(w_in_t, w_in_t, w_in_t, w_in_t, w_in_t, w_in_t, w_out, gmh, bgate)


def _prompt_bias_t(kv, first):
    j = lax.broadcasted_iota(jnp.int32, (2 * ROWS, ROWS), 0)
    i = lax.broadcasted_iota(jnp.int32, (2 * ROWS, ROWS), 1)
    diff = ROWS + i - j
    valid = (diff >= 0) & (diff < ROWS)
    if first:
        valid = valid & (j >= ROWS)
    dfl = diff.astype(F32)
    return jnp.concatenate(
        [jnp.where(valid, -_slope(kv * ATT_GROUP + g) * dfl, NEG_BIG) * LOG2E for g in range(ATT_GROUP)], axis=1)


def _prompt_kernel(x_ref, xnext_ref, gpre_ref, wmain_ref, wt_ref, bg_ref, sink_ref, gmhb_ref, wout_ref, gpost_ref,
                   y_ref, wk_ref, wv_ref, c_ref, n_ref, m_ref,
                   proj_ref, projt_ref, cat_ref, xn_ref, kprev_ref, vtprev_ref, ct_ref, mst_ref, bias_ref,
                   *, tb, nt):
    bi = pl.program_id(0)
    j = pl.program_id(1)
    nchunks = tb // ROWS

    @pl.when((bi == 0) & (j == 0))
    def _():
        for first in range(2):
            for kv in range(ATT_KV):
                bias_ref[first * ATT_KV + kv] = _prompt_bias_t(kv, first)
        m_ref[...] = jnp.zeros_like(m_ref)

    @pl.when(j == 0)
    def _():
        kprev_ref[...] = jnp.zeros_like(kprev_ref)
        vtprev_ref[...] = jnp.zeros_like(vtprev_ref)
        ct_ref[...] = jnp.zeros_like(ct_ref)
        mst_ref[...] = jnp.zeros_like(mst_ref)

    sub = tb // PROMPT_SUBBLOCKS
    sub_chunks = sub // ROWS

    def norm_rows(sb):
        rows_sb = slice(sb * sub, (sb + 1) * sub)
        xn_ref[rows_sb, :] = _rms(x_ref[0, rows_sb, :], gpre_ref[...]).astype(BF16)

    def proj_cols(sb, c0, c1):
        rows_sb = slice(sb * sub, (sb + 1) * sub)
        proj_ref[rows_sb, c0:c1] = jnp.dot(xn_ref[rows_sb, :], wmain_ref[:, c0:c1], preferred_element_type=F32)

    def proj_t(sb):
        for r0, r1 in ((0, T_SPLIT), (T_SPLIT, T_ROWS)):
            pt = lax.dot_general(wt_ref[r0:r1, :], xn_ref[sb * sub:(sb + 1) * sub, :], NT_DIMS,
                                 preferred_element_type=F32)
            for c in range(sub_chunks):
                projt_ref[sb * sub_chunks + c, r0:r1, :] = pt[:, c * ROWS:(c + 1) * ROWS]

    def out_rows(sb):
        rows_sb = slice(sb * sub, (sb + 1) * sub)
        y_ref[0, rows_sb, :] = _out_tail(cat_ref[rows_sb, :], x_ref[0, rows_sb, :], wout_ref, gpost_ref)

    def proj_pieces(sb):
        bounds = list(range(0, P_MAIN, PROJ_COL_STEP)) + [P_MAIN]
        pieces = [functools.partial(proj_cols, sb, c0, c1) for c0, c1 in zip(bounds[:-1], bounds[1:])]
        return pieces[:2] + [functools.partial(proj_t, sb)] + pieces[2:]

    ri = lax.broadcasted_iota(jnp.int32, (ROWS, ROWS), 0)
    ci = lax.broadcasted_iota(jnp.int32, (ROWS, ROWS), 1)
    mask_t = ri <= ci
    row8 = lax.broadcasted_iota(jnp.int32, (8, ROWS), 0)
    tail_row = lax.broadcasted_iota(jnp.int32, (STATE_ROWS - M_DIM, 2 * ROWS), 0)
    tail_lane = lax.broadcasted_iota(jnp.int32, (STATE_ROWS - M_DIM, 2 * ROWS), 1)

    def chunk(c):
        rows = pl.ds(c * ROWS, ROWS)
        first = ((j == 0) & (c == 0)).astype(jnp.int32)
        yield

        qa = proj_ref[rows, P_QA:P_QA + 512] * (ATT_SCALE * LOG2E)
        kcur = proj_ref[rows, P_KA:P_KA + 128].astype(BF16)
        vtcur = proj_ref[rows, P_VA:P_VA + 128].T.astype(BF16)
        kcat = jnp.concatenate([kprev_ref[...], kcur], axis=0)
        vtcat = jnp.concatenate([vtprev_ref[...], vtcur], axis=1)
        r = _gate_rows(projt_ref[c, T_GATES:T_GATES + 8, :] + bg_ref[...], mst_ref[...], mask_t, None)

        scores, sinks = [], []
        for kv in range(ATT_KV):
            want_hi = kv == 1
            keep = (ci >= ATT_DIM) if want_hi else (ci < ATT_DIM)
            pieces = []
            for g in range(ATT_GROUP):
                hh = kv * ATT_GROUP + g
                blk = qa[:, (hh // 2) * 128:(hh // 2 + 1) * 128]
                if (hh % 2 == 1) != want_hi:
                    blk = pltpu.roll(blk, ATT_DIM, 1)
                pieces.append(jnp.where(keep, blk, 0.0))
            q4 = jnp.concatenate(pieces, axis=0)
            scores.append(_bdot_nt(kcat, q4) + bias_ref[first * ATT_KV + kv])
            sinks.append(jnp.concatenate(
                [jnp.broadcast_to(sink_ref[0:1, kv * ATT_GROUP + g:kv * ATT_GROUP + g + 1] * LOG2E, (1, ROWS))
                 for g in range(ATT_GROUP)], axis=1))
        yield
        w_cols = jnp.concatenate([r["w"], jnp.zeros((ROWS - 8, ROWS), F32)], axis=0).T
        zero_blk = jnp.zeros((M_DIM, M_DIM), F32)
        q_ts, kfs, raws = [], [], []
        for pair in range(M_HEADS // 2):
            q_pair = [proj_ref[rows, P_QM + h * M_DIM:P_QM + (h + 1) * M_DIM].T for h in (2 * pair, 2 * pair + 1)]
            k_pair = proj_ref[rows, P_KM + 2 * pair * M_DIM:P_KM + 2 * (pair + 1) * M_DIM] * K_SCALE
            q_diag = jnp.concatenate([jnp.concatenate([q_pair[0], zero_blk], axis=1),
                                      jnp.concatenate([zero_blk, q_pair[1]], axis=1)], axis=0)
            raw = _bdot(k_pair, q_diag)
            q_ts += q_pair
            kfs += [k_pair[:, 0:M_DIM], k_pair[:, M_DIM:2 * M_DIM]]
            raws += [raw[:, 0:ROWS], raw[:, ROWS:2 * ROWS]]
        lhs_nums, rhs_lows, sts = [], [], []
        for h in range(M_HEADS):
            vt = projt_ref[c, T_VM + h * M_DIM:T_VM + (h + 1) * M_DIM, :]
            ct = ct_ref[h]
            sts.append((raws[h], (kfs[h] * w_cols[:, h:h + 1]).astype(BF16)))
            rhs_lows.append(jnp.concatenate([(q_ts[h] * r["gexp"][h:h + 1, :]).astype(BF16),
                                             jnp.zeros((M_DIM, M_DIM), BF16)], axis=1))
            tail = jnp.concatenate([jnp.zeros((STATE_ROWS - M_DIM, ROWS), F32), ct[M_DIM:STATE_ROWS]], axis=1)
            tail = jnp.where((tail_row == 1) & (tail_lane < ROWS), 1.0, tail)
            lhs_nums.append(jnp.concatenate([jnp.concatenate([vt, ct[0:M_DIM]], axis=1), tail], axis=0).astype(BF16))
        mst_ref[...] = r["m_new"]
        yield

        outs, nums, dens = [], [], []
        for kv in range(ATT_KV):
            s, sink = scores[kv], sinks[kv]
            mx = jnp.maximum(jnp.max(s, axis=0, keepdims=True), sink)
            p = jnp.exp2(s - mx).astype(BF16)
            lhs = jnp.concatenate([vtcat[kv * ATT_DIM:(kv + 1) * ATT_DIM, :],
                                   jnp.ones((ONES_ROWS, 2 * ROWS), BF16)], axis=0)
            o = jnp.dot(lhs, p, preferred_element_type=F32)
            outs.append((o, jnp.exp2(sink - mx)))
        yield
        for h in range(M_HEADS):
            raw, wk = sts[h]
            st = raw * jnp.exp(r["a_masked"][h] + r["bm"][h:h + 1, :])
            rhs = jnp.concatenate([jnp.concatenate([st.astype(BF16), wk], axis=1), rhs_lows[h]], axis=0)
            res = jnp.dot(lhs_nums[h], rhs, preferred_element_type=F32)
            nums.append(res[0:M_DIM, 0:ROWS])
            dens.append(jnp.sum(st, axis=0, keepdims=True) + res[M_DIM:M_DIM + 1, 0:ROWS])
            dec = r["decay"][h:h + 1, 0:1]
            ct_ref[h, 0:M_DIM, :] = dec * ct_ref[h, 0:M_DIM, :] + res[0:M_DIM, ROWS:2 * ROWS]
            ct_ref[h, M_DIM:M_DIM + 1, :] = dec * ct_ref[h, M_DIM:M_DIM + 1, :] + res[M_DIM + 1:M_DIM + 2, ROWS:2 * ROWS]
        yield

        att = []
        for kv in range(ATT_KV):
            o, esink = outs[kv]
            on = o[0:ATT_DIM, :] * (1.0 / (o[ATT_DIM:ATT_DIM + 1, :] + esink))
            for pair in range(2):
                two = jnp.concatenate([on[:, (2 * pair) * ROWS:(2 * pair + 1) * ROWS],
                                       on[:, (2 * pair + 1) * ROWS:(2 * pair + 2) * ROWS]], axis=0)
                att.append(two.T)
        a_out = jnp.concatenate(att, axis=1) * _silu(proj_ref[rows, P_ZA:P_ZA + 512])
        kprev_ref[...] = kcur
        vtprev_ref[...] = vtcur
        yield
        m_out = []
        for h in range(M_HEADS):
            ht = nums[h] * (1.0 / jnp.maximum(jnp.abs(dens[h]), r["enm"][h:h + 1, :]))
            hn = ht * lax.rsqrt(jnp.mean(ht * ht, axis=0, keepdims=True) + NORM_EPS) * gmhb_ref[h]
            m_out.append(_sigmoid(proj_ref[rows, P_OM + h * M_DIM:P_OM + (h + 1) * M_DIM]) * hn.T
                         * _silu(proj_ref[rows, P_ZM + h * M_DIM:P_ZM + (h + 1) * M_DIM]))

        cat_ref[rows, :] = jnp.concatenate([a_out] + m_out, axis=1).astype(BF16)
        yield

    @pl.when((bi == 0) & (j == 0))
    def _():
        for sb in range(PROMPT_SUBBLOCKS):
            norm_rows(sb)
        for piece in proj_pieces(0):
            piece()

    def norm_next():
        xn_ref[...] = _rms(xnext_ref[0], gpre_ref[...]).astype(BF16)

    def run_chunks(sb, fillers):
        n_fill, n_slots, slot = len(fillers), sub_chunks * CHUNK_STAGES, 0
        for c in range(sb * sub_chunks, (sb + 1) * sub_chunks):
            for _ in chunk(c):
                for _ in range(-(-(slot + 1) * n_fill // n_slots) + (-slot * n_fill // n_slots)):
                    fillers.pop(0)()
                slot += 1
        assert slot == n_slots and not fillers

    assert PROMPT_SUBBLOCKS == 2
    run_chunks(0, proj_pieces(1))
    nxt = proj_pieces(0)
    run_chunks(1, [functools.partial(out_rows, 0), norm_next] + nxt[:-TAIL_PIECES])
    out_rows(1)
    for piece in nxt[-TAIL_PIECES:]:
        piece()

    @pl.when(j == nt - 1)
    def _():
        wk_ref[0] = proj_ref[tb - ROWS:tb, P_KA:P_KA + 128].T
        wv_ref[0] = proj_ref[tb - ROWS:tb, P_VA:P_VA + 128].T
        for h in range(M_HEADS):
            ct = ct_ref[h]
            c_ref[0, h] = ct[0:M_DIM].T
            n_ref[0, h:h + 1, :] = ct[M_DIM:M_DIM + 1]
        seq_lane = lax.broadcasted_iota(jnp.int32, m_ref.shape, 1) == bi
        m_ref[...] = jnp.where(seq_lane, mst_ref[0:M_HEADS, 0:m_ref.shape[1]], m_ref[...])


def _prompt_call(x, gpre, wmain, wt, bg, sinks, gmhb, wout, gpost, tb):
    bsz, seq, _ = x.shape
    nt = seq // tb
    full = lambda shape: pl.BlockSpec(shape, lambda b, j: (0,) * len(shape))

    def next_block(b, j):
        wrap = (j + 1 == nt).astype(jnp.int32)
        return (jnp.minimum(b + wrap, bsz - 1), (j + 1) * (1 - wrap), 0)

    out_shapes = (
        jax.ShapeDtypeStruct((bsz, seq, D_MODEL), F32),
        jax.ShapeDtypeStruct((bsz, 128, ROWS), F32),
        jax.ShapeDtypeStruct((bsz, 128, ROWS), F32),
        jax.ShapeDtypeStruct((bsz, M_HEADS, M_DIM, M_DIM), F32),
        jax.ShapeDtypeStruct((bsz, M_HEADS, M_DIM), F32),
        jax.ShapeDtypeStruct((M_HEADS, bsz), F32),
    )
    return pl.pallas_call(
        functools.partial(_prompt_kernel, tb=tb, nt=nt),
        grid=(bsz, nt),
        in_specs=[
            pl.BlockSpec((1, tb, D_MODEL), lambda b, j: (b, j, 0)),
            pl.BlockSpec((1, tb, D_MODEL), next_block),
            full((1, D_MODEL)), full((D_MODEL, P_MAIN)), full((T_ROWS, D_MODEL)), full((8, ROWS)),
            full((1, ATT_HEADS)), full((M_HEADS, M_DIM, ROWS)), full((D_MODEL, D_MODEL)), full((1, D_MODEL)),
        ],
        out_specs=(
            pl.BlockSpec((1, tb, D_MODEL), lambda b, j: (b, j, 0)),
            pl.BlockSpec((1, 128, ROWS), lambda b, j: (b, 0, 0)),
            pl.BlockSpec((1, 128, ROWS), lambda b, j: (b, 0, 0)),
            pl.BlockSpec((1, M_HEADS, M_DIM, M_DIM), lambda b, j: (b, 0, 0, 0)),
            pl.BlockSpec((1, M_HEADS, M_DIM), lambda b, j: (b, 0, 0)),
            pl.BlockSpec((M_HEADS, bsz), lambda b, j: (0, 0)),
        ),
        out_shape=out_shapes,
        scratch_shapes=[
            pltpu.VMEM((tb, P_MAIN), F32),
            pltpu.VMEM((tb // ROWS, T_ROWS, ROWS), F32),
            pltpu.VMEM((tb, D_MODEL), BF16),
            pltpu.VMEM((tb, D_MODEL), BF16),
            pltpu.VMEM((ROWS, 128), BF16),
            pltpu.VMEM((128, ROWS), BF16),
            pltpu.VMEM((M_HEADS, STATE_ROWS, M_DIM), F32),
            pltpu.VMEM((8, ROWS), F32),
            pltpu.VMEM((2 * ATT_KV, 2 * ROWS, ATT_GROUP * ROWS), F32),
        ],
        compiler_params=pltpu.CompilerParams(
            dimension_semantics=("arbitrary", "arbitrary"), vmem_limit_bytes=VMEM_LIMIT_BYTES),
        name="prompt_layer",
    )(x, x, gpre, wmain, wt, bg, sinks, gmhb, wout, gpost)


def _sample_bias_new(head):
    r = lax.broadcasted_iota(jnp.int32, (ROWS, ROWS), 0)
    c = lax.broadcasted_iota(jnp.int32, (ROWS, ROWS), 1)
    valid = ((r >> 3) == (c >> 3)) & (r >= c)
    return jnp.where(valid, -_slope(head) * (r - c).astype(F32), NEG_BIG)


def _sample_bias_cache(head):
    r = lax.broadcasted_iota(jnp.int32, (ROWS, ROWS), 0)
    c = lax.broadcasted_iota(jnp.int32, (ROWS, ROWS), 1)
    diff = (r & (SAMPLE_SEQ - 1)) + ROWS - c
    return jnp.where(diff < ROWS, -_slope(head) * diff.astype(F32), NEG_BIG)


def _sample_kernel(x_ref, kct_ref, vct_ref, cin_ref, nin_ref, m0_ref,
                   gpre_ref, wmain_ref, wt_ref, bg_ref, sink_ref, gmhb_ref, wout_ref, gpost_ref,
                   y_ref, kot_ref, vot_ref, cout_ref, nout_ref, mout_ref,
                   projfull_ref, projt_ref, cat_ref, qh_ref, sc_ref, oc_ref, qc_ref, wv_ref, kt_ref,
                   decb_ref, bn_ref, bc_ref):
    step, part = pl.program_id(0), pl.program_id(1)

    @pl.when((step == 0) & (part == 0))
    def _():
        for hh in range(ATT_HEADS):
            bn_ref[hh] = _sample_bias_new(hh)
            bc_ref[hh] = _sample_bias_cache(hh)
        mout_ref[...] = jnp.zeros_like(mout_ref)

    @pl.when(part == 0)
    def _():
        xn = _rms(x_ref[...], gpre_ref[...]).astype(BF16)
        projfull_ref[...] = jnp.dot(xn, wmain_ref[...], preferred_element_type=F32)
        ptf = lax.dot_general(wt_ref[...], xn, NT_DIMS, preferred_element_type=F32)
        for g in range(SAMPLE_PARTS):
            projt_ref[g] = ptf[:, g * ROWS:(g + 1) * ROWS]

    part_rows = pl.ds(pl.multiple_of(part * ROWS, ROWS), ROWS)
    proj_ref = projfull_ref.at[part_rows]
    pt = projt_ref[part]

    ri = lax.broadcasted_iota(jnp.int32, (ROWS, ROWS), 0)
    ci = lax.broadcasted_iota(jnp.int32, (ROWS, ROWS), 1)
    same_seq = (ri >> 3) == (ci >> 3)
    mask_t = same_seq & (ri <= ci)
    last_sel = (same_seq & ((ri & (SAMPLE_SEQ - 1)) == SAMPLE_SEQ - 1)).astype(BF16)
    row16 = lax.broadcasted_iota(jnp.int32, (SAMPLE_GROUP, ROWS), 0)
    lane16 = lax.broadcasted_iota(jnp.int32, (SAMPLE_GROUP, ROWS), 1)
    seq_of_lane = (lane16 >> 3) == row16
    ones_rows = jnp.ones((ONES_ROWS, ROWS), F32)

    qa = proj_ref[:, P_QA:P_QA + 512] * ATT_SCALE
    for hh in range(ATT_HEADS):
        blk = qa[:, (hh // 2) * 128:(hh // 2 + 1) * 128]
        if hh % 2 == 1:
            blk = pltpu.roll(blk, ATT_DIM, 1)
        qh_ref[hh] = blk[:, 0:ATT_DIM]
    ka = proj_ref[:, P_KA:P_KA + 128]
    kat = ka.T
    vat = proj_ref[:, P_VA:P_VA + 128].T

    keep_new = ci >= ROWS - SAMPLE_SEQ

    for b in range(SAMPLE_GROUP):
        rows = pl.ds(b * SAMPLE_SEQ, SAMPLE_SEQ)
        kct = kct_ref[b]
        shift = (ROWS - SAMPLE_SEQ - b * SAMPLE_SEQ) % ROWS
        new_k, new_v = (kat, vat) if shift == 0 else (pltpu.roll(kat, shift, 1), pltpu.roll(vat, shift, 1))
        kot_ref[b] = jnp.where(keep_new, new_k, pltpu.roll(kct, ROWS - SAMPLE_SEQ, 1))
        vot_ref[b] = jnp.where(keep_new, new_v, pltpu.roll(vct_ref[b], ROWS - SAMPLE_SEQ, 1))
        for kv in range(ATT_KV):
            lhs = jnp.concatenate([qh_ref[kv * ATT_GROUP + g, rows, :] for g in range(ATT_GROUP)], axis=0)
            res = _bdot(lhs, kct[kv * ATT_DIM:(kv + 1) * ATT_DIM, :])
            for g in range(ATT_GROUP):
                sc_ref[kv * ATT_GROUP + g, rows, :] = res[g * SAMPLE_SEQ:(g + 1) * SAMPLE_SEQ]
        for h in range(M_HEADS):
            qc_ref[h, rows, :] = _bdot(proj_ref[rows, P_QM + h * M_DIM:P_QM + (h + 1) * M_DIM], cin_ref[b, h])

    o_new, esinks = [], []
    for hh in range(ATT_HEADS):
        kv = hh // ATT_GROUP
        s_n = _bdot(qh_ref[hh], kat[kv * ATT_DIM:(kv + 1) * ATT_DIM, :]) + bn_ref[hh]
        s_c = sc_ref[hh] + bc_ref[hh]
        sink = sink_ref[0:1, hh:hh + 1]
        mx = jnp.maximum(jnp.max(jnp.maximum(s_n, s_c), axis=-1, keepdims=True), sink)
        mx_b = jnp.broadcast_to(mx, (ROWS, ROWS))
        p_n = jnp.exp(s_n - mx_b)
        sc_ref[hh] = jnp.exp(s_c - mx_b)
        vaug = jnp.concatenate([vat[kv * ATT_DIM:(kv + 1) * ATT_DIM, :], ones_rows], axis=0)
        o_new.append(_bdot_nt(p_n, vaug))
        esinks.append(jnp.exp(sink - mx))

    first_seq = (step * SAMPLE_PARTS + part) * SAMPLE_GROUP
    seq_step = ri == first_seq + (ci >> 3)
    m0 = _exact_dot(jnp.concatenate([m0_ref[...], jnp.zeros((8 - M_HEADS, ROWS), F32)], axis=0),
                    seq_step.astype(BF16))
    r = _gate_rows(pt[T_GATES:T_GATES + 8, :] + bg_ref[...], m0, mask_t, last_sel)
    step_seq = (ci == first_seq + (ri >> 3)) & ((ri & (SAMPLE_SEQ - 1)) == 0)
    mout_ref[...] += _exact_dot(r["m_new"], step_seq.astype(BF16))[0:M_HEADS]
    m_out = []
    for h in range(M_HEADS):
        q = proj_ref[:, P_QM + h * M_DIM:P_QM + (h + 1) * M_DIM].astype(BF16)
        kf = proj_ref[:, P_KM + h * M_DIM:P_KM + (h + 1) * M_DIM] * K_SCALE
        k = kf.astype(BF16)
        vt = pt[T_VM + h * M_DIM:T_VM + (h + 1) * M_DIM, :]
        n_h = nin_ref[h]
        r1 = lax.dot_general(jnp.concatenate([k, n_h.astype(BF16)], axis=0), q, NT_DIMS,
                             preferred_element_type=F32)
        st = r1[0:ROWS] * jnp.exp(r["a_masked"][h] + r["bm"][h:h + 1, :])
        q_n = jnp.sum(jnp.where(seq_of_lane, r1[ROWS:ROWS + SAMPLE_GROUP], 0.0), axis=0, keepdims=True)
        g_row = r["gexp"][h:h + 1, :]
        num = _bdot(vt, st) + g_row * qc_ref[h].T
        den = jnp.sum(st, axis=0, keepdims=True) + g_row * q_n
        ht = num * (1.0 / jnp.maximum(jnp.abs(den), r["enm"][h:h + 1, :]))
        hn = ht * lax.rsqrt(jnp.mean(ht * ht, axis=0, keepdims=True) + NORM_EPS) * gmhb_ref[h]
        m_out.append(_sigmoid(proj_ref[:, P_OM + h * M_DIM:P_OM + (h + 1) * M_DIM]) * hn.T
                     * _silu(proj_ref[:, P_ZM + h * M_DIM:P_ZM + (h + 1) * M_DIM]))
        w_row = r["w"][h:h + 1, :]
        dec16 = jnp.sum(jnp.where(lane16 == row16 * SAMPLE_SEQ, r["decay"][h:h + 1, :], 0.0),
                        axis=1, keepdims=True)
        nout_ref[h] = dec16 * n_h + _bdot(jnp.where(seq_of_lane, w_row, 0.0), k)
        decb_ref[h] = jnp.broadcast_to(dec16, (SAMPLE_GROUP, ROWS))
        wv_ref[h] = (vt * w_row).T
        kt_ref[h] = kf.T.astype(BF16)

    for b in range(SAMPLE_GROUP):
        rows = pl.ds(b * SAMPLE_SEQ, SAMPLE_SEQ)
        vct = vct_ref[b]
        for kv in range(ATT_KV):
            vaug = jnp.concatenate([vct[kv * ATT_DIM:(kv + 1) * ATT_DIM, :], ones_rows], axis=0)
            pl_ = jnp.concatenate([sc_ref[kv * ATT_GROUP + g, rows, :] for g in range(ATT_GROUP)], axis=0)
            res = _bdot_nt(pl_, vaug)
            for g in range(ATT_GROUP):
                oc_ref[kv * ATT_GROUP + g, rows, 0:ATT_DIM + ONES_ROWS] = res[g * SAMPLE_SEQ:(g + 1) * SAMPLE_SEQ]
        in_seq = (ri >> 3) == b
        for h in range(M_HEADS):
            upd = jnp.dot(kt_ref[h], jnp.where(in_seq, wv_ref[h], 0.0).astype(BF16), preferred_element_type=F32)
            cout_ref[b, h] = decb_ref[h, b:b + 1, :] * cin_ref[b, h] + upd

    att = []
    for pair in range(ATT_HEADS // 2):
        halves = []
        for hh in (2 * pair, 2 * pair + 1):
            on, oc = o_new[hh], oc_ref[hh]
            den = on[:, ATT_DIM:ATT_DIM + 1] + oc[:, ATT_DIM:ATT_DIM + 1] + esinks[hh]
            halves.append((on[:, 0:ATT_DIM] + oc[:, 0:ATT_DIM]) * (1.0 / den))
        att.append(jnp.concatenate(halves, axis=1))
    a_out = jnp.concatenate(att, axis=1) * _silu(proj_ref[:, P_ZA:P_ZA + 512])
    cat_ref[part_rows, :] = jnp.concatenate([a_out] + m_out, axis=1).astype(BF16)

    @pl.when(part == SAMPLE_PARTS - 1)
    def _():
        y_ref[...] = _out_tail(cat_ref[...], x_ref[...], wout_ref, gpost_ref)


def _sample_call(x, kct, vct, cin, nin, m0, gpre, wmain, wt, bg, sinks, gmhb, wout, gpost):
    nrows = x.shape[0]
    ngroups = nrows // ROWS
    nseq = ngroups * SAMPLE_GROUP
    parts = SAMPLE_PARTS
    assert ngroups % parts == 0
    assert nseq == ROWS, "the per-sequence stabiliser state is handled as one 128-lane row per head"
    full = lambda shape: pl.BlockSpec(shape, lambda i, p: (0,) * len(shape))
    grp = SAMPLE_GROUP
    group = lambda i, p: i * parts + p
    out_shapes = (
        jax.ShapeDtypeStruct((nrows, D_MODEL), F32),
        jax.ShapeDtypeStruct((nseq, 128, ROWS), F32),
        jax.ShapeDtypeStruct((nseq, 128, ROWS), F32),
        jax.ShapeDtypeStruct((nseq, M_HEADS, M_DIM, M_DIM), F32),
        jax.ShapeDtypeStruct((M_HEADS, nseq, M_DIM), F32),
        jax.ShapeDtypeStruct((M_HEADS, nseq), F32),
    )
    return pl.pallas_call(
        _sample_kernel,
        grid=(ngroups // parts, parts),
        in_specs=[
            pl.BlockSpec((parts * ROWS, D_MODEL), lambda i, p: (i, 0)),
            pl.BlockSpec((grp, 128, ROWS), lambda i, p: (group(i, p), 0, 0)),
            pl.BlockSpec((grp, 128, ROWS), lambda i, p: (group(i, p), 0, 0)),
            pl.BlockSpec((grp, M_HEADS, M_DIM, M_DIM), lambda i, p: (group(i, p), 0, 0, 0)),
            pl.BlockSpec((M_HEADS, grp, M_DIM), lambda i, p: (0, group(i, p), 0)),
            full((M_HEADS, nseq)),
            full((1, D_MODEL)), full((D_MODEL, P_MAIN)), full((T_ROWS, D_MODEL)), full((8, ROWS)),
            full((1, ATT_HEADS)), full((M_HEADS, M_DIM, ROWS)), full((D_MODEL, D_MODEL)), full((1, D_MODEL)),
        ],
        out_specs=(
            pl.BlockSpec((parts * ROWS, D_MODEL), lambda i, p: (i, 0)),
            pl.BlockSpec((grp, 128, ROWS), lambda i, p: (group(i, p), 0, 0)),
            pl.BlockSpec((grp, 128, ROWS), lambda i, p: (group(i, p), 0, 0)),
            pl.BlockSpec((grp, M_HEADS, M_DIM, M_DIM), lambda i, p: (group(i, p), 0, 0, 0)),
            pl.BlockSpec((M_HEADS, grp, M_DIM), lambda i, p: (0, group(i, p), 0)),
            full((M_HEADS, nseq)),
        ),
        out_shape=out_shapes,
        scratch_shapes=[
            pltpu.VMEM((parts * ROWS, P_MAIN), F32),
            pltpu.VMEM((parts, T_ROWS, ROWS), F32),
            pltpu.VMEM((parts * ROWS, D_MODEL), BF16),
            pltpu.VMEM((ATT_HEADS, ROWS, ATT_DIM), F32),
            pltpu.VMEM((ATT_HEADS, ROWS, ROWS), F32),
            pltpu.VMEM((ATT_HEADS, ROWS, ROWS), F32),
            pltpu.VMEM((M_HEADS, ROWS, M_DIM), F32),
            pltpu.VMEM((M_HEADS, ROWS, M_DIM), F32),
            pltpu.VMEM((M_HEADS, M_DIM, ROWS), BF16),
            pltpu.VMEM((M_HEADS, SAMPLE_GROUP, ROWS), F32),
            pltpu.VMEM((ATT_HEADS, ROWS, ROWS), F32),
            pltpu.VMEM((ATT_HEADS, ROWS, ROWS), F32),
        ],
        compiler_params=pltpu.CompilerParams(
            dimension_semantics=("arbitrary", "arbitrary"), vmem_limit_bytes=VMEM_LIMIT_BYTES),
        name="sample_layer",
    )(x, kct, vct, cin, nin, m0, gpre, wmain, wt, bg, sinks, gmhb, wout, gpost)


PROMPT_BLOCK = 512
PROMPT_SUBBLOCKS = 2
PROJ_COL_STEP = 512
CHUNK_STAGES = 7
TAIL_PIECES = 3


def _window_in(cache):
    nseq = cache.shape[0]
    return cache.transpose(0, 2, 3, 1).reshape(nseq, ATT_KV * ATT_DIM, ROWS)


def _window_out(win_t):
    nseq = win_t.shape[0]
    return win_t.reshape(nseq, ATT_KV, ATT_DIM, ROWS).transpose(0, 3, 1, 2)[None]


def kernel(x_prompt, x_sample, cache_win_k, cache_win_v, state_C, state_n, state_m,
           g_pre, w_in, b_gate, attn_sinks, g_mh, w_out, g_post):
    depth = g_pre.shape[0]
    assert depth == 1, "single-layer trunk"
    nseq, sseq, _ = x_sample.shape
    assert sseq == SAMPLE_SEQ and nseq % SAMPLE_GROUP == 0

    gpre = g_pre[0].reshape(1, D_MODEL)
    gpost = g_post[0].reshape(1, D_MODEL)
    wmain, wt, wout, gmhb, bg = _weights_call(w_in[0].T, w_out[0], g_mh[0], b_gate)
    sinks = attn_sinks[0].reshape(1, ATT_HEADS)

    yp, wkp, wvp, cp, np_, mp = _prompt_call(x_prompt, gpre, wmain, wt, bg, sinks, gmhb, wout, gpost, PROMPT_BLOCK)

    ys, wks, wvs, cs, ns, ms = _sample_call(
        x_sample.reshape(nseq * sseq, D_MODEL), _window_in(cache_win_k[0]), _window_in(cache_win_v[0]),
        state_C[0], state_n[0].transpose(1, 0, 2), state_m[0].T,
        gpre, wmain, wt, bg, sinks, gmhb, wout, gpost)

    return (yp, ys.reshape(nseq, sseq, D_MODEL), _window_out(wkp), _window_out(wvp),
            cp[None], np_[None], mp.T[None],
            _window_out(wks), _window_out(wvs), cs[None], ns.transpose(1, 0, 2)[None], ms.T[None])
```

```python
import functools

import jax
import jax.numpy as jnp
from jax import lax
from jax.experimental import pallas as pl
from jax.experimental.pallas import tpu as pltpu

F32 = jnp.float32
BF16 = jnp.bfloat16

D_MODEL = 1024
ROWS = 128
ATT_HEADS, ATT_KV, ATT_GROUP, ATT_DIM = 8, 2, 4, 64
M_HEADS, M_DIM = 4, 128
NORM_EPS = 1e-6
NEG_BIG = -1e30
ATT_SCALE = ATT_DIM ** -0.5
LOG2E = 1.4426950408889634
K_SCALE = M_DIM ** -0.5

VM, GATES = 2304, 3840

P_QA, P_ZA, P_QM, P_KM, P_OM, P_ZM, P_KA, P_VA = 0, 512, 1024, 1536, 2048, 2560, 3072, 3200
P_MAIN = 3328
T_VM, T_GATES = 0, 512
T_ROWS = 528
T_SPLIT = 272
STATE_ROWS = 144
ONES_ROWS = 16

SAMPLE_SEQ = 8
SAMPLE_GROUP = ROWS // SAMPLE_SEQ
SAMPLE_PARTS = 2

VMEM_BYTES_V7X = 64 * 1024 * 1024
VMEM_LIMIT_BYTES = VMEM_BYTES_V7X * 7 // 8
NT_DIMS = (((1,), (1,)), ((), ()))


def _rms(x, g):
    return x * lax.rsqrt(jnp.mean(x * x, axis=-1, keepdims=True) + NORM_EPS) * g


def _sigmoid(x):
    return 0.5 + 0.5 * jnp.tanh(0.5 * x)


def _silu(x):
    h = 0.5 * x
    return h + h * jnp.tanh(h)


def _log_sigmoid(x):
    return -(jnp.maximum(-x, 0.0) + jnp.log1p(jnp.exp(-jnp.abs(x))))


def _slope(head):
    return 2.0 ** -(head + 1)


def _bdot(a, b):
    return jnp.dot(a.astype(BF16), b.astype(BF16), preferred_element_type=F32)


def _bdot_nt(a, b):
    return lax.dot_general(a.astype(BF16), b.astype(BF16), NT_DIMS, preferred_element_type=F32)


def _exact_dot(x, m):
    hi = x.astype(BF16).astype(F32)
    mid = (x - hi).astype(BF16).astype(F32)
    lo = (x - hi - mid).astype(BF16).astype(F32)
    parts = jnp.dot(jnp.concatenate([hi, mid, lo, jnp.zeros_like(hi)], axis=0).astype(BF16), m,
                    preferred_element_type=F32)
    return parts[0:8] + parts[8:16] + parts[16:24]


def _gate_rows(x, m0, mask_t, last_sel):
    row = lax.broadcasted_iota(jnp.int32, (8, ROWS), 0)
    head_rows = row < M_HEADS
    ic = jnp.where(head_rows, x, 0.0)
    fc = jnp.where(head_rows, _log_sigmoid(pltpu.roll(x, M_HEADS, 0)), 0.0)
    b = _exact_dot(fc, mask_t.astype(BF16))
    a = ic - b
    a_cols = jnp.concatenate([a, jnp.zeros((ROWS - 8, ROWS), F32)], axis=0).T
    a_masked = [jnp.where(mask_t, a_cols[:, h:h + 1], -jnp.inf) for h in range(M_HEADS)]
    cm = jnp.concatenate([jnp.max(am, axis=0, keepdims=True) for am in a_masked]
                         + [jnp.zeros((8 - M_HEADS, ROWS), F32)], axis=0)
    m_t = jnp.maximum(b + m0, b + cm)
    if last_sel is None:
        b_last = jnp.broadcast_to(b[:, ROWS - 1:ROWS], b.shape)
        m_new = jnp.broadcast_to(m_t[:, ROWS - 1:ROWS], b.shape)
    else:
        both = _exact_dot(jnp.where(head_rows, b, pltpu.roll(m_t, M_HEADS, 0)), last_sel)
        b_last = jnp.where(head_rows, both, 0.0)
        m_new = jnp.where(head_rows, pltpu.roll(both, M_HEADS, 0), 0.0)
    return dict(a_masked=a_masked, bm=b - m_t, gexp=jnp.exp(b + m0 - m_t), enm=jnp.exp(-m_t), m_new=m_new,
                w=jnp.exp(b_last - b + ic - m_new), decay=jnp.exp(b_last + m0 - m_new))


def _out_tail(cat, x, wout_ref, gpost_ref):
    y = jnp.dot(cat, wout_ref[...], preferred_element_type=F32)
    return x + _rms(y, gpost_ref[...])


W_BLOCK = 256
MAIN_BLOCKS = P_MAIN // W_BLOCK


def _main_src_block(i):
    return jnp.where(i < 2, i, jnp.where(i < 8, i + 1, jnp.where(i < 12, i + 3, 2)))


def _weights_kernel(main_ref, vm0_ref, vm1_ref, vm2_ref, vm3_ref, gates_ref, wout_ref, gmh_ref, bgate_ref,
                    wmain_ref, wt_ref, woutb_ref, gmhb_ref, bgb_ref):
    wmain_ref[...] = main_ref[...].T.astype(BF16)

    @pl.when(pl.program_id(0) == 0)
    def _():
        wt_ref[...] = jnp.concatenate(
            [vm0_ref[...], vm1_ref[...], vm2_ref[...], vm3_ref[...], gates_ref[...],
             jnp.zeros((T_ROWS - T_GATES - 8, D_MODEL), F32)], axis=0).astype(BF16)
        woutb_ref[...] = wout_ref[...].astype(BF16)
        for h in range(M_HEADS):
            gmhb_ref[h] = jnp.broadcast_to(gmh_ref[h:h + 1, :], (M_DIM, ROWS)).T
        bg_row = jnp.concatenate([bgate_ref[...], jnp.zeros((1, ROWS - 2 * M_HEADS), F32)], axis=1)
        bgb_ref[...] = jnp.broadcast_to(bg_row, (ROWS, ROWS)).T[0:2 * M_HEADS, :]


def _weights_call(w_in_t, w_out, gmh, bgate):
    blk = lambda r: pl.BlockSpec((128, D_MODEL), lambda i, r=r: (r, 0))
    const = lambda shape: pl.BlockSpec(shape, lambda i: (0,) * len(shape))
    return pl.pallas_call(
        _weights_kernel,
        grid=(MAIN_BLOCKS,),
        in_specs=[
            pl.BlockSpec((W_BLOCK, D_MODEL), lambda i: (_main_src_block(i), 0)),
            blk(VM // 128), blk(VM // 128 + 1), blk(VM // 128 + 2), blk(VM // 128 + 3),
            pl.BlockSpec((8, D_MODEL), lambda i: (GATES // 8, 0)),
            const((D_MODEL, D_MODEL)), const((M_HEADS, M_DIM)), const((1, 2 * M_HEADS)),
        ],
        out_specs=(
            pl.BlockSpec((D_MODEL, W_BLOCK), lambda i: (0, i)),
            const((T_ROWS, D_MODEL)), const((D_MODEL, D_MODEL)),
            const((M_HEADS, M_DIM, ROWS)), const((2 * M_HEADS, ROWS)),
        ),
        out_shape=(
            jax.ShapeDtypeStruct((D_MODEL, P_MAIN), BF16),
            jax.ShapeDtypeStruct((T_ROWS, D_MODEL), BF16),
            jax.ShapeDtypeStruct((D_MODEL, D_MODEL), BF16),
            jax.ShapeDtypeStruct((M_HEADS, M_DIM, ROWS), F32),
            jax.ShapeDtypeStruct((2 * M_HEADS, ROWS), F32),
        ),
        compiler_params=pltpu.CompilerParams(
            dimension_semantics=("arbitrary",), vmem_limit_bytes=VMEM_LIMIT_BYTES),
        name="layer_weights",
    )(w_in_t, w_in_t, w_in_t, w_in_t, w_in_t, w_in_t, w_out, gmh, bgate)


def _prompt_bias_t(kv, first):
    j = lax.broadcasted_iota(jnp.int32, (2 * ROWS, ROWS), 0)
    i = lax.broadcasted_iota(jnp.int32, (2 * ROWS, ROWS), 1)
    diff = ROWS + i - j
    valid = (diff >= 0) & (diff < ROWS)
    if first:
        valid = valid & (j >= ROWS)
    dfl = diff.astype(F32)
    return jnp.concatenate(
        [jnp.where(valid, -_slope(kv * ATT_GROUP + g) * dfl, NEG_BIG) * LOG2E for g in range(ATT_GROUP)], axis=1)


def _prompt_kernel(x_ref, xnext_ref, gpre_ref, wmain_ref, wt_ref, bg_ref, sink_ref, gmhb_ref, wout_ref, gpost_ref,
                   y_ref, wk_ref, wv_ref, c_ref, n_ref, m_ref,
                   proj_ref, projt_ref, cat_ref, xn_ref, kprev_ref, vtprev_ref, ct_ref, mst_ref, bias_ref,
                   *, tb, nt):
    bi = pl.program_id(0)
    j = pl.program_id(1)
    nchunks = tb // ROWS

    @pl.when((bi == 0) & (j == 0))
    def _():
        for first in range(2):
            for kv in range(ATT_KV):
                bias_ref[first * ATT_KV + kv] = _prompt_bias_t(kv, first)
        m_ref[...] = jnp.zeros_like(m_ref)

    @pl.when(j == 0)
    def _():
        kprev_ref[...] = jnp.zeros_like(kprev_ref)
        vtprev_ref[...] = jnp.zeros_like(vtprev_ref)
        ct_ref[...] = jnp.zeros_like(ct_ref)
        mst_ref[...] = jnp.zeros_like(mst_ref)

    sub = tb // PROMPT_SUBBLOCKS
    sub_chunks = sub // ROWS

    def norm_rows(sb):
        rows_sb = slice(sb * sub, (sb + 1) * sub)
        xn_ref[rows_sb, :] = _rms(x_ref[0, rows_sb, :], gpre_ref[...]).astype(BF16)

    def proj_cols(sb, c0, c1):
        rows_sb = slice(sb * sub, (sb + 1) * sub)
        proj_ref[rows_sb, c0:c1] = jnp.dot(xn_ref[rows_sb, :], wmain_ref[:, c0:c1], preferred_element_type=F32)

    def proj_t(sb):
        for r0, r1 in ((0, T_SPLIT), (T_SPLIT, T_ROWS)):
            pt = lax.dot_general(wt_ref[r0:r1, :], xn_ref[sb * sub:(sb + 1) * sub, :], NT_DIMS,
                                 preferred_element_type=F32)
            for c in range(sub_chunks):
                projt_ref[sb * sub_chunks + c, r0:r1, :] = pt[:, c * ROWS:(c + 1) * ROWS]

    def out_rows(sb):
        rows_sb = slice(sb * sub, (sb + 1) * sub)
        y_ref[0, rows_sb, :] = _out_tail(cat_ref[rows_sb, :], x_ref[0, rows_sb, :], wout_ref, gpost_ref)

    def proj_pieces(sb):
        bounds = list(range(0, P_MAIN, PROJ_COL_STEP)) + [P_MAIN]
        pieces = [functools.partial(proj_cols, sb, c0, c1) for c0, c1 in zip(bounds[:-1], bounds[1:])]
        return pieces[:2] + [functools.partial(proj_t, sb)] + pieces[2:]

    ri = lax.broadcasted_iota(jnp.int32, (ROWS, ROWS), 0)
    ci = lax.broadcasted_iota(jnp.int32, (ROWS, ROWS), 1)
    mask_t = ri <= ci
    tail_row = lax.broadcasted_iota(jnp.int32, (STATE_ROWS - M_DIM, 2 * ROWS), 0)
    tail_lane = lax.broadcasted_iota(jnp.int32, (STATE_ROWS - M_DIM, 2 * ROWS), 1)

    def chunk(c):
        rows = pl.ds(c * ROWS, ROWS)
        first = ((j == 0) & (c == 0)).astype(jnp.int32)
        yield

        qa = proj_ref[rows, P_QA:P_QA + 512] * (ATT_SCALE * LOG2E)
        kcur = proj_ref[rows, P_KA:P_KA + 128].astype(BF16)
        vtcur = proj_ref[rows, P_VA:P_VA + 128].T.astype(BF16)
        kcat = jnp.concatenate([kprev_ref[...], kcur], axis=0)
        vtcat = jnp.concatenate([vtprev_ref[...], vtcur], axis=1)
        r = _gate_rows(projt_ref[c, T_GATES:T_GATES + 8, :] + bg_ref[...], mst_ref[...], mask_t, None)

        scores, sinks = [], []
        for kv in range(ATT_KV):
            want_hi = kv == 1
            keep = (ci >= ATT_DIM) if want_hi else (ci < ATT_DIM)
            pieces = []
            for g in range(ATT_GROUP):
                hh = kv * ATT_GROUP + g
                blk = qa[:, (hh // 2) * 128:(hh // 2 + 1) * 128]
                if (hh % 2 == 1) != want_hi:
                    blk = pltpu.roll(blk, ATT_DIM, 1)
                pieces.append(jnp.where(keep, blk, 0.0))
            q4 = jnp.concatenate(pieces, axis=0)
            scores.append(_bdot_nt(kcat, q4) + bias_ref[first * ATT_KV + kv])
            sinks.append(jnp.concatenate(
                [jnp.broadcast_to(sink_ref[0:1, kv * ATT_GROUP + g:kv * ATT_GROUP + g + 1] * LOG2E, (1, ROWS))
                 for g in range(ATT_GROUP)], axis=1))
        yield
        w_cols = jnp.concatenate([r["w"], jnp.zeros((ROWS - 8, ROWS), F32)], axis=0).T
        zero_blk = jnp.zeros((M_DIM, M_DIM), BF16)
        q_ts, kfs, raws = [], [], []
        for pair in range(M_HEADS // 2):
            q_pair = [proj_ref[rows, P_QM + h * M_DIM:P_QM + (h + 1) * M_DIM].T for h in (2 * pair, 2 * pair + 1)]
            k_pair = proj_ref[rows, P_KM + 2 * pair * M_DIM:P_KM + 2 * (pair + 1) * M_DIM] * K_SCALE
            q_diag = jnp.concatenate([jnp.concatenate([q_pair[0].astype(BF16), zero_blk], axis=1),
                                      jnp.concatenate([zero_blk, q_pair[1].astype(BF16)], axis=1)], axis=0)
            raw = _bdot(k_pair, q_diag)
            q_ts += q_pair
            kfs += [k_pair[:, 0:M_DIM], k_pair[:, M_DIM:2 * M_DIM]]
            raws += [raw[:, 0:ROWS], raw[:, ROWS:2 * ROWS]]
        lhs_nums, rhs_lows, sts = [], [], []
        for h in range(M_HEADS):
            vt = projt_ref[c, T_VM + h * M_DIM:T_VM + (h + 1) * M_DIM, :]
            ct = ct_ref[h]
            sts.append((raws[h], (kfs[h] * w_cols[:, h:h + 1]).astype(BF16)))
            rhs_lows.append(jnp.concatenate([(q_ts[h] * r["gexp"][h:h + 1, :]).astype(BF16),
                                             jnp.zeros((M_DIM, M_DIM), BF16)], axis=1))
            tail = jnp.concatenate([jnp.zeros((STATE_ROWS - M_DIM, ROWS), F32), ct[M_DIM:STATE_ROWS]], axis=1)
            tail = jnp.where((tail_row == 1) & (tail_lane < ROWS), 1.0, tail)
            lhs_nums.append(jnp.concatenate([jnp.concatenate([vt, ct[0:M_DIM]], axis=1), tail], axis=0).astype(BF16))
        mst_ref[...] = r["m_new"]
        yield

        outs, nums, dens = [], [], []
        for kv in range(ATT_KV):
            s, sink = scores[kv], sinks[kv]
            mx = jnp.maximum(jnp.max(s, axis=0, keepdims=True), sink)
            p = jnp.exp2(s - mx).astype(BF16)
            lhs = jnp.concatenate([vtcat[kv * ATT_DIM:(kv + 1) * ATT_DIM, :],
                                   jnp.ones((ONES_ROWS, 2 * ROWS), BF16)], axis=0)
            o = jnp.dot(lhs, p, preferred_element_type=F32)
            outs.append((o, jnp.exp2(sink - mx)))
        yield
        for h in range(M_HEADS):
            raw, wk = sts[h]
            st = raw * jnp.exp(r["a_masked"][h] + r["bm"][h:h + 1, :])
            rhs = jnp.concatenate([jnp.concatenate([st.astype(BF16), wk], axis=1), rhs_lows[h]], axis=0)
            res = jnp.dot(lhs_nums[h], rhs, preferred_element_type=F32)
            nums.append(res[0:M_DIM, 0:ROWS])
            dens.append(jnp.sum(st, axis=0, keepdims=True) + res[M_DIM:M_DIM + 1, 0:ROWS])
            dec = r["decay"][h:h + 1, 0:1]
            ct_ref[h, 0:M_DIM, :] = dec * ct_ref[h, 0:M_DIM, :] + res[0:M_DIM, ROWS:2 * ROWS]
            ct_ref[h, M_DIM:M_DIM + 1, :] = dec * ct_ref[h, M_DIM:M_DIM + 1, :] + res[M_DIM + 1:M_DIM + 2, ROWS:2 * ROWS]
        yield

        att = []
        for kv in range(ATT_KV):
            o, esink = outs[kv]
            on = o[0:ATT_DIM, :] * (1.0 / (o[ATT_DIM:ATT_DIM + 1, :] + esink))
            for pair in range(2):
                two = jnp.concatenate([on[:, (2 * pair) * ROWS:(2 * pair + 1) * ROWS],
                                       on[:, (2 * pair + 1) * ROWS:(2 * pair + 2) * ROWS]], axis=0)
                att.append(two.T)
        a_out = jnp.concatenate(att, axis=1) * _silu(proj_ref[rows, P_ZA:P_ZA + 512])
        kprev_ref[...] = kcur
        vtprev_ref[...] = vtcur
        yield
        m_out = []
        for h in range(M_HEADS):
            ht = nums[h] * (1.0 / jnp.maximum(jnp.abs(dens[h]), r["enm"][h:h + 1, :]))
            hn = ht * lax.rsqrt(jnp.mean(ht * ht, axis=0, keepdims=True) + NORM_EPS) * gmhb_ref[h]
            m_out.append(_sigmoid(proj_ref[rows, P_OM + h * M_DIM:P_OM + (h + 1) * M_DIM]) * hn.T
                         * _silu(proj_ref[rows, P_ZM + h * M_DIM:P_ZM + (h + 1) * M_DIM]))

        cat_ref[rows, :] = jnp.concatenate([a_out] + m_out, axis=1).astype(BF16)
        yield

    @pl.when((bi == 0) & (j == 0))
    def _():
        for sb in range(PROMPT_SUBBLOCKS):
            norm_rows(sb)
        for piece in proj_pieces(0):
            piece()

    def norm_next():
        xn_ref[...] = _rms(xnext_ref[0], gpre_ref[...]).astype(BF16)

    def run_chunks(sb, fillers):
        n_fill, n_slots, slot = len(fillers), sub_chunks * CHUNK_STAGES, 0
        for c in range(sb * sub_chunks, (sb + 1) * sub_chunks):
            for _ in chunk(c):
                for _ in range(-(-(slot + 1) * n_fill // n_slots) + (-slot * n_fill // n_slots)):
                    fillers.pop(0)()
                slot += 1
        assert slot == n_slots and not fillers

    assert PROMPT_SUBBLOCKS == 2
    run_chunks(0, proj_pieces(1))
    nxt = proj_pieces(0)
    run_chunks(1, [functools.partial(out_rows, 0), norm_next] + nxt[:-TAIL_PIECES])
    out_rows(1)
    for piece in nxt[-TAIL_PIECES:]:
        piece()

    @pl.when(j == nt - 1)
    def _():
        wk_ref[0] = proj_ref[tb - ROWS:tb, P_KA:P_KA + 128].T
        wv_ref[0] = proj_ref[tb - ROWS:tb, P_VA:P_VA + 128].T
        for h in range(M_HEADS):
            ct = ct_ref[h]
            c_ref[0, h] = ct[0:M_DIM].T
            n_ref[0, h:h + 1, :] = ct[M_DIM:M_DIM + 1]
        seq_lane = lax.broadcasted_iota(jnp.int32, m_ref.shape, 1) == bi
        m_ref[...] = jnp.where(seq_lane, mst_ref[0:M_HEADS, 0:m_ref.shape[1]], m_ref[...])


def _prompt_call(x, gpre, wmain, wt, bg, sinks, gmhb, wout, gpost, tb):
    bsz, seq, _ = x.shape
    nt = seq // tb
    full = lambda shape: pl.BlockSpec(shape, lambda b, j: (0,) * len(shape))

    def next_block(b, j):
        wrap = (j + 1 == nt).astype(jnp.int32)
        return (jnp.minimum(b + wrap, bsz - 1), (j + 1) * (1 - wrap), 0)

    out_shapes = (
        jax.ShapeDtypeStruct((bsz, seq, D_MODEL), F32),
        jax.ShapeDtypeStruct((bsz, 128, ROWS), F32),
        jax.ShapeDtypeStruct((bsz, 128, ROWS), F32),
        jax.ShapeDtypeStruct((bsz, M_HEADS, M_DIM, M_DIM), F32),
        jax.ShapeDtypeStruct((bsz, M_HEADS, M_DIM), F32),
        jax.ShapeDtypeStruct((M_HEADS, bsz), F32),
    )
    return pl.pallas_call(
        functools.partial(_prompt_kernel, tb=tb, nt=nt),
        grid=(bsz, nt),
        in_specs=[
            pl.BlockSpec((1, tb, D_MODEL), lambda b, j: (b, j, 0)),
            pl.BlockSpec((1, tb, D_MODEL), next_block),
            full((1, D_MODEL)), full((D_MODEL, P_MAIN)), full((T_ROWS, D_MODEL)), full((8, ROWS)),
            full((1, ATT_HEADS)), full((M_HEADS, M_DIM, ROWS)), full((D_MODEL, D_MODEL)), full((1, D_MODEL)),
        ],
        out_specs=(
            pl.BlockSpec((1, tb, D_MODEL), lambda b, j: (b, j, 0)),
            pl.BlockSpec((1, 128, ROWS), lambda b, j: (b, 0, 0)),
            pl.BlockSpec((1, 128, ROWS), lambda b, j: (b, 0, 0)),
            pl.BlockSpec((1, M_HEADS, M_DIM, M_DIM), lambda b, j: (b, 0, 0, 0)),
            pl.BlockSpec((1, M_HEADS, M_DIM), lambda b, j: (b, 0, 0)),
            pl.BlockSpec((M_HEADS, bsz), lambda b, j: (0, 0)),
        ),
        out_shape=out_shapes,
        scratch_shapes=[
            pltpu.VMEM((tb, P_MAIN), F32),
            pltpu.VMEM((tb // ROWS, T_ROWS, ROWS), F32),
            pltpu.VMEM((tb, D_MODEL), BF16),
            pltpu.VMEM((tb, D_MODEL), BF16),
            pltpu.VMEM((ROWS, 128), BF16),
            pltpu.VMEM((128, ROWS), BF16),
            pltpu.VMEM((M_HEADS, STATE_ROWS, M_DIM), F32),
            pltpu.VMEM((8, ROWS), F32),
            pltpu.VMEM((2 * ATT_KV, 2 * ROWS, ATT_GROUP * ROWS), F32),
        ],
        compiler_params=pltpu.CompilerParams(
            dimension_semantics=("arbitrary", "arbitrary"), vmem_limit_bytes=VMEM_LIMIT_BYTES),
        name="prompt_layer",
    )(x, x, gpre, wmain, wt, bg, sinks, gmhb, wout, gpost)


def _sample_bias_new(head):
    r = lax.broadcasted_iota(jnp.int32, (ROWS, ROWS), 0)
    c = lax.broadcasted_iota(jnp.int32, (ROWS, ROWS), 1)
    valid = ((r >> 3) == (c >> 3)) & (r >= c)
    return jnp.where(valid, -_slope(head) * (r - c).astype(F32), NEG_BIG)


def _sample_bias_cache(head):
    r = lax.broadcasted_iota(jnp.int32, (ROWS, ROWS), 0)
    c = lax.broadcasted_iota(jnp.int32, (ROWS, ROWS), 1)
    diff = (r & (SAMPLE_SEQ - 1)) + ROWS - c
    return jnp.where(diff < ROWS, -_slope(head) * diff.astype(F32), NEG_BIG)


def _sample_kernel(x_ref, kct_ref, vct_ref, cin_ref, nin_ref, m0_ref,
                   gpre_ref, wmain_ref, wt_ref, bg_ref, sink_ref, gmhb_ref, wout_ref, gpost_ref,
                   y_ref, kot_ref, vot_ref, cout_ref, nout_ref, mout_ref,
                   projfull_ref, projt_ref, cat_ref, qh_ref, sc_ref, oc_ref, qc_ref, wv_ref, kt_ref,
                   decb_ref, bn_ref, bc_ref):
    step, part = pl.program_id(0), pl.program_id(1)

    @pl.when((step == 0) & (part == 0))
    def _():
        for hh in range(ATT_HEADS):
            bn_ref[hh] = _sample_bias_new(hh)
            bc_ref[hh] = _sample_bias_cache(hh)
        mout_ref[...] = jnp.zeros_like(mout_ref)

    @pl.when(part == 0)
    def _():
        xn = _rms(x_ref[...], gpre_ref[...]).astype(BF16)
        projfull_ref[...] = jnp.dot(xn, wmain_ref[...], preferred_element_type=F32)
        ptf = lax.dot_general(wt_ref[...], xn, NT_DIMS, preferred_element_type=F32)
        for g in range(SAMPLE_PARTS):
            projt_ref[g] = ptf[:, g * ROWS:(g + 1) * ROWS]

    part_rows = pl.ds(pl.multiple_of(part * ROWS, ROWS), ROWS)
    proj_ref = projfull_ref.at[part_rows]
    pt = projt_ref[part]

    ri = lax.broadcasted_iota(jnp.int32, (ROWS, ROWS), 0)
    ci = lax.broadcasted_iota(jnp.int32, (ROWS, ROWS), 1)
    same_seq = (ri >> 3) == (ci >> 3)
    mask_t = same_seq & (ri <= ci)
    last_sel = (same_seq & ((ri & (SAMPLE_SEQ - 1)) == SAMPLE_SEQ - 1)).astype(BF16)
    row16 = lax.broadcasted_iota(jnp.int32, (SAMPLE_GROUP, ROWS), 0)
    lane16 = lax.broadcasted_iota(jnp.int32, (SAMPLE_GROUP, ROWS), 1)
    seq_of_lane = (lane16 >> 3) == row16
    ones_rows = jnp.ones((ONES_ROWS, ROWS), F32)

    qa = proj_ref[:, P_QA:P_QA + 512] * ATT_SCALE
    for hh in range(ATT_HEADS):
        blk = qa[:, (hh // 2) * 128:(hh // 2 + 1) * 128]
        if hh % 2 == 1:
            blk = pltpu.roll(blk, ATT_DIM, 1)
        qh_ref[hh] = blk[:, 0:ATT_DIM]
    ka = proj_ref[:, P_KA:P_KA + 128]
    kat = ka.T
    vat = proj_ref[:, P_VA:P_VA + 128].T

    keep_new = ci >= ROWS - SAMPLE_SEQ

    for b in range(SAMPLE_GROUP):
        rows = pl.ds(b * SAMPLE_SEQ, SAMPLE_SEQ)
        kct = kct_ref[b]
        shift = (ROWS - SAMPLE_SEQ - b * SAMPLE_SEQ) % ROWS
        new_k, new_v = (kat, vat) if shift == 0 else (pltpu.roll(kat, shift, 1), pltpu.roll(vat, shift, 1))
        kot_ref[b] = jnp.where(keep_new, new_k, pltpu.roll(kct, ROWS - SAMPLE_SEQ, 1))
        vot_ref[b] = jnp.where(keep_new, new_v, pltpu.roll(vct_ref[b], ROWS - SAMPLE_SEQ, 1))
        for kv in range(ATT_KV):
            lhs = jnp.concatenate([qh_ref[kv * ATT_GROUP + g, rows, :] for g in range(ATT_GROUP)], axis=0)
            res = _bdot(lhs, kct[kv * ATT_DIM:(kv + 1) * ATT_DIM, :])
            for g in range(ATT_GROUP):
                sc_ref[kv * ATT_GROUP + g, rows, :] = res[g * SAMPLE_SEQ:(g + 1) * SAMPLE_SEQ]
        for h in range(M_HEADS):
            qc_ref[h, rows, :] = _bdot(proj_ref[rows, P_QM + h * M_DIM:P_QM + (h + 1) * M_DIM], cin_ref[b, h])

    o_new, esinks = [], []
    for hh in range(ATT_HEADS):
        kv = hh // ATT_GROUP
        s_n = _bdot(qh_ref[hh], kat[kv * ATT_DIM:(kv + 1) * ATT_DIM, :]) + bn_ref[hh]
        s_c = sc_ref[hh] + bc_ref[hh]
        sink = sink_ref[0:1, hh:hh + 1]
        mx = jnp.maximum(jnp.max(jnp.maximum(s_n, s_c), axis=-1, keepdims=True), sink)
        mx_b = jnp.broadcast_to(mx, (ROWS, ROWS))
        p_n = jnp.exp(s_n - mx_b)
        sc_ref[hh] = jnp.exp(s_c - mx_b)
        vaug = jnp.concatenate([vat[kv * ATT_DIM:(kv + 1) * ATT_DIM, :], ones_rows], axis=0)
        o_new.append(_bdot_nt(p_n, vaug))
        esinks.append(jnp.exp(sink - mx))

    first_seq = (step * SAMPLE_PARTS + part) * SAMPLE_GROUP
    seq_step = ri == first_seq + (ci >> 3)
    m0 = _exact_dot(jnp.concatenate([m0_ref[...], jnp.zeros((8 - M_HEADS, ROWS), F32)], axis=0),
                    seq_step.astype(BF16))
    r = _gate_rows(pt[T_GATES:T_GATES + 8, :] + bg_ref[...], m0, mask_t, last_sel)
    step_seq = (ci == first_seq + (ri >> 3)) & ((ri & (SAMPLE_SEQ - 1)) == 0)
    mout_ref[...] += _exact_dot(r["m_new"], step_seq.astype(BF16))[0:M_HEADS]
    m_out = []
    for h in range(M_HEADS):
        q = proj_ref[:, P_QM + h * M_DIM:P_QM + (h + 1) * M_DIM].astype(BF16)
        kf = proj_ref[:, P_KM + h * M_DIM:P_KM + (h + 1) * M_DIM] * K_SCALE
        k = kf.astype(BF16)
        vt = pt[T_VM + h * M_DIM:T_VM + (h + 1) * M_DIM, :]
        n_h = nin_ref[h]
        r1 = lax.dot_general(jnp.concatenate([k, n_h.astype(BF16)], axis=0), q, NT_DIMS,
                             preferred_element_type=F32)
        st = r1[0:ROWS] * jnp.exp(r["a_masked"][h] + r["bm"][h:h + 1, :])
        q_n = jnp.sum(jnp.where(seq_of_lane, r1[ROWS:ROWS + SAMPLE_GROUP], 0.0), axis=0, keepdims=True)
        g_row = r["gexp"][h:h + 1, :]
        num = _bdot(vt, st) + g_row * qc_ref[h].T
        den = jnp.sum(st, axis=0, keepdims=True) + g_row * q_n
        ht = num * (1.0 / jnp.maximum(jnp.abs(den), r["enm"][h:h + 1, :]))
        hn = ht * lax.rsqrt(jnp.mean(ht * ht, axis=0, keepdims=True) + NORM_EPS) * gmhb_ref[h]
        m_out.append(_sigmoid(proj_ref[:, P_OM + h * M_DIM:P_OM + (h + 1) * M_DIM]) * hn.T
                     * _silu(proj_ref[:, P_ZM + h * M_DIM:P_ZM + (h + 1) * M_DIM]))
        w_row = r["w"][h:h + 1, :]
        dec16 = jnp.sum(jnp.where(lane16 == row16 * SAMPLE_SEQ, r["decay"][h:h + 1, :], 0.0),
                        axis=1, keepdims=True)
        nout_ref[h] = dec16 * n_h + _bdot(jnp.where(seq_of_lane, w_row, 0.0), k)
        decb_ref[h] = jnp.broadcast_to(dec16, (SAMPLE_GROUP, ROWS))
        wv_ref[h] = (vt * w_row).T
        kt_ref[h] = kf.T.astype(BF16)

    for b in range(SAMPLE_GROUP):
        rows = pl.ds(b * SAMPLE_SEQ, SAMPLE_SEQ)
        vct = vct_ref[b]
        for kv in range(ATT_KV):
            vaug = jnp.concatenate([vct[kv * ATT_DIM:(kv + 1) * ATT_DIM, :], ones_rows], axis=0)
            pl_ = jnp.concatenate([sc_ref[kv * ATT_GROUP + g, rows, :] for g in range(ATT_GROUP)], axis=0)
            res = _bdot_nt(pl_, vaug)
            for g in range(ATT_GROUP):
                oc_ref[kv * ATT_GROUP + g, rows, 0:ATT_DIM + ONES_ROWS] = res[g * SAMPLE_SEQ:(g + 1) * SAMPLE_SEQ]
        in_seq = (ri >> 3) == b
        for h in range(M_HEADS):
            upd = jnp.dot(kt_ref[h], jnp.where(in_seq, wv_ref[h], 0.0).astype(BF16), preferred_element_type=F32)
            cout_ref[b, h] = decb_ref[h, b:b + 1, :] * cin_ref[b, h] + upd

    att = []
    for pair in range(ATT_HEADS // 2):
        halves = []
        for hh in (2 * pair, 2 * pair + 1):
            on, oc = o_new[hh], oc_ref[hh]
            den = on[:, ATT_DIM:ATT_DIM + 1] + oc[:, ATT_DIM:ATT_DIM + 1] + esinks[hh]
            halves.append((on[:, 0:ATT_DIM] + oc[:, 0:ATT_DIM]) * (1.0 / den))
        att.append(jnp.concatenate(halves, axis=1))
    a_out = jnp.concatenate(att, axis=1) * _silu(proj_ref[:, P_ZA:P_ZA + 512])
    cat_ref[part_rows, :] = jnp.concatenate([a_out] + m_out, axis=1).astype(BF16)

    @pl.when(part == SAMPLE_PARTS - 1)
    def _():
        y_ref[...] = _out_tail(cat_ref[...], x_ref[...], wout_ref, gpost_ref)


def _sample_call(x, kct, vct, cin, nin, m0, gpre, wmain, wt, bg, sinks, gmhb, wout, gpost):
    nrows = x.shape[0]
    ngroups = nrows // ROWS
    nseq = ngroups * SAMPLE_GROUP
    parts = SAMPLE_PARTS
    assert ngroups % parts == 0
    assert nseq == ROWS, "the per-sequence stabiliser state is handled as one 128-lane row per head"
    full = lambda shape: pl.BlockSpec(shape, lambda i, p: (0,) * len(shape))
    grp = SAMPLE_GROUP
    group = lambda i, p: i * parts + p
    out_shapes = (
        jax.ShapeDtypeStruct((nrows, D_MODEL), F32),
        jax.ShapeDtypeStruct((nseq, 128, ROWS), F32),
        jax.ShapeDtypeStruct((nseq, 128, ROWS), F32),
        jax.ShapeDtypeStruct((nseq, M_HEADS, M_DIM, M_DIM), F32),
        jax.ShapeDtypeStruct((M_HEADS, nseq, M_DIM), F32),
        jax.ShapeDtypeStruct((M_HEADS, nseq), F32),
    )
    return pl.pallas_call(
        _sample_kernel,
        grid=(ngroups // parts, parts),
        in_specs=[
            pl.BlockSpec((parts * ROWS, D_MODEL), lambda i, p: (i, 0)),
            pl.BlockSpec((grp, 128, ROWS), lambda i, p: (group(i, p), 0, 0)),
            pl.BlockSpec((grp, 128, ROWS), lambda i, p: (group(i, p), 0, 0)),
            pl.BlockSpec((grp, M_HEADS, M_DIM, M_DIM), lambda i, p: (group(i, p), 0, 0, 0)),
            pl.BlockSpec((M_HEADS, grp, M_DIM), lambda i, p: (0, group(i, p), 0)),
            full((M_HEADS, nseq)),
            full((1, D_MODEL)), full((D_MODEL, P_MAIN)), full((T_ROWS, D_MODEL)), full((8, ROWS)),
            full((1, ATT_HEADS)), full((M_HEADS, M_DIM, ROWS)), full((D_MODEL, D_MODEL)), full((1, D_MODEL)),
        ],
        out_specs=(
            pl.BlockSpec((parts * ROWS, D_MODEL), lambda i, p: (i, 0)),
            pl.BlockSpec((grp, 128, ROWS), lambda i, p: (group(i, p), 0, 0)),
            pl.BlockSpec((grp, 128, ROWS), lambda i, p: (group(i, p), 0, 0)),
            pl.BlockSpec((grp, M_HEADS, M_DIM, M_DIM), lambda i, p: (group(i, p), 0, 0, 0)),
            pl.BlockSpec((M_HEADS, grp, M_DIM), lambda i, p: (0, group(i, p), 0)),
            full((M_HEADS, nseq)),
        ),
        out_shape=out_shapes,
        scratch_shapes=[
            pltpu.VMEM((parts * ROWS, P_MAIN), F32),
            pltpu.VMEM((parts, T_ROWS, ROWS), F32),
            pltpu.VMEM((parts * ROWS, D_MODEL), BF16),
            pltpu.VMEM((ATT_HEADS, ROWS, ATT_DIM), F32),
            pltpu.VMEM((ATT_HEADS, ROWS, ROWS), F32),
            pltpu.VMEM((ATT_HEADS, ROWS, ROWS), F32),
            pltpu.VMEM((M_HEADS, ROWS, M_DIM), F32),
            pltpu.VMEM((M_HEADS, ROWS, M_DIM), F32),
            pltpu.VMEM((M_HEADS, M_DIM, ROWS), BF16),
            pltpu.VMEM((M_HEADS, SAMPLE_GROUP, ROWS), F32),
            pltpu.VMEM((ATT_HEADS, ROWS, ROWS), F32),
            pltpu.VMEM((ATT_HEADS, ROWS, ROWS), F32),
        ],
        compiler_params=pltpu.CompilerParams(
            dimension_semantics=("arbitrary", "arbitrary"), vmem_limit_bytes=VMEM_LIMIT_BYTES),
        name="sample_layer",
    )(x, kct, vct, cin, nin, m0, gpre, wmain, wt, bg, sinks, gmhb, wout, gpost)


PROMPT_BLOCK = 512
PROMPT_SUBBLOCKS = 2
PROJ_COL_STEP = 512
CHUNK_STAGES = 7
TAIL_PIECES = 3


def _window_in(cache):
    nseq = cache.shape[0]
    return cache.transpose(0, 2, 3, 1).reshape(nseq, ATT_KV * ATT_DIM, ROWS)


def _window_out(win_t):
    nseq = win_t.shape[0]
    return win_t.reshape(nseq, ATT_KV, ATT_DIM, ROWS).transpose(0, 3, 1, 2)[None]


def kernel(x_prompt, x_sample, cache_win_k, cache_win_v, state_C, state_n, state_m,
           g_pre, w_in, b_gate, attn_sinks, g_mh, w_out, g_post):
    depth = g_pre.shape[0]
    assert depth == 1, "single-layer trunk"
    nseq, sseq, _ = x_sample.shape
    assert sseq == SAMPLE_SEQ and nseq % SAMPLE_GROUP == 0

    gpre = g_pre[0].reshape(1, D_MODEL)
    gpost = g_post[0].reshape(1, D_MODEL)
    wmain, wt, wout, gmhb, bg = _weights_call(w_in[0].T, w_out[0], g_mh[0], b_gate)
    sinks = attn_sinks[0].reshape(1, ATT_HEADS)

    yp, wkp, wvp, cp, np_, mp = _prompt_call(x_prompt, gpre, wmain, wt, bg, sinks, gmhb, wout, gpost, PROMPT_BLOCK)

    ys, wks, wvs, cs, ns, ms = _sample_call(
        x_sample.reshape(nseq * sseq, D_MODEL), _window_in(cache_win_k[0]), _window_in(cache_win_v[0]),
        state_C[0], state_n[0].transpose(1, 0, 2), state_m[0].T,
        gpre, wmain, wt, bg, sinks, gmhb, wout, gpost)

    return (yp, ys.reshape(nseq, sseq, D_MODEL), _window_out(wkp), _window_out(wvp),
            cp[None], np_[None], mp.T[None],
            _window_out(wks), _window_out(wvs), cs[None], ns.transpose(1, 0, 2)[None], ms.T[None])
```

```python
import functools

import jax
import jax.numpy as jnp
from jax import lax
from jax.experimental import pallas as pl
from jax.experimental.pallas import tpu as pltpu

F32 = jnp.float32
BF16 = jnp.bfloat16

D_MODEL = 1024
ROWS = 128
ATT_HEADS, ATT_KV, ATT_GROUP, ATT_DIM = 8, 2, 4, 64
M_HEADS, M_DIM = 4, 128
NORM_EPS = 1e-6
NEG_BIG = -1e30
ATT_SCALE = ATT_DIM ** -0.5
LOG2E = 1.4426950408889634
K_SCALE = M_DIM ** -0.5

VM, GATES = 2304, 3840

P_QA, P_ZA, P_QM, P_KM, P_OM, P_ZM, P_KA, P_VA = 0, 512, 1024, 1536, 2048, 2560, 3072, 3200
P_MAIN = 3328
T_VM, T_GATES = 0, 512
T_ROWS = 528
T_SPLIT = 272
STATE_ROWS = 144
ONES_ROWS = 16

SAMPLE_SEQ = 8
SAMPLE_GROUP = ROWS // SAMPLE_SEQ
SAMPLE_PARTS = 2

VMEM_BYTES_V7X = 64 * 1024 * 1024
VMEM_LIMIT_BYTES = VMEM_BYTES_V7X * 7 // 8
NT_DIMS = (((1,), (1,)), ((), ()))


def _rms(x, g):
    return x * lax.rsqrt(jnp.mean(x * x, axis=-1, keepdims=True) + NORM_EPS) * g


def _sigmoid(x):
    return 0.5 + 0.5 * jnp.tanh(0.5 * x)


def _silu(x):
    h = 0.5 * x
    return h + h * jnp.tanh(h)


def _log_sigmoid(x):
    return -(jnp.maximum(-x, 0.0) + jnp.log1p(jnp.exp(-jnp.abs(x))))


def _slope(head):
    return 2.0 ** -(head + 1)


def _bdot(a, b):
    return jnp.dot(a.astype(BF16), b.astype(BF16), preferred_element_type=F32)


def _bdot_nt(a, b):
    return lax.dot_general(a.astype(BF16), b.astype(BF16), NT_DIMS, preferred_element_type=F32)


def _exact_dot(x, m):
    hi = x.astype(BF16).astype(F32)
    mid = (x - hi).astype(BF16).astype(F32)
    lo = (x - hi - mid).astype(BF16).astype(F32)
    parts = jnp.dot(jnp.concatenate([hi, mid, lo, jnp.zeros_like(hi)], axis=0).astype(BF16), m,
                    preferred_element_type=F32)
    return parts[0:8] + parts[8:16] + parts[16:24]


def _gate_rows(x, m0, mask_t, last_sel):
    row = lax.broadcasted_iota(jnp.int32, (8, ROWS), 0)
    head_rows = row < M_HEADS
    ic = jnp.where(head_rows, x, 0.0)
    fc = jnp.where(head_rows, _log_sigmoid(pltpu.roll(x, M_HEADS, 0)), 0.0)
    b = _exact_dot(fc, mask_t.astype(BF16))
    a = ic - b
    a_cols = jnp.concatenate([a, jnp.zeros((ROWS - 8, ROWS), F32)], axis=0).T
    a_masked = [jnp.where(mask_t, a_cols[:, h:h + 1], -jnp.inf) for h in range(M_HEADS)]
    cm = jnp.concatenate([jnp.max(am, axis=0, keepdims=True) for am in a_masked]
                         + [jnp.zeros((8 - M_HEADS, ROWS), F32)], axis=0)
    m_t = jnp.maximum(b + m0, b + cm)
    if last_sel is None:
        b_last = jnp.broadcast_to(b[:, ROWS - 1:ROWS], b.shape)
        m_new = jnp.broadcast_to(m_t[:, ROWS - 1:ROWS], b.shape)
    else:
        both = _exact_dot(jnp.where(head_rows, b, pltpu.roll(m_t, M_HEADS, 0)), last_sel)
        b_last = jnp.where(head_rows, both, 0.0)
        m_new = jnp.where(head_rows, pltpu.roll(both, M_HEADS, 0), 0.0)
    return dict(a_masked=a_masked, bm=b - m_t, gexp=jnp.exp(b + m0 - m_t), enm=jnp.exp(-m_t), m_new=m_new,
                w=jnp.exp(b_last - b + ic - m_new), decay=jnp.exp(b_last + m0 - m_new))


def _out_tail(cat, x, wout_ref, gpost_ref):
    y = jnp.dot(cat, wout_ref[...], preferred_element_type=F32)
    return x + _rms(y, gpost_ref[...])


W_BLOCK = 256
MAIN_BLOCKS = P_MAIN // W_BLOCK


def _main_src_block(i):
    return jnp.where(i < 2, i, jnp.where(i < 8, i + 1, jnp.where(i < 12, i + 3, 2)))


def _weights_kernel(main_ref, vm0_ref, vm1_ref, vm2_ref, vm3_ref, gates_ref, wout_ref, gmh_ref, bgate_ref,
                    wmain_ref, wt_ref, woutb_ref, gmhb_ref, bgb_ref):
    wmain_ref[...] = main_ref[...].T.astype(BF16)

    @pl.when(pl.program_id(0) == 0)
    def _():
        wt_ref[...] = jnp.concatenate(
            [vm0_ref[...], vm1_ref[...], vm2_ref[...], vm3_ref[...], gates_ref[...],
             jnp.zeros((T_ROWS - T_GATES - 8, D_MODEL), F32)], axis=0).astype(BF16)
        woutb_ref[...] = wout_ref[...].astype(BF16)
        for h in range(M_HEADS):
            gmhb_ref[h] = jnp.broadcast_to(gmh_ref[h:h + 1, :], (M_DIM, ROWS)).T
        bg_row = jnp.concatenate([bgate_ref[...], jnp.zeros((1, ROWS - 2 * M_HEADS), F32)], axis=1)
        bgb_ref[...] = jnp.broadcast_to(bg_row, (ROWS, ROWS)).T[0:2 * M_HEADS, :]


def _weights_call(w_in_t, w_out, gmh, bgate):
    blk = lambda r: pl.BlockSpec((128, D_MODEL), lambda i, r=r: (r, 0))
    const = lambda shape: pl.BlockSpec(shape, lambda i: (0,) * len(shape))
    return pl.pallas_call(
        _weights_kernel,
        grid=(MAIN_BLOCKS,),
        in_specs=[
            pl.BlockSpec((W_BLOCK, D_MODEL), lambda i: (_main_src_block(i), 0)),
            blk(VM // 128), blk(VM // 128 + 1), blk(VM // 128 + 2), blk(VM // 128 + 3),
            pl.BlockSpec((8, D_MODEL), lambda i: (GATES // 8, 0)),
            const((D_MODEL, D_MODEL)), const((M_HEADS, M_DIM)), const((1, 2 * M_HEADS)),
        ],
        out_specs=(
            pl.BlockSpec((D_MODEL, W_BLOCK), lambda i: (0, i)),
            const((T_ROWS, D_MODEL)), const((D_MODEL, D_MODEL)),
            const((M_HEADS, M_DIM, ROWS)), const((2 * M_HEADS, ROWS)),
        ),
        out_shape=(
            jax.ShapeDtypeStruct((D_MODEL, P_MAIN), BF16),
            jax.ShapeDtypeStruct((T_ROWS, D_MODEL), BF16),
            jax.ShapeDtypeStruct((D_MODEL, D_MODEL), BF16),
            jax.ShapeDtypeStruct((M_HEADS, M_DIM, ROWS), F32),
            jax.ShapeDtypeStruct((2 * M_HEADS, ROWS), F32),
        ),
        compiler_params=pltpu.CompilerParams(
            dimension_semantics=("arbitrary",), vmem_limit_bytes=VMEM_LIMIT_BYTES),
        name="layer_weights",
    )(w_in_t, w_in_t, w_in_t, w_in_t, w_in_t, w_in_t, w_out, gmh, bgate)


def _prompt_bias_t(kv, first):
    j = lax.broadcasted_iota(jnp.int32, (2 * ROWS, ROWS), 0)
    i = lax.broadcasted_iota(jnp.int32, (2 * ROWS, ROWS), 1)
    diff = ROWS + i - j
    valid = (diff >= 0) & (diff < ROWS)
    if first:
        valid = valid & (j >= ROWS)
    dfl = diff.astype(F32)
    return jnp.concatenate(
        [jnp.where(valid, -_slope(kv * ATT_GROUP + g) * dfl, NEG_BIG) * LOG2E for g in range(ATT_GROUP)], axis=1)


def _prompt_kernel(x_ref, xnext_ref, gpre_ref, wmain_ref, wt_ref, bg_ref, sink_ref, gmhb_ref, wout_ref, gpost_ref,
                   y_ref, wk_ref, wv_ref, c_ref, n_ref, m_ref,
                   proj_ref, projt_ref, cat_ref, xn_ref, kprev_ref, vtprev_ref, ct_ref, mst_ref, bias_ref,
                   *, tb, nt):
    bi = pl.program_id(0)
    j = pl.program_id(1)
    nchunks = tb // ROWS

    @pl.when((bi == 0) & (j == 0))
    def _():
        for first in range(2):
            for kv in range(ATT_KV):
                bias_ref[first * ATT_KV + kv] = _prompt_bias_t(kv, first)
        m_ref[...] = jnp.zeros_like(m_ref)

    @pl.when(j == 0)
    def _():
        kprev_ref[...] = jnp.zeros_like(kprev_ref)
        vtprev_ref[...] = jnp.zeros_like(vtprev_ref)
        ct_ref[...] = jnp.zeros_like(ct_ref)
        mst_ref[...] = jnp.zeros_like(mst_ref)

    sub = tb // PROMPT_SUBBLOCKS
    sub_chunks = sub // ROWS

    def norm_rows(sb):
        rows_sb = slice(sb * sub, (sb + 1) * sub)
        xn_ref[rows_sb, :] = _rms(x_ref[0, rows_sb, :], gpre_ref[...]).astype(BF16)

    def proj_cols(sb, c0, c1):
        rows_sb = slice(sb * sub, (sb + 1) * sub)
        proj_ref[rows_sb, c0:c1] = jnp.dot(xn_ref[rows_sb, :], wmain_ref[:, c0:c1], preferred_element_type=F32)

    def proj_t(sb):
        for r0, r1 in ((0, T_SPLIT), (T_SPLIT, T_ROWS)):
            pt = lax.dot_general(wt_ref[r0:r1, :], xn_ref[sb * sub:(sb + 1) * sub, :], NT_DIMS,
                                 preferred_element_type=F32)
            for c in range(sub_chunks):
                projt_ref[sb * sub_chunks + c, r0:r1, :] = pt[:, c * ROWS:(c + 1) * ROWS]

    def out_rows(sb):
        rows_sb = slice(sb * sub, (sb + 1) * sub)
        y_ref[0, rows_sb, :] = _out_tail(cat_ref[rows_sb, :], x_ref[0, rows_sb, :], wout_ref, gpost_ref)

    def proj_pieces(sb):
        bounds = list(range(0, P_MAIN, PROJ_COL_STEP)) + [P_MAIN]
        pieces = [functools.partial(proj_cols, sb, c0, c1) for c0, c1 in zip(bounds[:-1], bounds[1:])]
        return pieces[:2] + [functools.partial(proj_t, sb)] + pieces[2:]

    ri = lax.broadcasted_iota(jnp.int32, (ROWS, ROWS), 0)
    ci = lax.broadcasted_iota(jnp.int32, (ROWS, ROWS), 1)
    mask_t = ri <= ci
    tail_row = lax.broadcasted_iota(jnp.int32, (STATE_ROWS - M_DIM, 2 * ROWS), 0)
    tail_lane = lax.broadcasted_iota(jnp.int32, (STATE_ROWS - M_DIM, 2 * ROWS), 1)

    def chunk(c):
        rows = pl.ds(c * ROWS, ROWS)
        first = ((j == 0) & (c == 0)).astype(jnp.int32)
        yield

        qa = proj_ref[rows, P_QA:P_QA + 512] * (ATT_SCALE * LOG2E)
        kcur = proj_ref[rows, P_KA:P_KA + 128].astype(BF16)
        vtcur = proj_ref[rows, P_VA:P_VA + 128].T.astype(BF16)
        kcat = jnp.concatenate([kprev_ref[...], kcur], axis=0)
        vtcat = jnp.concatenate([vtprev_ref[...], vtcur], axis=1)
        r = _gate_rows(projt_ref[c, T_GATES:T_GATES + 8, :] + bg_ref[...], mst_ref[...], mask_t, None)

        scores, sinks = [], []
        for kv in range(ATT_KV):
            want_hi = kv == 1
            keep = (ci >= ATT_DIM) if want_hi else (ci < ATT_DIM)
            pieces = []
            for g in range(ATT_GROUP):
                hh = kv * ATT_GROUP + g
                blk = qa[:, (hh // 2) * 128:(hh // 2 + 1) * 128]
                if (hh % 2 == 1) != want_hi:
                    blk = pltpu.roll(blk, ATT_DIM, 1)
                pieces.append(jnp.where(keep, blk, 0.0))
            q4 = jnp.concatenate(pieces, axis=0)
            scores.append(_bdot_nt(kcat, q4) + bias_ref[first * ATT_KV + kv])
            sinks.append(jnp.concatenate(
                [jnp.broadcast_to(sink_ref[0:1, kv * ATT_GROUP + g:kv * ATT_GROUP + g + 1] * LOG2E, (1, ROWS))
                 for g in range(ATT_GROUP)], axis=1))
        yield
        w_cols = jnp.concatenate([r["w"], jnp.zeros((ROWS - 8, ROWS), F32)], axis=0).T
        zero_blk = jnp.zeros((M_DIM, M_DIM), BF16)
        q_ts, kfs, raws = [], [], []
        for pair in range(M_HEADS // 2):
            q_pair = [proj_ref[rows, P_QM + h * M_DIM:P_QM + (h + 1) * M_DIM].T for h in (2 * pair, 2 * pair + 1)]
            k_pair = proj_ref[rows, P_KM + 2 * pair * M_DIM:P_KM + 2 * (pair + 1) * M_DIM] * K_SCALE
            q_diag = jnp.concatenate([jnp.concatenate([q_pair[0].astype(BF16), zero_blk], axis=1),
                                      jnp.concatenate([zero_blk, q_pair[1].astype(BF16)], axis=1)], axis=0)
            raw = _bdot(k_pair, q_diag)
            q_ts += q_pair
            kfs += [k_pair[:, 0:M_DIM], k_pair[:, M_DIM:2 * M_DIM]]
            raws += [raw[:, 0:ROWS], raw[:, ROWS:2 * ROWS]]
        lhs_nums, rhs_lows, sts = [], [], []
        for h in range(M_HEADS):
            vt = projt_ref[c, T_VM + h * M_DIM:T_VM + (h + 1) * M_DIM, :]
            ct = ct_ref[h]
            sts.append((raws[h], (kfs[h] * w_cols[:, h:h + 1]).astype(BF16)))
            rhs_lows.append(jnp.concatenate([(q_ts[h] * r["gexp"][h:h + 1, :]).astype(BF16),
                                             jnp.zeros((M_DIM, M_DIM), BF16)], axis=1))
            tail = jnp.concatenate([jnp.zeros((STATE_ROWS - M_DIM, ROWS), F32), ct[M_DIM:STATE_ROWS]], axis=1)
            tail = jnp.where((tail_row == 1) & (tail_lane < ROWS), 1.0, tail)
            lhs_nums.append(jnp.concatenate([jnp.concatenate([vt, ct[0:M_DIM]], axis=1), tail], axis=0).astype(BF16))
        mst_ref[...] = r["m_new"]
        yield

        outs, nums, dens = [], [], []
        for kv in range(ATT_KV):
            s, sink = scores[kv], sinks[kv]
            mx = jnp.maximum(jnp.max(s, axis=0, keepdims=True), sink)
            p = jnp.exp2(s - mx).astype(BF16)
            lhs = jnp.concatenate([vtcat[kv * ATT_DIM:(kv + 1) * ATT_DIM, :],
                                   jnp.ones((ONES_ROWS, 2 * ROWS), BF16)], axis=0)
            o = jnp.dot(lhs, p, preferred_element_type=F32)
            outs.append((o, jnp.exp2(sink - mx)))
        yield
        for h in range(M_HEADS):
            raw, wk = sts[h]
            st = raw * jnp.exp(r["a_masked"][h] + r["bm"][h:h + 1, :])
            rhs = jnp.concatenate([jnp.concatenate([st.astype(BF16), wk], axis=1), rhs_lows[h]], axis=0)
            res = jnp.dot(lhs_nums[h], rhs, preferred_element_type=F32)
            nums.append(res[0:M_DIM, 0:ROWS])
            dens.append(jnp.sum(st, axis=0, keepdims=True) + res[M_DIM:M_DIM + 1, 0:ROWS])
            dec = r["decay"][h:h + 1, 0:1]
            ct_ref[h, 0:M_DIM, :] = dec * ct_ref[h, 0:M_DIM, :] + res[0:M_DIM, ROWS:2 * ROWS]
            ct_ref[h, M_DIM:M_DIM + 1, :] = dec * ct_ref[h, M_DIM:M_DIM + 1, :] + res[M_DIM + 1:M_DIM + 2, ROWS:2 * ROWS]
        yield

        att = []
        for kv in range(ATT_KV):
            o, esink = outs[kv]
            on = o[0:ATT_DIM, :] * (1.0 / (o[ATT_DIM:ATT_DIM + 1, :] + esink))
            for pair in range(2):
                two = jnp.concatenate([on[:, (2 * pair) * ROWS:(2 * pair + 1) * ROWS],
                                       on[:, (2 * pair + 1) * ROWS:(2 * pair + 2) * ROWS]], axis=0)
                att.append(two.T)
        a_out = jnp.concatenate(att, axis=1) * _silu(proj_ref[rows, P_ZA:P_ZA + 512])
        kprev_ref[...] = kcur
        vtprev_ref[...] = vtcur
        yield
        m_out = []
        for h in range(M_HEADS):
            ht = nums[h] * (1.0 / jnp.maximum(jnp.abs(dens[h]), r["enm"][h:h + 1, :]))
            hn = ht * lax.rsqrt(jnp.mean(ht * ht, axis=0, keepdims=True) + NORM_EPS) * gmhb_ref[h]
            m_out.append(_sigmoid(proj_ref[rows, P_OM + h * M_DIM:P_OM + (h + 1) * M_DIM]) * hn.T
                         * _silu(proj_ref[rows, P_ZM + h * M_DIM:P_ZM + (h + 1) * M_DIM]))

        cat_ref[rows, :] = jnp.concatenate([a_out] + m_out, axis=1).astype(BF16)
        yield

    @pl.when((bi == 0) & (j == 0))
    def _():
        for sb in range(PROMPT_SUBBLOCKS):
            norm_rows(sb)
        for piece in proj_pieces(0):
            piece()

    def norm_next():
        xn_ref[...] = _rms(xnext_ref[0], gpre_ref[...]).astype(BF16)

    def run_chunks(sb, fillers):
        n_fill, n_slots, slot = len(fillers), sub_chunks * CHUNK_STAGES, 0
        for c in range(sb * sub_chunks, (sb + 1) * sub_chunks):
            for _ in chunk(c):
                for _ in range(-(-(slot + 1) * n_fill // n_slots) + (-slot * n_fill // n_slots)):
                    fillers.pop(0)()
                slot += 1
        assert slot == n_slots and not fillers

    assert PROMPT_SUBBLOCKS == 2
    run_chunks(0, proj_pieces(1))
    nxt = proj_pieces(0)
    run_chunks(1, [functools.partial(out_rows, 0), norm_next] + nxt[:-TAIL_PIECES])
    out_rows(1)
    for piece in nxt[-TAIL_PIECES:]:
        piece()

    @pl.when(j == nt - 1)
    def _():
        wk_ref[0] = proj_ref[tb - ROWS:tb, P_KA:P_KA + 128].T
        wv_ref[0] = proj_ref[tb - ROWS:tb, P_VA:P_VA + 128].T
        for h in range(M_HEADS):
            ct = ct_ref[h]
            c_ref[0, h] = ct[0:M_DIM].T
            n_ref[0, h:h + 1, :] = ct[M_DIM:M_DIM + 1]
        seq_lane = lax.broadcasted_iota(jnp.int32, m_ref.shape, 1) == bi
        m_ref[...] = jnp.where(seq_lane, mst_ref[0:M_HEADS, 0:m_ref.shape[1]], m_ref[...])


def _prompt_call(x, gpre, wmain, wt, bg, sinks, gmhb, wout, gpost, tb):
    bsz, seq, _ = x.shape
    nt = seq // tb
    full = lambda shape: pl.BlockSpec(shape, lambda b, j: (0,) * len(shape))

    def next_block(b, j):
        wrap = (j + 1 == nt).astype(jnp.int32)
        return (jnp.minimum(b + wrap, bsz - 1), (j + 1) * (1 - wrap), 0)

    out_shapes = (
        jax.ShapeDtypeStruct((bsz, seq, D_MODEL), F32),
        jax.ShapeDtypeStruct((bsz, 128, ROWS), F32),
        jax.ShapeDtypeStruct((bsz, 128, ROWS), F32),
        jax.ShapeDtypeStruct((bsz, M_HEADS, M_DIM, M_DIM), F32),
        jax.ShapeDtypeStruct((bsz, M_HEADS, M_DIM), F32),
        jax.ShapeDtypeStruct((M_HEADS, bsz), F32),
    )
    return pl.pallas_call(
        functools.partial(_prompt_kernel, tb=tb, nt=nt),
        grid=(bsz, nt),
        in_specs=[
            pl.BlockSpec((1, tb, D_MODEL), lambda b, j: (b, j, 0)),
            pl.BlockSpec((1, tb, D_MODEL), next_block),
            full((1, D_MODEL)), full((D_MODEL, P_MAIN)), full((T_ROWS, D_MODEL)), full((8, ROWS)),
            full((1, ATT_HEADS)), full((M_HEADS, M_DIM, ROWS)), full((D_MODEL, D_MODEL)), full((1, D_MODEL)),
        ],
        out_specs=(
            pl.BlockSpec((1, tb, D_MODEL), lambda b, j: (b, j, 0)),
            pl.BlockSpec((1, 128, ROWS), lambda b, j: (b, 0, 0)),
            pl.BlockSpec((1, 128, ROWS), lambda b, j: (b, 0, 0)),
            pl.BlockSpec((1, M_HEADS, M_DIM, M_DIM), lambda b, j: (b, 0, 0, 0)),
            pl.BlockSpec((1, M_HEADS, M_DIM), lambda b, j: (b, 0, 0)),
            pl.BlockSpec((M_HEADS, bsz), lambda b, j: (0, 0)),
        ),
        out_shape=out_shapes,
        scratch_shapes=[
            pltpu.VMEM((tb, P_MAIN), F32),
            pltpu.VMEM((tb // ROWS, T_ROWS, ROWS), F32),
            pltpu.VMEM((tb, D_MODEL), BF16),
            pltpu.VMEM((tb, D_MODEL), BF16),
            pltpu.VMEM((ROWS, 128), BF16),
            pltpu.VMEM((128, ROWS), BF16),
            pltpu.VMEM((M_HEADS, STATE_ROWS, M_DIM), F32),
            pltpu.VMEM((8, ROWS), F32),
            pltpu.VMEM((2 * ATT_KV, 2 * ROWS, ATT_GROUP * ROWS), F32),
        ],
        compiler_params=pltpu.CompilerParams(
            dimension_semantics=("arbitrary", "arbitrary"), vmem_limit_bytes=VMEM_LIMIT_BYTES),
        name="prompt_layer",
    )(x, x, gpre, wmain, wt, bg, sinks, gmhb, wout, gpost)


def _sample_bias_new(head):
    r = lax.broadcasted_iota(jnp.int32, (ROWS, ROWS), 0)
    c = lax.broadcasted_iota(jnp.int32, (ROWS, ROWS), 1)
    valid = ((r >> 3) == (c >> 3)) & (r >= c)
    return jnp.where(valid, -_slope(head) * (r - c).astype(F32), NEG_BIG)


def _sample_bias_cache(head):
    r = lax.broadcasted_iota(jnp.int32, (ROWS, ROWS), 0)
    c = lax.broadcasted_iota(jnp.int32, (ROWS, ROWS), 1)
    diff = (r & (SAMPLE_SEQ - 1)) + ROWS - c
    return jnp.where(diff < ROWS, -_slope(head) * diff.astype(F32), NEG_BIG)


def _sample_kernel(x_ref, kct_ref, vct_ref, cin_ref, nin_ref, m0_ref,
                   gpre_ref, wmain_ref, wt_ref, bg_ref, sink_ref, gmhb_ref, wout_ref, gpost_ref,
                   y_ref, kot_ref, vot_ref, cout_ref, nout_ref, mout_ref,
                   projfull_ref, projt_ref, cat_ref, qh_ref, sc_ref, oc_ref, qc_ref, wv_ref, kt_ref,
                   decb_ref, bn_ref, bc_ref, sn_ref, pn_ref):
    step, part = pl.program_id(0), pl.program_id(1)

    @pl.when((step == 0) & (part == 0))
    def _():
        for hh in range(ATT_HEADS):
            bn_ref[hh] = _sample_bias_new(hh)
            bc_ref[hh] = _sample_bias_cache(hh)
        mout_ref[...] = jnp.zeros_like(mout_ref)

    @pl.when(part == 0)
    def _():
        xn = _rms(x_ref[...], gpre_ref[...]).astype(BF16)
        projfull_ref[...] = jnp.dot(xn, wmain_ref[...], preferred_element_type=F32)
        ptf = lax.dot_general(wt_ref[...], xn, NT_DIMS, preferred_element_type=F32)
        for g in range(SAMPLE_PARTS):
            projt_ref[g] = ptf[:, g * ROWS:(g + 1) * ROWS]

    part_rows = pl.ds(pl.multiple_of(part * ROWS, ROWS), ROWS)
    proj_ref = projfull_ref.at[part_rows]
    pt = projt_ref[part]

    ri = lax.broadcasted_iota(jnp.int32, (ROWS, ROWS), 0)
    ci = lax.broadcasted_iota(jnp.int32, (ROWS, ROWS), 1)
    same_seq = (ri >> 3) == (ci >> 3)
    mask_t = same_seq & (ri <= ci)
    last_sel = (same_seq & ((ri & (SAMPLE_SEQ - 1)) == SAMPLE_SEQ - 1)).astype(BF16)
    row16 = lax.broadcasted_iota(jnp.int32, (SAMPLE_GROUP, ROWS), 0)
    lane16 = lax.broadcasted_iota(jnp.int32, (SAMPLE_GROUP, ROWS), 1)
    seq_of_lane = (lane16 >> 3) == row16
    ones_rows = jnp.ones((ONES_ROWS, ROWS), F32)

    qa = proj_ref[:, P_QA:P_QA + 512] * ATT_SCALE
    for hh in range(ATT_HEADS):
        blk = qa[:, (hh // 2) * 128:(hh // 2 + 1) * 128]
        if hh % 2 == 1:
            blk = pltpu.roll(blk, ATT_DIM, 1)
        qh_ref[hh] = blk[:, 0:ATT_DIM]
    ka = proj_ref[:, P_KA:P_KA + 128]
    kat = ka.T
    vat = proj_ref[:, P_VA:P_VA + 128].T

    keep_new = ci >= ROWS - SAMPLE_SEQ

    for b in range(SAMPLE_GROUP):
        rows = pl.ds(b * SAMPLE_SEQ, SAMPLE_SEQ)
        kct = kct_ref[b]
        shift = (ROWS - SAMPLE_SEQ - b * SAMPLE_SEQ) % ROWS
        new_k, new_v = (kat, vat) if shift == 0 else (pltpu.roll(kat, shift, 1), pltpu.roll(vat, shift, 1))
        kot_ref[b] = jnp.where(keep_new, new_k, pltpu.roll(kct, ROWS - SAMPLE_SEQ, 1))
        vot_ref[b] = jnp.where(keep_new, new_v, pltpu.roll(vct_ref[b], ROWS - SAMPLE_SEQ, 1))
        for kv in range(ATT_KV):
            lhs = jnp.concatenate([qh_ref[kv * ATT_GROUP + g, rows, :] for g in range(ATT_GROUP)], axis=0)
            res = _bdot(lhs, kct[kv * ATT_DIM:(kv + 1) * ATT_DIM, :])
            for g in range(ATT_GROUP):
                sc_ref[kv * ATT_GROUP + g, rows, :] = res[g * SAMPLE_SEQ:(g + 1) * SAMPLE_SEQ]
        for h in range(M_HEADS):
            qc_ref[h, rows, :] = _bdot(proj_ref[rows, P_QM + h * M_DIM:P_QM + (h + 1) * M_DIM], cin_ref[b, h])

    o_new, esinks = [], []
    for hh in range(ATT_HEADS):
        kv = hh // ATT_GROUP
        sn_ref[hh] = _bdot(qh_ref[hh], kat[kv * ATT_DIM:(kv + 1) * ATT_DIM, :]) + bn_ref[hh]
    for hh in range(ATT_HEADS):
        s_n = sn_ref[hh]
        s_c = sc_ref[hh] + bc_ref[hh]
        sink = sink_ref[0:1, hh:hh + 1]
        mx = jnp.maximum(jnp.max(jnp.maximum(s_n, s_c), axis=-1, keepdims=True), sink)
        mx_b = jnp.broadcast_to(mx, (ROWS, ROWS))
        pn_ref[hh] = jnp.exp(s_n - mx_b).astype(BF16)
        sc_ref[hh] = jnp.exp(s_c - mx_b)
        esinks.append(jnp.exp(sink - mx))
    for hh in range(ATT_HEADS):
        kv = hh // ATT_GROUP
        vaug = jnp.concatenate([vat[kv * ATT_DIM:(kv + 1) * ATT_DIM, :], ones_rows], axis=0)
        o_new.append(_bdot_nt(pn_ref[hh], vaug))

    first_seq = (step * SAMPLE_PARTS + part) * SAMPLE_GROUP
    seq_step = ri == first_seq + (ci >> 3)
    m0 = _exact_dot(jnp.concatenate([m0_ref[...], jnp.zeros((8 - M_HEADS, ROWS), F32)], axis=0),
                    seq_step.astype(BF16))
    r = _gate_rows(pt[T_GATES:T_GATES + 8, :] + bg_ref[...], m0, mask_t, last_sel)
    step_seq = (ci == first_seq + (ri >> 3)) & ((ri & (SAMPLE_SEQ - 1)) == 0)
    mout_ref[...] += _exact_dot(r["m_new"], step_seq.astype(BF16))[0:M_HEADS]
    m_out = []
    for h in range(M_HEADS):
        q = proj_ref[:, P_QM + h * M_DIM:P_QM + (h + 1) * M_DIM].astype(BF16)
        kf = proj_ref[:, P_KM + h * M_DIM:P_KM + (h + 1) * M_DIM] * K_SCALE
        k = kf.astype(BF16)
        vt = pt[T_VM + h * M_DIM:T_VM + (h + 1) * M_DIM, :]
        n_h = nin_ref[h]
        r1 = lax.dot_general(jnp.concatenate([k, n_h.astype(BF16)], axis=0), q, NT_DIMS,
                             preferred_element_type=F32)
        st = r1[0:ROWS] * jnp.exp(r["a_masked"][h] + r["bm"][h:h + 1, :])
        q_n = jnp.sum(jnp.where(seq_of_lane, r1[ROWS:ROWS + SAMPLE_GROUP], 0.0), axis=0, keepdims=True)
        g_row = r["gexp"][h:h + 1, :]
        num = _bdot(vt, st) + g_row * qc_ref[h].T
        den = jnp.sum(st, axis=0, keepdims=True) + g_row * q_n
        ht = num * (1.0 / jnp.maximum(jnp.abs(den), r["enm"][h:h + 1, :]))
        hn = ht * lax.rsqrt(jnp.mean(ht * ht, axis=0, keepdims=True) + NORM_EPS) * gmhb_ref[h]
        m_out.append(_sigmoid(proj_ref[:, P_OM + h * M_DIM:P_OM + (h + 1) * M_DIM]) * hn.T
                     * _silu(proj_ref[:, P_ZM + h * M_DIM:P_ZM + (h + 1) * M_DIM]))
        w_row = r["w"][h:h + 1, :]
        dec16 = jnp.sum(jnp.where(lane16 == row16 * SAMPLE_SEQ, r["decay"][h:h + 1, :], 0.0),
                        axis=1, keepdims=True)
        nout_ref[h] = dec16 * n_h + _bdot(jnp.where(seq_of_lane, w_row, 0.0), k)
        decb_ref[h] = jnp.broadcast_to(dec16, (SAMPLE_GROUP, ROWS))
        wv_ref[h] = (vt * w_row).T
        kt_ref[h] = kf.T.astype(BF16)

    for b in range(SAMPLE_GROUP):
        rows = pl.ds(b * SAMPLE_SEQ, SAMPLE_SEQ)
        vct = vct_ref[b]
        for kv in range(ATT_KV):
            vaug = jnp.concatenate([vct[kv * ATT_DIM:(kv + 1) * ATT_DIM, :], ones_rows], axis=0)
            pl_ = jnp.concatenate([sc_ref[kv * ATT_GROUP + g, rows, :] for g in range(ATT_GROUP)], axis=0)
            res = _bdot_nt(pl_, vaug)
            for g in range(ATT_GROUP):
                oc_ref[kv * ATT_GROUP + g, rows, 0:ATT_DIM + ONES_ROWS] = res[g * SAMPLE_SEQ:(g + 1) * SAMPLE_SEQ]
        in_seq = (ri >> 3) == b
        for h in range(M_HEADS):
            upd = jnp.dot(kt_ref[h], jnp.where(in_seq, wv_ref[h], 0.0).astype(BF16), preferred_element_type=F32)
            cout_ref[b, h] = decb_ref[h, b:b + 1, :] * cin_ref[b, h] + upd

    att = []
    for pair in range(ATT_HEADS // 2):
        halves = []
        for hh in (2 * pair, 2 * pair + 1):
            on, oc = o_new[hh], oc_ref[hh]
            den = on[:, ATT_DIM:ATT_DIM + 1] + oc[:, ATT_DIM:ATT_DIM + 1] + esinks[hh]
            halves.append((on[:, 0:ATT_DIM] + oc[:, 0:ATT_DIM]) * (1.0 / den))
        att.append(jnp.concatenate(halves, axis=1))
    a_out = jnp.concatenate(att, axis=1) * _silu(proj_ref[:, P_ZA:P_ZA + 512])
    cat_ref[part_rows, :] = jnp.concatenate([a_out] + m_out, axis=1).astype(BF16)

    @pl.when(part == SAMPLE_PARTS - 1)
    def _():
        y_ref[...] = _out_tail(cat_ref[...], x_ref[...], wout_ref, gpost_ref)


def _sample_call(x, kct, vct, cin, nin, m0, gpre, wmain, wt, bg, sinks, gmhb, wout, gpost):
    nrows = x.shape[0]
    ngroups = nrows // ROWS
    nseq = ngroups * SAMPLE_GROUP
    parts = SAMPLE_PARTS
    assert ngroups % parts == 0
    assert nseq == ROWS, "the per-sequence stabiliser state is handled as one 128-lane row per head"
    full = lambda shape: pl.BlockSpec(shape, lambda i, p: (0,) * len(shape))
    grp = SAMPLE_GROUP
    group = lambda i, p: i * parts + p
    out_shapes = (
        jax.ShapeDtypeStruct((nrows, D_MODEL), F32),
        jax.ShapeDtypeStruct((nseq, 128, ROWS), F32),
        jax.ShapeDtypeStruct((nseq, 128, ROWS), F32),
        jax.ShapeDtypeStruct((nseq, M_HEADS, M_DIM, M_DIM), F32),
        jax.ShapeDtypeStruct((M_HEADS, nseq, M_DIM), F32),
        jax.ShapeDtypeStruct((M_HEADS, nseq), F32),
    )
    return pl.pallas_call(
        _sample_kernel,
        grid=(ngroups // parts, parts),
        in_specs=[
            pl.BlockSpec((parts * ROWS, D_MODEL), lambda i, p: (i, 0)),
            pl.BlockSpec((grp, 128, ROWS), lambda i, p: (group(i, p), 0, 0)),
            pl.BlockSpec((grp, 128, ROWS), lambda i, p: (group(i, p), 0, 0)),
            pl.BlockSpec((grp, M_HEADS, M_DIM, M_DIM), lambda i, p: (group(i, p), 0, 0, 0)),
            pl.BlockSpec((M_HEADS, grp, M_DIM), lambda i, p: (0, group(i, p), 0)),
            full((M_HEADS, nseq)),
            full((1, D_MODEL)), full((D_MODEL, P_MAIN)), full((T_ROWS, D_MODEL)), full((8, ROWS)),
            full((1, ATT_HEADS)), full((M_HEADS, M_DIM, ROWS)), full((D_MODEL, D_MODEL)), full((1, D_MODEL)),
        ],
        out_specs=(
            pl.BlockSpec((parts * ROWS, D_MODEL), lambda i, p: (i, 0)),
            pl.BlockSpec((grp, 128, ROWS), lambda i, p: (group(i, p), 0, 0)),
            pl.BlockSpec((grp, 128, ROWS), lambda i, p: (group(i, p), 0, 0)),
            pl.BlockSpec((grp, M_HEADS, M_DIM, M_DIM), lambda i, p: (group(i, p), 0, 0, 0)),
            pl.BlockSpec((M_HEADS, grp, M_DIM), lambda i, p: (0, group(i, p), 0)),
            full((M_HEADS, nseq)),
        ),
        out_shape=out_shapes,
        scratch_shapes=[
            pltpu.VMEM((parts * ROWS, P_MAIN), F32),
            pltpu.VMEM((parts, T_ROWS, ROWS), F32),
            pltpu.VMEM((parts * ROWS, D_MODEL), BF16),
            pltpu.VMEM((ATT_HEADS, ROWS, ATT_DIM), F32),
            pltpu.VMEM((ATT_HEADS, ROWS, ROWS), F32),
            pltpu.VMEM((ATT_HEADS, ROWS, ROWS), F32),
            pltpu.VMEM((M_HEADS, ROWS, M_DIM), F32),
            pltpu.VMEM((M_HEADS, ROWS, M_DIM), F32),
            pltpu.VMEM((M_HEADS, M_DIM, ROWS), BF16),
            pltpu.VMEM((M_HEADS, SAMPLE_GROUP, ROWS), F32),
            pltpu.VMEM((ATT_HEADS, ROWS, ROWS), F32),
            pltpu.VMEM((ATT_HEADS, ROWS, ROWS), F32),
            pltpu.VMEM((ATT_HEADS, ROWS, ROWS), F32),
            pltpu.VMEM((ATT_HEADS, ROWS, ROWS), BF16),
        ],
        compiler_params=pltpu.CompilerParams(
            dimension_semantics=("arbitrary", "arbitrary"), vmem_limit_bytes=VMEM_LIMIT_BYTES),
        name="sample_layer",
    )(x, kct, vct, cin, nin, m0, gpre, wmain, wt, bg, sinks, gmhb, wout, gpost)


PROMPT_BLOCK = 512
PROMPT_SUBBLOCKS = 2
PROJ_COL_STEP = 512
CHUNK_STAGES = 7
TAIL_PIECES = 3


def _window_in(cache):
    nseq = cache.shape[0]
    return cache.transpose(0, 2, 3, 1).reshape(nseq, ATT_KV * ATT_DIM, ROWS)


def _window_out(win_t):
    nseq = win_t.shape[0]
    return win_t.reshape(nseq, ATT_KV, ATT_DIM, ROWS).transpose(0, 3, 1, 2)[None]


def kernel(x_prompt, x_sample, cache_win_k, cache_win_v, state_C, state_n, state_m,
           g_pre, w_in, b_gate, attn_sinks, g_mh, w_out, g_post):
    depth = g_pre.shape[0]
    assert depth == 1, "single-layer trunk"
    nseq, sseq, _ = x_sample.shape
    assert sseq == SAMPLE_SEQ and nseq % SAMPLE_GROUP == 0

    gpre = g_pre[0].reshape(1, D_MODEL)
    gpost = g_post[0].reshape(1, D_MODEL)
    wmain, wt, wout, gmhb, bg = _weights_call(w_in[0].T, w_out[0], g_mh[0], b_gate)
    sinks = attn_sinks[0].reshape(1, ATT_HEADS)

    yp, wkp, wvp, cp, np_, mp = _prompt_call(x_prompt, gpre, wmain, wt, bg, sinks, gmhb, wout, gpost, PROMPT_BLOCK)

    ys, wks, wvs, cs, ns, ms = _sample_call(
        x_sample.reshape(nseq * sseq, D_MODEL), _window_in(cache_win_k[0]), _window_in(cache_win_v[0]),
        state_C[0], state_n[0].transpose(1, 0, 2), state_m[0].T,
        gpre, wmain, wt, bg, sinks, gmhb, wout, gpost)

    return (yp, ys.reshape(nseq, sseq, D_MODEL), _window_out(wkp), _window_out(wvp),
            cp[None], np_[None], mp.T[None],
            _window_out(wks), _window_out(wvs), cs[None], ns.transpose(1, 0, 2)[None], ms.T[None])
```

```python
import functools

import jax
import jax.numpy as jnp
from jax import lax
from jax.experimental import pallas as pl
from jax.experimental.pallas import tpu as pltpu

F32 = jnp.float32
BF16 = jnp.bfloat16

D_MODEL = 1024
ROWS = 128
ATT_HEADS, ATT_KV, ATT_GROUP, ATT_DIM = 8, 2, 4, 64
M_HEADS, M_DIM = 4, 128
NORM_EPS = 1e-6
NEG_BIG = -1e30
ATT_SCALE = ATT_DIM ** -0.5
LOG2E = 1.4426950408889634
K_SCALE = M_DIM ** -0.5

VM, GATES = 2304, 3840

P_QA, P_ZA, P_QM, P_KM, P_OM, P_ZM, P_KA, P_VA = 0, 512, 1024, 1536, 2048, 2560, 3072, 3200
P_MAIN = 3328
T_VM, T_GATES = 0, 512
T_ROWS = 528
T_SPLIT = 272
STATE_ROWS = 144
ONES_ROWS = 16

SAMPLE_SEQ = 8
SAMPLE_GROUP = ROWS // SAMPLE_SEQ
SAMPLE_PARTS = 2

VMEM_BYTES_V7X = 64 * 1024 * 1024
VMEM_LIMIT_BYTES = VMEM_BYTES_V7X * 7 // 8
NT_DIMS = (((1,), (1,)), ((), ()))


def _rms(x, g):
    return x * lax.rsqrt(jnp.mean(x * x, axis=-1, keepdims=True) + NORM_EPS) * g


def _sigmoid(x):
    return 0.5 + 0.5 * jnp.tanh(0.5 * x)


def _silu(x):
    h = 0.5 * x
    return h + h * jnp.tanh(h)


def _log_sigmoid(x):
    return -(jnp.maximum(-x, 0.0) + jnp.log1p(jnp.exp(-jnp.abs(x))))


def _slope(head):
    return 2.0 ** -(head + 1)


def _bdot(a, b):
    return jnp.dot(a.astype(BF16), b.astype(BF16), preferred_element_type=F32)


def _bdot_nt(a, b):
    return lax.dot_general(a.astype(BF16), b.astype(BF16), NT_DIMS, preferred_element_type=F32)


def _exact_dot(x, m):
    hi = x.astype(BF16).astype(F32)
    mid = (x - hi).astype(BF16).astype(F32)
    lo = (x - hi - mid).astype(BF16).astype(F32)
    parts = jnp.dot(jnp.concatenate([hi, mid, lo, jnp.zeros_like(hi)], axis=0).astype(BF16), m,
                    preferred_element_type=F32)
    return parts[0:8] + parts[8:16] + parts[16:24]


def _gate_rows(x, m0, mask_t, last_sel):
    row = lax.broadcasted_iota(jnp.int32, (8, ROWS), 0)
    head_rows = row < M_HEADS
    ic = jnp.where(head_rows, x, 0.0)
    fc = jnp.where(head_rows, _log_sigmoid(pltpu.roll(x, M_HEADS, 0)), 0.0)
    b = _exact_dot(fc, mask_t.astype(BF16))
    a = ic - b
    a_cols = jnp.concatenate([a, jnp.zeros((ROWS - 8, ROWS), F32)], axis=0).T
    a_masked = [jnp.where(mask_t, a_cols[:, h:h + 1], -jnp.inf) for h in range(M_HEADS)]
    cm = jnp.concatenate([jnp.max(am, axis=0, keepdims=True) for am in a_masked]
                         + [jnp.zeros((8 - M_HEADS, ROWS), F32)], axis=0)
    m_t = jnp.maximum(b + m0, b + cm)
    if last_sel is None:
        b_last = jnp.broadcast_to(b[:, ROWS - 1:ROWS], b.shape)
        m_new = jnp.broadcast_to(m_t[:, ROWS - 1:ROWS], b.shape)
    else:
        both = _exact_dot(jnp.where(head_rows, b, pltpu.roll(m_t, M_HEADS, 0)), last_sel)
        b_last = jnp.where(head_rows, both, 0.0)
        m_new = jnp.where(head_rows, pltpu.roll(both, M_HEADS, 0), 0.0)
    return dict(a_masked=a_masked, bm=b - m_t, gexp=jnp.exp(b + m0 - m_t), enm=jnp.exp(-m_t), m_new=m_new,
                w=jnp.exp(b_last - b + ic - m_new), decay=jnp.exp(b_last + m0 - m_new))


def _out_tail(cat, x, wout_ref, gpost_ref):
    y = jnp.dot(cat, wout_ref[...], preferred_element_type=F32)
    return x + _rms(y, gpost_ref[...])


W_BLOCK = 256
MAIN_BLOCKS = P_MAIN // W_BLOCK
WOUT_BLOCKS = 8


def _main_src_block(i):
    return jnp.where(i < 2, i, jnp.where(i < 8, i + 1, jnp.where(i < 12, i + 3, 2)))


def _weights_kernel(main_ref, vm0_ref, vm1_ref, vm2_ref, vm3_ref, gates_ref, wout_ref, gmh_ref, bgate_ref,
                    wmain_ref, wt_ref, woutb_ref, gmhb_ref, bgb_ref):
    wmain_ref[...] = main_ref[...].T.astype(BF16)

    @pl.when(pl.program_id(0) < WOUT_BLOCKS)
    def _():
        woutb_ref[...] = wout_ref[...].astype(BF16)

    @pl.when(pl.program_id(0) == 0)
    def _():
        wt_ref[...] = jnp.concatenate(
            [vm0_ref[...], vm1_ref[...], vm2_ref[...], vm3_ref[...], gates_ref[...],
             jnp.zeros((T_ROWS - T_GATES - 8, D_MODEL), F32)], axis=0).astype(BF16)
        for h in range(M_HEADS):
            gmhb_ref[h] = jnp.broadcast_to(gmh_ref[h:h + 1, :], (M_DIM, ROWS)).T
        bg_row = jnp.concatenate([bgate_ref[...], jnp.zeros((1, ROWS - 2 * M_HEADS), F32)], axis=1)
        bgb_ref[...] = jnp.broadcast_to(bg_row, (ROWS, ROWS)).T[0:2 * M_HEADS, :]


def _weights_call(w_in_t, w_out, gmh, bgate):
    blk = lambda r: pl.BlockSpec((128, D_MODEL), lambda i, r=r: (r, 0))
    const = lambda shape: pl.BlockSpec(shape, lambda i: (0,) * len(shape))
    assert WOUT_BLOCKS <= MAIN_BLOCKS
    wout_blk = pl.BlockSpec((D_MODEL // WOUT_BLOCKS, D_MODEL), lambda i: (jnp.minimum(i, WOUT_BLOCKS - 1), 0))
    return pl.pallas_call(
        _weights_kernel,
        grid=(MAIN_BLOCKS,),
        in_specs=[
            pl.BlockSpec((W_BLOCK, D_MODEL), lambda i: (_main_src_block(i), 0)),
            blk(VM // 128), blk(VM // 128 + 1), blk(VM // 128 + 2), blk(VM // 128 + 3),
            pl.BlockSpec((8, D_MODEL), lambda i: (GATES // 8, 0)),
            wout_blk, const((M_HEADS, M_DIM)), const((1, 2 * M_HEADS)),
        ],
        out_specs=(
            pl.BlockSpec((D_MODEL, W_BLOCK), lambda i: (0, i)),
            const((T_ROWS, D_MODEL)), wout_blk,
            const((M_HEADS, M_DIM, ROWS)), const((2 * M_HEADS, ROWS)),
        ),
        out_shape=(
            jax.ShapeDtypeStruct((D_MODEL, P_MAIN), BF16),
            jax.ShapeDtypeStruct((T_ROWS, D_MODEL), BF16),
            jax.ShapeDtypeStruct((D_MODEL, D_MODEL), BF16),
            jax.ShapeDtypeStruct((M_HEADS, M_DIM, ROWS), F32),
            jax.ShapeDtypeStruct((2 * M_HEADS, ROWS), F32),
        ),
        compiler_params=pltpu.CompilerParams(
            dimension_semantics=("arbitrary",), vmem_limit_bytes=VMEM_LIMIT_BYTES),
        name="layer_weights",
    )(w_in_t, w_in_t, w_in_t, w_in_t, w_in_t, w_in_t, w_out, gmh, bgate)


def _prompt_bias_t(kv, first):
    j = lax.broadcasted_iota(jnp.int32, (2 * ROWS, ROWS), 0)
    i = lax.broadcasted_iota(jnp.int32, (2 * ROWS, ROWS), 1)
    diff = ROWS + i - j
    valid = (diff >= 0) & (diff < ROWS)
    if first:
        valid = valid & (j >= ROWS)
    dfl = diff.astype(F32)
    return jnp.concatenate(
        [jnp.where(valid, -_slope(kv * ATT_GROUP + g) * dfl, NEG_BIG) * LOG2E for g in range(ATT_GROUP)], axis=1)


def _prompt_kernel(x_ref, xnext_ref, gpre_ref, wmain_ref, wt_ref, bg_ref, sink_ref, gmhb_ref, wout_ref, gpost_ref,
                   y_ref, wk_ref, wv_ref, c_ref, n_ref, m_ref,
                   proj_ref, projt_ref, cat_ref, xn_ref, kprev_ref, vtprev_ref, ct_ref, mst_ref, bias_ref,
                   *, tb, nt):
    bi = pl.program_id(0)
    j = pl.program_id(1)
    nchunks = tb // ROWS

    @pl.when((bi == 0) & (j == 0))
    def _():
        for first in range(2):
            for kv in range(ATT_KV):
                bias_ref[first * ATT_KV + kv] = _prompt_bias_t(kv, first)
        m_ref[...] = jnp.zeros_like(m_ref)

    @pl.when(j == 0)
    def _():
        kprev_ref[...] = jnp.zeros_like(kprev_ref)
        vtprev_ref[...] = jnp.zeros_like(vtprev_ref)
        ct_ref[...] = jnp.zeros_like(ct_ref)
        mst_ref[...] = jnp.zeros_like(mst_ref)

    sub = tb // PROMPT_SUBBLOCKS
    sub_chunks = sub // ROWS

    def norm_rows(sb):
        rows_sb = slice(sb * sub, (sb + 1) * sub)
        xn_ref[rows_sb, :] = _rms(x_ref[0, rows_sb, :], gpre_ref[...]).astype(BF16)

    def proj_cols(sb, c0, c1):
        rows_sb = slice(sb * sub, (sb + 1) * sub)
        proj_ref[rows_sb, c0:c1] = jnp.dot(xn_ref[rows_sb, :], wmain_ref[:, c0:c1], preferred_element_type=F32)

    def proj_t(sb):
        for r0, r1 in ((0, T_SPLIT), (T_SPLIT, T_ROWS)):
            pt = lax.dot_general(wt_ref[r0:r1, :], xn_ref[sb * sub:(sb + 1) * sub, :], NT_DIMS,
                                 preferred_element_type=F32)
            for c in range(sub_chunks):
                projt_ref[sb * sub_chunks + c, r0:r1, :] = pt[:, c * ROWS:(c + 1) * ROWS]

    def out_rows(sb):
        rows_sb = slice(sb * sub, (sb + 1) * sub)
        y_ref[0, rows_sb, :] = _out_tail(cat_ref[rows_sb, :], x_ref[0, rows_sb, :], wout_ref, gpost_ref)

    def proj_pieces(sb):
        bounds = list(range(0, P_MAIN, PROJ_COL_STEP)) + [P_MAIN]
        pieces = [functools.partial(proj_cols, sb, c0, c1) for c0, c1 in zip(bounds[:-1], bounds[1:])]
        return pieces[:2] + [functools.partial(proj_t, sb)] + pieces[2:]

    ri = lax.broadcasted_iota(jnp.int32, (ROWS, ROWS), 0)
    ci = lax.broadcasted_iota(jnp.int32, (ROWS, ROWS), 1)
    mask_t = ri <= ci
    tail_row = lax.broadcasted_iota(jnp.int32, (STATE_ROWS - M_DIM, 2 * ROWS), 0)
    tail_lane = lax.broadcasted_iota(jnp.int32, (STATE_ROWS - M_DIM, 2 * ROWS), 1)

    def chunk(c):
        rows = pl.ds(c * ROWS, ROWS)
        first = ((j == 0) & (c == 0)).astype(jnp.int32)
        yield

        qa = proj_ref[rows, P_QA:P_QA + 512] * (ATT_SCALE * LOG2E)
        kcur = proj_ref[rows, P_KA:P_KA + 128].astype(BF16)
        vtcur = proj_ref[rows, P_VA:P_VA + 128].T.astype(BF16)
        kcat = jnp.concatenate([kprev_ref[...], kcur], axis=0)
        vtcat = jnp.concatenate([vtprev_ref[...], vtcur], axis=1)
        r = _gate_rows(projt_ref[c, T_GATES:T_GATES + 8, :] + bg_ref[...], mst_ref[...], mask_t, None)

        scores, sinks = [], []
        for kv in range(ATT_KV):
            want_hi = kv == 1
            keep = (ci >= ATT_DIM) if want_hi else (ci < ATT_DIM)
            pieces = []
            for g in range(ATT_GROUP):
                hh = kv * ATT_GROUP + g
                blk = qa[:, (hh // 2) * 128:(hh // 2 + 1) * 128]
                if (hh % 2 == 1) != want_hi:
                    blk = pltpu.roll(blk, ATT_DIM, 1)
                pieces.append(jnp.where(keep, blk, 0.0))
            q4 = jnp.concatenate(pieces, axis=0)
            scores.append(_bdot_nt(kcat, q4) + bias_ref[first * ATT_KV + kv])
            sinks.append(jnp.concatenate(
                [jnp.broadcast_to(sink_ref[0:1, kv * ATT_GROUP + g:kv * ATT_GROUP + g + 1] * LOG2E, (1, ROWS))
                 for g in range(ATT_GROUP)], axis=1))
        yield
        w_cols = jnp.concatenate([r["w"], jnp.zeros((ROWS - 8, ROWS), F32)], axis=0).T
        zero_blk = jnp.zeros((M_DIM, M_DIM), BF16)
        q_ts, kfs, raws = [], [], []
        for pair in range(M_HEADS // 2):
            q_pair = [proj_ref[rows, P_QM + h * M_DIM:P_QM + (h + 1) * M_DIM].T for h in (2 * pair, 2 * pair + 1)]
            k_pair = proj_ref[rows, P_KM + 2 * pair * M_DIM:P_KM + 2 * (pair + 1) * M_DIM] * K_SCALE
            q_diag = jnp.concatenate([jnp.concatenate([q_pair[0].astype(BF16), zero_blk], axis=1),
                                      jnp.concatenate([zero_blk, q_pair[1].astype(BF16)], axis=1)], axis=0)
            raw = _bdot(k_pair, q_diag)
            q_ts += q_pair
            kfs += [k_pair[:, 0:M_DIM], k_pair[:, M_DIM:2 * M_DIM]]
            raws += [raw[:, 0:ROWS], raw[:, ROWS:2 * ROWS]]
        lhs_nums, rhs_lows, sts = [], [], []
        for h in range(M_HEADS):
            vt = projt_ref[c, T_VM + h * M_DIM:T_VM + (h + 1) * M_DIM, :]
            ct = ct_ref[h]
            sts.append((raws[h], (kfs[h] * w_cols[:, h:h + 1]).astype(BF16)))
            rhs_lows.append(jnp.concatenate([(q_ts[h] * r["gexp"][h:h + 1, :]).astype(BF16),
                                             jnp.zeros((M_DIM, M_DIM), BF16)], axis=1))
            tail = jnp.concatenate([jnp.zeros((STATE_ROWS - M_DIM, ROWS), F32), ct[M_DIM:STATE_ROWS]], axis=1)
            tail = jnp.where((tail_row == 1) & (tail_lane < ROWS), 1.0, tail)
            lhs_nums.append(jnp.concatenate([jnp.concatenate([vt, ct[0:M_DIM]], axis=1), tail], axis=0).astype(BF16))
        mst_ref[...] = r["m_new"]
        yield

        outs, nums, dens = [], [], []
        for kv in range(ATT_KV):
            s, sink = scores[kv], sinks[kv]
            mx = jnp.maximum(jnp.max(s, axis=0, keepdims=True), sink)
            p = jnp.exp2(s - mx).astype(BF16)
            lhs = jnp.concatenate([vtcat[kv * ATT_DIM:(kv + 1) * ATT_DIM, :],
                                   jnp.ones((ONES_ROWS, 2 * ROWS), BF16)], axis=0)
            o = jnp.dot(lhs, p, preferred_element_type=F32)
            outs.append((o, jnp.exp2(sink - mx)))
        yield
        for h in range(M_HEADS):
            raw, wk = sts[h]
            st = raw * jnp.exp(r["a_masked"][h] + r["bm"][h:h + 1, :])
            rhs = jnp.concatenate([jnp.concatenate([st.astype(BF16), wk], axis=1), rhs_lows[h]], axis=0)
            res = jnp.dot(lhs_nums[h], rhs, preferred_element_type=F32)
            nums.append(res[0:M_DIM, 0:ROWS])
            dens.append(jnp.sum(st, axis=0, keepdims=True) + res[M_DIM:M_DIM + 1, 0:ROWS])
            dec = r["decay"][h:h + 1, 0:1]
            ct_ref[h, 0:M_DIM, :] = dec * ct_ref[h, 0:M_DIM, :] + res[0:M_DIM, ROWS:2 * ROWS]
            ct_ref[h, M_DIM:M_DIM + 1, :] = dec * ct_ref[h, M_DIM:M_DIM + 1, :] + res[M_DIM + 1:M_DIM + 2, ROWS:2 * ROWS]
        yield

        att = []
        for kv in range(ATT_KV):
            o, esink = outs[kv]
            on = o[0:ATT_DIM, :] * (1.0 / (o[ATT_DIM:ATT_DIM + 1, :] + esink))
            for pair in range(2):
                two = jnp.concatenate([on[:, (2 * pair) * ROWS:(2 * pair + 1) * ROWS],
                                       on[:, (2 * pair + 1) * ROWS:(2 * pair + 2) * ROWS]], axis=0)
                att.append(two.T)
        a_out = jnp.concatenate(att, axis=1) * _silu(proj_ref[rows, P_ZA:P_ZA + 512])
        kprev_ref[...] = kcur
        vtprev_ref[...] = vtcur
        yield
        m_out = []
        for h in range(M_HEADS):
            ht = nums[h] * (1.0 / jnp.maximum(jnp.abs(dens[h]), r["enm"][h:h + 1, :]))
            hn = ht * lax.rsqrt(jnp.mean(ht * ht, axis=0, keepdims=True) + NORM_EPS) * gmhb_ref[h]
            m_out.append(_sigmoid(proj_ref[rows, P_OM + h * M_DIM:P_OM + (h + 1) * M_DIM]) * hn.T
                         * _silu(proj_ref[rows, P_ZM + h * M_DIM:P_ZM + (h + 1) * M_DIM]))

        cat_ref[rows, :] = jnp.concatenate([a_out] + m_out, axis=1).astype(BF16)
        yield

    @pl.when((bi == 0) & (j == 0))
    def _():
        for sb in range(PROMPT_SUBBLOCKS):
            norm_rows(sb)
        for piece in proj_pieces(0):
            piece()

    def norm_next():
        xn_ref[...] = _rms(xnext_ref[0], gpre_ref[...]).astype(BF16)

    def run_chunks(sb, fillers):
        n_fill, n_slots, slot = len(fillers), sub_chunks * CHUNK_STAGES, 0
        for c in range(sb * sub_chunks, (sb + 1) * sub_chunks):
            for _ in chunk(c):
                for _ in range(-(-(slot + 1) * n_fill // n_slots) + (-slot * n_fill // n_slots)):
                    fillers.pop(0)()
                slot += 1
        assert slot == n_slots and not fillers

    assert PROMPT_SUBBLOCKS == 2
    run_chunks(0, proj_pieces(1))
    nxt = proj_pieces(0)
    run_chunks(1, [functools.partial(out_rows, 0), norm_next] + nxt[:-TAIL_PIECES])
    out_rows(1)
    for piece in nxt[-TAIL_PIECES:]:
        piece()

    @pl.when(j == nt - 1)
    def _():
        wk_ref[0] = proj_ref[tb - ROWS:tb, P_KA:P_KA + 128].T
        wv_ref[0] = proj_ref[tb - ROWS:tb, P_VA:P_VA + 128].T
        for h in range(M_HEADS):
            ct = ct_ref[h]
            c_ref[0, h] = ct[0:M_DIM].T
            n_ref[0, h:h + 1, :] = ct[M_DIM:M_DIM + 1]
        seq_lane = lax.broadcasted_iota(jnp.int32, m_ref.shape, 1) == bi
        m_ref[...] = jnp.where(seq_lane, mst_ref[0:M_HEADS, 0:m_ref.shape[1]], m_ref[...])


def _prompt_call(x, gpre, wmain, wt, bg, sinks, gmhb, wout, gpost, tb):
    bsz, seq, _ = x.shape
    nt = seq // tb
    full = lambda shape: pl.BlockSpec(shape, lambda b, j: (0,) * len(shape))

    def next_block(b, j):
        wrap = (j + 1 == nt).astype(jnp.int32)
        return (jnp.minimum(b + wrap, bsz - 1), (j + 1) * (1 - wrap), 0)

    out_shapes = (
        jax.ShapeDtypeStruct((bsz, seq, D_MODEL), F32),
        jax.ShapeDtypeStruct((bsz, 128, ROWS), F32),
        jax.ShapeDtypeStruct((bsz, 128, ROWS), F32),
        jax.ShapeDtypeStruct((bsz, M_HEADS, M_DIM, M_DIM), F32),
        jax.ShapeDtypeStruct((bsz, M_HEADS, M_DIM), F32),
        jax.ShapeDtypeStruct((M_HEADS, bsz), F32),
    )
    return pl.pallas_call(
        functools.partial(_prompt_kernel, tb=tb, nt=nt),
        grid=(bsz, nt),
        in_specs=[
            pl.BlockSpec((1, tb, D_MODEL), lambda b, j: (b, j, 0)),
            pl.BlockSpec((1, tb, D_MODEL), next_block),
            full((1, D_MODEL)), full((D_MODEL, P_MAIN)), full((T_ROWS, D_MODEL)), full((8, ROWS)),
            full((1, ATT_HEADS)), full((M_HEADS, M_DIM, ROWS)), full((D_MODEL, D_MODEL)), full((1, D_MODEL)),
        ],
        out_specs=(
            pl.BlockSpec((1, tb, D_MODEL), lambda b, j: (b, j, 0)),
            pl.BlockSpec((1, 128, ROWS), lambda b, j: (b, 0, 0)),
            pl.BlockSpec((1, 128, ROWS), lambda b, j: (b, 0, 0)),
            pl.BlockSpec((1, M_HEADS, M_DIM, M_DIM), lambda b, j: (b, 0, 0, 0)),
            pl.BlockSpec((1, M_HEADS, M_DIM), lambda b, j: (b, 0, 0)),
            pl.BlockSpec((M_HEADS, bsz), lambda b, j: (0, 0)),
        ),
        out_shape=out_shapes,
        scratch_shapes=[
            pltpu.VMEM((tb, P_MAIN), F32),
            pltpu.VMEM((tb // ROWS, T_ROWS, ROWS), F32),
            pltpu.VMEM((tb, D_MODEL), BF16),
            pltpu.VMEM((tb, D_MODEL), BF16),
            pltpu.VMEM((ROWS, 128), BF16),
            pltpu.VMEM((128, ROWS), BF16),
            pltpu.VMEM((M_HEADS, STATE_ROWS, M_DIM), F32),
            pltpu.VMEM((8, ROWS), F32),
            pltpu.VMEM((2 * ATT_KV, 2 * ROWS, ATT_GROUP * ROWS), F32),
        ],
        compiler_params=pltpu.CompilerParams(
            dimension_semantics=("arbitrary", "arbitrary"), vmem_limit_bytes=VMEM_LIMIT_BYTES),
        name="prompt_layer",
    )(x, x, gpre, wmain, wt, bg, sinks, gmhb, wout, gpost)


def _sample_bias_new(head):
    r = lax.broadcasted_iota(jnp.int32, (ROWS, ROWS), 0)
    c = lax.broadcasted_iota(jnp.int32, (ROWS, ROWS), 1)
    valid = ((r >> 3) == (c >> 3)) & (r >= c)
    return jnp.where(valid, -_slope(head) * (r - c).astype(F32), NEG_BIG)


def _sample_bias_cache(head):
    r = lax.broadcasted_iota(jnp.int32, (ROWS, ROWS), 0)
    c = lax.broadcasted_iota(jnp.int32, (ROWS, ROWS), 1)
    diff = (r & (SAMPLE_SEQ - 1)) + ROWS - c
    return jnp.where(diff < ROWS, -_slope(head) * diff.astype(F32), NEG_BIG)


def _sample_kernel(x_ref, kct_ref, vct_ref, cin_ref, nin_ref, m0_ref,
                   gpre_ref, wmain_ref, wt_ref, bg_ref, sink_ref, gmhb_ref, wout_ref, gpost_ref,
                   y_ref, kot_ref, vot_ref, cout_ref, nout_ref, mout_ref,
                   projfull_ref, projt_ref, cat_ref, qh_ref, sc_ref, oc_ref, qc_ref, wv_ref, kt_ref,
                   decb_ref, bn_ref, bc_ref, sn_ref, pn_ref):
    step, part = pl.program_id(0), pl.program_id(1)

    @pl.when((step == 0) & (part == 0))
    def _():
        for hh in range(ATT_HEADS):
            bn_ref[hh] = _sample_bias_new(hh)
            bc_ref[hh] = _sample_bias_cache(hh)
        mout_ref[...] = jnp.zeros_like(mout_ref)

    @pl.when(part == 0)
    def _():
        xn = _rms(x_ref[...], gpre_ref[...]).astype(BF16)
        projfull_ref[...] = jnp.dot(xn, wmain_ref[...], preferred_element_type=F32)
        ptf = lax.dot_general(wt_ref[...], xn, NT_DIMS, preferred_element_type=F32)
        for g in range(SAMPLE_PARTS):
            projt_ref[g] = ptf[:, g * ROWS:(g + 1) * ROWS]

    part_rows = pl.ds(pl.multiple_of(part * ROWS, ROWS), ROWS)
    proj_ref = projfull_ref.at[part_rows]
    pt = projt_ref[part]

    ri = lax.broadcasted_iota(jnp.int32, (ROWS, ROWS), 0)
    ci = lax.broadcasted_iota(jnp.int32, (ROWS, ROWS), 1)
    same_seq = (ri >> 3) == (ci >> 3)
    mask_t = same_seq & (ri <= ci)
    last_sel = (same_seq & ((ri & (SAMPLE_SEQ - 1)) == SAMPLE_SEQ - 1)).astype(BF16)
    row16 = lax.broadcasted_iota(jnp.int32, (SAMPLE_GROUP, ROWS), 0)
    lane16 = lax.broadcasted_iota(jnp.int32, (SAMPLE_GROUP, ROWS), 1)
    seq_of_lane = (lane16 >> 3) == row16
    ones_rows = jnp.ones((ONES_ROWS, ROWS), F32)

    qa = proj_ref[:, P_QA:P_QA + 512] * ATT_SCALE
    for hh in range(ATT_HEADS):
        blk = qa[:, (hh // 2) * 128:(hh // 2 + 1) * 128]
        if hh % 2 == 1:
            blk = pltpu.roll(blk, ATT_DIM, 1)
        qh_ref[hh] = blk[:, 0:ATT_DIM]
    ka = proj_ref[:, P_KA:P_KA + 128]
    kat = ka.T
    vat = proj_ref[:, P_VA:P_VA + 128].T

    keep_new = ci >= ROWS - SAMPLE_SEQ

    for b in range(SAMPLE_GROUP):
        rows = pl.ds(b * SAMPLE_SEQ, SAMPLE_SEQ)
        kct = kct_ref[b]
        shift = (ROWS - SAMPLE_SEQ - b * SAMPLE_SEQ) % ROWS
        new_k, new_v = (kat, vat) if shift == 0 else (pltpu.roll(kat, shift, 1), pltpu.roll(vat, shift, 1))
        kot_ref[b] = jnp.where(keep_new, new_k, pltpu.roll(kct, ROWS - SAMPLE_SEQ, 1))
        vot_ref[b] = jnp.where(keep_new, new_v, pltpu.roll(vct_ref[b], ROWS - SAMPLE_SEQ, 1))
        for kv in range(ATT_KV):
            lhs = jnp.concatenate([qh_ref[kv * ATT_GROUP + g, rows, :] for g in range(ATT_GROUP)], axis=0)
            res = _bdot(lhs, kct[kv * ATT_DIM:(kv + 1) * ATT_DIM, :])
            for g in range(ATT_GROUP):
                sc_ref[kv * ATT_GROUP + g, rows, :] = res[g * SAMPLE_SEQ:(g + 1) * SAMPLE_SEQ]
        for h in range(M_HEADS):
            qc_ref[h, rows, :] = _bdot(proj_ref[rows, P_QM + h * M_DIM:P_QM + (h + 1) * M_DIM], cin_ref[b, h])

    o_new, esinks = [], []
    for hh in range(ATT_HEADS):
        kv = hh // ATT_GROUP
        sn_ref[hh] = _bdot(qh_ref[hh], kat[kv * ATT_DIM:(kv + 1) * ATT_DIM, :]) + bn_ref[hh]
    for hh in range(ATT_HEADS):
        s_n = sn_ref[hh]
        s_c = sc_ref[hh] + bc_ref[hh]
        sink = sink_ref[0:1, hh:hh + 1]
        mx = jnp.maximum(jnp.max(jnp.maximum(s_n, s_c), axis=-1, keepdims=True), sink)
        mx_b = jnp.broadcast_to(mx, (ROWS, ROWS))
        pn_ref[hh] = jnp.exp(s_n - mx_b).astype(BF16)
        sc_ref[hh] = jnp.exp(s_c - mx_b)
        esinks.append(jnp.exp(sink - mx))
    for hh in range(ATT_HEADS):
        kv = hh // ATT_GROUP
        vaug = jnp.concatenate([vat[kv * ATT_DIM:(kv + 1) * ATT_DIM, :], ones_rows], axis=0)
        o_new.append(_bdot_nt(pn_ref[hh], vaug))

    first_seq = (step * SAMPLE_PARTS + part) * SAMPLE_GROUP
    seq_step = ri == first_seq + (ci >> 3)
    m0 = _exact_dot(jnp.concatenate([m0_ref[...], jnp.zeros((8 - M_HEADS, ROWS), F32)], axis=0),
                    seq_step.astype(BF16))
    r = _gate_rows(pt[T_GATES:T_GATES + 8, :] + bg_ref[...], m0, mask_t, last_sel)
    step_seq = (ci == first_seq + (ri >> 3)) & ((ri & (SAMPLE_SEQ - 1)) == 0)
    mout_ref[...] += _exact_dot(r["m_new"], step_seq.astype(BF16))[0:M_HEADS]
    m_out = []
    for h in range(M_HEADS):
        q = proj_ref[:, P_QM + h * M_DIM:P_QM + (h + 1) * M_DIM].astype(BF16)
        kf = proj_ref[:, P_KM + h * M_DIM:P_KM + (h + 1) * M_DIM] * K_SCALE
        k = kf.astype(BF16)
        vt = pt[T_VM + h * M_DIM:T_VM + (h + 1) * M_DIM, :]
        n_h = nin_ref[h]
        r1 = lax.dot_general(jnp.concatenate([k, n_h.astype(BF16)], axis=0), q, NT_DIMS,
                             preferred_element_type=F32)
        st = r1[0:ROWS] * jnp.exp(r["a_masked"][h] + r["bm"][h:h + 1, :])
        q_n = jnp.sum(jnp.where(seq_of_lane, r1[ROWS:ROWS + SAMPLE_GROUP], 0.0), axis=0, keepdims=True)
        g_row = r["gexp"][h:h + 1, :]
        num = _bdot(vt, st) + g_row * qc_ref[h].T
        den = jnp.sum(st, axis=0, keepdims=True) + g_row * q_n
        ht = num * (1.0 / jnp.maximum(jnp.abs(den), r["enm"][h:h + 1, :]))
        hn = ht * lax.rsqrt(jnp.mean(ht * ht, axis=0, keepdims=True) + NORM_EPS) * gmhb_ref[h]
        m_out.append(_sigmoid(proj_ref[:, P_OM + h * M_DIM:P_OM + (h + 1) * M_DIM]) * hn.T
                     * _silu(proj_ref[:, P_ZM + h * M_DIM:P_ZM + (h + 1) * M_DIM]))
        w_row = r["w"][h:h + 1, :]
        dec16 = jnp.sum(jnp.where(lane16 == row16 * SAMPLE_SEQ, r["decay"][h:h + 1, :], 0.0),
                        axis=1, keepdims=True)
        nout_ref[h] = dec16 * n_h + _bdot(jnp.where(seq_of_lane, w_row, 0.0), k)
        decb_ref[h] = jnp.broadcast_to(dec16, (SAMPLE_GROUP, ROWS))
        wv_ref[h] = (vt * w_row).T
        kt_ref[h] = kf.T.astype(BF16)

    for b in range(SAMPLE_GROUP):
        rows = pl.ds(b * SAMPLE_SEQ, SAMPLE_SEQ)
        vct = vct_ref[b]
        for kv in range(ATT_KV):
            vaug = jnp.concatenate([vct[kv * ATT_DIM:(kv + 1) * ATT_DIM, :], ones_rows], axis=0)
            pl_ = jnp.concatenate([sc_ref[kv * ATT_GROUP + g, rows, :] for g in range(ATT_GROUP)], axis=0)
            res = _bdot_nt(pl_, vaug)
            for g in range(ATT_GROUP):
                oc_ref[kv * ATT_GROUP + g, rows, 0:ATT_DIM + ONES_ROWS] = res[g * SAMPLE_SEQ:(g + 1) * SAMPLE_SEQ]
        in_seq = (ri >> 3) == b
        for h in range(M_HEADS):
            upd = jnp.dot(kt_ref[h], jnp.where(in_seq, wv_ref[h], 0.0).astype(BF16), preferred_element_type=F32)
            cout_ref[b, h] = decb_ref[h, b:b + 1, :] * cin_ref[b, h] + upd

    att = []
    for pair in range(ATT_HEADS // 2):
        halves = []
        for hh in (2 * pair, 2 * pair + 1):
            on, oc = o_new[hh], oc_ref[hh]
            den = on[:, ATT_DIM:ATT_DIM + 1] + oc[:, ATT_DIM:ATT_DIM + 1] + esinks[hh]
            halves.append((on[:, 0:ATT_DIM] + oc[:, 0:ATT_DIM]) * (1.0 / den))
        att.append(jnp.concatenate(halves, axis=1))
    a_out = jnp.concatenate(att, axis=1) * _silu(proj_ref[:, P_ZA:P_ZA + 512])
    cat_ref[part_rows, :] = jnp.concatenate([a_out] + m_out, axis=1).astype(BF16)

    @pl.when(part == SAMPLE_PARTS - 1)
    def _():
        y_ref[...] = _out_tail(cat_ref[...], x_ref[...], wout_ref, gpost_ref)


def _sample_call(x, kct, vct, cin, nin, m0, gpre, wmain, wt, bg, sinks, gmhb, wout, gpost):
    nrows = x.shape[0]
    ngroups = nrows // ROWS
    nseq = ngroups * SAMPLE_GROUP
    parts = SAMPLE_PARTS
    assert ngroups % parts == 0
    assert nseq == ROWS, "the per-sequence stabiliser state is handled as one 128-lane row per head"
    full = lambda shape: pl.BlockSpec(shape, lambda i, p: (0,) * len(shape))
    grp = SAMPLE_GROUP
    group = lambda i, p: i * parts + p
    out_shapes = (
        jax.ShapeDtypeStruct((nrows, D_MODEL), F32),
        jax.ShapeDtypeStruct((nseq, 128, ROWS), F32),
        jax.ShapeDtypeStruct((nseq, 128, ROWS), F32),
        jax.ShapeDtypeStruct((nseq, M_HEADS, M_DIM, M_DIM), F32),
        jax.ShapeDtypeStruct((M_HEADS, nseq, M_DIM), F32),
        jax.ShapeDtypeStruct((M_HEADS, nseq), F32),
    )
    return pl.pallas_call(
        _sample_kernel,
        grid=(ngroups // parts, parts),
        in_specs=[
            pl.BlockSpec((parts * ROWS, D_MODEL), lambda i, p: (i, 0)),
            pl.BlockSpec((grp, 128, ROWS), lambda i, p: (group(i, p), 0, 0)),
            pl.BlockSpec((grp, 128, ROWS), lambda i, p: (group(i, p), 0, 0)),
            pl.BlockSpec((grp, M_HEADS, M_DIM, M_DIM), lambda i, p: (group(i, p), 0, 0, 0)),
            pl.BlockSpec((M_HEADS, grp, M_DIM), lambda i, p: (0, group(i, p), 0)),
            full((M_HEADS, nseq)),
            full((1, D_MODEL)), full((D_MODEL, P_MAIN)), full((T_ROWS, D_MODEL)), full((8, ROWS)),
            full((1, ATT_HEADS)), full((M_HEADS, M_DIM, ROWS)), full((D_MODEL, D_MODEL)), full((1, D_MODEL)),
        ],
        out_specs=(
            pl.BlockSpec((parts * ROWS, D_MODEL), lambda i, p: (i, 0)),
            pl.BlockSpec((grp, 128, ROWS), lambda i, p: (group(i, p), 0, 0)),
            pl.BlockSpec((grp, 128, ROWS), lambda i, p: (group(i, p), 0, 0)),
            pl.BlockSpec((grp, M_HEADS, M_DIM, M_DIM), lambda i, p: (group(i, p), 0, 0, 0)),
            pl.BlockSpec((M_HEADS, grp, M_DIM), lambda i, p: (0, group(i, p), 0)),
            full((M_HEADS, nseq)),
        ),
        out_shape=out_shapes,
        scratch_shapes=[
            pltpu.VMEM((parts * ROWS, P_MAIN), F32),
            pltpu.VMEM((parts, T_ROWS, ROWS), F32),
            pltpu.VMEM((parts * ROWS, D_MODEL), BF16),
            pltpu.VMEM((ATT_HEADS, ROWS, ATT_DIM), F32),
            pltpu.VMEM((ATT_HEADS, ROWS, ROWS), F32),
            pltpu.VMEM((ATT_HEADS, ROWS, ROWS), F32),
            pltpu.VMEM((M_HEADS, ROWS, M_DIM), F32),
            pltpu.VMEM((M_HEADS, ROWS, M_DIM), F32),
            pltpu.VMEM((M_HEADS, M_DIM, ROWS), BF16),
            pltpu.VMEM((M_HEADS, SAMPLE_GROUP, ROWS), F32),
            pltpu.VMEM((ATT_HEADS, ROWS, ROWS), F32),
            pltpu.VMEM((ATT_HEADS, ROWS, ROWS), F32),
            pltpu.VMEM((ATT_HEADS, ROWS, ROWS), F32),
            pltpu.VMEM((ATT_HEADS, ROWS, ROWS), BF16),
        ],
        compiler_params=pltpu.CompilerParams(
            dimension_semantics=("arbitrary", "arbitrary"), vmem_limit_bytes=VMEM_LIMIT_BYTES),
        name="sample_layer",
    )(x, kct, vct, cin, nin, m0, gpre, wmain, wt, bg, sinks, gmhb, wout, gpost)


PROMPT_BLOCK = 512
PROMPT_SUBBLOCKS = 2
PROJ_COL_STEP = 512
CHUNK_STAGES = 7
TAIL_PIECES = 3


def _window_in(cache):
    nseq = cache.shape[0]
    return cache.transpose(0, 2, 3, 1).reshape(nseq, ATT_KV * ATT_DIM, ROWS)


def _window_out(win_t):
    nseq = win_t.shape[0]
    return win_t.reshape(nseq, ATT_KV, ATT_DIM, ROWS).transpose(0, 3, 1, 2)[None]


def kernel(x_prompt, x_sample, cache_win_k, cache_win_v, state_C, state_n, state_m,
           g_pre, w_in, b_gate, attn_sinks, g_mh, w_out, g_post):
    depth = g_pre.shape[0]
    assert depth == 1, "single-layer trunk"
    nseq, sseq, _ = x_sample.shape
    assert sseq == SAMPLE_SEQ and nseq % SAMPLE_GROUP == 0

    gpre = g_pre[0].reshape(1, D_MODEL)
    gpost = g_post[0].reshape(1, D_MODEL)
    wmain, wt, wout, gmhb, bg = _weights_call(w_in[0].T, w_out[0], g_mh[0], b_gate)
    sinks = attn_sinks[0].reshape(1, ATT_HEADS)

    yp, wkp, wvp, cp, np_, mp = _prompt_call(x_prompt, gpre, wmain, wt, bg, sinks, gmhb, wout, gpost, PROMPT_BLOCK)

    ys, wks, wvs, cs, ns, ms = _sample_call(
        x_sample.reshape(nseq * sseq, D_MODEL), _window_in(cache_win_k[0]), _window_in(cache_win_v[0]),
        state_C[0], state_n[0].transpose(1, 0, 2), state_m[0].T,
        gpre, wmain, wt, bg, sinks, gmhb, wout, gpost)

    return (yp, ys.reshape(nseq, sseq, D_MODEL), _window_out(wkp), _window_out(wvp),
            cp[None], np_[None], mp.T[None],
            _window_out(wks), _window_out(wvs), cs[None], ns.transpose(1, 0, 2)[None], ms.T[None])
```

```python
import functools

import jax
import jax.numpy as jnp
from jax import lax
from jax.experimental import pallas as pl
from jax.experimental.pallas import tpu as pltpu

F32 = jnp.float32
BF16 = jnp.bfloat16

D_MODEL = 1024
ROWS = 128
ATT_HEADS, ATT_KV, ATT_GROUP, ATT_DIM = 8, 2, 4, 64
M_HEADS, M_DIM = 4, 128
NORM_EPS = 1e-6
NEG_BIG = -1e30
ATT_SCALE = ATT_DIM ** -0.5
LOG2E = 1.4426950408889634
K_SCALE = M_DIM ** -0.5

VM, GATES = 2304, 3840

P_QA, P_ZA, P_QM, P_KM, P_OM, P_ZM, P_KA, P_VA = 0, 512, 1024, 1536, 2048, 2560, 3072, 3200
P_MAIN = 3328
T_VM, T_GATES = 0, 512
T_ROWS = 528
T_SPLIT = 272
STATE_ROWS = 144
ONES_ROWS = 16

SAMPLE_SEQ = 8
SAMPLE_GROUP = ROWS // SAMPLE_SEQ
SAMPLE_PARTS = 2

VMEM_BYTES_V7X = 64 * 1024 * 1024
VMEM_LIMIT_BYTES = VMEM_BYTES_V7X * 7 // 8
NT_DIMS = (((1,), (1,)), ((), ()))


def _rms(x, g):
    return x * lax.rsqrt(jnp.mean(x * x, axis=-1, keepdims=True) + NORM_EPS) * g


def _sigmoid(x):
    return 0.5 + 0.5 * jnp.tanh(0.5 * x)


def _silu(x):
    h = 0.5 * x
    return h + h * jnp.tanh(h)


def _log_sigmoid(x):
    return -(jnp.maximum(-x, 0.0) + jnp.log1p(jnp.exp(-jnp.abs(x))))


def _slope(head):
    return 2.0 ** -(head + 1)


def _bdot(a, b):
    return jnp.dot(a.astype(BF16), b.astype(BF16), preferred_element_type=F32)


def _bdot_nt(a, b):
    return lax.dot_general(a.astype(BF16), b.astype(BF16), NT_DIMS, preferred_element_type=F32)


def _exact_dot(x, m):
    hi = x.astype(BF16).astype(F32)
    mid = (x - hi).astype(BF16).astype(F32)
    lo = (x - hi - mid).astype(BF16).astype(F32)
    parts = jnp.dot(jnp.concatenate([hi, mid, lo, jnp.zeros_like(hi)], axis=0).astype(BF16), m,
                    preferred_element_type=F32)
    return parts[0:8] + parts[8:16] + parts[16:24]


def _gate_rows(x, m0, mask_t, last_sel):
    row = lax.broadcasted_iota(jnp.int32, (8, ROWS), 0)
    head_rows = row < M_HEADS
    ic = jnp.where(head_rows, x, 0.0)
    fc = jnp.where(head_rows, _log_sigmoid(pltpu.roll(x, M_HEADS, 0)), 0.0)
    b = _exact_dot(fc, mask_t.astype(BF16))
    a = ic - b
    a_cols = jnp.concatenate([a, jnp.zeros((ROWS - 8, ROWS), F32)], axis=0).T
    a_masked = [jnp.where(mask_t, a_cols[:, h:h + 1], -jnp.inf) for h in range(M_HEADS)]
    cm = jnp.concatenate([jnp.max(am, axis=0, keepdims=True) for am in a_masked]
                         + [jnp.zeros((8 - M_HEADS, ROWS), F32)], axis=0)
    m_t = jnp.maximum(b + m0, b + cm)
    if last_sel is None:
        b_last = jnp.broadcast_to(b[:, ROWS - 1:ROWS], b.shape)
        m_new = jnp.broadcast_to(m_t[:, ROWS - 1:ROWS], b.shape)
    else:
        both = _exact_dot(jnp.where(head_rows, b, pltpu.roll(m_t, M_HEADS, 0)), last_sel)
        b_last = jnp.where(head_rows, both, 0.0)
        m_new = jnp.where(head_rows, pltpu.roll(both, M_HEADS, 0), 0.0)
    return dict(a_masked=a_masked, bm=b - m_t, gexp=jnp.exp(b + m0 - m_t), enm=jnp.exp(-m_t), m_new=m_new,
                w=jnp.exp(b_last - b + ic - m_new), decay=jnp.exp(b_last + m0 - m_new))


def _out_tail(cat, x, wout_ref, gpost_ref):
    y = jnp.dot(cat, wout_ref[...], preferred_element_type=F32)
    return x + _rms(y, gpost_ref[...])


W_BLOCK = 256
MAIN_BLOCKS = P_MAIN // W_BLOCK
W_STEP_BLOCKS = 2
W_STEPS = -(-MAIN_BLOCKS // W_STEP_BLOCKS)
P_MAIN_PAD = W_STEPS * W_STEP_BLOCKS * W_BLOCK
WOUT_BLOCKS = 4


def _main_src_block(i):
    return jnp.where(i < 2, i, jnp.where(i < 8, i + 1, jnp.where(i < 12, i + 3, 2)))


def _weights_kernel(main0_ref, main1_ref, vm0_ref, vm1_ref, vm2_ref, vm3_ref, gates_ref, wout_ref, gmh_ref,
                    bgate_ref, wmain_ref, wt_ref, woutb_ref, gmhb_ref, bgb_ref):
    for s, src_ref in enumerate((main0_ref, main1_ref)):
        wmain_ref[:, s * W_BLOCK:(s + 1) * W_BLOCK] = src_ref[...].T.astype(BF16)

    @pl.when(pl.program_id(0) < WOUT_BLOCKS)
    def _():
        woutb_ref[...] = wout_ref[...].astype(BF16)

    @pl.when(pl.program_id(0) == 0)
    def _():
        wt_ref[...] = jnp.concatenate(
            [vm0_ref[...], vm1_ref[...], vm2_ref[...], vm3_ref[...], gates_ref[...],
             jnp.zeros((T_ROWS - T_GATES - 8, D_MODEL), F32)], axis=0).astype(BF16)
        for h in range(M_HEADS):
            gmhb_ref[h] = jnp.broadcast_to(gmh_ref[h:h + 1, :], (M_DIM, ROWS)).T
        bg_row = jnp.concatenate([bgate_ref[...], jnp.zeros((1, ROWS - 2 * M_HEADS), F32)], axis=1)
        bgb_ref[...] = jnp.broadcast_to(bg_row, (ROWS, ROWS)).T[0:2 * M_HEADS, :]


def _weights_call(w_in_t, w_out, gmh, bgate):
    blk = lambda r: pl.BlockSpec((128, D_MODEL), lambda i, r=r: (r, 0))
    const = lambda shape: pl.BlockSpec(shape, lambda i: (0,) * len(shape))
    assert WOUT_BLOCKS <= W_STEPS and W_STEP_BLOCKS == 2
    wout_blk = pl.BlockSpec((D_MODEL // WOUT_BLOCKS, D_MODEL), lambda i: (jnp.minimum(i, WOUT_BLOCKS - 1), 0))
    main_blk = lambda s: pl.BlockSpec(
        (W_BLOCK, D_MODEL), lambda i, s=s: (_main_src_block(jnp.minimum(W_STEP_BLOCKS * i + s, MAIN_BLOCKS - 1)), 0))
    return pl.pallas_call(
        _weights_kernel,
        grid=(W_STEPS,),
        in_specs=[
            main_blk(0), main_blk(1),
            blk(VM // 128), blk(VM // 128 + 1), blk(VM // 128 + 2), blk(VM // 128 + 3),
            pl.BlockSpec((8, D_MODEL), lambda i: (GATES // 8, 0)),
            wout_blk, const((M_HEADS, M_DIM)), const((1, 2 * M_HEADS)),
        ],
        out_specs=(
            pl.BlockSpec((D_MODEL, W_STEP_BLOCKS * W_BLOCK), lambda i: (0, i)),
            const((T_ROWS, D_MODEL)), wout_blk,
            const((M_HEADS, M_DIM, ROWS)), const((2 * M_HEADS, ROWS)),
        ),
        out_shape=(
            jax.ShapeDtypeStruct((D_MODEL, P_MAIN_PAD), BF16),
            jax.ShapeDtypeStruct((T_ROWS, D_MODEL), BF16),
            jax.ShapeDtypeStruct((D_MODEL, D_MODEL), BF16),
            jax.ShapeDtypeStruct((M_HEADS, M_DIM, ROWS), F32),
            jax.ShapeDtypeStruct((2 * M_HEADS, ROWS), F32),
        ),
        compiler_params=pltpu.CompilerParams(
            dimension_semantics=("arbitrary",), vmem_limit_bytes=VMEM_LIMIT_BYTES),
        name="layer_weights",
    )(w_in_t, w_in_t, w_in_t, w_in_t, w_in_t, w_in_t, w_in_t, w_out, gmh, bgate)


def _prompt_bias_t(kv, first):
    j = lax.broadcasted_iota(jnp.int32, (2 * ROWS, ROWS), 0)
    i = lax.broadcasted_iota(jnp.int32, (2 * ROWS, ROWS), 1)
    diff = ROWS + i - j
    valid = (diff >= 0) & (diff < ROWS)
    if first:
        valid = valid & (j >= ROWS)
    dfl = diff.astype(F32)
    return jnp.concatenate(
        [jnp.where(valid, -_slope(kv * ATT_GROUP + g) * dfl, NEG_BIG) * LOG2E for g in range(ATT_GROUP)], axis=1)


def _prompt_kernel(x_ref, xnext_ref, gpre_ref, wmain_ref, wt_ref, bg_ref, sink_ref, gmhb_ref, wout_ref, gpost_ref,
                   y_ref, wk_ref, wv_ref, c_ref, n_ref, m_ref,
                   proj_ref, projt_ref, cat_ref, xn_ref, kprev_ref, vtprev_ref, ct_ref, mst_ref, bias_ref,
                   *, tb, nt):
    bi = pl.program_id(0)
    j = pl.program_id(1)
    nchunks = tb // ROWS

    @pl.when((bi == 0) & (j == 0))
    def _():
        for first in range(2):
            for kv in range(ATT_KV):
                bias_ref[first * ATT_KV + kv] = _prompt_bias_t(kv, first)
        m_ref[...] = jnp.zeros_like(m_ref)

    @pl.when(j == 0)
    def _():
        kprev_ref[...] = jnp.zeros_like(kprev_ref)
        vtprev_ref[...] = jnp.zeros_like(vtprev_ref)
        ct_ref[...] = jnp.zeros_like(ct_ref)
        mst_ref[...] = jnp.zeros_like(mst_ref)

    sub = tb // PROMPT_SUBBLOCKS
    sub_chunks = sub // ROWS

    def norm_rows(sb):
        rows_sb = slice(sb * sub, (sb + 1) * sub)
        xn_ref[rows_sb, :] = _rms(x_ref[0, rows_sb, :], gpre_ref[...]).astype(BF16)

    def proj_cols(sb, c0, c1):
        rows_sb = slice(sb * sub, (sb + 1) * sub)
        proj_ref[rows_sb, c0:c1] = jnp.dot(xn_ref[rows_sb, :], wmain_ref[:, c0:c1], preferred_element_type=F32)

    def proj_t(sb):
        for r0, r1 in ((0, T_SPLIT), (T_SPLIT, T_ROWS)):
            pt = lax.dot_general(wt_ref[r0:r1, :], xn_ref[sb * sub:(sb + 1) * sub, :], NT_DIMS,
                                 preferred_element_type=F32)
            for c in range(sub_chunks):
                projt_ref[sb * sub_chunks + c, r0:r1, :] = pt[:, c * ROWS:(c + 1) * ROWS]

    def out_rows(sb):
        rows_sb = slice(sb * sub, (sb + 1) * sub)
        y_ref[0, rows_sb, :] = _out_tail(cat_ref[rows_sb, :], x_ref[0, rows_sb, :], wout_ref, gpost_ref)

    def proj_pieces(sb):
        bounds = list(range(0, P_MAIN, PROJ_COL_STEP)) + [P_MAIN]
        pieces = [functools.partial(proj_cols, sb, c0, c1) for c0, c1 in zip(bounds[:-1], bounds[1:])]
        return pieces[:2] + [functools.partial(proj_t, sb)] + pieces[2:]

    ri = lax.broadcasted_iota(jnp.int32, (ROWS, ROWS), 0)
    ci = lax.broadcasted_iota(jnp.int32, (ROWS, ROWS), 1)
    mask_t = ri <= ci
    tail_row = lax.broadcasted_iota(jnp.int32, (STATE_ROWS - M_DIM, 2 * ROWS), 0)
    tail_lane = lax.broadcasted_iota(jnp.int32, (STATE_ROWS - M_DIM, 2 * ROWS), 1)

    def chunk(c):
        rows = pl.ds(c * ROWS, ROWS)
        first = ((j == 0) & (c == 0)).astype(jnp.int32)
        yield

        qa = proj_ref[rows, P_QA:P_QA + 512] * (ATT_SCALE * LOG2E)
        kcur = proj_ref[rows, P_KA:P_KA + 128].astype(BF16)
        vtcur = proj_ref[rows, P_VA:P_VA + 128].T.astype(BF16)
        kcat = jnp.concatenate([kprev_ref[...], kcur], axis=0)
        vtcat = jnp.concatenate([vtprev_ref[...], vtcur], axis=1)
        r = _gate_rows(projt_ref[c, T_GATES:T_GATES + 8, :] + bg_ref[...], mst_ref[...], mask_t, None)

        scores, sinks = [], []
        for kv in range(ATT_KV):
            want_hi = kv == 1
            keep = (ci >= ATT_DIM) if want_hi else (ci < ATT_DIM)
            pieces = []
            for g in range(ATT_GROUP):
                hh = kv * ATT_GROUP + g
                blk = qa[:, (hh // 2) * 128:(hh // 2 + 1) * 128]
                if (hh % 2 == 1) != want_hi:
                    blk = pltpu.roll(blk, ATT_DIM, 1)
                pieces.append(jnp.where(keep, blk, 0.0))
            q4 = jnp.concatenate(pieces, axis=0)
            scores.append(_bdot_nt(kcat, q4) + bias_ref[first * ATT_KV + kv])
            sinks.append(jnp.concatenate(
                [jnp.broadcast_to(sink_ref[0:1, kv * ATT_GROUP + g:kv * ATT_GROUP + g + 1] * LOG2E, (1, ROWS))
                 for g in range(ATT_GROUP)], axis=1))
        yield
        w_cols = jnp.concatenate([r["w"], jnp.zeros((ROWS - 8, ROWS), F32)], axis=0).T
        zero_blk = jnp.zeros((M_DIM, M_DIM), BF16)
        q_ts, kfs, raws = [], [], []
        for pair in range(M_HEADS // 2):
            q_pair = [proj_ref[rows, P_QM + h * M_DIM:P_QM + (h + 1) * M_DIM].T for h in (2 * pair, 2 * pair + 1)]
            k_pair = proj_ref[rows, P_KM + 2 * pair * M_DIM:P_KM + 2 * (pair + 1) * M_DIM] * K_SCALE
            q_diag = jnp.concatenate([jnp.concatenate([q_pair[0].astype(BF16), zero_blk], axis=1),
                                      jnp.concatenate([zero_blk, q_pair[1].astype(BF16)], axis=1)], axis=0)
            raw = _bdot(k_pair, q_diag)
            q_ts += q_pair
            kfs += [k_pair[:, 0:M_DIM], k_pair[:, M_DIM:2 * M_DIM]]
            raws += [raw[:, 0:ROWS], raw[:, ROWS:2 * ROWS]]
        lhs_nums, rhs_lows, sts = [], [], []
        for h in range(M_HEADS):
            vt = projt_ref[c, T_VM + h * M_DIM:T_VM + (h + 1) * M_DIM, :]
            ct = ct_ref[h]
            sts.append((raws[h], (kfs[h] * w_cols[:, h:h + 1]).astype(BF16)))
            rhs_lows.append(jnp.concatenate([(q_ts[h] * r["gexp"][h:h + 1, :]).astype(BF16),
                                             jnp.zeros((M_DIM, M_DIM), BF16)], axis=1))
            tail = jnp.concatenate([jnp.zeros((STATE_ROWS - M_DIM, ROWS), F32), ct[M_DIM:STATE_ROWS]], axis=1)
            tail = jnp.where((tail_row == 1) & (tail_lane < ROWS), 1.0, tail)
            lhs_nums.append(jnp.concatenate([jnp.concatenate([vt, ct[0:M_DIM]], axis=1), tail], axis=0).astype(BF16))
        mst_ref[...] = r["m_new"]
        yield

        outs, nums, dens = [], [], []
        for kv in range(ATT_KV):
            s, sink = scores[kv], sinks[kv]
            mx = jnp.maximum(jnp.max(s, axis=0, keepdims=True), sink)
            p = jnp.exp2(s - mx).astype(BF16)
            lhs = jnp.concatenate([vtcat[kv * ATT_DIM:(kv + 1) * ATT_DIM, :],
                                   jnp.ones((ONES_ROWS, 2 * ROWS), BF16)], axis=0)
            o = jnp.dot(lhs, p, preferred_element_type=F32)
            outs.append((o, jnp.exp2(sink - mx)))
        yield
        for h in range(M_HEADS):
            raw, wk = sts[h]
            st = raw * jnp.exp(r["a_masked"][h] + r["bm"][h:h + 1, :])
            rhs = jnp.concatenate([jnp.concatenate([st.astype(BF16), wk], axis=1), rhs_lows[h]], axis=0)
            res = jnp.dot(lhs_nums[h], rhs, preferred_element_type=F32)
            nums.append(res[0:M_DIM, 0:ROWS])
            dens.append(jnp.sum(st, axis=0, keepdims=True) + res[M_DIM:M_DIM + 1, 0:ROWS])
            dec = r["decay"][h:h + 1, 0:1]
            ct_ref[h, 0:M_DIM, :] = dec * ct_ref[h, 0:M_DIM, :] + res[0:M_DIM, ROWS:2 * ROWS]
            ct_ref[h, M_DIM:M_DIM + 1, :] = dec * ct_ref[h, M_DIM:M_DIM + 1, :] + res[M_DIM + 1:M_DIM + 2, ROWS:2 * ROWS]
        yield

        att = []
        for kv in range(ATT_KV):
            o, esink = outs[kv]
            on = o[0:ATT_DIM, :] * (1.0 / (o[ATT_DIM:ATT_DIM + 1, :] + esink))
            for pair in range(2):
                two = jnp.concatenate([on[:, (2 * pair) * ROWS:(2 * pair + 1) * ROWS],
                                       on[:, (2 * pair + 1) * ROWS:(2 * pair + 2) * ROWS]], axis=0)
                att.append(two.T)
        a_out = jnp.concatenate(att, axis=1) * _silu(proj_ref[rows, P_ZA:P_ZA + 512])
        kprev_ref[...] = kcur
        vtprev_ref[...] = vtcur
        yield
        m_out = []
        for h in range(M_HEADS):
            ht = nums[h] * (1.0 / jnp.maximum(jnp.abs(dens[h]), r["enm"][h:h + 1, :]))
            hn = ht * lax.rsqrt(jnp.mean(ht * ht, axis=0, keepdims=True) + NORM_EPS) * gmhb_ref[h]
            m_out.append(_sigmoid(proj_ref[rows, P_OM + h * M_DIM:P_OM + (h + 1) * M_DIM]) * hn.T
                         * _silu(proj_ref[rows, P_ZM + h * M_DIM:P_ZM + (h + 1) * M_DIM]))

        cat_ref[rows, :] = jnp.concatenate([a_out] + m_out, axis=1).astype(BF16)
        yield

    @pl.when((bi == 0) & (j == 0))
    def _():
        for sb in range(PROMPT_SUBBLOCKS):
            norm_rows(sb)
        for piece in proj_pieces(0):
            piece()

    def norm_next():
        xn_ref[...] = _rms(xnext_ref[0], gpre_ref[...]).astype(BF16)

    def run_chunks(sb, fillers):
        n_fill, n_slots, slot = len(fillers), sub_chunks * CHUNK_STAGES, 0
        for c in range(sb * sub_chunks, (sb + 1) * sub_chunks):
            for _ in chunk(c):
                for _ in range(-(-(slot + 1) * n_fill // n_slots) + (-slot * n_fill // n_slots)):
                    fillers.pop(0)()
                slot += 1
        assert slot == n_slots and not fillers

    assert PROMPT_SUBBLOCKS == 2
    run_chunks(0, proj_pieces(1))
    nxt = proj_pieces(0)
    run_chunks(1, [functools.partial(out_rows, 0), norm_next] + nxt[:-TAIL_PIECES])
    out_rows(1)
    for piece in nxt[-TAIL_PIECES:]:
        piece()

    @pl.when(j == nt - 1)
    def _():
        wk_ref[0] = proj_ref[tb - ROWS:tb, P_KA:P_KA + 128].T
        wv_ref[0] = proj_ref[tb - ROWS:tb, P_VA:P_VA + 128].T
        for h in range(M_HEADS):
            ct = ct_ref[h]
            c_ref[0, h] = ct[0:M_DIM].T
            n_ref[0, h:h + 1, :] = ct[M_DIM:M_DIM + 1]
        seq_lane = lax.broadcasted_iota(jnp.int32, m_ref.shape, 1) == bi
        m_ref[...] = jnp.where(seq_lane, mst_ref[0:M_HEADS, 0:m_ref.shape[1]], m_ref[...])


def _prompt_call(x, gpre, wmain, wt, bg, sinks, gmhb, wout, gpost, tb):
    bsz, seq, _ = x.shape
    nt = seq // tb
    full = lambda shape: pl.BlockSpec(shape, lambda b, j: (0,) * len(shape))

    def next_block(b, j):
        wrap = (j + 1 == nt).astype(jnp.int32)
        return (jnp.minimum(b + wrap, bsz - 1), (j + 1) * (1 - wrap), 0)

    out_shapes = (
        jax.ShapeDtypeStruct((bsz, seq, D_MODEL), F32),
        jax.ShapeDtypeStruct((bsz, 128, ROWS), F32),
        jax.ShapeDtypeStruct((bsz, 128, ROWS), F32),
        jax.ShapeDtypeStruct((bsz, M_HEADS, M_DIM, M_DIM), F32),
        jax.ShapeDtypeStruct((bsz, M_HEADS, M_DIM), F32),
        jax.ShapeDtypeStruct((M_HEADS, bsz), F32),
    )
    return pl.pallas_call(
        functools.partial(_prompt_kernel, tb=tb, nt=nt),
        grid=(bsz, nt),
        in_specs=[
            pl.BlockSpec((1, tb, D_MODEL), lambda b, j: (b, j, 0)),
            pl.BlockSpec((1, tb, D_MODEL), next_block),
            full((1, D_MODEL)), full((D_MODEL, P_MAIN_PAD)), full((T_ROWS, D_MODEL)), full((8, ROWS)),
            full((1, ATT_HEADS)), full((M_HEADS, M_DIM, ROWS)), full((D_MODEL, D_MODEL)), full((1, D_MODEL)),
        ],
        out_specs=(
            pl.BlockSpec((1, tb, D_MODEL), lambda b, j: (b, j, 0)),
            pl.BlockSpec((1, 128, ROWS), lambda b, j: (b, 0, 0)),
            pl.BlockSpec((1, 128, ROWS), lambda b, j: (b, 0, 0)),
            pl.BlockSpec((1, M_HEADS, M_DIM, M_DIM), lambda b, j: (b, 0, 0, 0)),
            pl.BlockSpec((1, M_HEADS, M_DIM), lambda b, j: (b, 0, 0)),
            pl.BlockSpec((M_HEADS, bsz), lambda b, j: (0, 0)),
        ),
        out_shape=out_shapes,
        scratch_shapes=[
            pltpu.VMEM((tb, P_MAIN), F32),
            pltpu.VMEM((tb // ROWS, T_ROWS, ROWS), F32),
            pltpu.VMEM((tb, D_MODEL), BF16),
            pltpu.VMEM((tb, D_MODEL), BF16),
            pltpu.VMEM((ROWS, 128), BF16),
            pltpu.VMEM((128, ROWS), BF16),
            pltpu.VMEM((M_HEADS, STATE_ROWS, M_DIM), F32),
            pltpu.VMEM((8, ROWS), F32),
            pltpu.VMEM((2 * ATT_KV, 2 * ROWS, ATT_GROUP * ROWS), F32),
        ],
        compiler_params=pltpu.CompilerParams(
            dimension_semantics=("arbitrary", "arbitrary"), vmem_limit_bytes=VMEM_LIMIT_BYTES),
        name="prompt_layer",
    )(x, x, gpre, wmain, wt, bg, sinks, gmhb, wout, gpost)


def _sample_bias_new(head):
    r = lax.broadcasted_iota(jnp.int32, (ROWS, ROWS), 0)
    c = lax.broadcasted_iota(jnp.int32, (ROWS, ROWS), 1)
    valid = ((r >> 3) == (c >> 3)) & (r >= c)
    return jnp.where(valid, -_slope(head) * (r - c).astype(F32), NEG_BIG)


def _sample_bias_cache(head):
    r = lax.broadcasted_iota(jnp.int32, (ROWS, ROWS), 0)
    c = lax.broadcasted_iota(jnp.int32, (ROWS, ROWS), 1)
    diff = (r & (SAMPLE_SEQ - 1)) + ROWS - c
    return jnp.where(diff < ROWS, -_slope(head) * diff.astype(F32), NEG_BIG)


def _sample_kernel(x_ref, kct_ref, vct_ref, cin_ref, nin_ref, m0_ref,
                   gpre_ref, wmain_ref, wt_ref, bg_ref, sink_ref, gmhb_ref, wout_ref, gpost_ref,
                   y_ref, kot_ref, vot_ref, cout_ref, nout_ref, mout_ref,
                   projfull_ref, projt_ref, cat_ref, qh_ref, sc_ref, oc_ref, qc_ref, wv_ref, kt_ref,
                   decb_ref, bn_ref, bc_ref, sn_ref, pn_ref):
    step, part = pl.program_id(0), pl.program_id(1)

    @pl.when((step == 0) & (part == 0))
    def _():
        for hh in range(ATT_HEADS):
            bn_ref[hh] = _sample_bias_new(hh)
            bc_ref[hh] = _sample_bias_cache(hh)
        mout_ref[...] = jnp.zeros_like(mout_ref)

    @pl.when(part == 0)
    def _():
        xn = _rms(x_ref[...], gpre_ref[...]).astype(BF16)
        projfull_ref[...] = jnp.dot(xn, wmain_ref[:, 0:P_MAIN], preferred_element_type=F32)
        ptf = lax.dot_general(wt_ref[...], xn, NT_DIMS, preferred_element_type=F32)
        for g in range(SAMPLE_PARTS):
            projt_ref[g] = ptf[:, g * ROWS:(g + 1) * ROWS]

    part_rows = pl.ds(pl.multiple_of(part * ROWS, ROWS), ROWS)
    proj_ref = projfull_ref.at[part_rows]
    pt = projt_ref[part]

    ri = lax.broadcasted_iota(jnp.int32, (ROWS, ROWS), 0)
    ci = lax.broadcasted_iota(jnp.int32, (ROWS, ROWS), 1)
    same_seq = (ri >> 3) == (ci >> 3)
    mask_t = same_seq & (ri <= ci)
    last_sel = (same_seq & ((ri & (SAMPLE_SEQ - 1)) == SAMPLE_SEQ - 1)).astype(BF16)
    row16 = lax.broadcasted_iota(jnp.int32, (SAMPLE_GROUP, ROWS), 0)
    lane16 = lax.broadcasted_iota(jnp.int32, (SAMPLE_GROUP, ROWS), 1)
    seq_of_lane = (lane16 >> 3) == row16
    ones_rows = jnp.ones((ONES_ROWS, ROWS), F32)

    qa = proj_ref[:, P_QA:P_QA + 512] * ATT_SCALE
    for hh in range(ATT_HEADS):
        blk = qa[:, (hh // 2) * 128:(hh // 2 + 1) * 128]
        if hh % 2 == 1:
            blk = pltpu.roll(blk, ATT_DIM, 1)
        qh_ref[hh] = blk[:, 0:ATT_DIM]
    ka = proj_ref[:, P_KA:P_KA + 128]
    kat = ka.T
    vat = proj_ref[:, P_VA:P_VA + 128].T

    keep_new = ci >= ROWS - SAMPLE_SEQ

    for b in range(SAMPLE_GROUP):
        rows = pl.ds(b * SAMPLE_SEQ, SAMPLE_SEQ)
        kct = kct_ref[b]
        shift = (ROWS - SAMPLE_SEQ - b * SAMPLE_SEQ) % ROWS
        new_k, new_v = (kat, vat) if shift == 0 else (pltpu.roll(kat, shift, 1), pltpu.roll(vat, shift, 1))
        kot_ref[b] = jnp.where(keep_new, new_k, pltpu.roll(kct, ROWS - SAMPLE_SEQ, 1))
        vot_ref[b] = jnp.where(keep_new, new_v, pltpu.roll(vct_ref[b], ROWS - SAMPLE_SEQ, 1))
        for kv in range(ATT_KV):
            lhs = jnp.concatenate([qh_ref[kv * ATT_GROUP + g, rows, :] for g in range(ATT_GROUP)], axis=0)
            res = _bdot(lhs, kct[kv * ATT_DIM:(kv + 1) * ATT_DIM, :])
            for g in range(ATT_GROUP):
                sc_ref[kv * ATT_GROUP + g, rows, :] = res[g * SAMPLE_SEQ:(g + 1) * SAMPLE_SEQ]
        for h in range(M_HEADS):
            qc_ref[h, rows, :] = _bdot(proj_ref[rows, P_QM + h * M_DIM:P_QM + (h + 1) * M_DIM], cin_ref[b, h])

    o_new, esinks = [], []
    for hh in range(ATT_HEADS):
        kv = hh // ATT_GROUP
        sn_ref[hh] = _bdot(qh_ref[hh], kat[kv * ATT_DIM:(kv + 1) * ATT_DIM, :]) + bn_ref[hh]
    for hh in range(ATT_HEADS):
        s_n = sn_ref[hh]
        s_c = sc_ref[hh] + bc_ref[hh]
        sink = sink_ref[0:1, hh:hh + 1]
        mx = jnp.maximum(jnp.max(jnp.maximum(s_n, s_c), axis=-1, keepdims=True), sink)
        mx_b = jnp.broadcast_to(mx, (ROWS, ROWS))
        pn_ref[hh] = jnp.exp(s_n - mx_b).astype(BF16)
        sc_ref[hh] = jnp.exp(s_c - mx_b)
        esinks.append(jnp.exp(sink - mx))
    for hh in range(ATT_HEADS):
        kv = hh // ATT_GROUP
        vaug = jnp.concatenate([vat[kv * ATT_DIM:(kv + 1) * ATT_DIM, :], ones_rows], axis=0)
        o_new.append(_bdot_nt(pn_ref[hh], vaug))

    first_seq = (step * SAMPLE_PARTS + part) * SAMPLE_GROUP
    seq_step = ri == first_seq + (ci >> 3)
    m0 = _exact_dot(jnp.concatenate([m0_ref[...], jnp.zeros((8 - M_HEADS, ROWS), F32)], axis=0),
                    seq_step.astype(BF16))
    r = _gate_rows(pt[T_GATES:T_GATES + 8, :] + bg_ref[...], m0, mask_t, last_sel)
    step_seq = (ci == first_seq + (ri >> 3)) & ((ri & (SAMPLE_SEQ - 1)) == 0)
    mout_ref[...] += _exact_dot(r["m_new"], step_seq.astype(BF16))[0:M_HEADS]
    m_out = []
    for h in range(M_HEADS):
        q = proj_ref[:, P_QM + h * M_DIM:P_QM + (h + 1) * M_DIM].astype(BF16)
        kf = proj_ref[:, P_KM + h * M_DIM:P_KM + (h + 1) * M_DIM] * K_SCALE
        k = kf.astype(BF16)
        vt = pt[T_VM + h * M_DIM:T_VM + (h + 1) * M_DIM, :]
        n_h = nin_ref[h]
        r1 = lax.dot_general(jnp.concatenate([k, n_h.astype(BF16)], axis=0), q, NT_DIMS,
                             preferred_element_type=F32)
        st = r1[0:ROWS] * jnp.exp(r["a_masked"][h] + r["bm"][h:h + 1, :])
        q_n = jnp.sum(jnp.where(seq_of_lane, r1[ROWS:ROWS + SAMPLE_GROUP], 0.0), axis=0, keepdims=True)
        g_row = r["gexp"][h:h + 1, :]
        num = _bdot(vt, st) + g_row * qc_ref[h].T
        den = jnp.sum(st, axis=0, keepdims=True) + g_row * q_n
        ht = num * (1.0 / jnp.maximum(jnp.abs(den), r["enm"][h:h + 1, :]))
        hn = ht * lax.rsqrt(jnp.mean(ht * ht, axis=0, keepdims=True) + NORM_EPS) * gmhb_ref[h]
        m_out.append(_sigmoid(proj_ref[:, P_OM + h * M_DIM:P_OM + (h + 1) * M_DIM]) * hn.T
                     * _silu(proj_ref[:, P_ZM + h * M_DIM:P_ZM + (h + 1) * M_DIM]))
        w_row = r["w"][h:h + 1, :]
        dec16 = jnp.sum(jnp.where(lane16 == row16 * SAMPLE_SEQ, r["decay"][h:h + 1, :], 0.0),
                        axis=1, keepdims=True)
        nout_ref[h] = dec16 * n_h + _bdot(jnp.where(seq_of_lane, w_row, 0.0), k)
        decb_ref[h] = jnp.broadcast_to(dec16, (SAMPLE_GROUP, ROWS))
        wv_ref[h] = (vt * w_row).T
        kt_ref[h] = kf.T.astype(BF16)

    for b in range(SAMPLE_GROUP):
        rows = pl.ds(b * SAMPLE_SEQ, SAMPLE_SEQ)
        vct = vct_ref[b]
        for kv in range(ATT_KV):
            vaug = jnp.concatenate([vct[kv * ATT_DIM:(kv + 1) * ATT_DIM, :], ones_rows], axis=0)
            pl_ = jnp.concatenate([sc_ref[kv * ATT_GROUP + g, rows, :] for g in range(ATT_GROUP)], axis=0)
            res = _bdot_nt(pl_, vaug)
            for g in range(ATT_GROUP):
                oc_ref[kv * ATT_GROUP + g, rows, 0:ATT_DIM + ONES_ROWS] = res[g * SAMPLE_SEQ:(g + 1) * SAMPLE_SEQ]
        in_seq = (ri >> 3) == b
        for h in range(M_HEADS):
            upd = jnp.dot(kt_ref[h], jnp.where(in_seq, wv_ref[h], 0.0).astype(BF16), preferred_element_type=F32)
            cout_ref[b, h] = decb_ref[h, b:b + 1, :] * cin_ref[b, h] + upd

    att = []
    for pair in range(ATT_HEADS // 2):
        halves = []
        for hh in (2 * pair, 2 * pair + 1):
            on, oc = o_new[hh], oc_ref[hh]
            den = on[:, ATT_DIM:ATT_DIM + 1] + oc[:, ATT_DIM:ATT_DIM + 1] + esinks[hh]
            halves.append((on[:, 0:ATT_DIM] + oc[:, 0:ATT_DIM]) * (1.0 / den))
        att.append(jnp.concatenate(halves, axis=1))
    a_out = jnp.concatenate(att, axis=1) * _silu(proj_ref[:, P_ZA:P_ZA + 512])
    cat_ref[part_rows, :] = jnp.concatenate([a_out] + m_out, axis=1).astype(BF16)

    @pl.when(part == SAMPLE_PARTS - 1)
    def _():
        y_ref[...] = _out_tail(cat_ref[...], x_ref[...], wout_ref, gpost_ref)


def _sample_call(x, kct, vct, cin, nin, m0, gpre, wmain, wt, bg, sinks, gmhb, wout, gpost):
    nrows = x.shape[0]
    ngroups = nrows // ROWS
    nseq = ngroups * SAMPLE_GROUP
    parts = SAMPLE_PARTS
    assert ngroups % parts == 0
    assert nseq == ROWS, "the per-sequence stabiliser state is handled as one 128-lane row per head"
    full = lambda shape: pl.BlockSpec(shape, lambda i, p: (0,) * len(shape))
    grp = SAMPLE_GROUP
    group = lambda i, p: i * parts + p
    out_shapes = (
        jax.ShapeDtypeStruct((nrows, D_MODEL), F32),
        jax.ShapeDtypeStruct((nseq, 128, ROWS), F32),
        jax.ShapeDtypeStruct((nseq, 128, ROWS), F32),
        jax.ShapeDtypeStruct((nseq, M_HEADS, M_DIM, M_DIM), F32),
        jax.ShapeDtypeStruct((M_HEADS, nseq, M_DIM), F32),
        jax.ShapeDtypeStruct((M_HEADS, nseq), F32),
    )
    return pl.pallas_call(
        _sample_kernel,
        grid=(ngroups // parts, parts),
        in_specs=[
            pl.BlockSpec((parts * ROWS, D_MODEL), lambda i, p: (i, 0)),
            pl.BlockSpec((grp, 128, ROWS), lambda i, p: (group(i, p), 0, 0)),
            pl.BlockSpec((grp, 128, ROWS), lambda i, p: (group(i, p), 0, 0)),
            pl.BlockSpec((grp, M_HEADS, M_DIM, M_DIM), lambda i, p: (group(i, p), 0, 0, 0)),
            pl.BlockSpec((M_HEADS, grp, M_DIM), lambda i, p: (0, group(i, p), 0)),
            full((M_HEADS, nseq)),
            full((1, D_MODEL)), full((D_MODEL, P_MAIN_PAD)), full((T_ROWS, D_MODEL)), full((8, ROWS)),
            full((1, ATT_HEADS)), full((M_HEADS, M_DIM, ROWS)), full((D_MODEL, D_MODEL)), full((1, D_MODEL)),
        ],
        out_specs=(
            pl.BlockSpec((parts * ROWS, D_MODEL), lambda i, p: (i, 0)),
            pl.BlockSpec((grp, 128, ROWS), lambda i, p: (group(i, p), 0, 0)),
            pl.BlockSpec((grp, 128, ROWS), lambda i, p: (group(i, p), 0, 0)),
            pl.BlockSpec((grp, M_HEADS, M_DIM, M_DIM), lambda i, p: (group(i, p), 0, 0, 0)),
            pl.BlockSpec((M_HEADS, grp, M_DIM), lambda i, p: (0, group(i, p), 0)),
            full((M_HEADS, nseq)),
        ),
        out_shape=out_shapes,
        scratch_shapes=[
            pltpu.VMEM((parts * ROWS, P_MAIN), F32),
            pltpu.VMEM((parts, T_ROWS, ROWS), F32),
            pltpu.VMEM((parts * ROWS, D_MODEL), BF16),
            pltpu.VMEM((ATT_HEADS, ROWS, ATT_DIM), F32),
            pltpu.VMEM((ATT_HEADS, ROWS, ROWS), F32),
            pltpu.VMEM((ATT_HEADS, ROWS, ROWS), F32),
            pltpu.VMEM((M_HEADS, ROWS, M_DIM), F32),
            pltpu.VMEM((M_HEADS, ROWS, M_DIM), F32),
            pltpu.VMEM((M_HEADS, M_DIM, ROWS), BF16),
            pltpu.VMEM((M_HEADS, SAMPLE_GROUP, ROWS), F32),
            pltpu.VMEM((ATT_HEADS, ROWS, ROWS), F32),
            pltpu.VMEM((ATT_HEADS, ROWS, ROWS), F32),
            pltpu.VMEM((ATT_HEADS, ROWS, ROWS), F32),
            pltpu.VMEM((ATT_HEADS, ROWS, ROWS), BF16),
        ],
        compiler_params=pltpu.CompilerParams(
            dimension_semantics=("arbitrary", "arbitrary"), vmem_limit_bytes=VMEM_LIMIT_BYTES),
        name="sample_layer",
    )(x, kct, vct, cin, nin, m0, gpre, wmain, wt, bg, sinks, gmhb, wout, gpost)


PROMPT_BLOCK = 512
PROMPT_SUBBLOCKS = 2
PROJ_COL_STEP = 512
CHUNK_STAGES = 7
TAIL_PIECES = 3


def _window_in(cache):
    nseq = cache.shape[0]
    return cache.transpose(0, 2, 3, 1).reshape(nseq, ATT_KV * ATT_DIM, ROWS)


def _window_out(win_t):
    nseq = win_t.shape[0]
    return win_t.reshape(nseq, ATT_KV, ATT_DIM, ROWS).transpose(0, 3, 1, 2)[None]


def kernel(x_prompt, x_sample, cache_win_k, cache_win_v, state_C, state_n, state_m,
           g_pre, w_in, b_gate, attn_sinks, g_mh, w_out, g_post):
    depth = g_pre.shape[0]
    assert depth == 1, "single-layer trunk"
    nseq, sseq, _ = x_sample.shape
    assert sseq == SAMPLE_SEQ and nseq % SAMPLE_GROUP == 0

    gpre = g_pre[0].reshape(1, D_MODEL)
    gpost = g_post[0].reshape(1, D_MODEL)
    wmain, wt, wout, gmhb, bg = _weights_call(w_in[0].T, w_out[0], g_mh[0], b_gate)
    sinks = attn_sinks[0].reshape(1, ATT_HEADS)

    yp, wkp, wvp, cp, np_, mp = _prompt_call(x_prompt, gpre, wmain, wt, bg, sinks, gmhb, wout, gpost, PROMPT_BLOCK)

    ys, wks, wvs, cs, ns, ms = _sample_call(
        x_sample.reshape(nseq * sseq, D_MODEL), _window_in(cache_win_k[0]), _window_in(cache_win_v[0]),
        state_C[0], state_n[0].transpose(1, 0, 2), state_m[0].T,
        gpre, wmain, wt, bg, sinks, gmhb, wout, gpost)

    return (yp, ys.reshape(nseq, sseq, D_MODEL), _window_out(wkp), _window_out(wvp),
            cp[None], np_[None], mp.T[None],
            _window_out(wks), _window_out(wvs), cs[None], ns.transpose(1, 0, 2)[None], ms.T[None])
```

```python
import functools

import jax
import jax.numpy as jnp
from jax import lax
from jax.experimental import pallas as pl
from jax.experimental.pallas import tpu as pltpu

F32 = jnp.float32
BF16 = jnp.bfloat16

D_MODEL = 1024
ROWS = 128
ATT_HEADS, ATT_KV, ATT_GROUP, ATT_DIM = 8, 2, 4, 64
M_HEADS, M_DIM = 4, 128
NORM_EPS = 1e-6
NEG_BIG = -1e30
ATT_SCALE = ATT_DIM ** -0.5
LOG2E = 1.4426950408889634
K_SCALE = M_DIM ** -0.5

VM, GATES = 2304, 3840

P_QA, P_ZA, P_QM, P_KM, P_OM, P_ZM, P_KA, P_VA = 0, 512, 1024, 1536, 2048, 2560, 3072, 3200
P_MAIN = 3328
T_VM, T_GATES = 0, 512
T_ROWS = 528
T_SPLIT = 272
STATE_ROWS = 144
ONES_ROWS = 16

SAMPLE_SEQ = 8
SAMPLE_GROUP = ROWS // SAMPLE_SEQ
SAMPLE_PARTS = 2
STATE_SLOTS = 3

VMEM_BYTES_V7X = 64 * 1024 * 1024
VMEM_LIMIT_BYTES = VMEM_BYTES_V7X * 7 // 8
NT_DIMS = (((1,), (1,)), ((), ()))


def _rms(x, g):
    return x * lax.rsqrt(jnp.mean(x * x, axis=-1, keepdims=True) + NORM_EPS) * g


def _sigmoid(x):
    return 0.5 + 0.5 * jnp.tanh(0.5 * x)


def _silu(x):
    h = 0.5 * x
    return h + h * jnp.tanh(h)


def _log_sigmoid(x):
    return -(jnp.maximum(-x, 0.0) + jnp.log1p(jnp.exp(-jnp.abs(x))))


def _slope(head):
    return 2.0 ** -(head + 1)


def _bdot(a, b):
    return jnp.dot(a.astype(BF16), b.astype(BF16), preferred_element_type=F32)


def _bdot_nt(a, b):
    return lax.dot_general(a.astype(BF16), b.astype(BF16), NT_DIMS, preferred_element_type=F32)


def _exact_dot(x, m):
    hi = x.astype(BF16).astype(F32)
    mid = (x - hi).astype(BF16).astype(F32)
    lo = (x - hi - mid).astype(BF16).astype(F32)
    parts = jnp.dot(jnp.concatenate([hi, mid, lo, jnp.zeros_like(hi)], axis=0).astype(BF16), m,
                    preferred_element_type=F32)
    return parts[0:8] + parts[8:16] + parts[16:24]


def _gate_rows(x, m0, mask_t, last_sel):
    row = lax.broadcasted_iota(jnp.int32, (8, ROWS), 0)
    head_rows = row < M_HEADS
    ic = jnp.where(head_rows, x, 0.0)
    fc = jnp.where(head_rows, _log_sigmoid(pltpu.roll(x, M_HEADS, 0)), 0.0)
    b = _exact_dot(fc, mask_t.astype(BF16))
    a = ic - b
    a_cols = jnp.concatenate([a, jnp.zeros((ROWS - 8, ROWS), F32)], axis=0).T
    a_masked = [jnp.where(mask_t, a_cols[:, h:h + 1], -jnp.inf) for h in range(M_HEADS)]
    cm = jnp.concatenate([jnp.max(am, axis=0, keepdims=True) for am in a_masked]
                         + [jnp.zeros((8 - M_HEADS, ROWS), F32)], axis=0)
    m_t = jnp.maximum(b + m0, b + cm)
    if last_sel is None:
        b_last = jnp.broadcast_to(b[:, ROWS - 1:ROWS], b.shape)
        m_new = jnp.broadcast_to(m_t[:, ROWS - 1:ROWS], b.shape)
    else:
        both = _exact_dot(jnp.where(head_rows, b, pltpu.roll(m_t, M_HEADS, 0)), last_sel)
        b_last = jnp.where(head_rows, both, 0.0)
        m_new = jnp.where(head_rows, pltpu.roll(both, M_HEADS, 0), 0.0)
    return dict(a_masked=a_masked, bm=b - m_t, gexp=jnp.exp(b + m0 - m_t), enm=jnp.exp(-m_t), m_new=m_new,
                w=jnp.exp(b_last - b + ic - m_new), decay=jnp.exp(b_last + m0 - m_new))


def _out_tail(cat, x, wout_ref, gpost_ref):
    y = jnp.dot(cat, wout_ref[...], preferred_element_type=F32)
    return x + _rms(y, gpost_ref[...])


W_BLOCK = 256
MAIN_BLOCKS = P_MAIN // W_BLOCK
W_STEP_BLOCKS = 2
W_STEPS = -(-MAIN_BLOCKS // W_STEP_BLOCKS)
P_MAIN_PAD = W_STEPS * W_STEP_BLOCKS * W_BLOCK
WOUT_BLOCKS = 4


def _main_src_block(i):
    return jnp.where(i < 2, i, jnp.where(i < 8, i + 1, jnp.where(i < 12, i + 3, 2)))


def _weights_kernel(main0_ref, main1_ref, vm0_ref, vm1_ref, vm2_ref, vm3_ref, gates_ref, wout_ref, gmh_ref,
                    bgate_ref, wmain_ref, wt_ref, woutb_ref, gmhb_ref, bgb_ref):
    for s, src_ref in enumerate((main0_ref, main1_ref)):
        wmain_ref[:, s * W_BLOCK:(s + 1) * W_BLOCK] = src_ref[...].T.astype(BF16)

    @pl.when(pl.program_id(0) < WOUT_BLOCKS)
    def _():
        woutb_ref[...] = wout_ref[...].astype(BF16)

    @pl.when(pl.program_id(0) == 0)
    def _():
        wt_ref[...] = jnp.concatenate(
            [vm0_ref[...], vm1_ref[...], vm2_ref[...], vm3_ref[...], gates_ref[...],
             jnp.zeros((T_ROWS - T_GATES - 8, D_MODEL), F32)], axis=0).astype(BF16)
        for h in range(M_HEADS):
            gmhb_ref[h] = jnp.broadcast_to(gmh_ref[h:h + 1, :], (M_DIM, ROWS)).T
        bg_row = jnp.concatenate([bgate_ref[...], jnp.zeros((1, ROWS - 2 * M_HEADS), F32)], axis=1)
        bgb_ref[...] = jnp.broadcast_to(bg_row, (ROWS, ROWS)).T[0:2 * M_HEADS, :]


def _weights_call(w_in_t, w_out, gmh, bgate):
    blk = lambda r: pl.BlockSpec((128, D_MODEL), lambda i, r=r: (r, 0))
    const = lambda shape: pl.BlockSpec(shape, lambda i: (0,) * len(shape))
    assert WOUT_BLOCKS <= W_STEPS and W_STEP_BLOCKS == 2
    wout_blk = pl.BlockSpec((D_MODEL // WOUT_BLOCKS, D_MODEL), lambda i: (jnp.minimum(i, WOUT_BLOCKS - 1), 0))
    main_blk = lambda s: pl.BlockSpec(
        (W_BLOCK, D_MODEL), lambda i, s=s: (_main_src_block(jnp.minimum(W_STEP_BLOCKS * i + s, MAIN_BLOCKS - 1)), 0))
    return pl.pallas_call(
        _weights_kernel,
        grid=(W_STEPS,),
        in_specs=[
            main_blk(0), main_blk(1),
            blk(VM // 128), blk(VM // 128 + 1), blk(VM // 128 + 2), blk(VM // 128 + 3),
            pl.BlockSpec((8, D_MODEL), lambda i: (GATES // 8, 0)),
            wout_blk, const((M_HEADS, M_DIM)), const((1, 2 * M_HEADS)),
        ],
        out_specs=(
            pl.BlockSpec((D_MODEL, W_STEP_BLOCKS * W_BLOCK), lambda i: (0, i)),
            const((T_ROWS, D_MODEL)), wout_blk,
            const((M_HEADS, M_DIM, ROWS)), const((2 * M_HEADS, ROWS)),
        ),
        out_shape=(
            jax.ShapeDtypeStruct((D_MODEL, P_MAIN_PAD), BF16),
            jax.ShapeDtypeStruct((T_ROWS, D_MODEL), BF16),
            jax.ShapeDtypeStruct((D_MODEL, D_MODEL), BF16),
            jax.ShapeDtypeStruct((M_HEADS, M_DIM, ROWS), F32),
            jax.ShapeDtypeStruct((2 * M_HEADS, ROWS), F32),
        ),
        compiler_params=pltpu.CompilerParams(
            dimension_semantics=("arbitrary",), vmem_limit_bytes=VMEM_LIMIT_BYTES),
        name="layer_weights",
    )(w_in_t, w_in_t, w_in_t, w_in_t, w_in_t, w_in_t, w_in_t, w_out, gmh, bgate)


def _prompt_bias_t(kv, first):
    j = lax.broadcasted_iota(jnp.int32, (2 * ROWS, ROWS), 0)
    i = lax.broadcasted_iota(jnp.int32, (2 * ROWS, ROWS), 1)
    diff = ROWS + i - j
    valid = (diff >= 0) & (diff < ROWS)
    if first:
        valid = valid & (j >= ROWS)
    dfl = diff.astype(F32)
    return jnp.concatenate(
        [jnp.where(valid, -_slope(kv * ATT_GROUP + g) * dfl, NEG_BIG) * LOG2E for g in range(ATT_GROUP)], axis=1)


def _prompt_kernel(x_ref, xnext_ref, gpre_ref, wmain_ref, wt_ref, bg_ref, sink_ref, gmhb_ref, wout_ref, gpost_ref,
                   y_ref, wk_ref, wv_ref, c_ref, n_ref, m_ref,
                   proj_ref, projt_ref, cat_ref, xn_ref, kprev_ref, vtprev_ref, ct_ref, mst_ref, bias_ref,
                   *, tb, nt):
    bi = pl.program_id(0)
    j = pl.program_id(1)
    nchunks = tb // ROWS

    @pl.when((bi == 0) & (j == 0))
    def _():
        for first in range(2):
            for kv in range(ATT_KV):
                bias_ref[first * ATT_KV + kv] = _prompt_bias_t(kv, first)
        m_ref[...] = jnp.zeros_like(m_ref)

    @pl.when(j == 0)
    def _():
        kprev_ref[...] = jnp.zeros_like(kprev_ref)
        vtprev_ref[...] = jnp.zeros_like(vtprev_ref)
        ct_ref[...] = jnp.zeros_like(ct_ref)
        mst_ref[...] = jnp.zeros_like(mst_ref)

    sub = tb // PROMPT_SUBBLOCKS
    sub_chunks = sub // ROWS

    def norm_rows(sb):
        rows_sb = slice(sb * sub, (sb + 1) * sub)
        xn_ref[rows_sb, :] = _rms(x_ref[0, rows_sb, :], gpre_ref[...]).astype(BF16)

    def proj_cols(sb, c0, c1):
        rows_sb = slice(sb * sub, (sb + 1) * sub)
        proj_ref[rows_sb, c0:c1] = jnp.dot(xn_ref[rows_sb, :], wmain_ref[:, c0:c1], preferred_element_type=F32)

    def proj_t(sb):
        for r0, r1 in ((0, T_SPLIT), (T_SPLIT, T_ROWS)):
            pt = lax.dot_general(wt_ref[r0:r1, :], xn_ref[sb * sub:(sb + 1) * sub, :], NT_DIMS,
                                 preferred_element_type=F32)
            for c in range(sub_chunks):
                projt_ref[sb * sub_chunks + c, r0:r1, :] = pt[:, c * ROWS:(c + 1) * ROWS]

    def out_rows(sb):
        rows_sb = slice(sb * sub, (sb + 1) * sub)
        y_ref[0, rows_sb, :] = _out_tail(cat_ref[rows_sb, :], x_ref[0, rows_sb, :], wout_ref, gpost_ref)

    def proj_pieces(sb):
        bounds = list(range(0, P_MAIN, PROJ_COL_STEP)) + [P_MAIN]
        pieces = [functools.partial(proj_cols, sb, c0, c1) for c0, c1 in zip(bounds[:-1], bounds[1:])]
        return pieces[:2] + [functools.partial(proj_t, sb)] + pieces[2:]

    ri = lax.broadcasted_iota(jnp.int32, (ROWS, ROWS), 0)
    ci = lax.broadcasted_iota(jnp.int32, (ROWS, ROWS), 1)
    mask_t = ri <= ci
    tail_row = lax.broadcasted_iota(jnp.int32, (STATE_ROWS - M_DIM, 2 * ROWS), 0)
    tail_lane = lax.broadcasted_iota(jnp.int32, (STATE_ROWS - M_DIM, 2 * ROWS), 1)

    def chunk(c):
        rows = pl.ds(c * ROWS, ROWS)
        first = ((j == 0) & (c == 0)).astype(jnp.int32)
        yield

        qa = proj_ref[rows, P_QA:P_QA + 512] * (ATT_SCALE * LOG2E)
        kcur = proj_ref[rows, P_KA:P_KA + 128].astype(BF16)
        vtcur = proj_ref[rows, P_VA:P_VA + 128].T.astype(BF16)
        kcat = jnp.concatenate([kprev_ref[...], kcur], axis=0)
        vtcat = jnp.concatenate([vtprev_ref[...], vtcur], axis=1)
        r = _gate_rows(projt_ref[c, T_GATES:T_GATES + 8, :] + bg_ref[...], mst_ref[...], mask_t, None)

        scores, sinks = [], []
        for kv in range(ATT_KV):
            want_hi = kv == 1
            keep = (ci >= ATT_DIM) if want_hi else (ci < ATT_DIM)
            pieces = []
            for g in range(ATT_GROUP):
                hh = kv * ATT_GROUP + g
                blk = qa[:, (hh // 2) * 128:(hh // 2 + 1) * 128]
                if (hh % 2 == 1) != want_hi:
                    blk = pltpu.roll(blk, ATT_DIM, 1)
                pieces.append(jnp.where(keep, blk, 0.0))
            q4 = jnp.concatenate(pieces, axis=0)
            scores.append(_bdot_nt(kcat, q4) + bias_ref[first * ATT_KV + kv])
            sinks.append(jnp.concatenate(
                [jnp.broadcast_to(sink_ref[0:1, kv * ATT_GROUP + g:kv * ATT_GROUP + g + 1] * LOG2E, (1, ROWS))
                 for g in range(ATT_GROUP)], axis=1))
        yield
        w_cols = jnp.concatenate([r["w"], jnp.zeros((ROWS - 8, ROWS), F32)], axis=0).T
        zero_blk = jnp.zeros((M_DIM, M_DIM), BF16)
        q_ts, kfs, raws = [], [], []
        for pair in range(M_HEADS // 2):
            q_pair = [proj_ref[rows, P_QM + h * M_DIM:P_QM + (h + 1) * M_DIM].T for h in (2 * pair, 2 * pair + 1)]
            k_pair = proj_ref[rows, P_KM + 2 * pair * M_DIM:P_KM + 2 * (pair + 1) * M_DIM] * K_SCALE
            q_diag = jnp.concatenate([jnp.concatenate([q_pair[0].astype(BF16), zero_blk], axis=1),
                                      jnp.concatenate([zero_blk, q_pair[1].astype(BF16)], axis=1)], axis=0)
            raw = _bdot(k_pair, q_diag)
            q_ts += q_pair
            kfs += [k_pair[:, 0:M_DIM], k_pair[:, M_DIM:2 * M_DIM]]
            raws += [raw[:, 0:ROWS], raw[:, ROWS:2 * ROWS]]
        lhs_nums, rhs_lows, sts = [], [], []
        for h in range(M_HEADS):
            vt = projt_ref[c, T_VM + h * M_DIM:T_VM + (h + 1) * M_DIM, :]
            ct = ct_ref[h]
            sts.append((raws[h], (kfs[h] * w_cols[:, h:h + 1]).astype(BF16)))
            rhs_lows.append(jnp.concatenate([(q_ts[h] * r["gexp"][h:h + 1, :]).astype(BF16),
                                             jnp.zeros((M_DIM, M_DIM), BF16)], axis=1))
            tail = jnp.concatenate([jnp.zeros((STATE_ROWS - M_DIM, ROWS), F32), ct[M_DIM:STATE_ROWS]], axis=1)
            tail = jnp.where((tail_row == 1) & (tail_lane < ROWS), 1.0, tail)
            lhs_nums.append(jnp.concatenate([jnp.concatenate([vt, ct[0:M_DIM]], axis=1), tail], axis=0).astype(BF16))
        mst_ref[...] = r["m_new"]
        yield

        outs, nums, dens = [], [], []
        for kv in range(ATT_KV):
            s, sink = scores[kv], sinks[kv]
            mx = jnp.maximum(jnp.max(s, axis=0, keepdims=True), sink)
            p = jnp.exp2(s - mx).astype(BF16)
            lhs = jnp.concatenate([vtcat[kv * ATT_DIM:(kv + 1) * ATT_DIM, :],
                                   jnp.ones((ONES_ROWS, 2 * ROWS), BF16)], axis=0)
            o = jnp.dot(lhs, p, preferred_element_type=F32)
            outs.append((o, jnp.exp2(sink - mx)))
        yield
        for h in range(M_HEADS):
            raw, wk = sts[h]
            st = raw * jnp.exp(r["a_masked"][h] + r["bm"][h:h + 1, :])
            rhs = jnp.concatenate([jnp.concatenate([st.astype(BF16), wk], axis=1), rhs_lows[h]], axis=0)
            res = jnp.dot(lhs_nums[h], rhs, preferred_element_type=F32)
            nums.append(res[0:M_DIM, 0:ROWS])
            dens.append(jnp.sum(st, axis=0, keepdims=True) + res[M_DIM:M_DIM + 1, 0:ROWS])
            dec = r["decay"][h:h + 1, 0:1]
            ct_ref[h, 0:M_DIM, :] = dec * ct_ref[h, 0:M_DIM, :] + res[0:M_DIM, ROWS:2 * ROWS]
            ct_ref[h, M_DIM:M_DIM + 1, :] = dec * ct_ref[h, M_DIM:M_DIM + 1, :] + res[M_DIM + 1:M_DIM + 2, ROWS:2 * ROWS]
        yield

        att = []
        for kv in range(ATT_KV):
            o, esink = outs[kv]
            on = o[0:ATT_DIM, :] * (1.0 / (o[ATT_DIM:ATT_DIM + 1, :] + esink))
            for pair in range(2):
                two = jnp.concatenate([on[:, (2 * pair) * ROWS:(2 * pair + 1) * ROWS],
                                       on[:, (2 * pair + 1) * ROWS:(2 * pair + 2) * ROWS]], axis=0)
                att.append(two.T)
        a_out = jnp.concatenate(att, axis=1) * _silu(proj_ref[rows, P_ZA:P_ZA + 512])
        kprev_ref[...] = kcur
        vtprev_ref[...] = vtcur
        yield
        m_out = []
        for h in range(M_HEADS):
            ht = nums[h] * (1.0 / jnp.maximum(jnp.abs(dens[h]), r["enm"][h:h + 1, :]))
            hn = ht * lax.rsqrt(jnp.mean(ht * ht, axis=0, keepdims=True) + NORM_EPS) * gmhb_ref[h]
            m_out.append(_sigmoid(proj_ref[rows, P_OM + h * M_DIM:P_OM + (h + 1) * M_DIM]) * hn.T
                         * _silu(proj_ref[rows, P_ZM + h * M_DIM:P_ZM + (h + 1) * M_DIM]))

        cat_ref[rows, :] = jnp.concatenate([a_out] + m_out, axis=1).astype(BF16)
        yield

    @pl.when((bi == 0) & (j == 0))
    def _():
        for sb in range(PROMPT_SUBBLOCKS):
            norm_rows(sb)
        for piece in proj_pieces(0):
            piece()

    def norm_next():
        xn_ref[...] = _rms(xnext_ref[0], gpre_ref[...]).astype(BF16)

    def run_chunks(sb, fillers):
        n_fill, n_slots, slot = len(fillers), sub_chunks * CHUNK_STAGES, 0
        for c in range(sb * sub_chunks, (sb + 1) * sub_chunks):
            for _ in chunk(c):
                for _ in range(-(-(slot + 1) * n_fill // n_slots) + (-slot * n_fill // n_slots)):
                    fillers.pop(0)()
                slot += 1
        assert slot == n_slots and not fillers

    assert PROMPT_SUBBLOCKS == 2
    run_chunks(0, proj_pieces(1))
    nxt = proj_pieces(0)
    run_chunks(1, [functools.partial(out_rows, 0), norm_next] + nxt[:-TAIL_PIECES])
    out_rows(1)
    for piece in nxt[-TAIL_PIECES:]:
        piece()

    @pl.when(j == nt - 1)
    def _():
        wk_ref[0] = proj_ref[tb - ROWS:tb, P_KA:P_KA + 128].T
        wv_ref[0] = proj_ref[tb - ROWS:tb, P_VA:P_VA + 128].T
        for h in range(M_HEADS):
            ct = ct_ref[h]
            c_ref[0, h] = ct[0:M_DIM].T
            n_ref[0, h:h + 1, :] = ct[M_DIM:M_DIM + 1]
        seq_lane = lax.broadcasted_iota(jnp.int32, m_ref.shape, 1) == bi
        m_ref[...] = jnp.where(seq_lane, mst_ref[0:M_HEADS, 0:m_ref.shape[1]], m_ref[...])


def _prompt_call(x, gpre, wmain, wt, bg, sinks, gmhb, wout, gpost, tb):
    bsz, seq, _ = x.shape
    nt = seq // tb
    full = lambda shape: pl.BlockSpec(shape, lambda b, j: (0,) * len(shape))

    def next_block(b, j):
        wrap = (j + 1 == nt).astype(jnp.int32)
        return (jnp.minimum(b + wrap, bsz - 1), (j + 1) * (1 - wrap), 0)

    out_shapes = (
        jax.ShapeDtypeStruct((bsz, seq, D_MODEL), F32),
        jax.ShapeDtypeStruct((bsz, 128, ROWS), F32),
        jax.ShapeDtypeStruct((bsz, 128, ROWS), F32),
        jax.ShapeDtypeStruct((bsz, M_HEADS, M_DIM, M_DIM), F32),
        jax.ShapeDtypeStruct((bsz, M_HEADS, M_DIM), F32),
        jax.ShapeDtypeStruct((M_HEADS, bsz), F32),
    )
    return pl.pallas_call(
        functools.partial(_prompt_kernel, tb=tb, nt=nt),
        grid=(bsz, nt),
        in_specs=[
            pl.BlockSpec((1, tb, D_MODEL), lambda b, j: (b, j, 0)),
            pl.BlockSpec((1, tb, D_MODEL), next_block),
            full((1, D_MODEL)), full((D_MODEL, P_MAIN_PAD)), full((T_ROWS, D_MODEL)), full((8, ROWS)),
            full((1, ATT_HEADS)), full((M_HEADS, M_DIM, ROWS)), full((D_MODEL, D_MODEL)), full((1, D_MODEL)),
        ],
        out_specs=(
            pl.BlockSpec((1, tb, D_MODEL), lambda b, j: (b, j, 0)),
            pl.BlockSpec((1, 128, ROWS), lambda b, j: (b, 0, 0)),
            pl.BlockSpec((1, 128, ROWS), lambda b, j: (b, 0, 0)),
            pl.BlockSpec((1, M_HEADS, M_DIM, M_DIM), lambda b, j: (b, 0, 0, 0)),
            pl.BlockSpec((1, M_HEADS, M_DIM), lambda b, j: (b, 0, 0)),
            pl.BlockSpec((M_HEADS, bsz), lambda b, j: (0, 0)),
        ),
        out_shape=out_shapes,
        scratch_shapes=[
            pltpu.VMEM((tb, P_MAIN), F32),
            pltpu.VMEM((tb // ROWS, T_ROWS, ROWS), F32),
            pltpu.VMEM((tb, D_MODEL), BF16),
            pltpu.VMEM((tb, D_MODEL), BF16),
            pltpu.VMEM((ROWS, 128), BF16),
            pltpu.VMEM((128, ROWS), BF16),
            pltpu.VMEM((M_HEADS, STATE_ROWS, M_DIM), F32),
            pltpu.VMEM((8, ROWS), F32),
            pltpu.VMEM((2 * ATT_KV, 2 * ROWS, ATT_GROUP * ROWS), F32),
        ],
        compiler_params=pltpu.CompilerParams(
            dimension_semantics=("arbitrary", "arbitrary"), vmem_limit_bytes=VMEM_LIMIT_BYTES),
        name="prompt_layer",
    )(x, x, gpre, wmain, wt, bg, sinks, gmhb, wout, gpost)


def _sample_bias_new(head):
    r = lax.broadcasted_iota(jnp.int32, (ROWS, ROWS), 0)
    c = lax.broadcasted_iota(jnp.int32, (ROWS, ROWS), 1)
    valid = ((r >> 3) == (c >> 3)) & (r >= c)
    return jnp.where(valid, -_slope(head) * (r - c).astype(F32), NEG_BIG)


def _sample_bias_cache(head):
    r = lax.broadcasted_iota(jnp.int32, (ROWS, ROWS), 0)
    c = lax.broadcasted_iota(jnp.int32, (ROWS, ROWS), 1)
    diff = (r & (SAMPLE_SEQ - 1)) + ROWS - c
    return jnp.where(diff < ROWS, -_slope(head) * diff.astype(F32), NEG_BIG)


def _sample_kernel(x_ref, kct_hbm, vct_hbm, cin_hbm, nin_ref, m0_ref,
                   gpre_ref, wmain_ref, wt_ref, bg_ref, sink_ref, gmhb_ref, wout_ref, gpost_ref,
                   y_ref, kot_ref, vot_ref, cout_ref, nout_ref, mout_ref,
                   projfull_ref, projt_ref, cat_ref, qh_ref, sc_ref, oc_ref, qc_ref, wv_ref, kt_ref,
                   decb_ref, bn_ref, bc_ref, sn_ref, pn_ref, cbuf_ref, kbuf_ref, vbuf_ref, sems):
    step, part = pl.program_id(0), pl.program_id(1)

    point = step * SAMPLE_PARTS + part
    n_points = pl.num_programs(0) * SAMPLE_PARTS
    streams = ((cin_hbm, cbuf_ref), (kct_hbm, kbuf_ref), (vct_hbm, vbuf_ref))

    def ring_copies(p):
        slot = lax.rem(p, STATE_SLOTS)
        return [pltpu.make_async_copy(hbm.at[pl.ds(p * SAMPLE_GROUP, SAMPLE_GROUP)], buf.at[slot], sems.at[k, slot])
                for k, (hbm, buf) in enumerate(streams)]

    @pl.when(point == 0)
    def _():
        for p in range(STATE_SLOTS - 1):
            for copy in ring_copies(point + p):
                copy.start()

    @pl.when(point + STATE_SLOTS - 1 < n_points)
    def _():
        for copy in ring_copies(point + STATE_SLOTS - 1):
            copy.start()

    @pl.when((step == 0) & (part == 0))
    def _():
        for hh in range(ATT_HEADS):
            bn_ref[hh] = _sample_bias_new(hh)
            bc_ref[hh] = _sample_bias_cache(hh)
        mout_ref[...] = jnp.zeros_like(mout_ref)

    @pl.when(part == 0)
    def _():
        xn = _rms(x_ref[...], gpre_ref[...]).astype(BF16)
        projfull_ref[...] = jnp.dot(xn, wmain_ref[:, 0:P_MAIN], preferred_element_type=F32)
        ptf = lax.dot_general(wt_ref[...], xn, NT_DIMS, preferred_element_type=F32)
        for g in range(SAMPLE_PARTS):
            projt_ref[g] = ptf[:, g * ROWS:(g + 1) * ROWS]

    part_rows = pl.ds(pl.multiple_of(part * ROWS, ROWS), ROWS)
    proj_ref = projfull_ref.at[part_rows]
    pt = projt_ref[part]

    ri = lax.broadcasted_iota(jnp.int32, (ROWS, ROWS), 0)
    ci = lax.broadcasted_iota(jnp.int32, (ROWS, ROWS), 1)
    same_seq = (ri >> 3) == (ci >> 3)
    mask_t = same_seq & (ri <= ci)
    last_sel = (same_seq & ((ri & (SAMPLE_SEQ - 1)) == SAMPLE_SEQ - 1)).astype(BF16)
    row16 = lax.broadcasted_iota(jnp.int32, (SAMPLE_GROUP, ROWS), 0)
    lane16 = lax.broadcasted_iota(jnp.int32, (SAMPLE_GROUP, ROWS), 1)
    seq_of_lane = (lane16 >> 3) == row16
    ones_rows = jnp.ones((ONES_ROWS, ROWS), F32)

    qa = proj_ref[:, P_QA:P_QA + 512] * ATT_SCALE
    for hh in range(ATT_HEADS):
        blk = qa[:, (hh // 2) * 128:(hh // 2 + 1) * 128]
        if hh % 2 == 1:
            blk = pltpu.roll(blk, ATT_DIM, 1)
        qh_ref[hh] = blk[:, 0:ATT_DIM]
    ka = proj_ref[:, P_KA:P_KA + 128]
    kat = ka.T
    vat = proj_ref[:, P_VA:P_VA + 128].T

    keep_new = ci >= ROWS - SAMPLE_SEQ
    for copy in ring_copies(point):
        copy.wait()
    cin_ref, kct_ref, vct_ref = (buf.at[lax.rem(point, STATE_SLOTS)] for _, buf in streams)

    for b in range(SAMPLE_GROUP):
        rows = pl.ds(b * SAMPLE_SEQ, SAMPLE_SEQ)
        kct = kct_ref[b]
        shift = (ROWS - SAMPLE_SEQ - b * SAMPLE_SEQ) % ROWS
        new_k, new_v = (kat, vat) if shift == 0 else (pltpu.roll(kat, shift, 1), pltpu.roll(vat, shift, 1))
        kot_ref[b] = jnp.where(keep_new, new_k, pltpu.roll(kct, ROWS - SAMPLE_SEQ, 1))
        vot_ref[b] = jnp.where(keep_new, new_v, pltpu.roll(vct_ref[b], ROWS - SAMPLE_SEQ, 1))
        for kv in range(ATT_KV):
            lhs = jnp.concatenate([qh_ref[kv * ATT_GROUP + g, rows, :] for g in range(ATT_GROUP)], axis=0)
            res = _bdot(lhs, kct[kv * ATT_DIM:(kv + 1) * ATT_DIM, :])
            for g in range(ATT_GROUP):
                sc_ref[kv * ATT_GROUP + g, rows, :] = res[g * SAMPLE_SEQ:(g + 1) * SAMPLE_SEQ]
        for h in range(M_HEADS):
            qc_ref[h, rows, :] = _bdot(proj_ref[rows, P_QM + h * M_DIM:P_QM + (h + 1) * M_DIM], cin_ref[b, h])

    o_new, esinks = [], []
    for hh in range(ATT_HEADS):
        kv = hh // ATT_GROUP
        sn_ref[hh] = _bdot(qh_ref[hh], kat[kv * ATT_DIM:(kv + 1) * ATT_DIM, :]) + bn_ref[hh]
    for hh in range(ATT_HEADS):
        s_n = sn_ref[hh]
        s_c = sc_ref[hh] + bc_ref[hh]
        sink = sink_ref[0:1, hh:hh + 1]
        mx = jnp.maximum(jnp.max(jnp.maximum(s_n, s_c), axis=-1, keepdims=True), sink)
        mx_b = jnp.broadcast_to(mx, (ROWS, ROWS))
        pn_ref[hh] = jnp.exp(s_n - mx_b).astype(BF16)
        sc_ref[hh] = jnp.exp(s_c - mx_b)
        esinks.append(jnp.exp(sink - mx))
    for hh in range(ATT_HEADS):
        kv = hh // ATT_GROUP
        vaug = jnp.concatenate([vat[kv * ATT_DIM:(kv + 1) * ATT_DIM, :], ones_rows], axis=0)
        o_new.append(_bdot_nt(pn_ref[hh], vaug))

    first_seq = (step * SAMPLE_PARTS + part) * SAMPLE_GROUP
    seq_step = ri == first_seq + (ci >> 3)
    m0 = _exact_dot(jnp.concatenate([m0_ref[...], jnp.zeros((8 - M_HEADS, ROWS), F32)], axis=0),
                    seq_step.astype(BF16))
    r = _gate_rows(pt[T_GATES:T_GATES + 8, :] + bg_ref[...], m0, mask_t, last_sel)
    step_seq = (ci == first_seq + (ri >> 3)) & ((ri & (SAMPLE_SEQ - 1)) == 0)
    mout_ref[...] += _exact_dot(r["m_new"], step_seq.astype(BF16))[0:M_HEADS]
    m_out = []
    for h in range(M_HEADS):
        q = proj_ref[:, P_QM + h * M_DIM:P_QM + (h + 1) * M_DIM].astype(BF16)
        kf = proj_ref[:, P_KM + h * M_DIM:P_KM + (h + 1) * M_DIM] * K_SCALE
        k = kf.astype(BF16)
        vt = pt[T_VM + h * M_DIM:T_VM + (h + 1) * M_DIM, :]
        n_h = nin_ref[h]
        r1 = lax.dot_general(jnp.concatenate([k, n_h.astype(BF16)], axis=0), q, NT_DIMS,
                             preferred_element_type=F32)
        st = r1[0:ROWS] * jnp.exp(r["a_masked"][h] + r["bm"][h:h + 1, :])
        q_n = jnp.sum(jnp.where(seq_of_lane, r1[ROWS:ROWS + SAMPLE_GROUP], 0.0), axis=0, keepdims=True)
        g_row = r["gexp"][h:h + 1, :]
        num = _bdot(vt, st) + g_row * qc_ref[h].T
        den = jnp.sum(st, axis=0, keepdims=True) + g_row * q_n
        ht = num * (1.0 / jnp.maximum(jnp.abs(den), r["enm"][h:h + 1, :]))
        hn = ht * lax.rsqrt(jnp.mean(ht * ht, axis=0, keepdims=True) + NORM_EPS) * gmhb_ref[h]
        m_out.append(_sigmoid(proj_ref[:, P_OM + h * M_DIM:P_OM + (h + 1) * M_DIM]) * hn.T
                     * _silu(proj_ref[:, P_ZM + h * M_DIM:P_ZM + (h + 1) * M_DIM]))
        w_row = r["w"][h:h + 1, :]
        dec16 = jnp.sum(jnp.where(lane16 == row16 * SAMPLE_SEQ, r["decay"][h:h + 1, :], 0.0),
                        axis=1, keepdims=True)
        nout_ref[h] = dec16 * n_h + _bdot(jnp.where(seq_of_lane, w_row, 0.0), k)
        decb_ref[h] = jnp.broadcast_to(dec16, (SAMPLE_GROUP, ROWS))
        wv_ref[h] = (vt * w_row).T
        kt_ref[h] = kf.T.astype(BF16)

    for b in range(SAMPLE_GROUP):
        rows = pl.ds(b * SAMPLE_SEQ, SAMPLE_SEQ)
        vct = vct_ref[b]
        for kv in range(ATT_KV):
            vaug = jnp.concatenate([vct[kv * ATT_DIM:(kv + 1) * ATT_DIM, :], ones_rows], axis=0)
            pl_ = jnp.concatenate([sc_ref[kv * ATT_GROUP + g, rows, :] for g in range(ATT_GROUP)], axis=0)
            res = _bdot_nt(pl_, vaug)
            for g in range(ATT_GROUP):
                oc_ref[kv * ATT_GROUP + g, rows, 0:ATT_DIM + ONES_ROWS] = res[g * SAMPLE_SEQ:(g + 1) * SAMPLE_SEQ]
        in_seq = (ri >> 3) == b
        for h in range(M_HEADS):
            upd = jnp.dot(kt_ref[h], jnp.where(in_seq, wv_ref[h], 0.0).astype(BF16), preferred_element_type=F32)
            cout_ref[b, h] = decb_ref[h, b:b + 1, :] * cin_ref[b, h] + upd

    att = []
    for pair in range(ATT_HEADS // 2):
        halves = []
        for hh in (2 * pair, 2 * pair + 1):
            on, oc = o_new[hh], oc_ref[hh]
            den = on[:, ATT_DIM:ATT_DIM + 1] + oc[:, ATT_DIM:ATT_DIM + 1] + esinks[hh]
            halves.append((on[:, 0:ATT_DIM] + oc[:, 0:ATT_DIM]) * (1.0 / den))
        att.append(jnp.concatenate(halves, axis=1))
    a_out = jnp.concatenate(att, axis=1) * _silu(proj_ref[:, P_ZA:P_ZA + 512])
    cat_ref[part_rows, :] = jnp.concatenate([a_out] + m_out, axis=1).astype(BF16)

    @pl.when(part == SAMPLE_PARTS - 1)
    def _():
        y_ref[...] = _out_tail(cat_ref[...], x_ref[...], wout_ref, gpost_ref)


def _sample_call(x, kct, vct, cin, nin, m0, gpre, wmain, wt, bg, sinks, gmhb, wout, gpost):
    nrows = x.shape[0]
    ngroups = nrows // ROWS
    nseq = ngroups * SAMPLE_GROUP
    parts = SAMPLE_PARTS
    assert ngroups % parts == 0 and ngroups >= STATE_SLOTS - 1
    assert nseq == ROWS, "the per-sequence stabiliser state is handled as one 128-lane row per head"
    full = lambda shape: pl.BlockSpec(shape, lambda i, p: (0,) * len(shape))
    grp = SAMPLE_GROUP
    group = lambda i, p: i * parts + p
    out_shapes = (
        jax.ShapeDtypeStruct((nrows, D_MODEL), F32),
        jax.ShapeDtypeStruct((nseq, 128, ROWS), F32),
        jax.ShapeDtypeStruct((nseq, 128, ROWS), F32),
        jax.ShapeDtypeStruct((nseq, M_HEADS, M_DIM, M_DIM), F32),
        jax.ShapeDtypeStruct((M_HEADS, nseq, M_DIM), F32),
        jax.ShapeDtypeStruct((M_HEADS, nseq), F32),
    )
    return pl.pallas_call(
        _sample_kernel,
        grid=(ngroups // parts, parts),
        in_specs=[
            pl.BlockSpec((parts * ROWS, D_MODEL), lambda i, p: (i, 0)),
            pl.BlockSpec(memory_space=pl.ANY),
            pl.BlockSpec(memory_space=pl.ANY),
            pl.BlockSpec(memory_space=pl.ANY),
            pl.BlockSpec((M_HEADS, grp, M_DIM), lambda i, p: (0, group(i, p), 0)),
            full((M_HEADS, nseq)),
            full((1, D_MODEL)), full((D_MODEL, P_MAIN_PAD)), full((T_ROWS, D_MODEL)), full((8, ROWS)),
            full((1, ATT_HEADS)), full((M_HEADS, M_DIM, ROWS)), full((D_MODEL, D_MODEL)), full((1, D_MODEL)),
        ],
        out_specs=(
            pl.BlockSpec((parts * ROWS, D_MODEL), lambda i, p: (i, 0)),
            pl.BlockSpec((grp, 128, ROWS), lambda i, p: (group(i, p), 0, 0)),
            pl.BlockSpec((grp, 128, ROWS), lambda i, p: (group(i, p), 0, 0)),
            pl.BlockSpec((grp, M_HEADS, M_DIM, M_DIM), lambda i, p: (group(i, p), 0, 0, 0)),
            pl.BlockSpec((M_HEADS, grp, M_DIM), lambda i, p: (0, group(i, p), 0)),
            full((M_HEADS, nseq)),
        ),
        out_shape=out_shapes,
        scratch_shapes=[
            pltpu.VMEM((parts * ROWS, P_MAIN), F32),
            pltpu.VMEM((parts, T_ROWS, ROWS), F32),
            pltpu.VMEM((parts * ROWS, D_MODEL), BF16),
            pltpu.VMEM((ATT_HEADS, ROWS, ATT_DIM), F32),
            pltpu.VMEM((ATT_HEADS, ROWS, ROWS), F32),
            pltpu.VMEM((ATT_HEADS, ROWS, ROWS), F32),
            pltpu.VMEM((M_HEADS, ROWS, M_DIM), F32),
            pltpu.VMEM((M_HEADS, ROWS, M_DIM), F32),
            pltpu.VMEM((M_HEADS, M_DIM, ROWS), BF16),
            pltpu.VMEM((M_HEADS, SAMPLE_GROUP, ROWS), F32),
            pltpu.VMEM((ATT_HEADS, ROWS, ROWS), F32),
            pltpu.VMEM((ATT_HEADS, ROWS, ROWS), F32),
            pltpu.VMEM((ATT_HEADS, ROWS, ROWS), F32),
            pltpu.VMEM((ATT_HEADS, ROWS, ROWS), BF16),
            pltpu.VMEM((STATE_SLOTS, grp, M_HEADS, M_DIM, M_DIM), F32),
            pltpu.VMEM((STATE_SLOTS, grp, 128, ROWS), F32),
            pltpu.VMEM((STATE_SLOTS, grp, 128, ROWS), F32),
            pltpu.SemaphoreType.DMA((3, STATE_SLOTS)),
        ],
        compiler_params=pltpu.CompilerParams(
            dimension_semantics=("arbitrary", "arbitrary"), vmem_limit_bytes=VMEM_LIMIT_BYTES),
        name="sample_layer",
    )(x, kct, vct, cin, nin, m0, gpre, wmain, wt, bg, sinks, gmhb, wout, gpost)


PROMPT_BLOCK = 512
PROMPT_SUBBLOCKS = 2
PROJ_COL_STEP = 512
CHUNK_STAGES = 7
TAIL_PIECES = 3


def _window_in(cache):
    nseq = cache.shape[0]
    return cache.transpose(0, 2, 3, 1).reshape(nseq, ATT_KV * ATT_DIM, ROWS)


def _window_out(win_t):
    nseq = win_t.shape[0]
    return win_t.reshape(nseq, ATT_KV, ATT_DIM, ROWS).transpose(0, 3, 1, 2)[None]


def kernel(x_prompt, x_sample, cache_win_k, cache_win_v, state_C, state_n, state_m,
           g_pre, w_in, b_gate, attn_sinks, g_mh, w_out, g_post):
    depth = g_pre.shape[0]
    assert depth == 1, "single-layer trunk"
    nseq, sseq, _ = x_sample.shape
    assert sseq == SAMPLE_SEQ and nseq % SAMPLE_GROUP == 0

    gpre = g_pre[0].reshape(1, D_MODEL)
    gpost = g_post[0].reshape(1, D_MODEL)
    wmain, wt, wout, gmhb, bg = _weights_call(w_in[0].T, w_out[0], g_mh[0], b_gate)
    sinks = attn_sinks[0].reshape(1, ATT_HEADS)

    yp, wkp, wvp, cp, np_, mp = _prompt_call(x_prompt, gpre, wmain, wt, bg, sinks, gmhb, wout, gpost, PROMPT_BLOCK)

    ys, wks, wvs, cs, ns, ms = _sample_call(
        x_sample.reshape(nseq * sseq, D_MODEL), _window_in(cache_win_k[0]), _window_in(cache_win_v[0]),
        state_C[0], state_n[0].transpose(1, 0, 2), state_m[0].T,
        gpre, wmain, wt, bg, sinks, gmhb, wout, gpost)

    return (yp, ys.reshape(nseq, sseq, D_MODEL), _window_out(wkp), _window_out(wvp),
            cp[None], np_[None], mp.T[None],
            _window_out(wks), _window_out(wvs), cs[None], ns.transpose(1, 0, 2)[None], ms.T[None])
```
